```python
import jax, jax.numpy as jnp
from jax import lax
import numpy as np

D_MODEL = 1024
BATCH = 8
SEQ = 4096
DEPTH = 1

GRID_W = 64
CTX_LEN = 256
N_MOD = 6
EPS = 1e-6
MLA_HEADS = 4
MLA_Q_RANK = 256
MLA_KV_RANK = 128
MLA_NOPE = 128
MLA_ROPE = 64
MLA_V = 128
MLA_QK = MLA_NOPE + MLA_ROPE
ROPE_AXIS = MLA_ROPE // 2
ROPE_BASE = 10000.0
Q_BLOCK = 128
ML_HEADS = 4
ML_QK = 64
ML_V = 128
ML_CHUNK = 64
CONV_W = 3
FGATE_BIAS_LO = 3.0
FGATE_BIAS_HI = 6.0
MLA_COLS = MLA_Q_RANK + MLA_KV_RANK + MLA_ROPE
ML_COLS = 2 * ML_HEADS * ML_QK + 2 * ML_HEADS * ML_V + 4 * ML_HEADS
IN_COLS = MLA_COLS + ML_COLS
MIX_WIDTH = MLA_HEADS * MLA_V + ML_HEADS * ML_V
PEER_HEADS = 8
N_KEYS = 128
N_EXPERTS = N_KEYS * N_KEYS
PEER_DK = 256
PEER_DKH = PEER_DK // 2
PEER_TOPK = 16
PEER_BLOCK = 128

kernel_name = 'hybrid_mla_mlstm_peer_dit_block'


def rmsnorm(x, g):
    xf = x.astype(jnp.float32)
    y = xf * lax.rsqrt(jnp.mean(xf * xf, axis=-1, keepdims=True) + EPS)
    return (y * g.astype(jnp.float32)).astype(x.dtype)


def rope_tables(rows):
    row = jnp.repeat(jnp.arange(rows, dtype=jnp.float32), GRID_W)
    col = jnp.tile(jnp.arange(GRID_W, dtype=jnp.float32), rows)
    inv = ROPE_BASE ** (-jnp.arange(ROPE_AXIS // 2, dtype=jnp.float32) * (2.0 / ROPE_AXIS))
    ang_r = row[:, None] * inv
    ang_c = col[:, None] * inv
    return (jnp.cos(ang_r), jnp.sin(ang_r), jnp.cos(ang_c), jnp.sin(ang_c))


def rotate(t, cos, sin):
    half = t.shape[-1] // 2
    t1, t2 = t[..., :half], t[..., half:]
    cos = cos.astype(t.dtype)
    sin = sin.astype(t.dtype)
    return jnp.concatenate([t1 * cos - t2 * sin, t2 * cos + t1 * sin], axis=-1)


def apply_rope2d(t, rope):
    cos_r, sin_r, cos_c, sin_c = rope
    t_row = rotate(t[..., MLA_NOPE:MLA_NOPE + ROPE_AXIS], cos_r, sin_r)
    t_col = rotate(t[..., MLA_NOPE + ROPE_AXIS:], cos_c, sin_c)
    return jnp.concatenate([t[..., :MLA_NOPE], t_row, t_col], axis=-1)


def split_heads(t, n_heads):
    b, n, _ = t.shape
    return t.reshape(b, n, n_heads, -1).transpose(0, 2, 1, 3)


def merge_heads(t):
    b, h, n, d = t.shape
    return t.transpose(0, 2, 1, 3).reshape(b, n, h * d)


def mla_prep(cols, g_cq, w_uq, g_ckv, w_ukv, g_qn, g_kn, rope):
    b, n, _ = cols.shape
    c_q = cols[..., :MLA_Q_RANK]
    c_kv = cols[..., MLA_Q_RANK:MLA_Q_RANK + MLA_KV_RANK]
    k_rope = cols[..., MLA_Q_RANK + MLA_KV_RANK:]
    q = (rmsnorm(c_q, g_cq) @ w_uq).reshape(b, n, MLA_HEADS, MLA_QK)
    kv = (rmsnorm(c_kv, g_ckv) @ w_ukv).reshape(b, n, MLA_HEADS, MLA_NOPE + MLA_V)
    k = jnp.concatenate([kv[..., :MLA_NOPE],
                         jnp.broadcast_to(k_rope[:, :, None, :], (b, n, MLA_HEADS, MLA_ROPE))], axis=-1)
    v = kv[..., MLA_NOPE:].transpose(0, 2, 1, 3)
    q = rmsnorm(q, g_qn).transpose(0, 2, 1, 3)
    k = rmsnorm(k, g_kn).transpose(0, 2, 1, 3)
    if rope is not None:
        q = apply_rope2d(q, rope)
        k = apply_rope2d(k, rope)
    return q, k, v


def attend(q, k, v):
    s = jnp.einsum('bhqd,bhkd->bhqk', q, k).astype(jnp.float32) * (MLA_QK ** -0.5)
    p = jax.nn.softmax(s, axis=-1).astype(v.dtype)
    return jnp.einsum('bhqk,bhkd->bhqd', p, v)


def attend_blocked(q, k, v):
    b, h, n, dk = q.shape
    nb = n // Q_BLOCK
    qb = jnp.moveaxis(q.reshape(b, h, nb, Q_BLOCK, dk), 2, 0)
    ob = lax.map(lambda q_blk: attend(q_blk, k, v), qb)
    return jnp.moveaxis(ob, 0, 2).reshape(b, h, n, -1)


def mla_mixer(cols_lat, cols_ctx, g_cq, w_uq, g_ckv, w_ukv, g_qn, g_kn, rope, need_ctx_out):
    q_l, k_l, v_l = mla_prep(cols_lat, g_cq, w_uq, g_ckv, w_ukv, g_qn, g_kn, rope)
    q_c, k_c, v_c = mla_prep(cols_ctx, g_cq, w_uq, g_ckv, w_ukv, g_qn, g_kn, None)
    k_all = jnp.concatenate([k_c, k_l], axis=2)
    v_all = jnp.concatenate([v_c, v_l], axis=2)
    out_lat = merge_heads(attend_blocked(q_l, k_all, v_all))
    out_ctx = merge_heads(attend(q_c, k_c, v_c)) if need_ctx_out else None
    return out_lat, out_ctx


def centred_conv(u, w):
    k = w.shape[0]
    p = k // 2
    n = u.shape[1]
    up = jnp.pad(u, ((0, 0), (p, p), (0, 0)))
    return sum(up[:, j:j + n] * w[j] for j in range(k))


def mlstm_prep(cols, conv_w, b_i, b_f):
    b, n, _ = cols.shape
    n_qk = ML_HEADS * ML_QK
    n_v = ML_HEADS * ML_V
    qk = jax.nn.silu(centred_conv(cols[..., :2 * n_qk], conv_w))
    q = split_heads(qk[..., :n_qk], ML_HEADS).astype(jnp.float32) * (ML_QK ** -0.5)
    k = split_heads(qk[..., n_qk:], ML_HEADS).astype(jnp.float32)
    v = split_heads(cols[..., 2 * n_qk:2 * n_qk + n_v], ML_HEADS).astype(jnp.float32)
    o = cols[..., 2 * n_qk + n_v:2 * n_qk + 2 * n_v]
    gates = cols[..., 2 * n_qk + 2 * n_v:].astype(jnp.float32).reshape(b, n, 2, 2, ML_HEADS)
    ig = (gates[:, :, 0] + b_i.astype(jnp.float32)).transpose(2, 0, 3, 1)
    lf = jax.nn.log_sigmoid(gates[:, :, 1] + b_f.astype(jnp.float32)).transpose(2, 0, 3, 1)
    return q, k, v, o, ig, lf


def mlstm_scan(q, k, v, ig, lf, state, need_out):
    b, h, n_tok, _ = q.shape
    nc = n_tok // ML_CHUNK

    def chunks(a):
        return jnp.moveaxis(a.reshape(a.shape[:2] + (nc, ML_CHUNK) + a.shape[3:]), 2, 0)

    tri = jnp.tril(jnp.ones((ML_CHUNK, ML_CHUNK), dtype=bool))

    def body(carry, inp):
        c_mat, n_vec, m_prev = carry
        qc, kc, vc, ic, fc = inp
        cum_f = jnp.cumsum(fc, axis=-1)
        dmat = jnp.where(tri, cum_f[..., :, None] - cum_f[..., None, :] + ic[..., None, :], -jnp.inf)
        inter = cum_f + m_prev[..., None]
        m_t = jnp.maximum(inter, jnp.max(dmat, axis=-1))
        m_new = m_t[..., -1]
        w_s = jnp.exp(dmat[..., -1, :] - m_new[..., None])
        decay = jnp.exp(inter[..., -1] - m_new)
        c_new = decay[..., None, None] * c_mat + jnp.einsum('bhs,bhsv,bhsd->bhvd', w_s, vc, kc)
        n_new = decay[..., None] * n_vec + jnp.einsum('bhs,bhsd->bhd', w_s, kc)
        if not need_out:
            return (c_new, n_new, m_new), None
        wmat = jnp.exp(dmat - m_t[..., None]) * jnp.einsum('bhtd,bhsd->bhts', qc, kc)
        scale_inter = jnp.exp(inter - m_t)
        num = jnp.einsum('bhts,bhsv->bhtv', wmat, vc) + scale_inter[..., None] * jnp.einsum('bhvd,bhtd->bhtv', c_mat, qc)
        den = jnp.sum(wmat, axis=-1) + scale_inter * jnp.einsum('bhd,bhtd->bht', n_vec, qc)
        h_out = num / jnp.maximum(jnp.abs(den), jnp.exp(-m_t))[..., None]
        return (c_new, n_new, m_new), h_out

    state, hs = lax.scan(body, state, (chunks(q), chunks(k), chunks(v), chunks(ig), chunks(lf)))
    h_seq = jnp.moveaxis(hs, 0, 2).reshape(b, h, n_tok, -1) if need_out else None
    return state, h_seq


def mlstm_output(h_sum, o, g_out):
    b, h, n, dv = h_sum.shape
    hn = rmsnorm(h_sum.transpose(0, 2, 1, 3), g_out.reshape(h, dv)).reshape(b, n, h * dv)
    return (jax.nn.sigmoid(o.astype(jnp.float32)) * hn).astype(o.dtype)


def mlstm_mixer(cols_lat, cols_ctx, conv_w, b_i, b_f, g_out, need_ctx_out):
    q_l, k_l, v_l, o_l, ig_l, lf_l = mlstm_prep(cols_lat, conv_w, b_i, b_f)
    q_c, k_c, v_c, o_c, ig_c, lf_c = mlstm_prep(cols_ctx, conv_w, b_i, b_f)
    b = q_l.shape[0]
    outs_lat, outs_ctx = [], []
    for d in range(2):
        rev = d == 1

        def fl(a):
            return jnp.flip(a, axis=2) if rev else a

        state0 = (jnp.zeros((b, ML_HEADS, ML_V, ML_QK), jnp.float32),
                  jnp.zeros((b, ML_HEADS, ML_QK), jnp.float32),
                  jnp.zeros((b, ML_HEADS), jnp.float32))
        st_ctx, h_c = mlstm_scan(fl(q_c), fl(k_c), fl(v_c), fl(ig_c[d]), fl(lf_c[d]), state0, need_ctx_out)
        _, h_l = mlstm_scan(fl(q_l), fl(k_l), fl(v_l), fl(ig_l[d]), fl(lf_l[d]), st_ctx, True)
        outs_lat.append(fl(h_l))
        if need_ctx_out:
            outs_ctx.append(fl(h_c))
    out_lat = mlstm_output(outs_lat[0] + outs_lat[1], o_l, g_out)
    out_ctx = mlstm_output(outs_ctx[0] + outs_ctx[1], o_c, g_out) if need_ctx_out else None
    return out_lat, out_ctx


def peer(h, w_pq, sub_keys, expert_u, expert_v):
    b, n, d = h.shape
    xb_all = h.reshape(-1, PEER_BLOCK, d)

    def one(xb):
        q = (xb @ w_pq).reshape(PEER_BLOCK, PEER_HEADS, 2, PEER_DKH)
        s = jnp.einsum('thpd,hpkd->thpk', q, sub_keys).astype(jnp.float32)
        sv, si = lax.top_k(s, PEER_TOPK)
        cand = (sv[:, :, 0, :, None] + sv[:, :, 1, None, :]).reshape(PEER_BLOCK, PEER_HEADS, -1)
        cidx = (si[:, :, 0, :, None] * N_KEYS + si[:, :, 1, None, :]).reshape(PEER_BLOCK, PEER_HEADS, -1)
        best, pos = lax.top_k(cand, PEER_TOPK)
        eidx = jnp.take_along_axis(cidx, pos, axis=-1)
        g = jax.nn.softmax(best, axis=-1)
        act = jax.nn.gelu(jnp.einsum('thkd,td->thk', expert_u[eidx], xb).astype(jnp.float32), approximate=False)
        return jnp.einsum('thk,thkd->td', (g * act).astype(xb.dtype), expert_v[eidx])

    return lax.map(one, xb_all).reshape(b, n, d)


def setup_inputs(seed: int = 0) -> dict:
    key = jax.random.key(seed)
    ks = jax.random.split(key, 24)

    def nrm(k, shape, s):
        return jax.random.normal(k, shape, jnp.float32) * s

    def gain(k, shape):
        return 1.0 + nrm(k, shape, 0.01)

    D, L = D_MODEL, DEPTH
    fgate = jnp.linspace(FGATE_BIAS_LO, FGATE_BIAS_HI, ML_HEADS, dtype=jnp.float32)
    return {
        'x': nrm(ks[0], (BATCH, SEQ, D), 1.0),
        'c': nrm(ks[1], (BATCH, D), 1.0),
        'ctx': nrm(ks[2], (BATCH, CTX_LEN, D), 1.0),
        'c_ctx': nrm(ks[3], (D,), 1.0),
        'w_ada': nrm(ks[4], (L, D, N_MOD * D), 0.5 * D ** -0.5),
        'b_ada': nrm(ks[5], (L, N_MOD * D), 0.01),
        'g_norm1': gain(ks[6], (L, D)),
        'w_in': nrm(ks[7], (L, D, IN_COLS), D ** -0.5),
        'g_cq': gain(ks[8], (L, MLA_Q_RANK)),
        'w_uq': nrm(ks[9], (L, MLA_Q_RANK, MLA_HEADS * MLA_QK), MLA_Q_RANK ** -0.5),
        'g_ckv': gain(ks[10], (L, MLA_KV_RANK)),
        'w_ukv': nrm(ks[11], (L, MLA_KV_RANK, MLA_HEADS * (MLA_NOPE + MLA_V)), MLA_KV_RANK ** -0.5),
        'g_qn': gain(ks[12], (L, MLA_QK)),
        'g_kn': gain(ks[13], (L, MLA_QK)),
        'conv_qk': nrm(ks[14], (L, CONV_W, 2 * ML_HEADS * ML_QK), CONV_W ** -0.5),
        'b_igate': nrm(ks[15], (L, 2, ML_HEADS), 0.01),
        'b_fgate': fgate[None, None, :] + nrm(ks[16], (L, 2, ML_HEADS), 0.01),
        'g_mlstm': gain(ks[17], (L, ML_HEADS * ML_V)),
        'w_out': nrm(ks[18], (L, MIX_WIDTH, D), MIX_WIDTH ** -0.5),
        'g_norm2': gain(ks[19], (L, D)),
        'w_pq': nrm(ks[20], (L, D, PEER_HEADS * PEER_DK), D ** -0.5),
        'sub_keys': nrm(ks[21], (L, PEER_HEADS, 2, N_KEYS, PEER_DKH), PEER_DKH ** -0.5),
        'expert_u': nrm(ks[22], (L, N_EXPERTS, D), D ** -0.5),
        'expert_v': nrm(ks[23], (L, N_EXPERTS, D), PEER_HEADS ** -0.5),
    }


def reference(x, c, ctx, c_ctx, w_ada, b_ada, g_norm1, w_in, g_cq, w_uq, g_ckv, w_ukv, g_qn, g_kn,
              conv_qk, b_igate, b_fgate, g_mlstm, w_out, g_norm2, w_pq, sub_keys, expert_u, expert_v):
    n_lat = x.shape[1]
    ROWS = n_lat // GRID_W
    rope = rope_tables(ROWS)
    s_c = jax.nn.silu(c)
    s_cc = jax.nn.silu(c_ctx)
    for l in range(DEPTH):
        last = l == DEPTH - 1
        mod_lat = (s_c @ w_ada[l] + b_ada[l])[:, None, :]
        mod_ctx = s_cc @ w_ada[l] + b_ada[l]
        sh1, sc1, gt1, sh2, sc2, gt2 = jnp.split(mod_lat, N_MOD, axis=-1)
        csh1, csc1, cgt1, csh2, csc2, cgt2 = jnp.split(mod_ctx, N_MOD, axis=-1)
        h_lat = rmsnorm(x, g_norm1[l]) * (1.0 + sc1) + sh1
        h_ctx = rmsnorm(ctx, g_norm1[l]) * (1.0 + csc1) + csh1
        cols_lat = h_lat @ w_in[l]
        cols_ctx = h_ctx @ w_in[l]
        mla_lat, mla_ctx = mla_mixer(cols_lat[..., :MLA_COLS], cols_ctx[..., :MLA_COLS],
                                     g_cq[l], w_uq[l], g_ckv[l], w_ukv[l], g_qn[l], g_kn[l], rope, not last)
        ml_lat, ml_ctx = mlstm_mixer(cols_lat[..., MLA_COLS:], cols_ctx[..., MLA_COLS:],
                                     conv_qk[l], b_igate[l], b_fgate[l], g_mlstm[l], not last)
        x = x + gt1 * (jnp.concatenate([mla_lat, ml_lat], axis=-1) @ w_out[l])
        h2 = rmsnorm(x, g_norm2[l]) * (1.0 + sc2) + sh2
        x = x + gt2 * peer(h2, w_pq[l], sub_keys[l], expert_u[l], expert_v[l])
        if not last:
            ctx = ctx + cgt1 * (jnp.concatenate([mla_ctx, ml_ctx], axis=-1) @ w_out[l])
            hc2 = rmsnorm(ctx, g_norm2[l]) * (1.0 + csc2) + csh2
            ctx = ctx + cgt2 * peer(hc2, w_pq[l], sub_keys[l], expert_u[l], expert_v[l])
    return x
```

```python
import jax
import jax.numpy as jnp
from jax import lax
from jax.experimental import pallas as pl
from jax.experimental.pallas import tpu as pltpu

F32 = jnp.float32
BF16 = jnp.bfloat16
I32 = jnp.int32
EPS = 1e-6

GRID_W = 64
MLA_HEADS = 4
MLA_NOPE = 128
MLA_ROPE = 64
MLA_V = 128
MLA_QK = MLA_NOPE + MLA_ROPE
ROPE_BASE = 10000.0
ML_HEADS = 4
ML_QK = 64
ML_V = 128
ML_CHUNK = 64
PEER_HEADS = 8
N_KEYS = 128
PEER_TOPK = 16

LANES = 128
SUBLANES = 8
VMEM_LIMIT_BYTES = 56 * 1024 * 1024

C_Q0, C_KV0, C_KR0, C_KRS0, C_QK0, C_V0, C_O0, C_G0, C_END = 0, 256, 384, 448, 512, 1024, 1536, 2048, 2176
HIGHEST = lax.Precision.HIGHEST


def _params(sem):
    return pltpu.CompilerParams(dimension_semantics=sem, vmem_limit_bytes=VMEM_LIMIT_BYTES)


def _rms(x):
    return x * lax.rsqrt(jnp.mean(x * x, axis=-1, keepdims=True) + EPS)


def _ada_kernel(c_ref, w_ref, b_ref, o_ref):
    c = c_ref[...]
    s = c * jax.nn.sigmoid(c)
    o_ref[...] = jnp.dot(s.astype(BF16), w_ref[...], preferred_element_type=F32) + b_ref[...]


def _ada(cc, w, b):
    rows, d = cc.shape
    n = w.shape[1]
    bn = n // 4
    return pl.pallas_call(
        _ada_kernel,
        out_shape=jax.ShapeDtypeStruct((rows, n), F32),
        grid=(4,),
        in_specs=[pl.BlockSpec((rows, d), lambda j: (0, 0)),
                  pl.BlockSpec((d, bn), lambda j: (0, j)),
                  pl.BlockSpec((1, bn), lambda j: (0, j))],
        out_specs=pl.BlockSpec((rows, bn), lambda j: (0, j)),
        compiler_params=_params(("arbitrary",)),
        name="ada",
    )(cc, w, b)


def _inproj_kernel(x_ref, g_ref, sc_ref, sh_ref, w_ref, wg_ref, o_ref, gt_ref):
    x = x_ref[0]
    h = _rms(x) * g_ref[...] * (1.0 + sc_ref[0]) + sh_ref[0]
    hb = h.astype(BF16)
    o_ref[0] = jnp.dot(hb, w_ref[...], preferred_element_type=F32)
    gt_ref[0] = lax.dot_general(wg_ref[...], hb, (((1,), (1,)), ((), ())), preferred_element_type=F32)


def _inproj(x, g, sc, sh, w, wg, tm):
    b, n, d = x.shape
    nc = w.shape[1]
    return pl.pallas_call(
        _inproj_kernel,
        out_shape=(jax.ShapeDtypeStruct((b, n, nc), F32), jax.ShapeDtypeStruct((b, 16, n), F32)),
        grid=(b, n // tm),
        in_specs=[pl.BlockSpec((1, tm, d), lambda i, j: (i, j, 0)),
                  pl.BlockSpec((1, d), lambda i, j: (0, 0)),
                  pl.BlockSpec((1, 1, d), lambda i, j: (i, 0, 0)),
                  pl.BlockSpec((1, 1, d), lambda i, j: (i, 0, 0)),
                  pl.BlockSpec((d, nc), lambda i, j: (0, 0)),
                  pl.BlockSpec((16, d), lambda i, j: (0, 0))],
        out_specs=(pl.BlockSpec((1, tm, nc), lambda i, j: (i, j, 0)),
                   pl.BlockSpec((1, 16, tm), lambda i, j: (i, 0, j))),
        compiler_params=_params(("parallel", "parallel")),
        name="inproj",
    )(x, g, sc, sh, w, wg)


def _mla_prep_kernel(c_ref, gcq_ref, wuq_ref, gckv_ref, wukv_ref, gq_ref, gqs_ref, gk_ref, gks_ref,
                     cos_ref, sin_ref, q_ref, k_ref, v_ref):
    c = c_ref[0]
    cq = _rms(c[:, C_Q0:C_KV0]) * gcq_ref[...]
    ckv = _rms(c[:, C_KV0:C_KR0]) * gckv_ref[...]
    kr = c[:, C_KR0:C_KRS0]
    krs = c[:, C_KRS0:C_QK0]
    q_raw = jnp.dot(cq.astype(BF16), wuq_ref[...], preferred_element_type=F32)
    kv_raw = jnp.dot(ckv.astype(BF16), wukv_ref[...], preferred_element_type=F32)
    cos = cos_ref[...]
    sin = sin_ref[...]
    gq = gq_ref[...]
    gk = gk_ref[...]
    kr_ss = jnp.sum(kr * kr, axis=-1, keepdims=True)
    for h in range(MLA_HEADS):
        o = h * 256
        qn = q_raw[:, o:o + 128]
        qr = q_raw[:, o + 128:o + 192]
        qs = q_raw[:, o + 192:o + 256]
        ss = jnp.sum(qn * qn, axis=-1, keepdims=True) + jnp.sum(qr * qr, axis=-1, keepdims=True)
        r = lax.rsqrt(ss * (1.0 / MLA_QK) + EPS) * (MLA_QK ** -0.5)
        q_ref[0, h, :, 0:128] = (qn * r * gq[:, 0:128]).astype(BF16)
        q_ref[0, h, :, 128:192] = ((qr * r * gq[:, 128:192]) * cos + (qs * r * gqs_ref[...]) * sin).astype(BF16)
        kn = kv_raw[:, o:o + 128]
        ss = jnp.sum(kn * kn, axis=-1, keepdims=True) + kr_ss
        r = lax.rsqrt(ss * (1.0 / MLA_QK) + EPS)
        k_ref[0, h, :, 0:128] = (kn * r * gk[:, 0:128]).astype(BF16)
        k_ref[0, h, :, 128:192] = ((kr * r * gk[:, 128:192]) * cos + (krs * r * gks_ref[...]) * sin).astype(BF16)
        v_ref[0, h] = kv_raw[:, o + 128:o + 256].astype(BF16)


def _mla_prep(cols, gcq, wuq, gckv, wukv, gq, gqs, gk, gks, cos, sin, tm):
    b, n, _ = cols.shape
    full = lambda a: pl.BlockSpec(a.shape, lambda i, j: (0,) * a.ndim)
    return pl.pallas_call(
        _mla_prep_kernel,
        out_shape=(jax.ShapeDtypeStruct((b, MLA_HEADS, n, MLA_QK), BF16),
                   jax.ShapeDtypeStruct((b, MLA_HEADS, n, MLA_QK), BF16),
                   jax.ShapeDtypeStruct((b, MLA_HEADS, n, MLA_V), BF16)),
        grid=(b, n // tm),
        in_specs=[pl.BlockSpec((1, tm, 512), lambda i, j: (i, j, 0)),
                  full(gcq), full(wuq), full(gckv), full(wukv), full(gq), full(gqs), full(gk), full(gks),
                  pl.BlockSpec((tm, MLA_ROPE), lambda i, j: (j, 0)),
                  pl.BlockSpec((tm, MLA_ROPE), lambda i, j: (j, 0))],
        out_specs=(pl.BlockSpec((1, MLA_HEADS, tm, MLA_QK), lambda i, j: (i, 0, j, 0)),
                   pl.BlockSpec((1, MLA_HEADS, tm, MLA_QK), lambda i, j: (i, 0, j, 0)),
                   pl.BlockSpec((1, MLA_HEADS, tm, MLA_V), lambda i, j: (i, 0, j, 0))),
        compiler_params=_params(("parallel", "parallel")),
        name="mla_prep",
    )(cols, gcq, wuq, gckv, wukv, gq, gqs, gk, gks, cos, sin)


def _attn_kernel(q_ref, k_ref, v_ref, o_ref):
    q = q_ref[0, 0]
    s = lax.dot_general(q, k_ref[0, 0], (((1,), (1,)), ((), ())), preferred_element_type=F32)
    m = jnp.max(s, axis=-1, keepdims=True)
    p = jnp.exp(s - m)
    l = jnp.sum(p, axis=-1, keepdims=True)
    o = jnp.dot(p.astype(BF16), v_ref[0, 0], preferred_element_type=F32)
    o_ref[0] = (o / l).astype(BF16)


def _attention(q, k, v, tq):
    b, h, n, dk = q.shape
    nk = k.shape[2]
    dv = v.shape[3]
    return pl.pallas_call(
        _attn_kernel,
        out_shape=jax.ShapeDtypeStruct((b, n, h * dv), BF16),
        grid=(b, h, n // tq),
        in_specs=[pl.BlockSpec((1, 1, tq, dk), lambda i, j, t: (i, j, t, 0)),
                  pl.BlockSpec((1, 1, nk, dk), lambda i, j, t: (i, j, 0, 0)),
                  pl.BlockSpec((1, 1, nk, dv), lambda i, j, t: (i, j, 0, 0))],
        out_specs=pl.BlockSpec((1, tq, dv), lambda i, j, t: (i, t, j)),
        compiler_params=_params(("parallel", "parallel", "parallel")),
        name="attention",
    )(q, k, v)


def _ml_prep_kernel(qk_ref, prev_ref, next_ref, vin_ref, g_ref, gt_ref, cw_ref, bc_ref, br_ref,
                    q_ref, kt_ref, v_ref, a_ref, brow_ref):
    tn = qk_ref.shape[1]
    j = pl.program_id(1)
    nj = pl.num_programs(1)
    u = qk_ref[0]
    row = lax.broadcasted_iota(I32, (tn, 1), 0)
    before = jnp.where(j == 0, 0.0, prev_ref[0, SUBLANES - 1:SUBLANES, :])
    after = jnp.where(j == nj - 1, 0.0, next_ref[0, 0:1, :])
    up = jnp.where(row == 0, before, pltpu.roll(u, 1, 0))
    un = jnp.where(row == tn - 1, after, pltpu.roll(u, tn - 1, 0))
    cw = cw_ref[...]
    y = up * cw[0:1, :] + u * cw[1:2, :] + un * cw[2:3, :]
    y = y * jax.nn.sigmoid(y)
    hq = ML_HEADS * ML_QK
    for h in range(ML_HEADS):
        q_ref[0, h] = (y[:, h * ML_QK:(h + 1) * ML_QK] * (ML_QK ** -0.5)).astype(BF16)
    kt = y[:, hq:2 * hq].T
    L = ML_CHUNK
    for c in range(tn // L):
        kt_ref[0, c] = kt[:, c * L:(c + 1) * L].astype(BF16)
    v_ref[0] = vin_ref[0].astype(BF16)
    g = g_ref[0][:, 0:16] + bc_ref[...]
    lf_c = jax.nn.log_sigmoid(g[:, 8:16])
    gt = gt_ref[0] + br_ref[...]
    ig_r = gt[0:8, :]
    lf_r = jax.nn.log_sigmoid(gt[8:16, :])
    ti = lax.broadcasted_iota(I32, (L, L), 0)
    si = lax.broadcasted_iota(I32, (L, L), 1)
    lower = (si <= ti).astype(F32)
    upper = (si >= ti).astype(F32)
    lane_fwd = lax.broadcasted_iota(I32, (L, 8), 1) < ML_HEADS
    row_fwd = lax.broadcasted_iota(I32, (8, L), 0) < ML_HEADS
    for c in range(tn // L):
        lo = c * L
        lfc = lf_c[lo:lo + L, :]
        a_ref[0, lo:lo + L, :] = jnp.where(
            lane_fwd,
            jnp.dot(lower, lfc, precision=HIGHEST, preferred_element_type=F32),
            jnp.dot(upper, lfc, precision=HIGHEST, preferred_element_type=F32))
        lfr = lf_r[:, lo:lo + L]
        cf_r = jnp.where(
            row_fwd,
            jnp.dot(lfr, upper, precision=HIGHEST, preferred_element_type=F32),
            jnp.dot(lfr, lower, precision=HIGHEST, preferred_element_type=F32))
        brow_ref[0, c] = ig_r[:, lo:lo + L] - cf_r


def _ml_prep(cols, gates_t, cw, bias_col, bias_row, tn):
    b, n, _ = cols.shape
    r8 = tn // SUBLANES
    last8 = n // SUBLANES - 1
    L = ML_CHUNK
    hq = ML_HEADS * ML_QK
    return pl.pallas_call(
        _ml_prep_kernel,
        out_shape=(jax.ShapeDtypeStruct((b, ML_HEADS, n, ML_QK), BF16),
                   jax.ShapeDtypeStruct((b, n // L, hq, L), BF16),
                   jax.ShapeDtypeStruct((b, n, ML_HEADS * ML_V), BF16),
                   jax.ShapeDtypeStruct((b, n, 8), F32),
                   jax.ShapeDtypeStruct((b, n // L, 8, L), F32)),
        grid=(b, n // tn),
        in_specs=[pl.BlockSpec((1, tn, 512), lambda i, j: (i, j, C_QK0 // 512)),
                  pl.BlockSpec((1, SUBLANES, 512), lambda i, j: (i, jnp.maximum(j * r8 - 1, 0), C_QK0 // 512)),
                  pl.BlockSpec((1, SUBLANES, 512), lambda i, j: (i, jnp.minimum((j + 1) * r8, last8), C_QK0 // 512)),
                  pl.BlockSpec((1, tn, 512), lambda i, j: (i, j, C_V0 // 512)),
                  pl.BlockSpec((1, tn, 128), lambda i, j: (i, j, C_G0 // 128)),
                  pl.BlockSpec((1, 16, tn), lambda i, j: (i, 0, j)),
                  pl.BlockSpec((3, 512), lambda i, j: (0, 0)),
                  pl.BlockSpec((1, 16), lambda i, j: (0, 0)),
                  pl.BlockSpec((16, 1), lambda i, j: (0, 0))],
        out_specs=(pl.BlockSpec((1, ML_HEADS, tn, ML_QK), lambda i, j: (i, 0, j, 0)),
                   pl.BlockSpec((1, tn // L, hq, L), lambda i, j: (i, j, 0, 0)),
                   pl.BlockSpec((1, tn, ML_HEADS * ML_V), lambda i, j: (i, j, 0)),
                   pl.BlockSpec((1, tn, 8), lambda i, j: (i, j, 0)),
                   pl.BlockSpec((1, tn // L, 8, L), lambda i, j: (i, j, 0, 0))),
        compiler_params=_params(("parallel", "parallel")),
        name="ml_prep",
    )(cols, cols, cols, cols, cols, gates_t, cw, bias_col, bias_row)


def _ml_chunk(qb, kt, vaug, a_c, b_r, ct, m_prev, fwd, need_out):
    L = qb.shape[0]
    ti = lax.broadcasted_iota(I32, (L, L), 0)
    si = lax.broadcasted_iota(I32, (L, L), 1)
    mask = (si <= ti) if fwd else (si >= ti)
    dmat = jnp.where(mask, a_c + b_r, -jnp.inf)
    mloc = jnp.max(dmat, axis=1, keepdims=True)
    inter = a_c + m_prev
    m_t = jnp.maximum(inter, mloc)
    last = L - 1 if fwd else 0
    m_new = m_t[last:last + 1, :]
    a_last = a_c[last:last + 1, :]
    w_r = jnp.exp(a_last + b_r - m_new)
    decay = jnp.exp(a_last + m_prev - m_new)
    kw = (kt.astype(F32) * w_r).astype(BF16)
    ct_new = decay * ct + jnp.dot(kw, vaug, preferred_element_type=F32)
    if not need_out:
        return ct_new, m_new, None
    s = jnp.dot(qb, kt, preferred_element_type=F32)
    wmat = jnp.exp(dmat - m_t) * s
    sc = jnp.exp(inter - m_t)
    nd = (jnp.dot(wmat.astype(BF16), vaug, preferred_element_type=F32)
          + sc * jnp.dot(qb, ct.astype(BF16), preferred_element_type=F32))
    num = nd[:, 0:ML_V]
    den = nd[:, ML_V:ML_V + 1]
    return ct_new, m_new, num / jnp.maximum(jnp.abs(den), jnp.exp(-m_t))


def _ml_scan_kernel(ql_ref, ktl_ref, vl_ref, al_ref, brl_ref, qc_ref, ktc_ref, vc_ref, ac_ref, brc_ref,
                    out_ref, st_ref, m_ref):
    L = ML_CHUNK
    ncl = ql_ref.shape[2] // L
    ncc = qc_ref.shape[2] // L
    ones_col = (lax.broadcasted_iota(I32, (L, ML_V), 1) == 0).astype(BF16)
    st_ref[...] = jnp.zeros(st_ref.shape, F32)
    m_ref[...] = jnp.zeros(m_ref.shape, F32)

    def step(refs, c, d, fwd, need_out):
        q_ref, kt_ref, v_ref, a_ref, br_ref = refs
        sl = pl.ds(pl.multiple_of(c * L, L), L)
        a_all = a_ref[0, sl, :]
        br_all = br_ref[0, c]
        v_all = v_ref[0, sl, :]
        kt_all = kt_ref[0, c]
        hs = []
        for h in range(ML_HEADS):
            j = d * ML_HEADS + h
            vaug = jnp.concatenate([v_all[:, h * ML_V:(h + 1) * ML_V], ones_col], axis=1)
            ct_new, m_new, hh = _ml_chunk(
                q_ref[0, h, sl, :], kt_all[h * ML_QK:(h + 1) * ML_QK, :], vaug,
                a_all[:, j:j + 1], br_all[j:j + 1, :], st_ref[j], m_ref[j][0:1, 0:1], fwd, need_out)
            st_ref[j] = ct_new
            m_ref[j] = jnp.broadcast_to(m_new, (SUBLANES, LANES))
            hs.append(hh)
        return hs, sl

    ctx_refs = (qc_ref, ktc_ref, vc_ref, ac_ref, brc_ref)
    lat_refs = (ql_ref, ktl_ref, vl_ref, al_ref, brl_ref)

    def run(d, fwd):
        def ctx_body(i, carry):
            step(ctx_refs, i if fwd else ncc - 1 - i, d, fwd, False)
            return carry

        def lat_body(i, carry):
            hs, sl = step(lat_refs, i if fwd else ncl - 1 - i, d, fwd, True)
            hcat = jnp.concatenate(hs, axis=1)
            if fwd:
                out_ref[0, sl, :] = hcat
            else:
                out_ref[0, sl, :] = out_ref[0, sl, :] + hcat
            return carry

        lax.fori_loop(0, ncc, ctx_body, 0)
        lax.fori_loop(0, ncl, lat_body, 0)

    run(0, True)
    run(1, False)


def _ml_scan(ql, ktl, vl, al, brl, qc, ktc, vc, ac, brc):
    b, _, n, _ = ql.shape
    nctx = qc.shape[2]
    L = ML_CHUNK
    hv = ML_HEADS * ML_V
    hq = ML_HEADS * ML_QK
    qspec = lambda nn: pl.BlockSpec((1, ML_HEADS, nn, ML_QK), lambda i: (i, 0, 0, 0))
    ktspec = lambda nn: pl.BlockSpec((1, nn // L, hq, L), lambda i: (i, 0, 0, 0))
    vspec = lambda nn: pl.BlockSpec((1, nn, hv), lambda i: (i, 0, 0))
    aspec = lambda nn: pl.BlockSpec((1, nn, 8), lambda i: (i, 0, 0))
    rspec = lambda nn: pl.BlockSpec((1, nn // L, 8, L), lambda i: (i, 0, 0, 0))
    return pl.pallas_call(
        _ml_scan_kernel,
        out_shape=jax.ShapeDtypeStruct((b, n, hv), F32),
        grid=(b,),
        in_specs=[qspec(n), ktspec(n), vspec(n), aspec(n), rspec(n),
                  qspec(nctx), ktspec(nctx), vspec(nctx), aspec(nctx), rspec(nctx)],
        out_specs=pl.BlockSpec((1, n, hv), lambda i: (i, 0, 0)),
        scratch_shapes=[pltpu.VMEM((2 * ML_HEADS, ML_QK, 2 * ML_V), F32),
                        pltpu.VMEM((2 * ML_HEADS, SUBLANES, LANES), F32)],
        compiler_params=_params(("parallel",)),
        name="ml_scan",
    )(ql, ktl, vl, al, brl, qc, ktc, vc, ac, brc)


def _outproj_kernel(x_ref, mla_ref, hs_ref, o_ref, gm_ref, wa_ref, wb_ref, gt_ref, g2_ref, sc_ref, sh_ref,
                    wpq_ref, sk_ref, x1_ref, h2_ref, s_ref):
    hs = hs_ref[0]
    gm = gm_ref[...]
    hn = jnp.concatenate([_rms(hs[:, h * ML_V:(h + 1) * ML_V]) * gm[:, h * ML_V:(h + 1) * ML_V]
                          for h in range(ML_HEADS)], axis=1)
    ml = (jax.nn.sigmoid(o_ref[0]) * hn).astype(BF16)
    mix = (jnp.dot(mla_ref[0], wa_ref[...], preferred_element_type=F32)
           + jnp.dot(ml, wb_ref[...], preferred_element_type=F32))
    x1 = x_ref[0] + gt_ref[0] * mix
    x1_ref[0] = x1
    h2 = _rms(x1) * g2_ref[...] * (1.0 + sc_ref[0]) + sh_ref[0]
    h2_ref[0] = h2
    qp = jnp.dot(h2.astype(BF16), wpq_ref[...], preferred_element_type=F32).astype(BF16)
    for hp in range(2 * PEER_HEADS):
        s_ref[0, hp] = lax.dot_general(sk_ref[hp], qp[:, hp * N_KEYS:(hp + 1) * N_KEYS],
                                       (((1,), (1,)), ((), ())), preferred_element_type=F32)


def _outproj(x, mla, hsum, cols, gm, wa, wb, gt1, g2, sc2, sh2, wpq, sk, tm):
    b, n, d = x.shape
    hw = mla.shape[2]
    nq = wpq.shape[1]
    mod = pl.BlockSpec((1, 1, d), lambda i, j: (i, 0, 0))
    return pl.pallas_call(
        _outproj_kernel,
        out_shape=(jax.ShapeDtypeStruct((b, n, d), F32), jax.ShapeDtypeStruct((b, n, d), F32),
                   jax.ShapeDtypeStruct((b, 2 * PEER_HEADS, N_KEYS, n), F32)),
        grid=(b, n // tm),
        in_specs=[pl.BlockSpec((1, tm, d), lambda i, j: (i, j, 0)),
                  pl.BlockSpec((1, tm, hw), lambda i, j: (i, j, 0)),
                  pl.BlockSpec((1, tm, hw), lambda i, j: (i, j, 0)),
                  pl.BlockSpec((1, tm, 512), lambda i, j: (i, j, C_O0 // 512)),
                  pl.BlockSpec((1, hw), lambda i, j: (0, 0)),
                  pl.BlockSpec((hw, d), lambda i, j: (0, 0)),
                  pl.BlockSpec((hw, d), lambda i, j: (0, 0)),
                  mod, pl.BlockSpec((1, d), lambda i, j: (0, 0)), mod, mod,
                  pl.BlockSpec((d, nq), lambda i, j: (0, 0)),
                  pl.BlockSpec((2 * PEER_HEADS, N_KEYS, N_KEYS), lambda i, j: (0, 0, 0))],
        out_specs=(pl.BlockSpec((1, tm, d), lambda i, j: (i, j, 0)),
                   pl.BlockSpec((1, tm, d), lambda i, j: (i, j, 0)),
                   pl.BlockSpec((1, 2 * PEER_HEADS, N_KEYS, tm), lambda i, j: (i, 0, 0, j))),
        compiler_params=_params(("parallel", "parallel")),
        name="outproj",
    )(x, mla, hsum, cols, gm, wa, wb, gt1, g2, sc2, sh2, wpq, sk)


def _topk_rows(s, k):
    rows, t = s.shape
    rid = lax.broadcasted_iota(I32, (rows, t), 0)
    kid = lax.broadcasted_iota(I32, (k, t), 0)

    def body(r, carry):
        s, vals, idxs = carry
        m = jnp.max(s, axis=0, keepdims=True)
        i = jnp.min(jnp.where(s == m, rid, rows), axis=0, keepdims=True)
        vals = jnp.where(kid == r, m, vals)
        idxs = jnp.where(kid == r, i, idxs)
        s = jnp.where(rid == i, -jnp.inf, s)
        return s, vals, idxs

    _, vals, idxs = lax.fori_loop(0, k, body, (s, jnp.zeros((k, t), F32), jnp.zeros((k, t), I32)))
    return vals, idxs


def _route_kernel(s_ref, idx_ref, gw_ref, et_ref, gt_ref):
    K = PEER_TOPK

    def head(h, carry):
        sv0, si0 = _topk_rows(s_ref[0, 2 * h], K)
        sv1, si1 = _topk_rows(s_ref[0, 2 * h + 1], K)
        cand = jnp.concatenate([sv0[i:i + 1, :] + sv1 for i in range(K)], axis=0)
        best, pos = _topk_rows(cand, K)
        isel = pos >> 4
        jsel = pos & (K - 1)
        e0 = jnp.zeros_like(pos)
        e1 = jnp.zeros_like(pos)
        for i in range(K):
            e0 = jnp.where(isel == i, si0[i:i + 1, :], e0)
            e1 = jnp.where(jsel == i, si1[i:i + 1, :], e1)
        ex = jnp.exp(best - best[0:1, :])
        g = ex / jnp.sum(ex, axis=0, keepdims=True)
        sl = pl.ds(pl.multiple_of(h * K, K), K)
        et_ref[sl, :] = e0 * N_KEYS + e1
        gt_ref[sl, :] = g
        return carry

    lax.fori_loop(0, PEER_HEADS, head, 0)
    idx_ref[0] = et_ref[...].T
    gw_ref[0] = gt_ref[...].T


def _route(scores, tt):
    b, hp, nk, n = scores.shape
    nsel = PEER_HEADS * PEER_TOPK
    return pl.pallas_call(
        _route_kernel,
        out_shape=(jax.ShapeDtypeStruct((b, n, nsel), I32), jax.ShapeDtypeStruct((b, n, nsel), F32)),
        grid=(b, n // tt),
        in_specs=[pl.BlockSpec((1, hp, nk, tt), lambda i, j: (i, 0, 0, j))],
        out_specs=(pl.BlockSpec((1, tt, nsel), lambda i, j: (i, j, 0)),
                   pl.BlockSpec((1, tt, nsel), lambda i, j: (i, j, 0))),
        scratch_shapes=[pltpu.VMEM((nsel, tt), I32), pltpu.VMEM((nsel, tt), F32)],
        compiler_params=_params(("parallel", "parallel")),
        name="route",
    )(scores)


def _expert_row(tab_ref, e):
    tile = tab_ref[e >> 1].astype(F32)
    return jnp.where((e & 1) == 1, tile[SUBLANES:], tile[:SUBLANES])


def _sublane_sums(ps):
    sub = lax.broadcasted_iota(I32, (SUBLANES, LANES), 0)
    out = ps
    for dist in (4, 2, 1):
        keep = (sub & (2 * dist - 1)) < dist
        half = len(out) // 2
        out = [jnp.where(keep, out[i], pltpu.roll(out[i + half], dist, 0))
               + jnp.where(keep, pltpu.roll(out[i], SUBLANES - dist, 0), out[i + half]) for i in range(half)]
    return out[0]


def _peer_u_kernel(idx_ref, h_ref, tab_ref, g_ref, c_ref, red_ref, pre_ref):
    tb = h_ref.shape[0]
    nsel = g_ref.shape[1]

    def tok(t, carry):
        x = h_ref[t]

        def grp(gi, carry2):
            base = t * nsel + gi * SUBLANES
            ps = [_expert_row(tab_ref, idx_ref[base + kk]) * x for kk in range(SUBLANES)]
            red_ref[t, pl.ds(pl.multiple_of(gi * SUBLANES, SUBLANES), SUBLANES), :] = _sublane_sums(ps)
            return carry2

        lax.fori_loop(0, nsel // SUBLANES, grp, 0)
        return carry

    lax.fori_loop(0, tb, tok, 0)
    ones = jnp.ones((SUBLANES, LANES), F32)

    def fin(t, carry):
        s = lax.dot_general(ones, red_ref[t], (((1,), (1,)), ((), ())),
                            precision=HIGHEST, preferred_element_type=F32)
        pre_ref[pl.ds(t, 1), :] = s[0:1, :]
        return carry

    lax.fori_loop(0, tb, fin, 0)
    pre = pre_ref[...]
    c_ref[...] = g_ref[...] * (0.5 * pre * (1.0 + lax.erf(pre * (2.0 ** -0.5))))


def _peer_u(idx_flat, h2, tab, gw, tb):
    t = h2.shape[0]
    nsel = gw.shape[1]
    return pl.pallas_call(
        _peer_u_kernel,
        out_shape=jax.ShapeDtypeStruct((t, nsel), F32),
        grid=(t // tb,),
        in_specs=[pl.BlockSpec((tb * nsel,), lambda i: (i,), memory_space=pltpu.SMEM),
                  pl.BlockSpec((tb, SUBLANES, LANES), lambda i: (i, 0, 0)),
                  pl.BlockSpec(tab.shape, lambda i: (0, 0, 0), pipeline_mode=pl.Buffered(1)),
                  pl.BlockSpec((tb, nsel), lambda i: (i, 0))],
        out_specs=pl.BlockSpec((tb, nsel), lambda i: (i, 0)),
        scratch_shapes=[pltpu.VMEM((tb, nsel, LANES), F32), pltpu.VMEM((tb, nsel), F32)],
        compiler_params=_params(("arbitrary",)),
        name="peer_u",
    )(idx_flat, h2, tab, gw)


def _peer_v_kernel(idx_ref, c_ref, tab_ref, x1_ref, gt_ref, o_ref):
    tb = x1_ref.shape[0]
    nsel = idx_ref.shape[0] // tb
    nacc = 4

    def tok(t, carry):
        def grp(gi, accs):
            base = t * nsel + gi * nacc
            return tuple(accs[a] + c_ref[base + a] * _expert_row(tab_ref, idx_ref[base + a]) for a in range(nacc))

        accs = lax.fori_loop(0, nsel // nacc, grp, tuple(jnp.zeros((SUBLANES, LANES), F32) for _ in range(nacc)))
        o_ref[t] = x1_ref[t] + gt_ref[0] * ((accs[0] + accs[1]) + (accs[2] + accs[3]))
        return carry

    lax.fori_loop(0, tb, tok, 0)


def _peer_v(idx_flat, c_flat, tab, x1, gt2, tb, tok_per_batch):
    t = x1.shape[0]
    nsel = idx_flat.shape[0] // t
    bpb = tok_per_batch // tb
    return pl.pallas_call(
        _peer_v_kernel,
        out_shape=jax.ShapeDtypeStruct(x1.shape, F32),
        grid=(t // tb,),
        in_specs=[pl.BlockSpec((tb * nsel,), lambda i: (i,), memory_space=pltpu.SMEM),
                  pl.BlockSpec((tb * nsel,), lambda i: (i,), memory_space=pltpu.SMEM),
                  pl.BlockSpec(tab.shape, lambda i: (0, 0, 0), pipeline_mode=pl.Buffered(1)),
                  pl.BlockSpec((tb, SUBLANES, LANES), lambda i: (i, 0, 0)),
                  pl.BlockSpec((1, SUBLANES, LANES), lambda i: (i // bpb, 0, 0))],
        out_specs=pl.BlockSpec((tb, SUBLANES, LANES), lambda i: (i, 0, 0)),
        compiler_params=_params(("arbitrary",)),
        name="peer_v",
    )(idx_flat, c_flat, tab, x1, gt2)


def _rope_tables(n):
    axis = MLA_ROPE // 2
    t = jnp.arange(n, dtype=F32)
    row = jnp.floor(t / GRID_W)
    col = t - row * GRID_W
    inv = ROPE_BASE ** (-jnp.arange(axis // 2, dtype=F32) * (2.0 / axis))
    ar = row[:, None] * inv
    ac = col[:, None] * inv
    cos = jnp.concatenate([jnp.cos(ar), jnp.cos(ar), jnp.cos(ac), jnp.cos(ac)], axis=1)
    sin = jnp.concatenate([-jnp.sin(ar), jnp.sin(ar), -jnp.sin(ac), jnp.sin(ac)], axis=1)
    return cos, sin


def _pack_table(tab):
    e, d = tab.shape
    return tab.astype(BF16).reshape(e // 2, 2 * d // LANES, LANES)


def _block(n, want):
    return want if n % want == 0 else n


def kernel(x, c, ctx, c_ctx, w_ada, b_ada, g_norm1, w_in, g_cq, w_uq, g_ckv, w_ukv, g_qn, g_kn, conv_qk, b_igate, b_fgate, g_mlstm, w_out, g_norm2, w_pq, sub_keys, expert_u, expert_v):
    B, N, D = x.shape
    NC = ctx.shape[1]
    assert w_ada.shape[0] == 1, "one layer"
    assert N % 256 == 0 and NC % ML_CHUNK == 0 and D == SUBLANES * LANES
    q_rank = g_cq.shape[1]
    kv_rank = g_ckv.shape[1]
    mla_cols = q_rank + kv_rank + MLA_ROPE
    assert (q_rank, kv_rank) == (C_KV0 - C_Q0, C_KR0 - C_KV0)
    swap = jnp.arange(MLA_ROPE) ^ (MLA_ROPE // 4)

    cc = jnp.concatenate([c, c_ctx[None, :], jnp.zeros((16 - B - 1, D), F32)], axis=0)
    mod = _ada(cc, w_ada[0].astype(BF16), b_ada)
    sh1, sc1, gt1, sh2, sc2, gt2 = [mod[:, i * D:(i + 1) * D] for i in range(6)]
    lat = lambda m: m[:B].reshape(B, 1, D)
    ctxm = lambda m: jnp.broadcast_to(m[B:B + 1].reshape(1, 1, D), (B, 1, D))

    wi = w_in[0]
    n_qk = 2 * ML_HEADS * ML_QK
    n_v = ML_HEADS * ML_V
    n_g = 4 * ML_HEADS
    m0 = mla_cols
    w_cols = jnp.concatenate([
        wi[:, 0:mla_cols],
        wi[:, q_rank + kv_rank + swap],
        wi[:, m0:m0 + n_qk + 2 * n_v + n_g],
        jnp.zeros((D, C_END - C_G0 - n_g), F32)], axis=1).astype(BF16)
    assert w_cols.shape[1] == C_END
    w_gates_t = wi[:, m0 + n_qk + 2 * n_v:m0 + n_qk + 2 * n_v + n_g].T.astype(BF16)
    cols_l, gates_l = _inproj(x, g_norm1, lat(sc1), lat(sh1), w_cols, w_gates_t, _block(N, 512))
    cols_c, gates_c = _inproj(ctx, g_norm1, ctxm(sc1), ctxm(sh1), w_cols, w_gates_t, _block(NC, 512))

    wq = w_uq[0].reshape(q_rank, MLA_HEADS, MLA_QK)
    wuq = jnp.concatenate([wq, wq[:, :, MLA_NOPE + swap]], axis=2).reshape(q_rank, MLA_HEADS * 256).astype(BF16)
    wukv = w_ukv[0].astype(BF16)
    gqs, gks = g_qn[:, MLA_NOPE + swap], g_kn[:, MLA_NOPE + swap]
    cos_l, sin_l = _rope_tables(N)
    cos_c, sin_c = jnp.ones((NC, MLA_ROPE), F32), jnp.zeros((NC, MLA_ROPE), F32)
    q_l, k_l, v_l = _mla_prep(cols_l, g_cq, wuq, g_ckv, wukv, g_qn, gqs, g_kn, gks, cos_l, sin_l, _block(N, 512))
    _, k_c, v_c = _mla_prep(cols_c, g_cq, wuq, g_ckv, wukv, g_qn, gqs, g_kn, gks, cos_c, sin_c, _block(NC, 512))
    k_all = jnp.concatenate([k_c, k_l], axis=2)
    v_all = jnp.concatenate([v_c, v_l], axis=2)
    mla = _attention(q_l, k_all, v_all, 256)

    bias16 = jnp.concatenate([b_igate[0].reshape(-1), b_fgate[0].reshape(-1)])
    cw = conv_qk[0]
    pl_ = _ml_prep(cols_l, gates_l, cw, bias16[None, :], bias16[:, None], _block(N, 512))
    pc_ = _ml_prep(cols_c, gates_c, cw, bias16[None, :], bias16[:, None], _block(NC, 512))
    hsum = _ml_scan(*pl_, *pc_)

    wo = w_out[0].astype(BF16)
    hw = MLA_HEADS * MLA_V
    sk = sub_keys[0].reshape(2 * PEER_HEADS, N_KEYS, -1).astype(BF16)
    x1, h2, scores = _outproj(x, mla, hsum, cols_l, g_mlstm, wo[:hw], wo[hw:], lat(gt1), g_norm2, lat(sc2), lat(sh2),
                              w_pq[0].astype(BF16), sk, 256)
    idx, gw = _route(scores, 256)

    T = B * N
    nsel = PEER_HEADS * PEER_TOPK
    idx_flat = idx.reshape(T * nsel)
    tb = 32
    coef = _peer_u(idx_flat, h2.reshape(T, SUBLANES, LANES), _pack_table(expert_u[0]), gw.reshape(T, nsel), tb)
    out = _peer_v(idx_flat, coef.reshape(T * nsel), _pack_table(expert_v[0]), x1.reshape(T, SUBLANES, LANES),
                  lat(gt2).reshape(B, SUBLANES, LANES), tb, N)
    return out.reshape(B, N, D)
```

```python
import jax
import jax.numpy as jnp
from jax import lax
from jax.experimental import pallas as pl
from jax.experimental.pallas import tpu as pltpu

F32 = jnp.float32
BF16 = jnp.bfloat16
I32 = jnp.int32
EPS = 1e-6

GRID_W = 64
MLA_HEADS = 4
MLA_NOPE = 128
MLA_ROPE = 64
MLA_V = 128
MLA_QK = MLA_NOPE + MLA_ROPE
ROPE_BASE = 10000.0
ML_HEADS = 4
ML_QK = 64
ML_V = 128
ML_CHUNK = 64
PEER_HEADS = 8
N_KEYS = 128
PEER_TOPK = 16

LANES = 128
SUBLANES = 8
VMEM_LIMIT_BYTES = 56 * 1024 * 1024

C_Q0, C_KV0, C_KR0, C_KRS0, C_QK0, C_V0, C_O0, C_G0, C_END = 0, 256, 384, 448, 512, 1024, 1536, 2048, 2176
HIGHEST = lax.Precision.HIGHEST


def _params(sem):
    return pltpu.CompilerParams(dimension_semantics=sem, vmem_limit_bytes=VMEM_LIMIT_BYTES)


def _rms(x):
    return x * lax.rsqrt(jnp.mean(x * x, axis=-1, keepdims=True) + EPS)


def _ada_kernel(c_ref, w_ref, b_ref, o_ref):
    c = c_ref[...]
    s = c * jax.nn.sigmoid(c)
    o_ref[...] = jnp.dot(s.astype(BF16), w_ref[...], preferred_element_type=F32) + b_ref[...]


def _ada(cc, w, b):
    rows, d = cc.shape
    n = w.shape[1]
    bn = n // 4
    return pl.pallas_call(
        _ada_kernel,
        out_shape=jax.ShapeDtypeStruct((rows, n), F32),
        grid=(4,),
        in_specs=[pl.BlockSpec((rows, d), lambda j: (0, 0)),
                  pl.BlockSpec((d, bn), lambda j: (0, j)),
                  pl.BlockSpec((1, bn), lambda j: (0, j))],
        out_specs=pl.BlockSpec((rows, bn), lambda j: (0, j)),
        compiler_params=_params(("arbitrary",)),
        name="ada",
    )(cc, w, b)


def _inproj_kernel(x_ref, g_ref, sc_ref, sh_ref, w_ref, wg_ref, o_ref, gt_ref):
    x = x_ref[0]
    h = _rms(x) * g_ref[...] * (1.0 + sc_ref[0]) + sh_ref[0]
    hb = h.astype(BF16)
    o_ref[0] = jnp.dot(hb, w_ref[...], preferred_element_type=F32)
    gt_ref[0] = lax.dot_general(wg_ref[...], hb, (((1,), (1,)), ((), ())), preferred_element_type=F32)


def _inproj(x, g, sc, sh, w, wg, tm):
    b, n, d = x.shape
    nc = w.shape[1]
    return pl.pallas_call(
        _inproj_kernel,
        out_shape=(jax.ShapeDtypeStruct((b, n, nc), F32), jax.ShapeDtypeStruct((b, 16, n), F32)),
        grid=(b, n // tm),
        in_specs=[pl.BlockSpec((1, tm, d), lambda i, j: (i, j, 0)),
                  pl.BlockSpec((1, d), lambda i, j: (0, 0)),
                  pl.BlockSpec((1, 1, d), lambda i, j: (i, 0, 0)),
                  pl.BlockSpec((1, 1, d), lambda i, j: (i, 0, 0)),
                  pl.BlockSpec((d, nc), lambda i, j: (0, 0)),
                  pl.BlockSpec((16, d), lambda i, j: (0, 0))],
        out_specs=(pl.BlockSpec((1, tm, nc), lambda i, j: (i, j, 0)),
                   pl.BlockSpec((1, 16, tm), lambda i, j: (i, 0, j))),
        compiler_params=_params(("parallel", "parallel")),
        name="inproj",
    )(x, g, sc, sh, w, wg)


def _mla_prep_kernel(c_ref, gcq_ref, wuq_ref, gckv_ref, wukv_ref, gq_ref, gqs_ref, gk_ref, gks_ref,
                     cos_ref, sin_ref, q_ref, k_ref, v_ref):
    c = c_ref[0]
    cq = _rms(c[:, C_Q0:C_KV0]) * gcq_ref[...]
    ckv = _rms(c[:, C_KV0:C_KR0]) * gckv_ref[...]
    kr = c[:, C_KR0:C_KRS0]
    krs = c[:, C_KRS0:C_QK0]
    q_raw = jnp.dot(cq.astype(BF16), wuq_ref[...], preferred_element_type=F32)
    kv_raw = jnp.dot(ckv.astype(BF16), wukv_ref[...], preferred_element_type=F32)
    cos = cos_ref[...]
    sin = sin_ref[...]
    gq = gq_ref[...]
    gk = gk_ref[...]
    kr_ss = jnp.sum(kr * kr, axis=-1, keepdims=True)
    for h in range(MLA_HEADS):
        o = h * 256
        qn = q_raw[:, o:o + 128]
        qr = q_raw[:, o + 128:o + 192]
        qs = q_raw[:, o + 192:o + 256]
        ss = jnp.sum(qn * qn, axis=-1, keepdims=True) + jnp.sum(qr * qr, axis=-1, keepdims=True)
        r = lax.rsqrt(ss * (1.0 / MLA_QK) + EPS) * (MLA_QK ** -0.5)
        q_ref[0, h, :, 0:128] = (qn * r * gq[:, 0:128]).astype(BF16)
        q_ref[0, h, :, 128:192] = ((qr * r * gq[:, 128:192]) * cos + (qs * r * gqs_ref[...]) * sin).astype(BF16)
        kn = kv_raw[:, o:o + 128]
        ss = jnp.sum(kn * kn, axis=-1, keepdims=True) + kr_ss
        r = lax.rsqrt(ss * (1.0 / MLA_QK) + EPS)
        k_ref[0, h, :, 0:128] = (kn * r * gk[:, 0:128]).astype(BF16)
        k_ref[0, h, :, 128:192] = ((kr * r * gk[:, 128:192]) * cos + (krs * r * gks_ref[...]) * sin).astype(BF16)
        v_ref[0, h] = kv_raw[:, o + 128:o + 256].astype(BF16)


def _mla_prep(cols, gcq, wuq, gckv, wukv, gq, gqs, gk, gks, cos, sin, tm):
    b, n, _ = cols.shape
    full = lambda a: pl.BlockSpec(a.shape, lambda i, j: (0,) * a.ndim)
    return pl.pallas_call(
        _mla_prep_kernel,
        out_shape=(jax.ShapeDtypeStruct((b, MLA_HEADS, n, MLA_QK), BF16),
                   jax.ShapeDtypeStruct((b, MLA_HEADS, n, MLA_QK), BF16),
                   jax.ShapeDtypeStruct((b, MLA_HEADS, n, MLA_V), BF16)),
        grid=(b, n // tm),
        in_specs=[pl.BlockSpec((1, tm, 512), lambda i, j: (i, j, 0)),
                  full(gcq), full(wuq), full(gckv), full(wukv), full(gq), full(gqs), full(gk), full(gks),
                  pl.BlockSpec((tm, MLA_ROPE), lambda i, j: (j, 0)),
                  pl.BlockSpec((tm, MLA_ROPE), lambda i, j: (j, 0))],
        out_specs=(pl.BlockSpec((1, MLA_HEADS, tm, MLA_QK), lambda i, j: (i, 0, j, 0)),
                   pl.BlockSpec((1, MLA_HEADS, tm, MLA_QK), lambda i, j: (i, 0, j, 0)),
                   pl.BlockSpec((1, MLA_HEADS, tm, MLA_V), lambda i, j: (i, 0, j, 0))),
        compiler_params=_params(("parallel", "parallel")),
        name="mla_prep",
    )(cols, gcq, wuq, gckv, wukv, gq, gqs, gk, gks, cos, sin)


def _attn_kernel(q_ref, k_ref, v_ref, o_ref):
    q = q_ref[0, 0]
    s = lax.dot_general(q, k_ref[0, 0], (((1,), (1,)), ((), ())), preferred_element_type=F32)
    m = jnp.max(s, axis=-1, keepdims=True)
    p = jnp.exp(s - m)
    l = jnp.sum(p, axis=-1, keepdims=True)
    o = jnp.dot(p.astype(BF16), v_ref[0, 0], preferred_element_type=F32)
    o_ref[0] = (o / l).astype(BF16)


def _attention(q, k, v, tq):
    b, h, n, dk = q.shape
    nk = k.shape[2]
    dv = v.shape[3]
    return pl.pallas_call(
        _attn_kernel,
        out_shape=jax.ShapeDtypeStruct((b, n, h * dv), BF16),
        grid=(b, h, n // tq),
        in_specs=[pl.BlockSpec((1, 1, tq, dk), lambda i, j, t: (i, j, t, 0)),
                  pl.BlockSpec((1, 1, nk, dk), lambda i, j, t: (i, j, 0, 0)),
                  pl.BlockSpec((1, 1, nk, dv), lambda i, j, t: (i, j, 0, 0))],
        out_specs=pl.BlockSpec((1, tq, dv), lambda i, j, t: (i, t, j)),
        compiler_params=_params(("parallel", "parallel", "parallel")),
        name="attention",
    )(q, k, v)


def _ml_prep_kernel(qk_ref, prev_ref, next_ref, vin_ref, g_ref, gt_ref, cw_ref, bc_ref, br_ref,
                    q_ref, kt_ref, v_ref, a_ref, brow_ref):
    tn = qk_ref.shape[1]
    j = pl.program_id(1)
    nj = pl.num_programs(1)
    u = qk_ref[0]
    row = lax.broadcasted_iota(I32, (tn, 1), 0)
    before = jnp.where(j == 0, 0.0, prev_ref[0, SUBLANES - 1:SUBLANES, :])
    after = jnp.where(j == nj - 1, 0.0, next_ref[0, 0:1, :])
    up = jnp.where(row == 0, before, pltpu.roll(u, 1, 0))
    un = jnp.where(row == tn - 1, after, pltpu.roll(u, tn - 1, 0))
    cw = cw_ref[...]
    y = up * cw[0:1, :] + u * cw[1:2, :] + un * cw[2:3, :]
    y = y * jax.nn.sigmoid(y)
    hq = ML_HEADS * ML_QK
    for h in range(ML_HEADS):
        q_ref[0, h] = (y[:, h * ML_QK:(h + 1) * ML_QK] * (ML_QK ** -0.5)).astype(BF16)
    kt = y[:, hq:2 * hq].T
    L = ML_CHUNK
    for c in range(tn // L):
        kt_ref[0, c] = kt[:, c * L:(c + 1) * L].astype(BF16)
    v_ref[0] = vin_ref[0].astype(BF16)
    g = g_ref[0][:, 0:16] + bc_ref[...]
    lf_c = jax.nn.log_sigmoid(g[:, 8:16])
    gt = gt_ref[0] + br_ref[...]
    ig_r = gt[0:8, :]
    lf_r = jax.nn.log_sigmoid(gt[8:16, :])
    ti = lax.broadcasted_iota(I32, (L, L), 0)
    si = lax.broadcasted_iota(I32, (L, L), 1)
    lower = (si <= ti).astype(F32)
    upper = (si >= ti).astype(F32)
    lane_fwd = lax.broadcasted_iota(I32, (L, 8), 1) < ML_HEADS
    row_fwd = lax.broadcasted_iota(I32, (8, L), 0) < ML_HEADS
    for c in range(tn // L):
        lo = c * L
        lfc = lf_c[lo:lo + L, :]
        a_ref[0, lo:lo + L, :] = jnp.where(
            lane_fwd,
            jnp.dot(lower, lfc, precision=HIGHEST, preferred_element_type=F32),
            jnp.dot(upper, lfc, precision=HIGHEST, preferred_element_type=F32))
        lfr = lf_r[:, lo:lo + L]
        cf_r = jnp.where(
            row_fwd,
            jnp.dot(lfr, upper, precision=HIGHEST, preferred_element_type=F32),
            jnp.dot(lfr, lower, precision=HIGHEST, preferred_element_type=F32))
        brow_ref[0, c] = ig_r[:, lo:lo + L] - cf_r


def _ml_prep(cols, gates_t, cw, bias_col, bias_row, tn):
    b, n, _ = cols.shape
    r8 = tn // SUBLANES
    last8 = n // SUBLANES - 1
    L = ML_CHUNK
    hq = ML_HEADS * ML_QK
    return pl.pallas_call(
        _ml_prep_kernel,
        out_shape=(jax.ShapeDtypeStruct((b, ML_HEADS, n, ML_QK), BF16),
                   jax.ShapeDtypeStruct((b, n // L, hq, L), BF16),
                   jax.ShapeDtypeStruct((b, n, ML_HEADS * ML_V), BF16),
                   jax.ShapeDtypeStruct((b, n, 8), F32),
                   jax.ShapeDtypeStruct((b, n // L, 8, L), F32)),
        grid=(b, n // tn),
        in_specs=[pl.BlockSpec((1, tn, 512), lambda i, j: (i, j, C_QK0 // 512)),
                  pl.BlockSpec((1, SUBLANES, 512), lambda i, j: (i, jnp.maximum(j * r8 - 1, 0), C_QK0 // 512)),
                  pl.BlockSpec((1, SUBLANES, 512), lambda i, j: (i, jnp.minimum((j + 1) * r8, last8), C_QK0 // 512)),
                  pl.BlockSpec((1, tn, 512), lambda i, j: (i, j, C_V0 // 512)),
                  pl.BlockSpec((1, tn, 128), lambda i, j: (i, j, C_G0 // 128)),
                  pl.BlockSpec((1, 16, tn), lambda i, j: (i, 0, j)),
                  pl.BlockSpec((3, 512), lambda i, j: (0, 0)),
                  pl.BlockSpec((1, 16), lambda i, j: (0, 0)),
                  pl.BlockSpec((16, 1), lambda i, j: (0, 0))],
        out_specs=(pl.BlockSpec((1, ML_HEADS, tn, ML_QK), lambda i, j: (i, 0, j, 0)),
                   pl.BlockSpec((1, tn // L, hq, L), lambda i, j: (i, j, 0, 0)),
                   pl.BlockSpec((1, tn, ML_HEADS * ML_V), lambda i, j: (i, j, 0)),
                   pl.BlockSpec((1, tn, 8), lambda i, j: (i, j, 0)),
                   pl.BlockSpec((1, tn // L, 8, L), lambda i, j: (i, j, 0, 0))),
        compiler_params=_params(("parallel", "parallel")),
        name="ml_prep",
    )(cols, cols, cols, cols, cols, gates_t, cw, bias_col, bias_row)


def _ml_chunk(qb, kt, vaug, a_c, b_r, ct, m_prev, fwd, need_out):
    L = qb.shape[0]
    ti = lax.broadcasted_iota(I32, (L, L), 0)
    si = lax.broadcasted_iota(I32, (L, L), 1)
    mask = (si <= ti) if fwd else (si >= ti)
    dmat = jnp.where(mask, a_c + b_r, -jnp.inf)
    mloc = jnp.max(dmat, axis=1, keepdims=True)
    inter = a_c + m_prev
    m_t = jnp.maximum(inter, mloc)
    last = L - 1 if fwd else 0
    m_new = m_t[last:last + 1, :]
    a_last = a_c[last:last + 1, :]
    w_r = jnp.exp(a_last + b_r - m_new)
    decay = jnp.exp(a_last + m_prev - m_new)
    kw = (kt.astype(F32) * w_r).astype(BF16)
    ct_new = decay * ct + jnp.dot(kw, vaug, preferred_element_type=F32)
    if not need_out:
        return ct_new, m_new, None
    s = jnp.dot(qb, kt, preferred_element_type=F32)
    wmat = jnp.exp(dmat - m_t) * s
    sc = jnp.exp(inter - m_t)
    nd = (jnp.dot(wmat.astype(BF16), vaug, preferred_element_type=F32)
          + sc * jnp.dot(qb, ct.astype(BF16), preferred_element_type=F32))
    num = nd[:, 0:ML_V]
    den = nd[:, ML_V:ML_V + 1]
    return ct_new, m_new, num / jnp.maximum(jnp.abs(den), jnp.exp(-m_t))


def _ml_scan_kernel(ql_ref, ktl_ref, vl_ref, al_ref, brl_ref, qc_ref, ktc_ref, vc_ref, ac_ref, brc_ref,
                    out_ref, st_ref, m_ref):
    L = ML_CHUNK
    ncl = ql_ref.shape[2] // L
    ncc = qc_ref.shape[2] // L
    ones_col = (lax.broadcasted_iota(I32, (L, ML_V), 1) == 0).astype(BF16)
    st_ref[...] = jnp.zeros(st_ref.shape, F32)
    m_ref[...] = jnp.zeros(m_ref.shape, F32)

    def step(refs, c, d, fwd, need_out):
        q_ref, kt_ref, v_ref, a_ref, br_ref = refs
        sl = pl.ds(pl.multiple_of(c * L, L), L)
        a_all = a_ref[0, sl, :]
        br_all = br_ref[0, c]
        v_all = v_ref[0, sl, :]
        kt_all = kt_ref[0, c]
        hs = []
        for h in range(ML_HEADS):
            j = d * ML_HEADS + h
            vaug = jnp.concatenate([v_all[:, h * ML_V:(h + 1) * ML_V], ones_col], axis=1)
            ct_new, m_new, hh = _ml_chunk(
                q_ref[0, h, sl, :], kt_all[h * ML_QK:(h + 1) * ML_QK, :], vaug,
                a_all[:, j:j + 1], br_all[j:j + 1, :], st_ref[j], m_ref[j][0:1, 0:1], fwd, need_out)
            st_ref[j] = ct_new
            m_ref[j] = jnp.broadcast_to(m_new, (SUBLANES, LANES))
            hs.append(hh)
        return hs, sl

    ctx_refs = (qc_ref, ktc_ref, vc_ref, ac_ref, brc_ref)
    lat_refs = (ql_ref, ktl_ref, vl_ref, al_ref, brl_ref)

    def run(d, fwd):
        def ctx_body(i, carry):
            step(ctx_refs, i if fwd else ncc - 1 - i, d, fwd, False)
            return carry

        def lat_body(i, carry):
            hs, sl = step(lat_refs, i if fwd else ncl - 1 - i, d, fwd, True)
            hcat = jnp.concatenate(hs, axis=1)
            if fwd:
                out_ref[0, sl, :] = hcat
            else:
                out_ref[0, sl, :] = out_ref[0, sl, :] + hcat
            return carry

        lax.fori_loop(0, ncc, ctx_body, 0)
        lax.fori_loop(0, ncl, lat_body, 0)

    run(0, True)
    run(1, False)


def _ml_scan(ql, ktl, vl, al, brl, qc, ktc, vc, ac, brc):
    b, _, n, _ = ql.shape
    nctx = qc.shape[2]
    L = ML_CHUNK
    hv = ML_HEADS * ML_V
    hq = ML_HEADS * ML_QK
    qspec = lambda nn: pl.BlockSpec((1, ML_HEADS, nn, ML_QK), lambda i: (i, 0, 0, 0))
    ktspec = lambda nn: pl.BlockSpec((1, nn // L, hq, L), lambda i: (i, 0, 0, 0))
    vspec = lambda nn: pl.BlockSpec((1, nn, hv), lambda i: (i, 0, 0))
    aspec = lambda nn: pl.BlockSpec((1, nn, 8), lambda i: (i, 0, 0))
    rspec = lambda nn: pl.BlockSpec((1, nn // L, 8, L), lambda i: (i, 0, 0, 0))
    return pl.pallas_call(
        _ml_scan_kernel,
        out_shape=jax.ShapeDtypeStruct((b, n, hv), F32),
        grid=(b,),
        in_specs=[qspec(n), ktspec(n), vspec(n), aspec(n), rspec(n),
                  qspec(nctx), ktspec(nctx), vspec(nctx), aspec(nctx), rspec(nctx)],
        out_specs=pl.BlockSpec((1, n, hv), lambda i: (i, 0, 0)),
        scratch_shapes=[pltpu.VMEM((2 * ML_HEADS, ML_QK, 2 * ML_V), F32),
                        pltpu.VMEM((2 * ML_HEADS, SUBLANES, LANES), F32)],
        compiler_params=_params(("parallel",)),
        name="ml_scan",
    )(ql, ktl, vl, al, brl, qc, ktc, vc, ac, brc)


def _outproj_kernel(x_ref, mla_ref, hs_ref, o_ref, gm_ref, wa_ref, wb_ref, gt_ref, g2_ref, sc_ref, sh_ref,
                    wpq_ref, sk_ref, x1_ref, h2_ref, s_ref):
    hs = hs_ref[0]
    gm = gm_ref[...]
    hn = jnp.concatenate([_rms(hs[:, h * ML_V:(h + 1) * ML_V]) * gm[:, h * ML_V:(h + 1) * ML_V]
                          for h in range(ML_HEADS)], axis=1)
    ml = (jax.nn.sigmoid(o_ref[0]) * hn).astype(BF16)
    mix = (jnp.dot(mla_ref[0], wa_ref[...], preferred_element_type=F32)
           + jnp.dot(ml, wb_ref[...], preferred_element_type=F32))
    x1 = x_ref[0] + gt_ref[0] * mix
    x1_ref[0] = x1
    h2 = _rms(x1) * g2_ref[...] * (1.0 + sc_ref[0]) + sh_ref[0]
    h2_ref[0] = h2
    qp = jnp.dot(h2.astype(BF16), wpq_ref[...], preferred_element_type=F32).astype(BF16)
    for hp in range(2 * PEER_HEADS):
        s_ref[0, hp] = lax.dot_general(sk_ref[hp], qp[:, hp * N_KEYS:(hp + 1) * N_KEYS],
                                       (((1,), (1,)), ((), ())), preferred_element_type=F32)


def _outproj(x, mla, hsum, cols, gm, wa, wb, gt1, g2, sc2, sh2, wpq, sk, tm):
    b, n, d = x.shape
    hw = mla.shape[2]
    nq = wpq.shape[1]
    mod = pl.BlockSpec((1, 1, d), lambda i, j: (i, 0, 0))
    return pl.pallas_call(
        _outproj_kernel,
        out_shape=(jax.ShapeDtypeStruct((b, n, d), F32), jax.ShapeDtypeStruct((b, n, d), F32),
                   jax.ShapeDtypeStruct((b, 2 * PEER_HEADS, N_KEYS, n), F32)),
        grid=(b, n // tm),
        in_specs=[pl.BlockSpec((1, tm, d), lambda i, j: (i, j, 0)),
                  pl.BlockSpec((1, tm, hw), lambda i, j: (i, j, 0)),
                  pl.BlockSpec((1, tm, hw), lambda i, j: (i, j, 0)),
                  pl.BlockSpec((1, tm, 512), lambda i, j: (i, j, C_O0 // 512)),
                  pl.BlockSpec((1, hw), lambda i, j: (0, 0)),
                  pl.BlockSpec((hw, d), lambda i, j: (0, 0)),
                  pl.BlockSpec((hw, d), lambda i, j: (0, 0)),
                  mod, pl.BlockSpec((1, d), lambda i, j: (0, 0)), mod, mod,
                  pl.BlockSpec((d, nq), lambda i, j: (0, 0)),
                  pl.BlockSpec((2 * PEER_HEADS, N_KEYS, N_KEYS), lambda i, j: (0, 0, 0))],
        out_specs=(pl.BlockSpec((1, tm, d), lambda i, j: (i, j, 0)),
                   pl.BlockSpec((1, tm, d), lambda i, j: (i, j, 0)),
                   pl.BlockSpec((1, 2 * PEER_HEADS, N_KEYS, tm), lambda i, j: (i, 0, 0, j))),
        compiler_params=_params(("parallel", "parallel")),
        name="outproj",
    )(x, mla, hsum, cols, gm, wa, wb, gt1, g2, sc2, sh2, wpq, sk)


def _topk_rows(s, k):
    rows, t = s.shape
    rid = lax.broadcasted_iota(I32, (rows, t), 0)
    kid = lax.broadcasted_iota(I32, (k, t), 0)

    def body(r, carry):
        s, vals, idxs = carry
        m = jnp.max(s, axis=0, keepdims=True)
        i = jnp.min(jnp.where(s == m, rid, rows), axis=0, keepdims=True)
        vals = jnp.where(kid == r, m, vals)
        idxs = jnp.where(kid == r, i, idxs)
        s = jnp.where(rid == i, -jnp.inf, s)
        return s, vals, idxs

    _, vals, idxs = lax.fori_loop(0, k, body, (s, jnp.zeros((k, t), F32), jnp.zeros((k, t), I32)))
    return vals, idxs


def _route_kernel(s_ref, offu_ref, sh_ref, offv_ref, par_ref, gw_ref, ot_ref, st_ref, gt_ref):
    K = PEER_TOPK

    def head(h, carry):
        sv0, si0 = _topk_rows(s_ref[0, 2 * h], K)
        sv1, si1 = _topk_rows(s_ref[0, 2 * h + 1], K)
        cand = jnp.concatenate([sv0[i:i + 1, :] + sv1 for i in range(K)], axis=0)
        best, pos = _topk_rows(cand, K)
        isel = pos >> 4
        jsel = pos & (K - 1)
        e0 = jnp.zeros_like(pos)
        e1 = jnp.zeros_like(pos)
        for i in range(K):
            e0 = jnp.where(isel == i, si0[i:i + 1, :], e0)
            e1 = jnp.where(jsel == i, si1[i:i + 1, :], e1)
        ex = jnp.exp(best - best[0:1, :])
        g = ex / jnp.sum(ex, axis=0, keepdims=True)
        e = e0 * N_KEYS + e1
        sl = pl.ds(pl.multiple_of(h * K, K), K)
        ot_ref[sl, :] = e >> 1
        st_ref[sl, :] = (1 - (e & 1)) * 16
        gt_ref[sl, :] = g
        return carry

    lax.fori_loop(0, PEER_HEADS, head, 0)
    tile = ot_ref[...].T
    sh = st_ref[...].T
    offu_ref[0] = tile * SUBLANES
    offv_ref[0] = tile * (2 * SUBLANES)
    sh_ref[0] = sh
    par_ref[0] = (1 - (sh >> 4)).astype(F32)
    gw_ref[0] = gt_ref[...].T


def _route(scores, tt):
    b, hp, nk, n = scores.shape
    nsel = PEER_HEADS * PEER_TOPK
    ospec = pl.BlockSpec((1, tt, nsel), lambda i, j: (i, j, 0))
    return pl.pallas_call(
        _route_kernel,
        out_shape=(jax.ShapeDtypeStruct((b, n, nsel), I32), jax.ShapeDtypeStruct((b, n, nsel), I32),
                   jax.ShapeDtypeStruct((b, n, nsel), I32), jax.ShapeDtypeStruct((b, n, nsel), F32),
                   jax.ShapeDtypeStruct((b, n, nsel), F32)),
        grid=(b, n // tt),
        in_specs=[pl.BlockSpec((1, hp, nk, tt), lambda i, j: (i, 0, 0, j))],
        out_specs=(ospec, ospec, ospec, ospec, ospec),
        scratch_shapes=[pltpu.VMEM((nsel, tt), I32), pltpu.VMEM((nsel, tt), I32), pltpu.VMEM((nsel, tt), F32)],
        compiler_params=_params(("parallel", "parallel")),
        name="route",
    )(scores)


HIGH_HALF = 0xFFFF0000


def _expert_row(tab_ref, off, sh):
    w = tab_ref[pl.ds(pl.multiple_of(off, SUBLANES), SUBLANES), :]
    w = lax.shift_left(w, jnp.full(w.shape, sh, I32).astype(jnp.uint32)) & jnp.uint32(HIGH_HALF)
    return pltpu.bitcast(w, F32)


def _sublane_sums(ps):
    sub = lax.broadcasted_iota(I32, (SUBLANES, LANES), 0)
    out = ps
    for dist in (4, 2, 1):
        keep = (sub & (2 * dist - 1)) < dist
        half = len(out) // 2
        out = [jnp.where(keep, out[i], pltpu.roll(out[i + half], dist, 0))
               + jnp.where(keep, pltpu.roll(out[i], SUBLANES - dist, 0), out[i + half]) for i in range(half)]
    return out[0]


def _peer_u_kernel(off_ref, sh_ref, h_ref, tab_ref, g_ref, c_ref, red_ref, rs_ref):
    tb = h_ref.shape[0]
    nsel = g_ref.shape[1]

    def tok(t, carry):
        x = h_ref[t]
        for gi in range(nsel // SUBLANES):
            ps = [_expert_row(tab_ref, off_ref[t, gi * SUBLANES + kk], sh_ref[t, gi * SUBLANES + kk]) * x
                  for kk in range(SUBLANES)]
            red_ref[t, gi * SUBLANES:(gi + 1) * SUBLANES, :] = _sublane_sums(ps)
        return carry

    lax.fori_loop(0, tb, tok, 0)
    red = red_ref[...].reshape(tb * nsel, LANES)
    hi = red.astype(BF16)
    lo = (red - hi.astype(F32)).astype(BF16)
    ones = jnp.ones((LANES, LANES), BF16)
    rs = jnp.dot(hi, ones, preferred_element_type=F32) + jnp.dot(lo, ones, preferred_element_type=F32)
    rs_ref[...] = rs.reshape(tb, nsel, LANES)
    eye = (lax.broadcasted_iota(I32, (nsel, LANES), 0) == lax.broadcasted_iota(I32, (nsel, LANES), 1)).astype(F32)

    def fin(t, carry):
        pre = jnp.sum(rs_ref[t] * eye, axis=0, keepdims=True)
        act = 0.5 * pre * (1.0 + lax.erf(pre * (2.0 ** -0.5)))
        c_ref[pl.ds(t, 1), :] = g_ref[pl.ds(t, 1), :] * act
        return carry

    lax.fori_loop(0, tb, fin, 0)


def _smem_block(tb, nsel):
    return pl.BlockSpec((tb, nsel), lambda i: (i, 0), memory_space=pltpu.SMEM)


def _peer_u(off, sh, h2, tab, gw, tb):
    t, nsel = gw.shape
    return pl.pallas_call(
        _peer_u_kernel,
        out_shape=jax.ShapeDtypeStruct((t, nsel), F32),
        grid=(t // tb,),
        in_specs=[_smem_block(tb, nsel), _smem_block(tb, nsel),
                  pl.BlockSpec((tb, SUBLANES, LANES), lambda i: (i, 0, 0)),
                  pl.BlockSpec(tab.shape, lambda i: (0, 0), pipeline_mode=pl.Buffered(1)),
                  pl.BlockSpec((tb, nsel), lambda i: (i, 0))],
        out_specs=pl.BlockSpec((tb, nsel), lambda i: (i, 0)),
        scratch_shapes=[pltpu.VMEM((tb, nsel, LANES), F32), pltpu.VMEM((tb, nsel, LANES), F32)],
        compiler_params=_params(("arbitrary",)),
        name="peer_u",
    )(off, sh, h2, tab, gw)


TILE_ROWS = 2 * SUBLANES


def _peer_v_kernel(off_ref, c_ref, par_ref, tab_ref, x1_ref, gt_ref, o_ref, lhi_ref, llo_ref):
    tb, nsel = c_ref.shape
    width = nsel * TILE_ROWS
    chunk = 2 * LANES
    per = chunk // TILE_ROWS
    spread = (lax.broadcasted_iota(I32, (nsel, width), 1) // TILE_ROWS
              == lax.broadcasted_iota(I32, (nsel, width), 0)).astype(BF16)
    c = c_ref[...]
    c_hi = c.astype(BF16)
    c_lo = (c - c_hi.astype(F32)).astype(BF16)
    half = ((lax.broadcasted_iota(I32, (1, width), 1) // SUBLANES) % 2).astype(F32)
    mine = jnp.dot(par_ref[...].astype(BF16), spread, preferred_element_type=F32) == half
    lhi_ref[...] = jnp.where(mine, jnp.dot(c_hi, spread, preferred_element_type=F32), 0.0)
    llo_ref[...] = jnp.where(mine, jnp.dot(c_lo, spread, preferred_element_type=F32), 0.0)
    diag = (lax.broadcasted_iota(I32, (SUBLANES, width), 1) % SUBLANES
            == lax.broadcasted_iota(I32, (SUBLANES, width), 0)).astype(F32)

    def tok(t, carry):
        left = jnp.concatenate([lhi_ref[pl.ds(t, 1), :] * diag, llo_ref[pl.ds(t, 1), :] * diag], axis=0).astype(BF16)
        acc = jnp.zeros((TILE_ROWS, LANES), F32)
        for ch in range(width // chunk):
            tiles = [tab_ref[pl.ds(pl.multiple_of(off_ref[t, ch * per + kk], TILE_ROWS), TILE_ROWS), :]
                     for kk in range(per)]
            acc = acc + jnp.dot(left[:, ch * chunk:(ch + 1) * chunk], jnp.concatenate(tiles, axis=0),
                                preferred_element_type=F32)
        o_ref[t] = x1_ref[t] + gt_ref[0] * (acc[:SUBLANES] + acc[SUBLANES:])
        return carry

    lax.fori_loop(0, tb, tok, 0)


def _peer_v(off, coef, par, tab, x1, gt2, tb, tok_per_batch):
    t, nsel = coef.shape
    bpb = tok_per_batch // tb
    vspec = pl.BlockSpec((tb, nsel), lambda i: (i, 0))
    return pl.pallas_call(
        _peer_v_kernel,
        out_shape=jax.ShapeDtypeStruct(x1.shape, F32),
        grid=(t // tb,),
        in_specs=[_smem_block(tb, nsel), vspec, vspec,
                  pl.BlockSpec(tab.shape, lambda i: (0, 0), pipeline_mode=pl.Buffered(1)),
                  pl.BlockSpec((tb, SUBLANES, LANES), lambda i: (i, 0, 0)),
                  pl.BlockSpec((1, SUBLANES, LANES), lambda i: (i // bpb, 0, 0))],
        out_specs=pl.BlockSpec((tb, SUBLANES, LANES), lambda i: (i, 0, 0)),
        scratch_shapes=[pltpu.VMEM((tb, nsel * TILE_ROWS), F32), pltpu.VMEM((tb, nsel * TILE_ROWS), F32)],
        compiler_params=_params(("arbitrary",)),
        name="peer_v",
    )(off, coef, par, tab, x1, gt2)


def _rope_tables(n):
    axis = MLA_ROPE // 2
    t = jnp.arange(n, dtype=F32)
    row = jnp.floor(t / GRID_W)
    col = t - row * GRID_W
    inv = ROPE_BASE ** (-jnp.arange(axis // 2, dtype=F32) * (2.0 / axis))
    ar = row[:, None] * inv
    ac = col[:, None] * inv
    cos = jnp.concatenate([jnp.cos(ar), jnp.cos(ar), jnp.cos(ac), jnp.cos(ac)], axis=1)
    sin = jnp.concatenate([-jnp.sin(ar), jnp.sin(ar), -jnp.sin(ac), jnp.sin(ac)], axis=1)
    return cos, sin


def _pair_table(tab):
    e, d = tab.shape
    return tab.astype(BF16).reshape(e * (d // LANES), LANES)


def _pack_table(tab):
    e, d = tab.shape
    pairs = tab.astype(BF16).reshape(e // 2, 2, d).transpose(0, 2, 1)
    return lax.bitcast_convert_type(pairs, jnp.uint32).reshape(e // 2 * (d // LANES), LANES)


def _block(n, want):
    return want if n % want == 0 else n


def kernel(x, c, ctx, c_ctx, w_ada, b_ada, g_norm1, w_in, g_cq, w_uq, g_ckv, w_ukv, g_qn, g_kn, conv_qk, b_igate, b_fgate, g_mlstm, w_out, g_norm2, w_pq, sub_keys, expert_u, expert_v):
    B, N, D = x.shape
    NC = ctx.shape[1]
    assert w_ada.shape[0] == 1, "one layer"
    assert N % 256 == 0 and NC % ML_CHUNK == 0 and D == SUBLANES * LANES
    q_rank = g_cq.shape[1]
    kv_rank = g_ckv.shape[1]
    mla_cols = q_rank + kv_rank + MLA_ROPE
    assert (q_rank, kv_rank) == (C_KV0 - C_Q0, C_KR0 - C_KV0)
    swap = jnp.arange(MLA_ROPE) ^ (MLA_ROPE // 4)

    cc = jnp.concatenate([c, c_ctx[None, :], jnp.zeros((16 - B - 1, D), F32)], axis=0)
    mod = _ada(cc, w_ada[0].astype(BF16), b_ada)
    sh1, sc1, gt1, sh2, sc2, gt2 = [mod[:, i * D:(i + 1) * D] for i in range(6)]
    lat = lambda m: m[:B].reshape(B, 1, D)
    ctxm = lambda m: jnp.broadcast_to(m[B:B + 1].reshape(1, 1, D), (B, 1, D))

    wi = w_in[0]
    n_qk = 2 * ML_HEADS * ML_QK
    n_v = ML_HEADS * ML_V
    n_g = 4 * ML_HEADS
    m0 = mla_cols
    w_cols = jnp.concatenate([
        wi[:, 0:mla_cols],
        wi[:, q_rank + kv_rank + swap],
        wi[:, m0:m0 + n_qk + 2 * n_v + n_g],
        jnp.zeros((D, C_END - C_G0 - n_g), F32)], axis=1).astype(BF16)
    assert w_cols.shape[1] == C_END
    w_gates_t = wi[:, m0 + n_qk + 2 * n_v:m0 + n_qk + 2 * n_v + n_g].T.astype(BF16)
    cols_l, gates_l = _inproj(x, g_norm1, lat(sc1), lat(sh1), w_cols, w_gates_t, _block(N, 512))
    cols_c, gates_c = _inproj(ctx, g_norm1, ctxm(sc1), ctxm(sh1), w_cols, w_gates_t, _block(NC, 512))

    wq = w_uq[0].reshape(q_rank, MLA_HEADS, MLA_QK)
    wuq = jnp.concatenate([wq, wq[:, :, MLA_NOPE + swap]], axis=2).reshape(q_rank, MLA_HEADS * 256).astype(BF16)
    wukv = w_ukv[0].astype(BF16)
    gqs, gks = g_qn[:, MLA_NOPE + swap], g_kn[:, MLA_NOPE + swap]
    cos_l, sin_l = _rope_tables(N)
    cos_c, sin_c = jnp.ones((NC, MLA_ROPE), F32), jnp.zeros((NC, MLA_ROPE), F32)
    q_l, k_l, v_l = _mla_prep(cols_l, g_cq, wuq, g_ckv, wukv, g_qn, gqs, g_kn, gks, cos_l, sin_l, _block(N, 512))
    _, k_c, v_c = _mla_prep(cols_c, g_cq, wuq, g_ckv, wukv, g_qn, gqs, g_kn, gks, cos_c, sin_c, _block(NC, 512))
    k_all = jnp.concatenate([k_c, k_l], axis=2)
    v_all = jnp.concatenate([v_c, v_l], axis=2)
    mla = _attention(q_l, k_all, v_all, 256)

    bias16 = jnp.concatenate([b_igate[0].reshape(-1), b_fgate[0].reshape(-1)])
    cw = conv_qk[0]
    pl_ = _ml_prep(cols_l, gates_l, cw, bias16[None, :], bias16[:, None], _block(N, 512))
    pc_ = _ml_prep(cols_c, gates_c, cw, bias16[None, :], bias16[:, None], _block(NC, 512))
    hsum = _ml_scan(*pl_, *pc_)

    wo = w_out[0].astype(BF16)
    hw = MLA_HEADS * MLA_V
    sk = sub_keys[0].reshape(2 * PEER_HEADS, N_KEYS, -1).astype(BF16)
    x1, h2, scores = _outproj(x, mla, hsum, cols_l, g_mlstm, wo[:hw], wo[hw:], lat(gt1), g_norm2, lat(sc2), lat(sh2),
                              w_pq[0].astype(BF16), sk, 256)
    T = B * N
    nsel = PEER_HEADS * PEER_TOPK
    offu, sh, offv, par, gw = [a.reshape(T, nsel) for a in _route(scores, 256)]

    tb = 32
    coef = _peer_u(offu, sh, h2.reshape(T, SUBLANES, LANES), _pack_table(expert_u[0]), gw, tb)
    out = _peer_v(offv, coef, par, _pair_table(expert_v[0]), x1.reshape(T, SUBLANES, LANES),
                  lat(gt2).reshape(B, SUBLANES, LANES), tb, N)
    return out.reshape(B, N, D)
```

```python
import jax
import jax.numpy as jnp
from jax import lax
from jax.experimental import pallas as pl
from jax.experimental.pallas import tpu as pltpu

F32 = jnp.float32
BF16 = jnp.bfloat16
I32 = jnp.int32
EPS = 1e-6

GRID_W = 64
MLA_HEADS = 4
MLA_NOPE = 128
MLA_ROPE = 64
MLA_V = 128
MLA_QK = MLA_NOPE + MLA_ROPE
ROPE_BASE = 10000.0
ML_HEADS = 4
ML_QK = 64
ML_V = 128
ML_CHUNK = 64
PEER_HEADS = 8
N_KEYS = 128
PEER_TOPK = 16

LANES = 128
SUBLANES = 8
VMEM_LIMIT_BYTES = 56 * 1024 * 1024

C_Q0, C_KV0, C_KR0, C_KRS0, C_QK0, C_V0, C_O0, C_G0, C_END = 0, 256, 384, 448, 512, 1024, 1536, 2048, 2176
HIGHEST = lax.Precision.HIGHEST


def _params(sem):
    return pltpu.CompilerParams(dimension_semantics=sem, vmem_limit_bytes=VMEM_LIMIT_BYTES)


def _rms(x):
    return x * lax.rsqrt(jnp.mean(x * x, axis=-1, keepdims=True) + EPS)


def _ada_kernel(c_ref, w_ref, b_ref, o_ref):
    c = c_ref[...]
    s = c * jax.nn.sigmoid(c)
    o_ref[...] = jnp.dot(s.astype(BF16), w_ref[...], preferred_element_type=F32) + b_ref[...]


def _ada(cc, w, b):
    rows, d = cc.shape
    n = w.shape[1]
    bn = n // 4
    return pl.pallas_call(
        _ada_kernel,
        out_shape=jax.ShapeDtypeStruct((rows, n), F32),
        grid=(4,),
        in_specs=[pl.BlockSpec((rows, d), lambda j: (0, 0)),
                  pl.BlockSpec((d, bn), lambda j: (0, j)),
                  pl.BlockSpec((1, bn), lambda j: (0, j))],
        out_specs=pl.BlockSpec((rows, bn), lambda j: (0, j)),
        compiler_params=_params(("arbitrary",)),
        name="ada",
    )(cc, w, b)


def _inproj_kernel(x_ref, g_ref, sc_ref, sh_ref, w_ref, wg_ref, o_ref, gt_ref):
    x = x_ref[0]
    h = _rms(x) * g_ref[...] * (1.0 + sc_ref[0]) + sh_ref[0]
    hb = h.astype(BF16)
    o_ref[0] = jnp.dot(hb, w_ref[...], preferred_element_type=F32)
    gt_ref[0] = lax.dot_general(wg_ref[...], hb, (((1,), (1,)), ((), ())), preferred_element_type=F32)


def _inproj(x, g, sc, sh, w, wg, tm):
    b, n, d = x.shape
    nc = w.shape[1]
    return pl.pallas_call(
        _inproj_kernel,
        out_shape=(jax.ShapeDtypeStruct((b, n, nc), F32), jax.ShapeDtypeStruct((b, 16, n), F32)),
        grid=(b, n // tm),
        in_specs=[pl.BlockSpec((1, tm, d), lambda i, j: (i, j, 0)),
                  pl.BlockSpec((1, d), lambda i, j: (0, 0)),
                  pl.BlockSpec((1, 1, d), lambda i, j: (i, 0, 0)),
                  pl.BlockSpec((1, 1, d), lambda i, j: (i, 0, 0)),
                  pl.BlockSpec((d, nc), lambda i, j: (0, 0)),
                  pl.BlockSpec((16, d), lambda i, j: (0, 0))],
        out_specs=(pl.BlockSpec((1, tm, nc), lambda i, j: (i, j, 0)),
                   pl.BlockSpec((1, 16, tm), lambda i, j: (i, 0, j))),
        compiler_params=_params(("parallel", "parallel")),
        name="inproj",
    )(x, g, sc, sh, w, wg)


def _mla_prep_kernel(c_ref, gcq_ref, wuq_ref, gckv_ref, wukv_ref, gq_ref, gqs_ref, gk_ref, gks_ref,
                     cos_ref, sin_ref, q_ref, k_ref, v_ref):
    c = c_ref[0]
    cq = _rms(c[:, C_Q0:C_KV0]) * gcq_ref[...]
    ckv = _rms(c[:, C_KV0:C_KR0]) * gckv_ref[...]
    kr = c[:, C_KR0:C_KRS0]
    krs = c[:, C_KRS0:C_QK0]
    q_raw = jnp.dot(cq.astype(BF16), wuq_ref[...], preferred_element_type=F32)
    kv_raw = jnp.dot(ckv.astype(BF16), wukv_ref[...], preferred_element_type=F32)
    cos = cos_ref[...]
    sin = sin_ref[...]
    gq = gq_ref[...]
    gk = gk_ref[...]
    kr_ss = jnp.sum(kr * kr, axis=-1, keepdims=True)
    for h in range(MLA_HEADS):
        o = h * 256
        qn = q_raw[:, o:o + 128]
        qr = q_raw[:, o + 128:o + 192]
        qs = q_raw[:, o + 192:o + 256]
        ss = jnp.sum(qn * qn, axis=-1, keepdims=True) + jnp.sum(qr * qr, axis=-1, keepdims=True)
        r = lax.rsqrt(ss * (1.0 / MLA_QK) + EPS) * (MLA_QK ** -0.5)
        q_ref[0, h, :, 0:128] = (qn * r * gq[:, 0:128]).astype(BF16)
        q_ref[0, h, :, 128:192] = ((qr * r * gq[:, 128:192]) * cos + (qs * r * gqs_ref[...]) * sin).astype(BF16)
        kn = kv_raw[:, o:o + 128]
        ss = jnp.sum(kn * kn, axis=-1, keepdims=True) + kr_ss
        r = lax.rsqrt(ss * (1.0 / MLA_QK) + EPS)
        k_ref[0, h, :, 0:128] = (kn * r * gk[:, 0:128]).astype(BF16)
        k_ref[0, h, :, 128:192] = ((kr * r * gk[:, 128:192]) * cos + (krs * r * gks_ref[...]) * sin).astype(BF16)
        v_ref[0, h] = kv_raw[:, o + 128:o + 256].astype(BF16)


def _mla_prep(cols, gcq, wuq, gckv, wukv, gq, gqs, gk, gks, cos, sin, tm):
    b, n, _ = cols.shape
    full = lambda a: pl.BlockSpec(a.shape, lambda i, j: (0,) * a.ndim)
    return pl.pallas_call(
        _mla_prep_kernel,
        out_shape=(jax.ShapeDtypeStruct((b, MLA_HEADS, n, MLA_QK), BF16),
                   jax.ShapeDtypeStruct((b, MLA_HEADS, n, MLA_QK), BF16),
                   jax.ShapeDtypeStruct((b, MLA_HEADS, n, MLA_V), BF16)),
        grid=(b, n // tm),
        in_specs=[pl.BlockSpec((1, tm, 512), lambda i, j: (i, j, 0)),
                  full(gcq), full(wuq), full(gckv), full(wukv), full(gq), full(gqs), full(gk), full(gks),
                  pl.BlockSpec((tm, MLA_ROPE), lambda i, j: (j, 0)),
                  pl.BlockSpec((tm, MLA_ROPE), lambda i, j: (j, 0))],
        out_specs=(pl.BlockSpec((1, MLA_HEADS, tm, MLA_QK), lambda i, j: (i, 0, j, 0)),
                   pl.BlockSpec((1, MLA_HEADS, tm, MLA_QK), lambda i, j: (i, 0, j, 0)),
                   pl.BlockSpec((1, MLA_HEADS, tm, MLA_V), lambda i, j: (i, 0, j, 0))),
        compiler_params=_params(("parallel", "parallel")),
        name="mla_prep",
    )(cols, gcq, wuq, gckv, wukv, gq, gqs, gk, gks, cos, sin)


def _attn_kernel(q_ref, k_ref, v_ref, o_ref):
    q = q_ref[0, 0]
    s = lax.dot_general(q, k_ref[0, 0], (((1,), (1,)), ((), ())), preferred_element_type=F32)
    m = jnp.max(s, axis=-1, keepdims=True)
    p = jnp.exp(s - m)
    l = jnp.sum(p, axis=-1, keepdims=True)
    o = jnp.dot(p.astype(BF16), v_ref[0, 0], preferred_element_type=F32)
    o_ref[0] = (o / l).astype(BF16)


def _attention(q, k, v, tq):
    b, h, n, dk = q.shape
    nk = k.shape[2]
    dv = v.shape[3]
    return pl.pallas_call(
        _attn_kernel,
        out_shape=jax.ShapeDtypeStruct((b, n, h * dv), BF16),
        grid=(b, h, n // tq),
        in_specs=[pl.BlockSpec((1, 1, tq, dk), lambda i, j, t: (i, j, t, 0)),
                  pl.BlockSpec((1, 1, nk, dk), lambda i, j, t: (i, j, 0, 0)),
                  pl.BlockSpec((1, 1, nk, dv), lambda i, j, t: (i, j, 0, 0))],
        out_specs=pl.BlockSpec((1, tq, dv), lambda i, j, t: (i, t, j)),
        compiler_params=_params(("parallel", "parallel", "parallel")),
        name="attention",
    )(q, k, v)


def _ml_prep_kernel(qk_ref, prev_ref, next_ref, vin_ref, g_ref, gt_ref, cw_ref, bc_ref, br_ref,
                    q_ref, kt_ref, v_ref, a_ref, brow_ref):
    tn = qk_ref.shape[1]
    j = pl.program_id(1)
    nj = pl.num_programs(1)
    u = qk_ref[0]
    row = lax.broadcasted_iota(I32, (tn, 1), 0)
    before = jnp.where(j == 0, 0.0, prev_ref[0, SUBLANES - 1:SUBLANES, :])
    after = jnp.where(j == nj - 1, 0.0, next_ref[0, 0:1, :])
    up = jnp.where(row == 0, before, pltpu.roll(u, 1, 0))
    un = jnp.where(row == tn - 1, after, pltpu.roll(u, tn - 1, 0))
    cw = cw_ref[...]
    y = up * cw[0:1, :] + u * cw[1:2, :] + un * cw[2:3, :]
    y = y * jax.nn.sigmoid(y)
    hq = ML_HEADS * ML_QK
    for h in range(ML_HEADS):
        q_ref[0, h] = (y[:, h * ML_QK:(h + 1) * ML_QK] * (ML_QK ** -0.5)).astype(BF16)
    kt = y[:, hq:2 * hq].T
    L = ML_CHUNK
    for c in range(tn // L):
        kt_ref[0, c] = kt[:, c * L:(c + 1) * L].astype(BF16)
    v_ref[0] = vin_ref[0].astype(BF16)
    g = g_ref[0][:, 0:16] + bc_ref[...]
    lf_c = jax.nn.log_sigmoid(g[:, 8:16])
    gt = gt_ref[0] + br_ref[...]
    ig_r = gt[0:8, :]
    lf_r = jax.nn.log_sigmoid(gt[8:16, :])
    ti = lax.broadcasted_iota(I32, (L, L), 0)
    si = lax.broadcasted_iota(I32, (L, L), 1)
    lower = (si <= ti).astype(F32)
    upper = (si >= ti).astype(F32)
    lane_fwd = lax.broadcasted_iota(I32, (L, 8), 1) < ML_HEADS
    row_fwd = lax.broadcasted_iota(I32, (8, L), 0) < ML_HEADS
    for c in range(tn // L):
        lo = c * L
        lfc = lf_c[lo:lo + L, :]
        a_ref[0, lo:lo + L, :] = jnp.where(
            lane_fwd,
            jnp.dot(lower, lfc, precision=HIGHEST, preferred_element_type=F32),
            jnp.dot(upper, lfc, precision=HIGHEST, preferred_element_type=F32))
        lfr = lf_r[:, lo:lo + L]
        cf_r = jnp.where(
            row_fwd,
            jnp.dot(lfr, upper, precision=HIGHEST, preferred_element_type=F32),
            jnp.dot(lfr, lower, precision=HIGHEST, preferred_element_type=F32))
        brow_ref[0, c] = ig_r[:, lo:lo + L] - cf_r


def _ml_prep(cols, gates_t, cw, bias_col, bias_row, tn):
    b, n, _ = cols.shape
    r8 = tn // SUBLANES
    last8 = n // SUBLANES - 1
    L = ML_CHUNK
    hq = ML_HEADS * ML_QK
    return pl.pallas_call(
        _ml_prep_kernel,
        out_shape=(jax.ShapeDtypeStruct((b, ML_HEADS, n, ML_QK), BF16),
                   jax.ShapeDtypeStruct((b, n // L, hq, L), BF16),
                   jax.ShapeDtypeStruct((b, n, ML_HEADS * ML_V), BF16),
                   jax.ShapeDtypeStruct((b, n, 8), F32),
                   jax.ShapeDtypeStruct((b, n // L, 8, L), F32)),
        grid=(b, n // tn),
        in_specs=[pl.BlockSpec((1, tn, 512), lambda i, j: (i, j, C_QK0 // 512)),
                  pl.BlockSpec((1, SUBLANES, 512), lambda i, j: (i, jnp.maximum(j * r8 - 1, 0), C_QK0 // 512)),
                  pl.BlockSpec((1, SUBLANES, 512), lambda i, j: (i, jnp.minimum((j + 1) * r8, last8), C_QK0 // 512)),
                  pl.BlockSpec((1, tn, 512), lambda i, j: (i, j, C_V0 // 512)),
                  pl.BlockSpec((1, tn, 128), lambda i, j: (i, j, C_G0 // 128)),
                  pl.BlockSpec((1, 16, tn), lambda i, j: (i, 0, j)),
                  pl.BlockSpec((3, 512), lambda i, j: (0, 0)),
                  pl.BlockSpec((1, 16), lambda i, j: (0, 0)),
                  pl.BlockSpec((16, 1), lambda i, j: (0, 0))],
        out_specs=(pl.BlockSpec((1, ML_HEADS, tn, ML_QK), lambda i, j: (i, 0, j, 0)),
                   pl.BlockSpec((1, tn // L, hq, L), lambda i, j: (i, j, 0, 0)),
                   pl.BlockSpec((1, tn, ML_HEADS * ML_V), lambda i, j: (i, j, 0)),
                   pl.BlockSpec((1, tn, 8), lambda i, j: (i, j, 0)),
                   pl.BlockSpec((1, tn // L, 8, L), lambda i, j: (i, j, 0, 0))),
        compiler_params=_params(("parallel", "parallel")),
        name="ml_prep",
    )(cols, cols, cols, cols, cols, gates_t, cw, bias_col, bias_row)


def _ml_chunk(qb, kt, vaug, a_c, b_r, ct, m_prev, fwd, need_out):
    L = qb.shape[0]
    ti = lax.broadcasted_iota(I32, (L, L), 0)
    si = lax.broadcasted_iota(I32, (L, L), 1)
    mask = (si <= ti) if fwd else (si >= ti)
    dmat = jnp.where(mask, a_c + b_r, -jnp.inf)
    mloc = jnp.max(dmat, axis=1, keepdims=True)
    inter = a_c + m_prev
    m_t = jnp.maximum(inter, mloc)
    last = L - 1 if fwd else 0
    m_new = m_t[last:last + 1, :]
    a_last = a_c[last:last + 1, :]
    w_r = jnp.exp(a_last + b_r - m_new)
    decay = jnp.exp(a_last + m_prev - m_new)
    kw = (kt.astype(F32) * w_r).astype(BF16)
    ct_new = decay * ct + jnp.dot(kw, vaug, preferred_element_type=F32)
    if not need_out:
        return ct_new, m_new, None
    s = jnp.dot(qb, kt, preferred_element_type=F32)
    wmat = jnp.exp(dmat - m_t) * s
    sc = jnp.exp(inter - m_t)
    nd = (jnp.dot(wmat.astype(BF16), vaug, preferred_element_type=F32)
          + sc * jnp.dot(qb, ct.astype(BF16), preferred_element_type=F32))
    num = nd[:, 0:ML_V]
    den = nd[:, ML_V:ML_V + 1]
    return ct_new, m_new, num / jnp.maximum(jnp.abs(den), jnp.exp(-m_t))


def _ml_scan_kernel(ql_ref, ktl_ref, vl_ref, al_ref, brl_ref, qc_ref, ktc_ref, vc_ref, ac_ref, brc_ref,
                    out_ref, st_ref, m_ref):
    L = ML_CHUNK
    ncl = ql_ref.shape[2] // L
    ncc = qc_ref.shape[2] // L
    ones_col = (lax.broadcasted_iota(I32, (L, ML_V), 1) == 0).astype(BF16)
    st_ref[...] = jnp.zeros(st_ref.shape, F32)
    m_ref[...] = jnp.zeros(m_ref.shape, F32)

    def step(refs, c, d, fwd, need_out):
        q_ref, kt_ref, v_ref, a_ref, br_ref = refs
        sl = pl.ds(pl.multiple_of(c * L, L), L)
        a_all = a_ref[0, sl, :]
        br_all = br_ref[0, c]
        v_all = v_ref[0, sl, :]
        kt_all = kt_ref[0, c]
        hs = []
        for h in range(ML_HEADS):
            j = d * ML_HEADS + h
            vaug = jnp.concatenate([v_all[:, h * ML_V:(h + 1) * ML_V], ones_col], axis=1)
            ct_new, m_new, hh = _ml_chunk(
                q_ref[0, h, sl, :], kt_all[h * ML_QK:(h + 1) * ML_QK, :], vaug,
                a_all[:, j:j + 1], br_all[j:j + 1, :], st_ref[j], m_ref[j][0:1, 0:1], fwd, need_out)
            st_ref[j] = ct_new
            m_ref[j] = jnp.broadcast_to(m_new, (SUBLANES, LANES))
            hs.append(hh)
        return hs, sl

    ctx_refs = (qc_ref, ktc_ref, vc_ref, ac_ref, brc_ref)
    lat_refs = (ql_ref, ktl_ref, vl_ref, al_ref, brl_ref)

    def run(d, fwd):
        def ctx_body(i, carry):
            step(ctx_refs, i if fwd else ncc - 1 - i, d, fwd, False)
            return carry

        def lat_body(i, carry):
            hs, sl = step(lat_refs, i if fwd else ncl - 1 - i, d, fwd, True)
            hcat = jnp.concatenate(hs, axis=1)
            if fwd:
                out_ref[0, sl, :] = hcat
            else:
                out_ref[0, sl, :] = out_ref[0, sl, :] + hcat
            return carry

        lax.fori_loop(0, ncc, ctx_body, 0)
        lax.fori_loop(0, ncl, lat_body, 0)

    run(0, True)
    run(1, False)


def _ml_scan(ql, ktl, vl, al, brl, qc, ktc, vc, ac, brc):
    b, _, n, _ = ql.shape
    nctx = qc.shape[2]
    L = ML_CHUNK
    hv = ML_HEADS * ML_V
    hq = ML_HEADS * ML_QK
    qspec = lambda nn: pl.BlockSpec((1, ML_HEADS, nn, ML_QK), lambda i: (i, 0, 0, 0))
    ktspec = lambda nn: pl.BlockSpec((1, nn // L, hq, L), lambda i: (i, 0, 0, 0))
    vspec = lambda nn: pl.BlockSpec((1, nn, hv), lambda i: (i, 0, 0))
    aspec = lambda nn: pl.BlockSpec((1, nn, 8), lambda i: (i, 0, 0))
    rspec = lambda nn: pl.BlockSpec((1, nn // L, 8, L), lambda i: (i, 0, 0, 0))
    return pl.pallas_call(
        _ml_scan_kernel,
        out_shape=jax.ShapeDtypeStruct((b, n, hv), F32),
        grid=(b,),
        in_specs=[qspec(n), ktspec(n), vspec(n), aspec(n), rspec(n),
                  qspec(nctx), ktspec(nctx), vspec(nctx), aspec(nctx), rspec(nctx)],
        out_specs=pl.BlockSpec((1, n, hv), lambda i: (i, 0, 0)),
        scratch_shapes=[pltpu.VMEM((2 * ML_HEADS, ML_QK, 2 * ML_V), F32),
                        pltpu.VMEM((2 * ML_HEADS, SUBLANES, LANES), F32)],
        compiler_params=_params(("parallel",)),
        name="ml_scan",
    )(ql, ktl, vl, al, brl, qc, ktc, vc, ac, brc)


def _outproj_kernel(x_ref, mla_ref, hs_ref, o_ref, gm_ref, wa_ref, wb_ref, gt_ref, g2_ref, sc_ref, sh_ref,
                    wpq_ref, sk_ref, x1_ref, h2_ref, s_ref):
    hs = hs_ref[0]
    gm = gm_ref[...]
    hn = jnp.concatenate([_rms(hs[:, h * ML_V:(h + 1) * ML_V]) * gm[:, h * ML_V:(h + 1) * ML_V]
                          for h in range(ML_HEADS)], axis=1)
    ml = (jax.nn.sigmoid(o_ref[0]) * hn).astype(BF16)
    mix = (jnp.dot(mla_ref[0], wa_ref[...], preferred_element_type=F32)
           + jnp.dot(ml, wb_ref[...], preferred_element_type=F32))
    x1 = x_ref[0] + gt_ref[0] * mix
    x1_ref[0] = x1
    h2 = _rms(x1) * g2_ref[...] * (1.0 + sc_ref[0]) + sh_ref[0]
    h2_ref[0] = h2
    qp = jnp.dot(h2.astype(BF16), wpq_ref[...], preferred_element_type=F32).astype(BF16)
    for hp in range(2 * PEER_HEADS):
        s_ref[0, hp] = lax.dot_general(sk_ref[hp], qp[:, hp * N_KEYS:(hp + 1) * N_KEYS],
                                       (((1,), (1,)), ((), ())), preferred_element_type=F32)


def _outproj(x, mla, hsum, cols, gm, wa, wb, gt1, g2, sc2, sh2, wpq, sk, tm):
    b, n, d = x.shape
    hw = mla.shape[2]
    nq = wpq.shape[1]
    mod = pl.BlockSpec((1, 1, d), lambda i, j: (i, 0, 0))
    return pl.pallas_call(
        _outproj_kernel,
        out_shape=(jax.ShapeDtypeStruct((b, n, d), F32), jax.ShapeDtypeStruct((b, n, d), F32),
                   jax.ShapeDtypeStruct((b, 2 * PEER_HEADS, N_KEYS, n), F32)),
        grid=(b, n // tm),
        in_specs=[pl.BlockSpec((1, tm, d), lambda i, j: (i, j, 0)),
                  pl.BlockSpec((1, tm, hw), lambda i, j: (i, j, 0)),
                  pl.BlockSpec((1, tm, hw), lambda i, j: (i, j, 0)),
                  pl.BlockSpec((1, tm, 512), lambda i, j: (i, j, C_O0 // 512)),
                  pl.BlockSpec((1, hw), lambda i, j: (0, 0)),
                  pl.BlockSpec((hw, d), lambda i, j: (0, 0)),
                  pl.BlockSpec((hw, d), lambda i, j: (0, 0)),
                  mod, pl.BlockSpec((1, d), lambda i, j: (0, 0)), mod, mod,
                  pl.BlockSpec((d, nq), lambda i, j: (0, 0)),
                  pl.BlockSpec((2 * PEER_HEADS, N_KEYS, N_KEYS), lambda i, j: (0, 0, 0))],
        out_specs=(pl.BlockSpec((1, tm, d), lambda i, j: (i, j, 0)),
                   pl.BlockSpec((1, tm, d), lambda i, j: (i, j, 0)),
                   pl.BlockSpec((1, 2 * PEER_HEADS, N_KEYS, tm), lambda i, j: (i, 0, 0, j))),
        compiler_params=_params(("parallel", "parallel")),
        name="outproj",
    )(x, mla, hsum, cols, gm, wa, wb, gt1, g2, sc2, sh2, wpq, sk)


def _topk_rows(s, k, rid=None):
    rows, t = s.shape
    if rid is None:
        rid = lax.broadcasted_iota(I32, (rows, t), 0)
    kid = lax.broadcasted_iota(I32, (k, t), 0)
    big = jnp.iinfo(jnp.int32).max

    def body(r, carry):
        s, vals, idxs = carry
        m = jnp.max(s, axis=0, keepdims=True)
        i = jnp.min(jnp.where(s == m, rid, big), axis=0, keepdims=True)
        vals = jnp.where(kid == r, m, vals)
        idxs = jnp.where(kid == r, i, idxs)
        s = jnp.where(rid == i, -jnp.inf, s)
        return s, vals, idxs

    _, vals, idxs = lax.fori_loop(0, k, body, (s, jnp.zeros((k, t), F32), jnp.zeros((k, t), I32)))
    return vals, idxs


def _pair_candidates(sv0, sv1):
    K, t = sv0.shape
    h = K // 2
    iid = lax.broadcasted_iota(I32, (h, t), 0)
    vals = [sv0[0:h] + sv1[0:1], sv0[h:K] + sv1[0:1]]
    ids = [iid * K, (iid + h) * K]
    for j in range(1, h):
        vals.append(sv0[0:h] + sv1[j:j + 1])
        ids.append(iid * K + j)
    vals.append(sv0[0:1] + sv1[h:K])
    ids.append(iid + h)
    return jnp.concatenate(vals, axis=0), jnp.concatenate(ids, axis=0)


def _route_kernel(s_ref, offu_ref, sh_ref, offv_ref, par_ref, gw_ref, ot_ref, st_ref, gt_ref):
    K = PEER_TOPK

    def head(h, carry):
        sv0, si0 = _topk_rows(s_ref[0, 2 * h], K)
        sv1, si1 = _topk_rows(s_ref[0, 2 * h + 1], K)
        cand, cid = _pair_candidates(sv0, sv1)
        best, pos = _topk_rows(cand, K, cid)
        isel = pos >> 4
        jsel = pos & (K - 1)
        e0 = jnp.zeros_like(pos)
        e1 = jnp.zeros_like(pos)
        for i in range(K):
            e0 = jnp.where(isel == i, si0[i:i + 1, :], e0)
            e1 = jnp.where(jsel == i, si1[i:i + 1, :], e1)
        ex = jnp.exp(best - best[0:1, :])
        g = ex / jnp.sum(ex, axis=0, keepdims=True)
        e = e0 * N_KEYS + e1
        sl = pl.ds(pl.multiple_of(h * K, K), K)
        ot_ref[sl, :] = e >> 1
        st_ref[sl, :] = (1 - (e & 1)) * 16
        gt_ref[sl, :] = g
        return carry

    lax.fori_loop(0, PEER_HEADS, head, 0)
    tile = ot_ref[...].T
    sh = st_ref[...].T
    offu_ref[0] = tile * SUBLANES
    offv_ref[0] = tile
    sh_ref[0] = sh
    par_ref[0] = (1 - (sh >> 4)).astype(F32)
    gw_ref[0] = gt_ref[...].T


def _route(scores, tt):
    b, hp, nk, n = scores.shape
    nsel = PEER_HEADS * PEER_TOPK
    ospec = pl.BlockSpec((1, tt, nsel), lambda i, j: (i, j, 0))
    return pl.pallas_call(
        _route_kernel,
        out_shape=(jax.ShapeDtypeStruct((b, n, nsel), I32), jax.ShapeDtypeStruct((b, n, nsel), I32),
                   jax.ShapeDtypeStruct((b, n, nsel), I32), jax.ShapeDtypeStruct((b, n, nsel), F32),
                   jax.ShapeDtypeStruct((b, n, nsel), F32)),
        grid=(b, n // tt),
        in_specs=[pl.BlockSpec((1, hp, nk, tt), lambda i, j: (i, 0, 0, j))],
        out_specs=(ospec, ospec, ospec, ospec, ospec),
        scratch_shapes=[pltpu.VMEM((nsel, tt), I32), pltpu.VMEM((nsel, tt), I32), pltpu.VMEM((nsel, tt), F32)],
        compiler_params=_params(("parallel", "parallel")),
        name="route",
    )(scores)


HIGH_HALF = 0xFFFF0000


def _expert_row(tab_ref, off, sh):
    w = tab_ref[pl.ds(pl.multiple_of(off, SUBLANES), SUBLANES), :]
    w = lax.shift_left(w, jnp.full(w.shape, sh, I32).astype(jnp.uint32)) & jnp.uint32(HIGH_HALF)
    return pltpu.bitcast(w, F32)


def _sublane_sums(ps):
    sub = lax.broadcasted_iota(I32, (SUBLANES, LANES), 0)
    out = ps
    for dist in (4, 2, 1):
        keep = (sub & (2 * dist - 1)) < dist
        half = len(out) // 2
        out = [jnp.where(keep, out[i], pltpu.roll(out[i + half], dist, 0))
               + jnp.where(keep, pltpu.roll(out[i], SUBLANES - dist, 0), out[i + half]) for i in range(half)]
    return out[0]


def _peer_u_kernel(off_ref, sh_ref, h_ref, tab_ref, g_ref, c_ref, red_ref, rs_ref):
    tb = h_ref.shape[0]
    nsel = g_ref.shape[1]

    def tok(t, carry):
        x = h_ref[t]
        for gi in range(nsel // SUBLANES):
            ps = [_expert_row(tab_ref, off_ref[t, gi * SUBLANES + kk], sh_ref[t, gi * SUBLANES + kk]) * x
                  for kk in range(SUBLANES)]
            red_ref[t, gi * SUBLANES:(gi + 1) * SUBLANES, :] = _sublane_sums(ps)
        return carry

    lax.fori_loop(0, tb, tok, 0)
    red = red_ref[...].reshape(tb * nsel, LANES)
    hi = red.astype(BF16)
    lo = (red - hi.astype(F32)).astype(BF16)
    ones = jnp.ones((LANES, LANES), BF16)
    rs = jnp.dot(hi, ones, preferred_element_type=F32) + jnp.dot(lo, ones, preferred_element_type=F32)
    rs_ref[...] = rs.reshape(tb, nsel, LANES)
    eye = (lax.broadcasted_iota(I32, (nsel, LANES), 0) == lax.broadcasted_iota(I32, (nsel, LANES), 1)).astype(F32)

    def fin(t, carry):
        c_ref[pl.ds(t, 1), :] = jnp.sum(rs_ref[t] * eye, axis=0, keepdims=True)
        return carry

    lax.fori_loop(0, tb, fin, 0)
    pre = c_ref[...]
    c_ref[...] = g_ref[...] * (0.5 * pre * (1.0 + lax.erf(pre * (2.0 ** -0.5))))


def _smem_block(tb, nsel):
    return pl.BlockSpec((tb, nsel), lambda i: (i, 0), memory_space=pltpu.SMEM)


def _peer_u(off, sh, h2, tab, gw, tb):
    t, nsel = gw.shape
    return pl.pallas_call(
        _peer_u_kernel,
        out_shape=jax.ShapeDtypeStruct((t, nsel), F32),
        grid=(t // tb,),
        in_specs=[_smem_block(tb, nsel), _smem_block(tb, nsel),
                  pl.BlockSpec((tb, SUBLANES, LANES), lambda i: (i, 0, 0)),
                  pl.BlockSpec(tab.shape, lambda i: (0, 0), pipeline_mode=pl.Buffered(1)),
                  pl.BlockSpec((tb, nsel), lambda i: (i, 0))],
        out_specs=pl.BlockSpec((tb, nsel), lambda i: (i, 0)),
        scratch_shapes=[pltpu.VMEM((tb, nsel, LANES), F32), pltpu.VMEM((tb, nsel, LANES), F32)],
        compiler_params=_params(("arbitrary",)),
        name="peer_u",
    )(off, sh, h2, tab, gw)


TILE_ROWS = 2 * SUBLANES
TOKENS_PER_ITER = 4


def _peer_v_kernel(off_ref, c_ref, par_ref, tab_ref, x1_ref, gt_ref, o_ref, lhi_ref, llo_ref):
    tb, nsel = c_ref.shape
    width = nsel * TILE_ROWS
    spread = (lax.broadcasted_iota(I32, (nsel, width), 1) // TILE_ROWS
              == lax.broadcasted_iota(I32, (nsel, width), 0)).astype(BF16)
    c = c_ref[...]
    c_hi = c.astype(BF16)
    c_lo = (c - c_hi.astype(F32)).astype(BF16)
    half = ((lax.broadcasted_iota(I32, (1, width), 1) // SUBLANES) % 2).astype(F32)
    mine = jnp.dot(par_ref[...].astype(BF16), spread, preferred_element_type=F32) == half
    lhi_ref[...] = jnp.where(mine, jnp.dot(c_hi, spread, preferred_element_type=F32), 0.0)
    llo_ref[...] = jnp.where(mine, jnp.dot(c_lo, spread, preferred_element_type=F32), 0.0)
    diag = (lax.broadcasted_iota(I32, (SUBLANES, width), 1) % SUBLANES
            == lax.broadcasted_iota(I32, (SUBLANES, width), 0)).astype(F32)

    def one(t):
        left = jnp.concatenate([lhi_ref[pl.ds(t, 1), :] * diag, llo_ref[pl.ds(t, 1), :] * diag], axis=0).astype(BF16)
        tiles = jnp.concatenate([tab_ref[off_ref[t, k]] for k in range(nsel)], axis=0)
        acc = jnp.dot(left, tiles, preferred_element_type=F32)
        o_ref[t] = x1_ref[t] + gt_ref[0] * (acc[:SUBLANES] + acc[SUBLANES:])

    def tok(i, carry):
        for u in range(TOKENS_PER_ITER):
            one(i * TOKENS_PER_ITER + u)
        return carry

    lax.fori_loop(0, tb // TOKENS_PER_ITER, tok, 0)


def _peer_v(off, coef, par, tab, x1, gt2, tb, tok_per_batch):
    t, nsel = coef.shape
    bpb = tok_per_batch // tb
    vspec = pl.BlockSpec((tb, nsel), lambda i: (i, 0))
    return pl.pallas_call(
        _peer_v_kernel,
        out_shape=jax.ShapeDtypeStruct(x1.shape, F32),
        grid=(t // tb,),
        in_specs=[_smem_block(tb, nsel), vspec, vspec,
                  pl.BlockSpec(tab.shape, lambda i: (0, 0, 0), pipeline_mode=pl.Buffered(1)),
                  pl.BlockSpec((tb, SUBLANES, LANES), lambda i: (i, 0, 0)),
                  pl.BlockSpec((1, SUBLANES, LANES), lambda i: (i // bpb, 0, 0))],
        out_specs=pl.BlockSpec((tb, SUBLANES, LANES), lambda i: (i, 0, 0)),
        scratch_shapes=[pltpu.VMEM((tb, nsel * TILE_ROWS), F32), pltpu.VMEM((tb, nsel * TILE_ROWS), F32)],
        compiler_params=_params(("arbitrary",)),
        name="peer_v",
    )(off, coef, par, tab, x1, gt2)


def _rope_tables(n):
    axis = MLA_ROPE // 2
    t = jnp.arange(n, dtype=F32)
    row = jnp.floor(t / GRID_W)
    col = t - row * GRID_W
    inv = ROPE_BASE ** (-jnp.arange(axis // 2, dtype=F32) * (2.0 / axis))
    ar = row[:, None] * inv
    ac = col[:, None] * inv
    cos = jnp.concatenate([jnp.cos(ar), jnp.cos(ar), jnp.cos(ac), jnp.cos(ac)], axis=1)
    sin = jnp.concatenate([-jnp.sin(ar), jnp.sin(ar), -jnp.sin(ac), jnp.sin(ac)], axis=1)
    return cos, sin


def _pair_table(tab):
    e, d = tab.shape
    return tab.astype(BF16).reshape(e // 2, 2 * (d // LANES), LANES)


def _pack_table(tab):
    e, d = tab.shape
    pairs = tab.astype(BF16).reshape(e // 2, 2, d).transpose(0, 2, 1)
    return lax.bitcast_convert_type(pairs, jnp.uint32).reshape(e // 2 * (d // LANES), LANES)


def _block(n, want):
    return want if n % want == 0 else n


def kernel(x, c, ctx, c_ctx, w_ada, b_ada, g_norm1, w_in, g_cq, w_uq, g_ckv, w_ukv, g_qn, g_kn, conv_qk, b_igate, b_fgate, g_mlstm, w_out, g_norm2, w_pq, sub_keys, expert_u, expert_v):
    B, N, D = x.shape
    NC = ctx.shape[1]
    assert w_ada.shape[0] == 1, "one layer"
    assert N % 256 == 0 and NC % ML_CHUNK == 0 and D == SUBLANES * LANES
    q_rank = g_cq.shape[1]
    kv_rank = g_ckv.shape[1]
    mla_cols = q_rank + kv_rank + MLA_ROPE
    assert (q_rank, kv_rank) == (C_KV0 - C_Q0, C_KR0 - C_KV0)
    swap = jnp.arange(MLA_ROPE) ^ (MLA_ROPE // 4)

    cc = jnp.concatenate([c, c_ctx[None, :], jnp.zeros((16 - B - 1, D), F32)], axis=0)
    mod = _ada(cc, w_ada[0].astype(BF16), b_ada)
    sh1, sc1, gt1, sh2, sc2, gt2 = [mod[:, i * D:(i + 1) * D] for i in range(6)]
    lat = lambda m: m[:B].reshape(B, 1, D)
    ctxm = lambda m: jnp.broadcast_to(m[B:B + 1].reshape(1, 1, D), (B, 1, D))

    wi = w_in[0]
    n_qk = 2 * ML_HEADS * ML_QK
    n_v = ML_HEADS * ML_V
    n_g = 4 * ML_HEADS
    m0 = mla_cols
    w_cols = jnp.concatenate([
        wi[:, 0:mla_cols],
        wi[:, q_rank + kv_rank + swap],
        wi[:, m0:m0 + n_qk + 2 * n_v + n_g],
        jnp.zeros((D, C_END - C_G0 - n_g), F32)], axis=1).astype(BF16)
    assert w_cols.shape[1] == C_END
    w_gates_t = wi[:, m0 + n_qk + 2 * n_v:m0 + n_qk + 2 * n_v + n_g].T.astype(BF16)
    cols_l, gates_l = _inproj(x, g_norm1, lat(sc1), lat(sh1), w_cols, w_gates_t, _block(N, 512))
    cols_c, gates_c = _inproj(ctx, g_norm1, ctxm(sc1), ctxm(sh1), w_cols, w_gates_t, _block(NC, 512))

    wq = w_uq[0].reshape(q_rank, MLA_HEADS, MLA_QK)
    wuq = jnp.concatenate([wq, wq[:, :, MLA_NOPE + swap]], axis=2).reshape(q_rank, MLA_HEADS * 256).astype(BF16)
    wukv = w_ukv[0].astype(BF16)
    gqs, gks = g_qn[:, MLA_NOPE + swap], g_kn[:, MLA_NOPE + swap]
    cos_l, sin_l = _rope_tables(N)
    cos_c, sin_c = jnp.ones((NC, MLA_ROPE), F32), jnp.zeros((NC, MLA_ROPE), F32)
    q_l, k_l, v_l = _mla_prep(cols_l, g_cq, wuq, g_ckv, wukv, g_qn, gqs, g_kn, gks, cos_l, sin_l, _block(N, 512))
    _, k_c, v_c = _mla_prep(cols_c, g_cq, wuq, g_ckv, wukv, g_qn, gqs, g_kn, gks, cos_c, sin_c, _block(NC, 512))
    k_all = jnp.concatenate([k_c, k_l], axis=2)
    v_all = jnp.concatenate([v_c, v_l], axis=2)
    mla = _attention(q_l, k_all, v_all, 256)

    bias16 = jnp.concatenate([b_igate[0].reshape(-1), b_fgate[0].reshape(-1)])
    cw = conv_qk[0]
    pl_ = _ml_prep(cols_l, gates_l, cw, bias16[None, :], bias16[:, None], _block(N, 512))
    pc_ = _ml_prep(cols_c, gates_c, cw, bias16[None, :], bias16[:, None], _block(NC, 512))
    hsum = _ml_scan(*pl_, *pc_)

    wo = w_out[0].astype(BF16)
    hw = MLA_HEADS * MLA_V
    sk = sub_keys[0].reshape(2 * PEER_HEADS, N_KEYS, -1).astype(BF16)
    x1, h2, scores = _outproj(x, mla, hsum, cols_l, g_mlstm, wo[:hw], wo[hw:], lat(gt1), g_norm2, lat(sc2), lat(sh2),
                              w_pq[0].astype(BF16), sk, 256)
    T = B * N
    nsel = PEER_HEADS * PEER_TOPK
    offu, sh, offv, par, gw = [a.reshape(T, nsel) for a in _route(scores, 256)]

    tb = 32
    coef = _peer_u(offu, sh, h2.reshape(T, SUBLANES, LANES), _pack_table(expert_u[0]), gw, tb)
    out = _peer_v(offv, coef, par, _pair_table(expert_v[0]), x1.reshape(T, SUBLANES, LANES),
                  lat(gt2).reshape(B, SUBLANES, LANES), tb, N)
    return out.reshape(B, N, D)
```

```python
import jax
import jax.numpy as jnp
from jax import lax
from jax.experimental import pallas as pl
from jax.experimental.pallas import tpu as pltpu

F32 = jnp.float32
BF16 = jnp.bfloat16
I32 = jnp.int32
EPS = 1e-6

GRID_W = 64
MLA_HEADS = 4
MLA_NOPE = 128
MLA_ROPE = 64
MLA_V = 128
MLA_QK = MLA_NOPE + MLA_ROPE
ROPE_BASE = 10000.0
ML_HEADS = 4
ML_QK = 64
ML_V = 128
ML_CHUNK = 64
PEER_HEADS = 8
N_KEYS = 128
PEER_TOPK = 16

LANES = 128
SUBLANES = 8
VMEM_LIMIT_BYTES = 56 * 1024 * 1024

C_Q0, C_KV0, C_KR0, C_KRS0, C_QK0, C_V0, C_O0, C_G0, C_END = 0, 256, 384, 448, 512, 1024, 1536, 2048, 2176
HIGHEST = lax.Precision.HIGHEST


def _params(sem):
    return pltpu.CompilerParams(dimension_semantics=sem, vmem_limit_bytes=VMEM_LIMIT_BYTES)


def _rms(x):
    return x * lax.rsqrt(jnp.mean(x * x, axis=-1, keepdims=True) + EPS)


def _ada_kernel(c_ref, w_ref, b_ref, o_ref):
    c = c_ref[...]
    s = c * jax.nn.sigmoid(c)
    o_ref[...] = jnp.dot(s.astype(BF16), w_ref[...], preferred_element_type=F32) + b_ref[...]


def _ada(cc, w, b):
    rows, d = cc.shape
    n = w.shape[1]
    bn = n // 4
    return pl.pallas_call(
        _ada_kernel,
        out_shape=jax.ShapeDtypeStruct((rows, n), F32),
        grid=(4,),
        in_specs=[pl.BlockSpec((rows, d), lambda j: (0, 0)),
                  pl.BlockSpec((d, bn), lambda j: (0, j)),
                  pl.BlockSpec((1, bn), lambda j: (0, j))],
        out_specs=pl.BlockSpec((rows, bn), lambda j: (0, j)),
        compiler_params=_params(("arbitrary",)),
        name="ada",
    )(cc, w, b)


def _inproj_kernel(x_ref, g_ref, sc_ref, sh_ref, w_ref, wg_ref, o_ref, gt_ref):
    x = x_ref[0]
    h = _rms(x) * g_ref[...] * (1.0 + sc_ref[0]) + sh_ref[0]
    hb = h.astype(BF16)
    o_ref[0] = jnp.dot(hb, w_ref[...], preferred_element_type=F32)
    gt_ref[0] = lax.dot_general(wg_ref[...], hb, (((1,), (1,)), ((), ())), preferred_element_type=F32)


def _inproj(x, g, sc, sh, w, wg, tm):
    b, n, d = x.shape
    nc = w.shape[1]
    return pl.pallas_call(
        _inproj_kernel,
        out_shape=(jax.ShapeDtypeStruct((b, n, nc), F32), jax.ShapeDtypeStruct((b, 16, n), F32)),
        grid=(b, n // tm),
        in_specs=[pl.BlockSpec((1, tm, d), lambda i, j: (i, j, 0)),
                  pl.BlockSpec((1, d), lambda i, j: (0, 0)),
                  pl.BlockSpec((1, 1, d), lambda i, j: (i, 0, 0)),
                  pl.BlockSpec((1, 1, d), lambda i, j: (i, 0, 0)),
                  pl.BlockSpec((d, nc), lambda i, j: (0, 0)),
                  pl.BlockSpec((16, d), lambda i, j: (0, 0))],
        out_specs=(pl.BlockSpec((1, tm, nc), lambda i, j: (i, j, 0)),
                   pl.BlockSpec((1, 16, tm), lambda i, j: (i, 0, j))),
        compiler_params=_params(("parallel", "parallel")),
        name="inproj",
    )(x, g, sc, sh, w, wg)


def _mla_prep_kernel(c_ref, gcq_ref, wuq_ref, gckv_ref, wukv_ref, gq_ref, gqs_ref, gk_ref, gks_ref,
                     cos_ref, sin_ref, q_ref, k_ref, v_ref):
    c = c_ref[0]
    cq = _rms(c[:, C_Q0:C_KV0]) * gcq_ref[...]
    ckv = _rms(c[:, C_KV0:C_KR0]) * gckv_ref[...]
    kr = c[:, C_KR0:C_KRS0]
    krs = c[:, C_KRS0:C_QK0]
    q_raw = jnp.dot(cq.astype(BF16), wuq_ref[...], preferred_element_type=F32)
    kv_raw = jnp.dot(ckv.astype(BF16), wukv_ref[...], preferred_element_type=F32)
    cos = cos_ref[...]
    sin = sin_ref[...]
    gq = gq_ref[...]
    gk = gk_ref[...]
    kr_ss = jnp.sum(kr * kr, axis=-1, keepdims=True)
    for h in range(MLA_HEADS):
        o = h * 256
        qn = q_raw[:, o:o + 128]
        qr = q_raw[:, o + 128:o + 192]
        qs = q_raw[:, o + 192:o + 256]
        ss = jnp.sum(qn * qn, axis=-1, keepdims=True) + jnp.sum(qr * qr, axis=-1, keepdims=True)
        r = lax.rsqrt(ss * (1.0 / MLA_QK) + EPS) * (MLA_QK ** -0.5)
        q_ref[0, h, :, 0:128] = (qn * r * gq[:, 0:128]).astype(BF16)
        q_ref[0, h, :, 128:192] = ((qr * r * gq[:, 128:192]) * cos + (qs * r * gqs_ref[...]) * sin).astype(BF16)
        kn = kv_raw[:, o:o + 128]
        ss = jnp.sum(kn * kn, axis=-1, keepdims=True) + kr_ss
        r = lax.rsqrt(ss * (1.0 / MLA_QK) + EPS)
        k_ref[0, h, :, 0:128] = (kn * r * gk[:, 0:128]).astype(BF16)
        k_ref[0, h, :, 128:192] = ((kr * r * gk[:, 128:192]) * cos + (krs * r * gks_ref[...]) * sin).astype(BF16)
        v_ref[0, h] = kv_raw[:, o + 128:o + 256].astype(BF16)


def _mla_prep(cols, gcq, wuq, gckv, wukv, gq, gqs, gk, gks, cos, sin, tm):
    b, n, _ = cols.shape
    full = lambda a: pl.BlockSpec(a.shape, lambda i, j: (0,) * a.ndim)
    return pl.pallas_call(
        _mla_prep_kernel,
        out_shape=(jax.ShapeDtypeStruct((b, MLA_HEADS, n, MLA_QK), BF16),
                   jax.ShapeDtypeStruct((b, MLA_HEADS, n, MLA_QK), BF16),
                   jax.ShapeDtypeStruct((b, MLA_HEADS, n, MLA_V), BF16)),
        grid=(b, n // tm),
        in_specs=[pl.BlockSpec((1, tm, 512), lambda i, j: (i, j, 0)),
                  full(gcq), full(wuq), full(gckv), full(wukv), full(gq), full(gqs), full(gk), full(gks),
                  pl.BlockSpec((tm, MLA_ROPE), lambda i, j: (j, 0)),
                  pl.BlockSpec((tm, MLA_ROPE), lambda i, j: (j, 0))],
        out_specs=(pl.BlockSpec((1, MLA_HEADS, tm, MLA_QK), lambda i, j: (i, 0, j, 0)),
                   pl.BlockSpec((1, MLA_HEADS, tm, MLA_QK), lambda i, j: (i, 0, j, 0)),
                   pl.BlockSpec((1, MLA_HEADS, tm, MLA_V), lambda i, j: (i, 0, j, 0))),
        compiler_params=_params(("parallel", "parallel")),
        name="mla_prep",
    )(cols, gcq, wuq, gckv, wukv, gq, gqs, gk, gks, cos, sin)


def _attn_kernel(q_ref, k_ref, v_ref, o_ref):
    q = q_ref[0, 0]
    s = lax.dot_general(q, k_ref[0, 0], (((1,), (1,)), ((), ())), preferred_element_type=F32)
    m = jnp.max(s, axis=-1, keepdims=True)
    p = jnp.exp(s - m)
    l = jnp.sum(p, axis=-1, keepdims=True)
    o = jnp.dot(p.astype(BF16), v_ref[0, 0], preferred_element_type=F32)
    o_ref[0] = (o / l).astype(BF16)


def _attention(q, k, v, tq):
    b, h, n, dk = q.shape
    nk = k.shape[2]
    dv = v.shape[3]
    return pl.pallas_call(
        _attn_kernel,
        out_shape=jax.ShapeDtypeStruct((b, n, h * dv), BF16),
        grid=(b, h, n // tq),
        in_specs=[pl.BlockSpec((1, 1, tq, dk), lambda i, j, t: (i, j, t, 0)),
                  pl.BlockSpec((1, 1, nk, dk), lambda i, j, t: (i, j, 0, 0)),
                  pl.BlockSpec((1, 1, nk, dv), lambda i, j, t: (i, j, 0, 0))],
        out_specs=pl.BlockSpec((1, tq, dv), lambda i, j, t: (i, t, j)),
        compiler_params=_params(("parallel", "parallel", "parallel")),
        name="attention",
    )(q, k, v)


def _ml_prep_kernel(qk_ref, prev_ref, next_ref, vin_ref, g_ref, gt_ref, cw_ref, bc_ref, br_ref,
                    q_ref, kt_ref, v_ref, a_ref, brow_ref):
    tn = qk_ref.shape[1]
    j = pl.program_id(1)
    nj = pl.num_programs(1)
    u = qk_ref[0]
    row = lax.broadcasted_iota(I32, (tn, 1), 0)
    before = jnp.where(j == 0, 0.0, prev_ref[0, SUBLANES - 1:SUBLANES, :])
    after = jnp.where(j == nj - 1, 0.0, next_ref[0, 0:1, :])
    up = jnp.where(row == 0, before, pltpu.roll(u, 1, 0))
    un = jnp.where(row == tn - 1, after, pltpu.roll(u, tn - 1, 0))
    cw = cw_ref[...]
    y = up * cw[0:1, :] + u * cw[1:2, :] + un * cw[2:3, :]
    y = y * jax.nn.sigmoid(y)
    hq = ML_HEADS * ML_QK
    for h in range(ML_HEADS):
        q_ref[0, h] = (y[:, h * ML_QK:(h + 1) * ML_QK] * (ML_QK ** -0.5)).astype(BF16)
    kt = y[:, hq:2 * hq].T
    L = ML_CHUNK
    for c in range(tn // L):
        kt_ref[0, c] = kt[:, c * L:(c + 1) * L].astype(BF16)
    v_ref[0] = vin_ref[0].astype(BF16)
    g = g_ref[0][:, 0:16] + bc_ref[...]
    lf_c = jax.nn.log_sigmoid(g[:, 8:16])
    gt = gt_ref[0] + br_ref[...]
    ig_r = gt[0:8, :]
    lf_r = jax.nn.log_sigmoid(gt[8:16, :])
    ti = lax.broadcasted_iota(I32, (L, L), 0)
    si = lax.broadcasted_iota(I32, (L, L), 1)
    lower = (si <= ti).astype(F32)
    upper = (si >= ti).astype(F32)
    lane_fwd = lax.broadcasted_iota(I32, (L, 8), 1) < ML_HEADS
    row_fwd = lax.broadcasted_iota(I32, (8, L), 0) < ML_HEADS
    for c in range(tn // L):
        lo = c * L
        lfc = lf_c[lo:lo + L, :]
        a_ref[0, lo:lo + L, :] = jnp.where(
            lane_fwd,
            jnp.dot(lower, lfc, precision=HIGHEST, preferred_element_type=F32),
            jnp.dot(upper, lfc, precision=HIGHEST, preferred_element_type=F32))
        lfr = lf_r[:, lo:lo + L]
        cf_r = jnp.where(
            row_fwd,
            jnp.dot(lfr, upper, precision=HIGHEST, preferred_element_type=F32),
            jnp.dot(lfr, lower, precision=HIGHEST, preferred_element_type=F32))
        brow_ref[0, c] = ig_r[:, lo:lo + L] - cf_r


def _ml_prep(cols, gates_t, cw, bias_col, bias_row, tn):
    b, n, _ = cols.shape
    r8 = tn // SUBLANES
    last8 = n // SUBLANES - 1
    L = ML_CHUNK
    hq = ML_HEADS * ML_QK
    return pl.pallas_call(
        _ml_prep_kernel,
        out_shape=(jax.ShapeDtypeStruct((b, ML_HEADS, n, ML_QK), BF16),
                   jax.ShapeDtypeStruct((b, n // L, hq, L), BF16),
                   jax.ShapeDtypeStruct((b, n, ML_HEADS * ML_V), BF16),
                   jax.ShapeDtypeStruct((b, n, 8), F32),
                   jax.ShapeDtypeStruct((b, n // L, 8, L), F32)),
        grid=(b, n // tn),
        in_specs=[pl.BlockSpec((1, tn, 512), lambda i, j: (i, j, C_QK0 // 512)),
                  pl.BlockSpec((1, SUBLANES, 512), lambda i, j: (i, jnp.maximum(j * r8 - 1, 0), C_QK0 // 512)),
                  pl.BlockSpec((1, SUBLANES, 512), lambda i, j: (i, jnp.minimum((j + 1) * r8, last8), C_QK0 // 512)),
                  pl.BlockSpec((1, tn, 512), lambda i, j: (i, j, C_V0 // 512)),
                  pl.BlockSpec((1, tn, 128), lambda i, j: (i, j, C_G0 // 128)),
                  pl.BlockSpec((1, 16, tn), lambda i, j: (i, 0, j)),
                  pl.BlockSpec((3, 512), lambda i, j: (0, 0)),
                  pl.BlockSpec((1, 16), lambda i, j: (0, 0)),
                  pl.BlockSpec((16, 1), lambda i, j: (0, 0))],
        out_specs=(pl.BlockSpec((1, ML_HEADS, tn, ML_QK), lambda i, j: (i, 0, j, 0)),
                   pl.BlockSpec((1, tn // L, hq, L), lambda i, j: (i, j, 0, 0)),
                   pl.BlockSpec((1, tn, ML_HEADS * ML_V), lambda i, j: (i, j, 0)),
                   pl.BlockSpec((1, tn, 8), lambda i, j: (i, j, 0)),
                   pl.BlockSpec((1, tn // L, 8, L), lambda i, j: (i, j, 0, 0))),
        compiler_params=_params(("parallel", "parallel")),
        name="ml_prep",
    )(cols, cols, cols, cols, cols, gates_t, cw, bias_col, bias_row)


def _ml_chunk(qb, kt, vaug, a_c, b_r, ct, m_prev, fwd, need_out):
    L = qb.shape[0]
    ti = lax.broadcasted_iota(I32, (L, L), 0)
    si = lax.broadcasted_iota(I32, (L, L), 1)
    mask = (si <= ti) if fwd else (si >= ti)
    dmat = jnp.where(mask, a_c + b_r, -jnp.inf)
    mloc = jnp.max(dmat, axis=1, keepdims=True)
    inter = a_c + m_prev
    m_t = jnp.maximum(inter, mloc)
    last = L - 1 if fwd else 0
    m_new = m_t[last:last + 1, :]
    a_last = a_c[last:last + 1, :]
    w_r = jnp.exp(a_last + b_r - m_new)
    decay = jnp.exp(a_last + m_prev - m_new)
    kw = (kt.astype(F32) * w_r).astype(BF16)
    ct_new = decay * ct + jnp.dot(kw, vaug, preferred_element_type=F32)
    if not need_out:
        return ct_new, m_new, None
    s = jnp.dot(qb, kt, preferred_element_type=F32)
    wmat = jnp.exp(dmat - m_t) * s
    sc = jnp.exp(inter - m_t)
    nd = (jnp.dot(wmat.astype(BF16), vaug, preferred_element_type=F32)
          + sc * jnp.dot(qb, ct.astype(BF16), preferred_element_type=F32))
    num = nd[:, 0:ML_V]
    den = nd[:, ML_V:ML_V + 1]
    return ct_new, m_new, num / jnp.maximum(jnp.abs(den), jnp.exp(-m_t))


def _ml_scan_kernel(ql_ref, ktl_ref, vl_ref, al_ref, brl_ref, qc_ref, ktc_ref, vc_ref, ac_ref, brc_ref,
                    out_ref, st_ref, m_ref):
    L = ML_CHUNK
    ncl = ql_ref.shape[2] // L
    ncc = qc_ref.shape[2] // L
    ones_col = (lax.broadcasted_iota(I32, (L, ML_V), 1) == 0).astype(BF16)
    st_ref[...] = jnp.zeros(st_ref.shape, F32)
    m_ref[...] = jnp.zeros(m_ref.shape, F32)

    def step(refs, c, d, fwd, need_out):
        q_ref, kt_ref, v_ref, a_ref, br_ref = refs
        sl = pl.ds(pl.multiple_of(c * L, L), L)
        a_all = a_ref[0, sl, :]
        br_all = br_ref[0, c]
        v_all = v_ref[0, sl, :]
        kt_all = kt_ref[0, c]
        hs = []
        for h in range(ML_HEADS):
            j = d * ML_HEADS + h
            vaug = jnp.concatenate([v_all[:, h * ML_V:(h + 1) * ML_V], ones_col], axis=1)
            ct_new, m_new, hh = _ml_chunk(
                q_ref[0, h, sl, :], kt_all[h * ML_QK:(h + 1) * ML_QK, :], vaug,
                a_all[:, j:j + 1], br_all[j:j + 1, :], st_ref[j], m_ref[j][0:1, 0:1], fwd, need_out)
            st_ref[j] = ct_new
            m_ref[j] = jnp.broadcast_to(m_new, (SUBLANES, LANES))
            hs.append(hh)
        return hs, sl

    ctx_refs = (qc_ref, ktc_ref, vc_ref, ac_ref, brc_ref)
    lat_refs = (ql_ref, ktl_ref, vl_ref, al_ref, brl_ref)

    def run(d, fwd):
        def ctx_body(i, carry):
            step(ctx_refs, i if fwd else ncc - 1 - i, d, fwd, False)
            return carry

        def lat_body(i, carry):
            hs, sl = step(lat_refs, i if fwd else ncl - 1 - i, d, fwd, True)
            hcat = jnp.concatenate(hs, axis=1)
            if fwd:
                out_ref[0, sl, :] = hcat
            else:
                out_ref[0, sl, :] = out_ref[0, sl, :] + hcat
            return carry

        lax.fori_loop(0, ncc, ctx_body, 0)
        lax.fori_loop(0, ncl, lat_body, 0)

    run(0, True)
    run(1, False)


def _ml_scan(ql, ktl, vl, al, brl, qc, ktc, vc, ac, brc):
    b, _, n, _ = ql.shape
    nctx = qc.shape[2]
    L = ML_CHUNK
    hv = ML_HEADS * ML_V
    hq = ML_HEADS * ML_QK
    qspec = lambda nn: pl.BlockSpec((1, ML_HEADS, nn, ML_QK), lambda i: (i, 0, 0, 0))
    ktspec = lambda nn: pl.BlockSpec((1, nn // L, hq, L), lambda i: (i, 0, 0, 0))
    vspec = lambda nn: pl.BlockSpec((1, nn, hv), lambda i: (i, 0, 0))
    aspec = lambda nn: pl.BlockSpec((1, nn, 8), lambda i: (i, 0, 0))
    rspec = lambda nn: pl.BlockSpec((1, nn // L, 8, L), lambda i: (i, 0, 0, 0))
    return pl.pallas_call(
        _ml_scan_kernel,
        out_shape=jax.ShapeDtypeStruct((b, n, hv), F32),
        grid=(b,),
        in_specs=[qspec(n), ktspec(n), vspec(n), aspec(n), rspec(n),
                  qspec(nctx), ktspec(nctx), vspec(nctx), aspec(nctx), rspec(nctx)],
        out_specs=pl.BlockSpec((1, n, hv), lambda i: (i, 0, 0)),
        scratch_shapes=[pltpu.VMEM((2 * ML_HEADS, ML_QK, 2 * ML_V), F32),
                        pltpu.VMEM((2 * ML_HEADS, SUBLANES, LANES), F32)],
        compiler_params=_params(("parallel",)),
        name="ml_scan",
    )(ql, ktl, vl, al, brl, qc, ktc, vc, ac, brc)


def _outproj_kernel(x_ref, mla_ref, hs_ref, o_ref, gm_ref, wa_ref, wb_ref, gt_ref, g2_ref, sc_ref, sh_ref,
                    wpq_ref, sk_ref, x1_ref, h2_ref, s_ref):
    hs = hs_ref[0]
    gm = gm_ref[...]
    hn = jnp.concatenate([_rms(hs[:, h * ML_V:(h + 1) * ML_V]) * gm[:, h * ML_V:(h + 1) * ML_V]
                          for h in range(ML_HEADS)], axis=1)
    ml = (jax.nn.sigmoid(o_ref[0]) * hn).astype(BF16)
    mix = (jnp.dot(mla_ref[0], wa_ref[...], preferred_element_type=F32)
           + jnp.dot(ml, wb_ref[...], preferred_element_type=F32))
    x1 = x_ref[0] + gt_ref[0] * mix
    x1_ref[0] = x1
    h2 = _rms(x1) * g2_ref[...] * (1.0 + sc_ref[0]) + sh_ref[0]
    h2_ref[0] = h2
    qp = jnp.dot(h2.astype(BF16), wpq_ref[...], preferred_element_type=F32).astype(BF16)
    for hp in range(2 * PEER_HEADS):
        s_ref[0, hp] = lax.dot_general(sk_ref[hp], qp[:, hp * N_KEYS:(hp + 1) * N_KEYS],
                                       (((1,), (1,)), ((), ())), preferred_element_type=F32)


def _outproj(x, mla, hsum, cols, gm, wa, wb, gt1, g2, sc2, sh2, wpq, sk, tm):
    b, n, d = x.shape
    hw = mla.shape[2]
    nq = wpq.shape[1]
    mod = pl.BlockSpec((1, 1, d), lambda i, j: (i, 0, 0))
    return pl.pallas_call(
        _outproj_kernel,
        out_shape=(jax.ShapeDtypeStruct((b, n, d), F32), jax.ShapeDtypeStruct((b, n, d), F32),
                   jax.ShapeDtypeStruct((b, 2 * PEER_HEADS, N_KEYS, n), F32)),
        grid=(b, n // tm),
        in_specs=[pl.BlockSpec((1, tm, d), lambda i, j: (i, j, 0)),
                  pl.BlockSpec((1, tm, hw), lambda i, j: (i, j, 0)),
                  pl.BlockSpec((1, tm, hw), lambda i, j: (i, j, 0)),
                  pl.BlockSpec((1, tm, 512), lambda i, j: (i, j, C_O0 // 512)),
                  pl.BlockSpec((1, hw), lambda i, j: (0, 0)),
                  pl.BlockSpec((hw, d), lambda i, j: (0, 0)),
                  pl.BlockSpec((hw, d), lambda i, j: (0, 0)),
                  mod, pl.BlockSpec((1, d), lambda i, j: (0, 0)), mod, mod,
                  pl.BlockSpec((d, nq), lambda i, j: (0, 0)),
                  pl.BlockSpec((2 * PEER_HEADS, N_KEYS, N_KEYS), lambda i, j: (0, 0, 0))],
        out_specs=(pl.BlockSpec((1, tm, d), lambda i, j: (i, j, 0)),
                   pl.BlockSpec((1, tm, d), lambda i, j: (i, j, 0)),
                   pl.BlockSpec((1, 2 * PEER_HEADS, N_KEYS, tm), lambda i, j: (i, 0, 0, j))),
        compiler_params=_params(("parallel", "parallel")),
        name="outproj",
    )(x, mla, hsum, cols, gm, wa, wb, gt1, g2, sc2, sh2, wpq, sk)


def _topk_rows(s, k, rid=None):
    rows, t = s.shape
    if rid is None:
        rid = lax.broadcasted_iota(I32, (rows, t), 0)
    kid = lax.broadcasted_iota(I32, (k, t), 0)
    big = jnp.iinfo(jnp.int32).max

    def body(r, carry):
        s, vals, idxs = carry
        m = jnp.max(s, axis=0, keepdims=True)
        i = jnp.min(jnp.where(s == m, rid, big), axis=0, keepdims=True)
        vals = jnp.where(kid == r, m, vals)
        idxs = jnp.where(kid == r, i, idxs)
        s = jnp.where(rid == i, -jnp.inf, s)
        return s, vals, idxs

    _, vals, idxs = lax.fori_loop(0, k, body, (s, jnp.zeros((k, t), F32), jnp.zeros((k, t), I32)))
    return vals, idxs


def _pair_candidates(sv0, sv1):
    K, t = sv0.shape
    h = K // 2
    iid = lax.broadcasted_iota(I32, (h, t), 0)
    vals = [sv0[0:h] + sv1[0:1], sv0[h:K] + sv1[0:1]]
    ids = [iid * K, (iid + h) * K]
    for j in range(1, h):
        vals.append(sv0[0:h] + sv1[j:j + 1])
        ids.append(iid * K + j)
    vals.append(sv0[0:1] + sv1[h:K])
    ids.append(iid + h)
    return jnp.concatenate(vals, axis=0), jnp.concatenate(ids, axis=0)


def _route_kernel(s_ref, off_ref, par_ref, gw_ref, et_ref, gt_ref):
    K = PEER_TOPK

    def head(h, carry):
        sv0, si0 = _topk_rows(s_ref[0, 2 * h], K)
        sv1, si1 = _topk_rows(s_ref[0, 2 * h + 1], K)
        cand, cid = _pair_candidates(sv0, sv1)
        best, pos = _topk_rows(cand, K, cid)
        isel = pos >> 4
        jsel = pos & (K - 1)
        e0 = jnp.zeros_like(pos)
        e1 = jnp.zeros_like(pos)
        for i in range(K):
            e0 = jnp.where(isel == i, si0[i:i + 1, :], e0)
            e1 = jnp.where(jsel == i, si1[i:i + 1, :], e1)
        ex = jnp.exp(best - best[0:1, :])
        sl = pl.ds(pl.multiple_of(h * K, K), K)
        et_ref[sl, :] = e0 * N_KEYS + e1
        gt_ref[sl, :] = ex / jnp.sum(ex, axis=0, keepdims=True)
        return carry

    lax.fori_loop(0, PEER_HEADS, head, 0)
    e = et_ref[...].T
    off_ref[0] = (e >> 1) * SUBLANES
    par_ref[0] = (e & 1).astype(F32)
    gw_ref[0] = gt_ref[...].T


def _route(scores, tt):
    b, hp, nk, n = scores.shape
    nsel = PEER_HEADS * PEER_TOPK
    ospec = pl.BlockSpec((1, tt, nsel), lambda i, j: (i, j, 0))
    return pl.pallas_call(
        _route_kernel,
        out_shape=(jax.ShapeDtypeStruct((b, n, nsel), I32), jax.ShapeDtypeStruct((b, n, nsel), F32),
                   jax.ShapeDtypeStruct((b, n, nsel), F32)),
        grid=(b, n // tt),
        in_specs=[pl.BlockSpec((1, hp, nk, tt), lambda i, j: (i, 0, 0, j))],
        out_specs=(ospec, ospec, ospec),
        scratch_shapes=[pltpu.VMEM((nsel, tt), I32), pltpu.VMEM((nsel, tt), F32)],
        compiler_params=_params(("parallel", "parallel")),
        name="route",
    )(scores)


TILE_ROWS = 2 * SUBLANES
TOKENS_PER_ITER = 4


def _gathered_tiles(tab_ref, off_ref, t, nsel):
    return jnp.concatenate(
        [pltpu.bitcast(tab_ref[pl.ds(pl.multiple_of(off_ref[t, k], SUBLANES), SUBLANES), :], BF16)
         for k in range(nsel)], axis=0)


def _row_masks(nsel):
    width = nsel * TILE_ROWS
    lane = lax.broadcasted_iota(I32, (nsel, width), 1)
    spread = (lane // TILE_ROWS == lax.broadcasted_iota(I32, (nsel, width), 0)).astype(BF16)
    lane1 = lax.broadcasted_iota(I32, (1, width), 1)
    half = (lane1 % 2).astype(F32)
    lane8 = lax.broadcasted_iota(I32, (SUBLANES, width), 1)
    diag = ((lane8 % TILE_ROWS) // 2 == lax.broadcasted_iota(I32, (SUBLANES, width), 0)).astype(F32)
    return spread, half, diag


def _split_bf16(v):
    hi = v.astype(BF16)
    return hi, (v - hi.astype(F32)).astype(BF16)


def _peer_u_kernel(off_ref, h_ref, tab_ref, par_ref, g_ref, c_ref, r_ref):
    tb, nsel = g_ref.shape
    spread, half, diag = _row_masks(nsel)

    def one(t):
        x_hi, x_lo = _split_bf16(h_ref[t])
        res = lax.dot_general(jnp.concatenate([x_hi, x_lo], axis=0), _gathered_tiles(tab_ref, off_ref, t, nsel),
                              (((1,), (1,)), ((), ())), preferred_element_type=F32)
        r_ref[pl.ds(t, 1), :] = jnp.sum((res[:SUBLANES] + res[SUBLANES:]) * diag, axis=0, keepdims=True)

    def tok(i, carry):
        for u in range(TOKENS_PER_ITER):
            one(i * TOKENS_PER_ITER + u)
        return carry

    lax.fori_loop(0, tb // TOKENS_PER_ITER, tok, 0)
    mine = jnp.dot(par_ref[...].astype(BF16), spread, preferred_element_type=F32) == half
    r_hi, r_lo = _split_bf16(jnp.where(mine, r_ref[...], 0.0))
    dn = (((1,), (1,)), ((), ()))
    pre = (lax.dot_general(r_hi, spread, dn, preferred_element_type=F32)
           + lax.dot_general(r_lo, spread, dn, preferred_element_type=F32))
    c_ref[...] = g_ref[...] * (0.5 * pre * (1.0 + lax.erf(pre * (2.0 ** -0.5))))


def _smem_block(tb, nsel):
    return pl.BlockSpec((tb, nsel), lambda i: (i, 0), memory_space=pltpu.SMEM)


def _peer_u(off, h2, tab, par, gw, tb):
    t, nsel = gw.shape
    vspec = pl.BlockSpec((tb, nsel), lambda i: (i, 0))
    return pl.pallas_call(
        _peer_u_kernel,
        out_shape=jax.ShapeDtypeStruct((t, nsel), F32),
        grid=(t // tb,),
        in_specs=[_smem_block(tb, nsel),
                  pl.BlockSpec((tb, SUBLANES, LANES), lambda i: (i, 0, 0)),
                  pl.BlockSpec(tab.shape, lambda i: (0, 0), pipeline_mode=pl.Buffered(1)),
                  vspec, vspec],
        out_specs=vspec,
        scratch_shapes=[pltpu.VMEM((tb, nsel * TILE_ROWS), F32)],
        compiler_params=_params(("arbitrary",)),
        name="peer_u",
    )(off, h2, tab, par, gw)


def _peer_v_kernel(off_ref, c_ref, par_ref, tab_ref, x1_ref, gt_ref, o_ref, lhi_ref, llo_ref):
    tb, nsel = c_ref.shape
    spread, half, diag = _row_masks(nsel)
    c_hi, c_lo = _split_bf16(c_ref[...])
    mine = jnp.dot(par_ref[...].astype(BF16), spread, preferred_element_type=F32) == half
    lhi_ref[...] = jnp.where(mine, jnp.dot(c_hi, spread, preferred_element_type=F32), 0.0)
    llo_ref[...] = jnp.where(mine, jnp.dot(c_lo, spread, preferred_element_type=F32), 0.0)

    def one(t):
        left = jnp.concatenate([lhi_ref[pl.ds(t, 1), :] * diag, llo_ref[pl.ds(t, 1), :] * diag], axis=0).astype(BF16)
        acc = jnp.dot(left, _gathered_tiles(tab_ref, off_ref, t, nsel), preferred_element_type=F32)
        o_ref[t] = x1_ref[t] + gt_ref[0] * (acc[:SUBLANES] + acc[SUBLANES:])

    def tok(i, carry):
        for u in range(TOKENS_PER_ITER):
            one(i * TOKENS_PER_ITER + u)
        return carry

    lax.fori_loop(0, tb // TOKENS_PER_ITER, tok, 0)


def _peer_v(off, coef, par, tab, x1, gt2, tb, tok_per_batch):
    t, nsel = coef.shape
    bpb = tok_per_batch // tb
    vspec = pl.BlockSpec((tb, nsel), lambda i: (i, 0))
    return pl.pallas_call(
        _peer_v_kernel,
        out_shape=jax.ShapeDtypeStruct(x1.shape, F32),
        grid=(t // tb,),
        in_specs=[_smem_block(tb, nsel), vspec, vspec,
                  pl.BlockSpec(tab.shape, lambda i: (0, 0), pipeline_mode=pl.Buffered(1)),
                  pl.BlockSpec((tb, SUBLANES, LANES), lambda i: (i, 0, 0)),
                  pl.BlockSpec((1, SUBLANES, LANES), lambda i: (i // bpb, 0, 0))],
        out_specs=pl.BlockSpec((tb, SUBLANES, LANES), lambda i: (i, 0, 0)),
        scratch_shapes=[pltpu.VMEM((tb, nsel * TILE_ROWS), F32), pltpu.VMEM((tb, nsel * TILE_ROWS), F32)],
        compiler_params=_params(("arbitrary",)),
        name="peer_v",
    )(off, coef, par, tab, x1, gt2)


def _rope_tables(n):
    axis = MLA_ROPE // 2
    t = jnp.arange(n, dtype=F32)
    row = jnp.floor(t / GRID_W)
    col = t - row * GRID_W
    inv = ROPE_BASE ** (-jnp.arange(axis // 2, dtype=F32) * (2.0 / axis))
    ar = row[:, None] * inv
    ac = col[:, None] * inv
    cos = jnp.concatenate([jnp.cos(ar), jnp.cos(ar), jnp.cos(ac), jnp.cos(ac)], axis=1)
    sin = jnp.concatenate([-jnp.sin(ar), jnp.sin(ar), -jnp.sin(ac), jnp.sin(ac)], axis=1)
    return cos, sin


def _pack_table(tab):
    e, d = tab.shape
    pairs = tab.astype(BF16).reshape(e // 2, 2, d).transpose(0, 2, 1)
    return lax.bitcast_convert_type(pairs, jnp.uint32).reshape(e // 2 * (d // LANES), LANES)


def _block(n, want):
    return want if n % want == 0 else n


def kernel(x, c, ctx, c_ctx, w_ada, b_ada, g_norm1, w_in, g_cq, w_uq, g_ckv, w_ukv, g_qn, g_kn, conv_qk, b_igate, b_fgate, g_mlstm, w_out, g_norm2, w_pq, sub_keys, expert_u, expert_v):
    B, N, D = x.shape
    NC = ctx.shape[1]
    assert w_ada.shape[0] == 1, "one layer"
    assert N % 256 == 0 and NC % ML_CHUNK == 0 and D == SUBLANES * LANES
    q_rank = g_cq.shape[1]
    kv_rank = g_ckv.shape[1]
    mla_cols = q_rank + kv_rank + MLA_ROPE
    assert (q_rank, kv_rank) == (C_KV0 - C_Q0, C_KR0 - C_KV0)
    swap = jnp.arange(MLA_ROPE) ^ (MLA_ROPE // 4)

    cc = jnp.concatenate([c, c_ctx[None, :], jnp.zeros((16 - B - 1, D), F32)], axis=0)
    mod = _ada(cc, w_ada[0].astype(BF16), b_ada)
    sh1, sc1, gt1, sh2, sc2, gt2 = [mod[:, i * D:(i + 1) * D] for i in range(6)]
    lat = lambda m: m[:B].reshape(B, 1, D)
    ctxm = lambda m: jnp.broadcast_to(m[B:B + 1].reshape(1, 1, D), (B, 1, D))

    wi = w_in[0]
    n_qk = 2 * ML_HEADS * ML_QK
    n_v = ML_HEADS * ML_V
    n_g = 4 * ML_HEADS
    m0 = mla_cols
    w_cols = jnp.concatenate([
        wi[:, 0:mla_cols],
        wi[:, q_rank + kv_rank + swap],
        wi[:, m0:m0 + n_qk + 2 * n_v + n_g],
        jnp.zeros((D, C_END - C_G0 - n_g), F32)], axis=1).astype(BF16)
    assert w_cols.shape[1] == C_END
    w_gates_t = wi[:, m0 + n_qk + 2 * n_v:m0 + n_qk + 2 * n_v + n_g].T.astype(BF16)
    cols_l, gates_l = _inproj(x, g_norm1, lat(sc1), lat(sh1), w_cols, w_gates_t, _block(N, 512))
    cols_c, gates_c = _inproj(ctx, g_norm1, ctxm(sc1), ctxm(sh1), w_cols, w_gates_t, _block(NC, 512))

    wq = w_uq[0].reshape(q_rank, MLA_HEADS, MLA_QK)
    wuq = jnp.concatenate([wq, wq[:, :, MLA_NOPE + swap]], axis=2).reshape(q_rank, MLA_HEADS * 256).astype(BF16)
    wukv = w_ukv[0].astype(BF16)
    gqs, gks = g_qn[:, MLA_NOPE + swap], g_kn[:, MLA_NOPE + swap]
    cos_l, sin_l = _rope_tables(N)
    cos_c, sin_c = jnp.ones((NC, MLA_ROPE), F32), jnp.zeros((NC, MLA_ROPE), F32)
    q_l, k_l, v_l = _mla_prep(cols_l, g_cq, wuq, g_ckv, wukv, g_qn, gqs, g_kn, gks, cos_l, sin_l, _block(N, 512))
    _, k_c, v_c = _mla_prep(cols_c, g_cq, wuq, g_ckv, wukv, g_qn, gqs, g_kn, gks, cos_c, sin_c, _block(NC, 512))
    k_all = jnp.concatenate([k_c, k_l], axis=2)
    v_all = jnp.concatenate([v_c, v_l], axis=2)
    mla = _attention(q_l, k_all, v_all, 256)

    bias16 = jnp.concatenate([b_igate[0].reshape(-1), b_fgate[0].reshape(-1)])
    cw = conv_qk[0]
    pl_ = _ml_prep(cols_l, gates_l, cw, bias16[None, :], bias16[:, None], _block(N, 512))
    pc_ = _ml_prep(cols_c, gates_c, cw, bias16[None, :], bias16[:, None], _block(NC, 512))
    hsum = _ml_scan(*pl_, *pc_)

    wo = w_out[0].astype(BF16)
    hw = MLA_HEADS * MLA_V
    sk = sub_keys[0].reshape(2 * PEER_HEADS, N_KEYS, -1).astype(BF16)
    x1, h2, scores = _outproj(x, mla, hsum, cols_l, g_mlstm, wo[:hw], wo[hw:], lat(gt1), g_norm2, lat(sc2), lat(sh2),
                              w_pq[0].astype(BF16), sk, 256)
    T = B * N
    nsel = PEER_HEADS * PEER_TOPK
    off, par, gw = [a.reshape(T, nsel) for a in _route(scores, 256)]

    tb = 32
    coef = _peer_u(off, h2.reshape(T, SUBLANES, LANES), _pack_table(expert_u[0]), par, gw, tb)
    out = _peer_v(off, coef, par, _pack_table(expert_v[0]), x1.reshape(T, SUBLANES, LANES),
                  lat(gt2).reshape(B, SUBLANES, LANES), tb, N)
    return out.reshape(B, N, D)
```

```python
import jax
import jax.numpy as jnp
from jax import lax
from jax.experimental import pallas as pl
from jax.experimental.pallas import tpu as pltpu

F32 = jnp.float32
BF16 = jnp.bfloat16
I32 = jnp.int32
EPS = 1e-6

GRID_W = 64
MLA_HEADS = 4
MLA_NOPE = 128
MLA_ROPE = 64
MLA_V = 128
MLA_QK = MLA_NOPE + MLA_ROPE
ROPE_BASE = 10000.0
ML_HEADS = 4
ML_QK = 64
ML_V = 128
ML_CHUNK = 64
PEER_HEADS = 8
N_KEYS = 128
PEER_TOPK = 16

LANES = 128
SUBLANES = 8
VMEM_LIMIT_BYTES = 56 * 1024 * 1024

C_Q0, C_KV0, C_KR0, C_KRS0, C_QK0, C_V0, C_O0, C_G0, C_END = 0, 256, 384, 448, 512, 1024, 1536, 2048, 2176
HIGHEST = lax.Precision.HIGHEST


def _params(sem):
    return pltpu.CompilerParams(dimension_semantics=sem, vmem_limit_bytes=VMEM_LIMIT_BYTES)


def _rms(x):
    return x * lax.rsqrt(jnp.mean(x * x, axis=-1, keepdims=True) + EPS)


def _ada_kernel(c_ref, w_ref, b_ref, o_ref):
    c = c_ref[...]
    s = c * jax.nn.sigmoid(c)
    o_ref[...] = jnp.dot(s.astype(BF16), w_ref[...], preferred_element_type=F32) + b_ref[...]


def _ada(cc, w, b):
    rows, d = cc.shape
    n = w.shape[1]
    bn = n // 4
    return pl.pallas_call(
        _ada_kernel,
        out_shape=jax.ShapeDtypeStruct((rows, n), F32),
        grid=(4,),
        in_specs=[pl.BlockSpec((rows, d), lambda j: (0, 0)),
                  pl.BlockSpec((d, bn), lambda j: (0, j)),
                  pl.BlockSpec((1, bn), lambda j: (0, j))],
        out_specs=pl.BlockSpec((rows, bn), lambda j: (0, j)),
        compiler_params=_params(("arbitrary",)),
        name="ada",
    )(cc, w, b)


def _inproj_kernel(x_ref, g_ref, sc_ref, sh_ref, w_ref, wg_ref, o_ref, gt_ref):
    x = x_ref[0]
    h = _rms(x) * g_ref[...] * (1.0 + sc_ref[0]) + sh_ref[0]
    hb = h.astype(BF16)
    o_ref[0] = jnp.dot(hb, w_ref[...], preferred_element_type=F32)
    gt_ref[0] = lax.dot_general(wg_ref[...], hb, (((1,), (1,)), ((), ())), preferred_element_type=F32)


def _inproj(x, g, sc, sh, w, wg, tm):
    b, n, d = x.shape
    nc = w.shape[1]
    return pl.pallas_call(
        _inproj_kernel,
        out_shape=(jax.ShapeDtypeStruct((b, n, nc), F32), jax.ShapeDtypeStruct((b, 16, n), F32)),
        grid=(b, n // tm),
        in_specs=[pl.BlockSpec((1, tm, d), lambda i, j: (i, j, 0)),
                  pl.BlockSpec((1, d), lambda i, j: (0, 0)),
                  pl.BlockSpec((1, 1, d), lambda i, j: (i, 0, 0)),
                  pl.BlockSpec((1, 1, d), lambda i, j: (i, 0, 0)),
                  pl.BlockSpec((d, nc), lambda i, j: (0, 0)),
                  pl.BlockSpec((16, d), lambda i, j: (0, 0))],
        out_specs=(pl.BlockSpec((1, tm, nc), lambda i, j: (i, j, 0)),
                   pl.BlockSpec((1, 16, tm), lambda i, j: (i, 0, j))),
        compiler_params=_params(("parallel", "parallel")),
        name="inproj",
    )(x, g, sc, sh, w, wg)


def _mla_prep_kernel(c_ref, gcq_ref, wuq_ref, gckv_ref, wukv_ref, gq_ref, gqs_ref, gk_ref, gks_ref,
                     cos_ref, sin_ref, q_ref, k_ref, v_ref):
    c = c_ref[0]
    cq = _rms(c[:, C_Q0:C_KV0]) * gcq_ref[...]
    ckv = _rms(c[:, C_KV0:C_KR0]) * gckv_ref[...]
    kr = c[:, C_KR0:C_KRS0]
    krs = c[:, C_KRS0:C_QK0]
    q_raw = jnp.dot(cq.astype(BF16), wuq_ref[...], preferred_element_type=F32)
    kv_raw = jnp.dot(ckv.astype(BF16), wukv_ref[...], preferred_element_type=F32)
    cos = cos_ref[...]
    sin = sin_ref[...]
    gq = gq_ref[...]
    gk = gk_ref[...]
    kr_ss = jnp.sum(kr * kr, axis=-1, keepdims=True)
    for h in range(MLA_HEADS):
        o = h * 256
        qn = q_raw[:, o:o + 128]
        qr = q_raw[:, o + 128:o + 192]
        qs = q_raw[:, o + 192:o + 256]
        ss = jnp.sum(qn * qn, axis=-1, keepdims=True) + jnp.sum(qr * qr, axis=-1, keepdims=True)
        r = lax.rsqrt(ss * (1.0 / MLA_QK) + EPS) * (MLA_QK ** -0.5)
        q_ref[0, h, :, 0:128] = (qn * r * gq[:, 0:128]).astype(BF16)
        q_ref[0, h, :, 128:192] = ((qr * r * gq[:, 128:192]) * cos + (qs * r * gqs_ref[...]) * sin).astype(BF16)
        kn = kv_raw[:, o:o + 128]
        ss = jnp.sum(kn * kn, axis=-1, keepdims=True) + kr_ss
        r = lax.rsqrt(ss * (1.0 / MLA_QK) + EPS)
        k_ref[0, h, :, 0:128] = (kn * r * gk[:, 0:128]).astype(BF16)
        k_ref[0, h, :, 128:192] = ((kr * r * gk[:, 128:192]) * cos + (krs * r * gks_ref[...]) * sin).astype(BF16)
        v_ref[0, h] = kv_raw[:, o + 128:o + 256].astype(BF16)


def _mla_prep(cols, gcq, wuq, gckv, wukv, gq, gqs, gk, gks, cos, sin, tm):
    b, n, _ = cols.shape
    full = lambda a: pl.BlockSpec(a.shape, lambda i, j: (0,) * a.ndim)
    return pl.pallas_call(
        _mla_prep_kernel,
        out_shape=(jax.ShapeDtypeStruct((b, MLA_HEADS, n, MLA_QK), BF16),
                   jax.ShapeDtypeStruct((b, MLA_HEADS, n, MLA_QK), BF16),
                   jax.ShapeDtypeStruct((b, MLA_HEADS, n, MLA_V), BF16)),
        grid=(b, n // tm),
        in_specs=[pl.BlockSpec((1, tm, 512), lambda i, j: (i, j, 0)),
                  full(gcq), full(wuq), full(gckv), full(wukv), full(gq), full(gqs), full(gk), full(gks),
                  pl.BlockSpec((tm, MLA_ROPE), lambda i, j: (j, 0)),
                  pl.BlockSpec((tm, MLA_ROPE), lambda i, j: (j, 0))],
        out_specs=(pl.BlockSpec((1, MLA_HEADS, tm, MLA_QK), lambda i, j: (i, 0, j, 0)),
                   pl.BlockSpec((1, MLA_HEADS, tm, MLA_QK), lambda i, j: (i, 0, j, 0)),
                   pl.BlockSpec((1, MLA_HEADS, tm, MLA_V), lambda i, j: (i, 0, j, 0))),
        compiler_params=_params(("parallel", "parallel")),
        name="mla_prep",
    )(cols, gcq, wuq, gckv, wukv, gq, gqs, gk, gks, cos, sin)


ATTN_TILES_PER_STEP = 2


def _attn_kernel(q_ref, k_ref, v_ref, o_ref):
    tq = q_ref.shape[2] // ATTN_TILES_PER_STEP
    k = k_ref[0, 0]
    v = v_ref[0, 0]
    for j in range(ATTN_TILES_PER_STEP):
        q = q_ref[0, 0, j * tq:(j + 1) * tq, :]
        s = lax.dot_general(q, k, (((1,), (1,)), ((), ())), preferred_element_type=F32)
        m = jnp.max(s, axis=-1, keepdims=True)
        p = jnp.exp(s - m)
        l = jnp.sum(p, axis=-1, keepdims=True)
        o = jnp.dot(p.astype(BF16), v, preferred_element_type=F32)
        o_ref[0, j * tq:(j + 1) * tq, :] = (o / l).astype(BF16)


def _attention(q, k, v, tq):
    b, h, n, dk = q.shape
    nk = k.shape[2]
    dv = v.shape[3]
    tb = tq * ATTN_TILES_PER_STEP
    return pl.pallas_call(
        _attn_kernel,
        out_shape=jax.ShapeDtypeStruct((b, n, h * dv), BF16),
        grid=(b, h, n // tb),
        in_specs=[pl.BlockSpec((1, 1, tb, dk), lambda i, j, t: (i, j, t, 0)),
                  pl.BlockSpec((1, 1, nk, dk), lambda i, j, t: (i, j, 0, 0)),
                  pl.BlockSpec((1, 1, nk, dv), lambda i, j, t: (i, j, 0, 0))],
        out_specs=pl.BlockSpec((1, tb, dv), lambda i, j, t: (i, t, j)),
        compiler_params=_params(("parallel", "parallel", "parallel")),
        name="attention",
    )(q, k, v)


def _ml_prep_kernel(qk_ref, prev_ref, next_ref, vin_ref, g_ref, gt_ref, cw_ref, bc_ref, br_ref,
                    q_ref, kt_ref, v_ref, a_ref, brow_ref):
    tn = qk_ref.shape[1]
    j = pl.program_id(1)
    nj = pl.num_programs(1)
    u = qk_ref[0]
    row = lax.broadcasted_iota(I32, (tn, 1), 0)
    before = jnp.where(j == 0, 0.0, prev_ref[0, SUBLANES - 1:SUBLANES, :])
    after = jnp.where(j == nj - 1, 0.0, next_ref[0, 0:1, :])
    up = jnp.where(row == 0, before, pltpu.roll(u, 1, 0))
    un = jnp.where(row == tn - 1, after, pltpu.roll(u, tn - 1, 0))
    cw = cw_ref[...]
    y = up * cw[0:1, :] + u * cw[1:2, :] + un * cw[2:3, :]
    y = y * jax.nn.sigmoid(y)
    hq = ML_HEADS * ML_QK
    for h in range(ML_HEADS):
        q_ref[0, h] = (y[:, h * ML_QK:(h + 1) * ML_QK] * (ML_QK ** -0.5)).astype(BF16)
    kt = y[:, hq:2 * hq].T
    L = ML_CHUNK
    for c in range(tn // L):
        kt_ref[0, c] = kt[:, c * L:(c + 1) * L].astype(BF16)
    v_ref[0] = vin_ref[0].astype(BF16)
    g = g_ref[0][:, 0:16] + bc_ref[...]
    lf_c = jax.nn.log_sigmoid(g[:, 8:16])
    gt = gt_ref[0] + br_ref[...]
    ig_r = gt[0:8, :]
    lf_r = jax.nn.log_sigmoid(gt[8:16, :])
    ti = lax.broadcasted_iota(I32, (L, L), 0)
    si = lax.broadcasted_iota(I32, (L, L), 1)
    lower = (si <= ti).astype(F32)
    upper = (si >= ti).astype(F32)
    lane_fwd = lax.broadcasted_iota(I32, (L, 8), 1) < ML_HEADS
    row_fwd = lax.broadcasted_iota(I32, (8, L), 0) < ML_HEADS
    for c in range(tn // L):
        lo = c * L
        lfc = lf_c[lo:lo + L, :]
        a_ref[0, lo:lo + L, :] = jnp.where(
            lane_fwd,
            jnp.dot(lower, lfc, precision=HIGHEST, preferred_element_type=F32),
            jnp.dot(upper, lfc, precision=HIGHEST, preferred_element_type=F32))
        lfr = lf_r[:, lo:lo + L]
        cf_r = jnp.where(
            row_fwd,
            jnp.dot(lfr, upper, precision=HIGHEST, preferred_element_type=F32),
            jnp.dot(lfr, lower, precision=HIGHEST, preferred_element_type=F32))
        brow_ref[0, c] = ig_r[:, lo:lo + L] - cf_r


def _ml_prep(cols, gates_t, cw, bias_col, bias_row, tn):
    b, n, _ = cols.shape
    r8 = tn // SUBLANES
    last8 = n // SUBLANES - 1
    L = ML_CHUNK
    hq = ML_HEADS * ML_QK
    return pl.pallas_call(
        _ml_prep_kernel,
        out_shape=(jax.ShapeDtypeStruct((b, ML_HEADS, n, ML_QK), BF16),
                   jax.ShapeDtypeStruct((b, n // L, hq, L), BF16),
                   jax.ShapeDtypeStruct((b, n, ML_HEADS * ML_V), BF16),
                   jax.ShapeDtypeStruct((b, n, 8), F32),
                   jax.ShapeDtypeStruct((b, n // L, 8, L), F32)),
        grid=(b, n // tn),
        in_specs=[pl.BlockSpec((1, tn, 512), lambda i, j: (i, j, C_QK0 // 512)),
                  pl.BlockSpec((1, SUBLANES, 512), lambda i, j: (i, jnp.maximum(j * r8 - 1, 0), C_QK0 // 512)),
                  pl.BlockSpec((1, SUBLANES, 512), lambda i, j: (i, jnp.minimum((j + 1) * r8, last8), C_QK0 // 512)),
                  pl.BlockSpec((1, tn, 512), lambda i, j: (i, j, C_V0 // 512)),
                  pl.BlockSpec((1, tn, 128), lambda i, j: (i, j, C_G0 // 128)),
                  pl.BlockSpec((1, 16, tn), lambda i, j: (i, 0, j)),
                  pl.BlockSpec((3, 512), lambda i, j: (0, 0)),
                  pl.BlockSpec((1, 16), lambda i, j: (0, 0)),
                  pl.BlockSpec((16, 1), lambda i, j: (0, 0))],
        out_specs=(pl.BlockSpec((1, ML_HEADS, tn, ML_QK), lambda i, j: (i, 0, j, 0)),
                   pl.BlockSpec((1, tn // L, hq, L), lambda i, j: (i, j, 0, 0)),
                   pl.BlockSpec((1, tn, ML_HEADS * ML_V), lambda i, j: (i, j, 0)),
                   pl.BlockSpec((1, tn, 8), lambda i, j: (i, j, 0)),
                   pl.BlockSpec((1, tn // L, 8, L), lambda i, j: (i, j, 0, 0))),
        compiler_params=_params(("parallel", "parallel")),
        name="ml_prep",
    )(cols, cols, cols, cols, cols, gates_t, cw, bias_col, bias_row)


def _ml_chunk(qb, kt, vaug, a_c, b_r, ct, m_prev, fwd, need_out):
    L = qb.shape[0]
    ti = lax.broadcasted_iota(I32, (L, L), 0)
    si = lax.broadcasted_iota(I32, (L, L), 1)
    mask = (si <= ti) if fwd else (si >= ti)
    dmat = jnp.where(mask, a_c + b_r, -jnp.inf)
    mloc = jnp.max(dmat, axis=1, keepdims=True)
    inter = a_c + m_prev
    m_t = jnp.maximum(inter, mloc)
    last = L - 1 if fwd else 0
    m_new = m_t[last:last + 1, :]
    a_last = a_c[last:last + 1, :]
    w_r = jnp.exp(a_last + b_r - m_new)
    decay = jnp.exp(a_last + m_prev - m_new)
    kw = (kt.astype(F32) * w_r).astype(BF16)
    ct_new = decay * ct + jnp.dot(kw, vaug, preferred_element_type=F32)
    if not need_out:
        return ct_new, m_new, None
    s = jnp.dot(qb, kt, preferred_element_type=F32)
    wmat = jnp.exp(dmat - m_t) * s
    sc = jnp.exp(inter - m_t)
    nd = (jnp.dot(wmat.astype(BF16), vaug, preferred_element_type=F32)
          + sc * jnp.dot(qb, ct.astype(BF16), preferred_element_type=F32))
    num = nd[:, 0:ML_V]
    den = nd[:, ML_V:ML_V + 1]
    return ct_new, m_new, num / jnp.maximum(jnp.abs(den), jnp.exp(-m_t))


def _ml_scan_kernel(ql_ref, ktl_ref, vl_ref, al_ref, brl_ref, qc_ref, ktc_ref, vc_ref, ac_ref, brc_ref,
                    out_ref, st_ref, m_ref):
    L = ML_CHUNK
    ncl = ql_ref.shape[2] // L
    ncc = qc_ref.shape[2] // L
    ones_col = (lax.broadcasted_iota(I32, (L, ML_V), 1) == 0).astype(BF16)
    st_ref[...] = jnp.zeros(st_ref.shape, F32)
    m_ref[...] = jnp.zeros(m_ref.shape, F32)

    def step(refs, c, d, fwd, need_out):
        q_ref, kt_ref, v_ref, a_ref, br_ref = refs
        sl = pl.ds(pl.multiple_of(c * L, L), L)
        a_all = a_ref[0, sl, :]
        br_all = br_ref[0, c]
        v_all = v_ref[0, sl, :]
        kt_all = kt_ref[0, c]
        hs = []
        for h in range(ML_HEADS):
            j = d * ML_HEADS + h
            vaug = jnp.concatenate([v_all[:, h * ML_V:(h + 1) * ML_V], ones_col], axis=1)
            ct_new, m_new, hh = _ml_chunk(
                q_ref[0, h, sl, :], kt_all[h * ML_QK:(h + 1) * ML_QK, :], vaug,
                a_all[:, j:j + 1], br_all[j:j + 1, :], st_ref[j], m_ref[j][0:1, 0:1], fwd, need_out)
            st_ref[j] = ct_new
            m_ref[j] = jnp.broadcast_to(m_new, (SUBLANES, LANES))
            hs.append(hh)
        return hs, sl

    ctx_refs = (qc_ref, ktc_ref, vc_ref, ac_ref, brc_ref)
    lat_refs = (ql_ref, ktl_ref, vl_ref, al_ref, brl_ref)

    out_ref[...] = jnp.zeros(out_ref.shape, F32)

    def ctx_body(i, carry):
        step(ctx_refs, i, 0, True, False)
        step(ctx_refs, ncc - 1 - i, 1, False, False)
        return carry

    def lat_body(i, carry):
        for d, fwd in ((0, True), (1, False)):
            hs, sl = step(lat_refs, i if fwd else ncl - 1 - i, d, fwd, True)
            out_ref[0, sl, :] = out_ref[0, sl, :] + jnp.concatenate(hs, axis=1)
        return carry

    lax.fori_loop(0, ncc, ctx_body, 0)
    lax.fori_loop(0, ncl, lat_body, 0)


def _ml_scan(ql, ktl, vl, al, brl, qc, ktc, vc, ac, brc):
    b, _, n, _ = ql.shape
    nctx = qc.shape[2]
    L = ML_CHUNK
    hv = ML_HEADS * ML_V
    hq = ML_HEADS * ML_QK
    qspec = lambda nn: pl.BlockSpec((1, ML_HEADS, nn, ML_QK), lambda i: (i, 0, 0, 0))
    ktspec = lambda nn: pl.BlockSpec((1, nn // L, hq, L), lambda i: (i, 0, 0, 0))
    vspec = lambda nn: pl.BlockSpec((1, nn, hv), lambda i: (i, 0, 0))
    aspec = lambda nn: pl.BlockSpec((1, nn, 8), lambda i: (i, 0, 0))
    rspec = lambda nn: pl.BlockSpec((1, nn // L, 8, L), lambda i: (i, 0, 0, 0))
    return pl.pallas_call(
        _ml_scan_kernel,
        out_shape=jax.ShapeDtypeStruct((b, n, hv), F32),
        grid=(b,),
        in_specs=[qspec(n), ktspec(n), vspec(n), aspec(n), rspec(n),
                  qspec(nctx), ktspec(nctx), vspec(nctx), aspec(nctx), rspec(nctx)],
        out_specs=pl.BlockSpec((1, n, hv), lambda i: (i, 0, 0)),
        scratch_shapes=[pltpu.VMEM((2 * ML_HEADS, ML_QK, 2 * ML_V), F32),
                        pltpu.VMEM((2 * ML_HEADS, SUBLANES, LANES), F32)],
        compiler_params=_params(("parallel",)),
        name="ml_scan",
    )(ql, ktl, vl, al, brl, qc, ktc, vc, ac, brc)


def _outproj_kernel(x_ref, mla_ref, hs_ref, o_ref, gm_ref, wa_ref, wb_ref, gt_ref, g2_ref, sc_ref, sh_ref,
                    wpq_ref, sk_ref, x1_ref, h2_ref, s_ref):
    hs = hs_ref[0]
    gm = gm_ref[...]
    hn = jnp.concatenate([_rms(hs[:, h * ML_V:(h + 1) * ML_V]) * gm[:, h * ML_V:(h + 1) * ML_V]
                          for h in range(ML_HEADS)], axis=1)
    ml = (jax.nn.sigmoid(o_ref[0]) * hn).astype(BF16)
    mix = (jnp.dot(mla_ref[0], wa_ref[...], preferred_element_type=F32)
           + jnp.dot(ml, wb_ref[...], preferred_element_type=F32))
    x1 = x_ref[0] + gt_ref[0] * mix
    x1_ref[0] = x1
    h2 = _rms(x1) * g2_ref[...] * (1.0 + sc_ref[0]) + sh_ref[0]
    h2_ref[0] = h2
    qp = jnp.dot(h2.astype(BF16), wpq_ref[...], preferred_element_type=F32).astype(BF16)
    for hp in range(2 * PEER_HEADS):
        s_ref[0, hp] = lax.dot_general(sk_ref[hp], qp[:, hp * N_KEYS:(hp + 1) * N_KEYS],
                                       (((1,), (1,)), ((), ())), preferred_element_type=F32)


def _outproj(x, mla, hsum, cols, gm, wa, wb, gt1, g2, sc2, sh2, wpq, sk, tm):
    b, n, d = x.shape
    hw = mla.shape[2]
    nq = wpq.shape[1]
    mod = pl.BlockSpec((1, 1, d), lambda i, j: (i, 0, 0))
    return pl.pallas_call(
        _outproj_kernel,
        out_shape=(jax.ShapeDtypeStruct((b, n, d), F32), jax.ShapeDtypeStruct((b, n, d), F32),
                   jax.ShapeDtypeStruct((b, 2 * PEER_HEADS, N_KEYS, n), F32)),
        grid=(b, n // tm),
        in_specs=[pl.BlockSpec((1, tm, d), lambda i, j: (i, j, 0)),
                  pl.BlockSpec((1, tm, hw), lambda i, j: (i, j, 0)),
                  pl.BlockSpec((1, tm, hw), lambda i, j: (i, j, 0)),
                  pl.BlockSpec((1, tm, 512), lambda i, j: (i, j, C_O0 // 512)),
                  pl.BlockSpec((1, hw), lambda i, j: (0, 0)),
                  pl.BlockSpec((hw, d), lambda i, j: (0, 0)),
                  pl.BlockSpec((hw, d), lambda i, j: (0, 0)),
                  mod, pl.BlockSpec((1, d), lambda i, j: (0, 0)), mod, mod,
                  pl.BlockSpec((d, nq), lambda i, j: (0, 0)),
                  pl.BlockSpec((2 * PEER_HEADS, N_KEYS, N_KEYS), lambda i, j: (0, 0, 0))],
        out_specs=(pl.BlockSpec((1, tm, d), lambda i, j: (i, j, 0)),
                   pl.BlockSpec((1, tm, d), lambda i, j: (i, j, 0)),
                   pl.BlockSpec((1, 2 * PEER_HEADS, N_KEYS, tm), lambda i, j: (i, 0, 0, j))),
        compiler_params=_params(("parallel", "parallel")),
        name="outproj",
    )(x, mla, hsum, cols, gm, wa, wb, gt1, g2, sc2, sh2, wpq, sk)


def _topk_rows(s, k, rid=None):
    rows, t = s.shape
    if rid is None:
        rid = lax.broadcasted_iota(I32, (rows, t), 0)
    kid = lax.broadcasted_iota(I32, (k, t), 0)
    big = jnp.iinfo(jnp.int32).max

    def body(r, carry):
        s, vals, idxs = carry
        m = jnp.max(s, axis=0, keepdims=True)
        i = jnp.min(jnp.where(s == m, rid, big), axis=0, keepdims=True)
        vals = jnp.where(kid == r, m, vals)
        idxs = jnp.where(kid == r, i, idxs)
        s = jnp.where(rid == i, -jnp.inf, s)
        return s, vals, idxs

    _, vals, idxs = lax.fori_loop(0, k, body, (s, jnp.zeros((k, t), F32), jnp.zeros((k, t), I32)))
    return vals, idxs


def _pair_candidates(sv0, sv1):
    K, t = sv0.shape
    h = K // 2
    iid = lax.broadcasted_iota(I32, (h, t), 0)
    vals = [sv0[0:h] + sv1[0:1], sv0[h:K] + sv1[0:1]]
    ids = [iid * K, (iid + h) * K]
    for j in range(1, h):
        vals.append(sv0[0:h] + sv1[j:j + 1])
        ids.append(iid * K + j)
    vals.append(sv0[0:1] + sv1[h:K])
    ids.append(iid + h)
    return jnp.concatenate(vals, axis=0), jnp.concatenate(ids, axis=0)


def _route_kernel(s_ref, off_ref, par_ref, gw_ref, et_ref, gt_ref):
    K = PEER_TOPK

    def head(h, carry):
        sv0, si0 = _topk_rows(s_ref[0, 2 * h], K)
        sv1, si1 = _topk_rows(s_ref[0, 2 * h + 1], K)
        cand, cid = _pair_candidates(sv0, sv1)
        best, pos = _topk_rows(cand, K, cid)
        isel = pos >> 4
        jsel = pos & (K - 1)
        e0 = jnp.zeros_like(pos)
        e1 = jnp.zeros_like(pos)
        for i in range(K):
            e0 = jnp.where(isel == i, si0[i:i + 1, :], e0)
            e1 = jnp.where(jsel == i, si1[i:i + 1, :], e1)
        ex = jnp.exp(best - best[0:1, :])
        sl = pl.ds(pl.multiple_of(h * K, K), K)
        et_ref[sl, :] = e0 * N_KEYS + e1
        gt_ref[sl, :] = ex / jnp.sum(ex, axis=0, keepdims=True)
        return carry

    lax.fori_loop(0, PEER_HEADS, head, 0)
    e = et_ref[...].T
    off_ref[0] = (e >> 1) * SUBLANES
    par_ref[0] = (e & 1).astype(F32)
    gw_ref[0] = gt_ref[...].T


def _route(scores, tt):
    b, hp, nk, n = scores.shape
    nsel = PEER_HEADS * PEER_TOPK
    ospec = pl.BlockSpec((1, tt, nsel), lambda i, j: (i, j, 0))
    return pl.pallas_call(
        _route_kernel,
        out_shape=(jax.ShapeDtypeStruct((b, n, nsel), I32), jax.ShapeDtypeStruct((b, n, nsel), F32),
                   jax.ShapeDtypeStruct((b, n, nsel), F32)),
        grid=(b, n // tt),
        in_specs=[pl.BlockSpec((1, hp, nk, tt), lambda i, j: (i, 0, 0, j))],
        out_specs=(ospec, ospec, ospec),
        scratch_shapes=[pltpu.VMEM((nsel, tt), I32), pltpu.VMEM((nsel, tt), F32)],
        compiler_params=_params(("parallel", "parallel")),
        name="route",
    )(scores)


TILE_ROWS = 2 * SUBLANES
TOKENS_PER_ITER = 8


def _gathered_tiles(tab_ref, off_ref, t, nsel):
    return jnp.concatenate(
        [pltpu.bitcast(tab_ref[pl.ds(pl.multiple_of(off_ref[t, k], SUBLANES), SUBLANES), :], BF16)
         for k in range(nsel)], axis=0)


def _row_masks(nsel):
    width = nsel * TILE_ROWS
    lane = lax.broadcasted_iota(I32, (nsel, width), 1)
    spread = (lane // TILE_ROWS == lax.broadcasted_iota(I32, (nsel, width), 0)).astype(BF16)
    lane1 = lax.broadcasted_iota(I32, (1, width), 1)
    half = (lane1 % 2).astype(F32)
    lane8 = lax.broadcasted_iota(I32, (SUBLANES, width), 1)
    diag = ((lane8 % TILE_ROWS) // 2 == lax.broadcasted_iota(I32, (SUBLANES, width), 0)).astype(F32)
    return spread, half, diag


def _split_bf16(v):
    hi = v.astype(BF16)
    return hi, (v - hi.astype(F32)).astype(BF16)


def _peer_u_kernel(off_ref, h_ref, tab_ref, par_ref, g_ref, c_ref, r_ref):
    tb, nsel = g_ref.shape
    spread, half, diag = _row_masks(nsel)

    def one(t):
        x_hi, x_lo = _split_bf16(h_ref[t])
        res = lax.dot_general(jnp.concatenate([x_hi, x_lo], axis=0), _gathered_tiles(tab_ref, off_ref, t, nsel),
                              (((1,), (1,)), ((), ())), preferred_element_type=F32)
        r_ref[pl.ds(t, 1), :] = jnp.sum((res[:SUBLANES] + res[SUBLANES:]) * diag, axis=0, keepdims=True)

    def tok(i, carry):
        for u in range(TOKENS_PER_ITER):
            one(i * TOKENS_PER_ITER + u)
        return carry

    lax.fori_loop(0, tb // TOKENS_PER_ITER, tok, 0)
    mine = jnp.dot(par_ref[...].astype(BF16), spread, preferred_element_type=F32) == half
    r_hi, r_lo = _split_bf16(jnp.where(mine, r_ref[...], 0.0))
    dn = (((1,), (1,)), ((), ()))
    pre = (lax.dot_general(r_hi, spread, dn, preferred_element_type=F32)
           + lax.dot_general(r_lo, spread, dn, preferred_element_type=F32))
    c_ref[...] = g_ref[...] * (0.5 * pre * (1.0 + lax.erf(pre * (2.0 ** -0.5))))


def _smem_block(tb, nsel):
    return pl.BlockSpec((tb, nsel), lambda i: (i, 0), memory_space=pltpu.SMEM)


def _peer_u(off, h2, tab, par, gw, tb):
    t, nsel = gw.shape
    vspec = pl.BlockSpec((tb, nsel), lambda i: (i, 0))
    return pl.pallas_call(
        _peer_u_kernel,
        out_shape=jax.ShapeDtypeStruct((t, nsel), F32),
        grid=(t // tb,),
        in_specs=[_smem_block(tb, nsel),
                  pl.BlockSpec((tb, SUBLANES, LANES), lambda i: (i, 0, 0)),
                  pl.BlockSpec(tab.shape, lambda i: (0, 0), pipeline_mode=pl.Buffered(1)),
                  vspec, vspec],
        out_specs=vspec,
        scratch_shapes=[pltpu.VMEM((tb, nsel * TILE_ROWS), F32)],
        compiler_params=_params(("arbitrary",)),
        name="peer_u",
    )(off, h2, tab, par, gw)


def _peer_v_kernel(off_ref, c_ref, par_ref, tab_ref, x1_ref, gt_ref, o_ref, lhi_ref, llo_ref):
    tb, nsel = c_ref.shape
    spread, half, diag = _row_masks(nsel)
    c_hi, c_lo = _split_bf16(c_ref[...])
    mine = jnp.dot(par_ref[...].astype(BF16), spread, preferred_element_type=F32) == half
    lhi_ref[...] = jnp.where(mine, jnp.dot(c_hi, spread, preferred_element_type=F32), 0.0)
    llo_ref[...] = jnp.where(mine, jnp.dot(c_lo, spread, preferred_element_type=F32), 0.0)

    def one(t):
        left = jnp.concatenate([lhi_ref[pl.ds(t, 1), :] * diag, llo_ref[pl.ds(t, 1), :] * diag], axis=0).astype(BF16)
        acc = jnp.dot(left, _gathered_tiles(tab_ref, off_ref, t, nsel), preferred_element_type=F32)
        o_ref[t] = x1_ref[t] + gt_ref[0] * (acc[:SUBLANES] + acc[SUBLANES:])

    def tok(i, carry):
        for u in range(TOKENS_PER_ITER):
            one(i * TOKENS_PER_ITER + u)
        return carry

    lax.fori_loop(0, tb // TOKENS_PER_ITER, tok, 0)


def _peer_v(off, coef, par, tab, x1, gt2, tb, tok_per_batch):
    t, nsel = coef.shape
    bpb = tok_per_batch // tb
    vspec = pl.BlockSpec((tb, nsel), lambda i: (i, 0))
    return pl.pallas_call(
        _peer_v_kernel,
        out_shape=jax.ShapeDtypeStruct(x1.shape, F32),
        grid=(t // tb,),
        in_specs=[_smem_block(tb, nsel), vspec, vspec,
                  pl.BlockSpec(tab.shape, lambda i: (0, 0), pipeline_mode=pl.Buffered(1)),
                  pl.BlockSpec((tb, SUBLANES, LANES), lambda i: (i, 0, 0)),
                  pl.BlockSpec((1, SUBLANES, LANES), lambda i: (i // bpb, 0, 0))],
        out_specs=pl.BlockSpec((tb, SUBLANES, LANES), lambda i: (i, 0, 0)),
        scratch_shapes=[pltpu.VMEM((tb, nsel * TILE_ROWS), F32), pltpu.VMEM((tb, nsel * TILE_ROWS), F32)],
        compiler_params=_params(("arbitrary",)),
        name="peer_v",
    )(off, coef, par, tab, x1, gt2)


def _rope_tables(n):
    axis = MLA_ROPE // 2
    t = jnp.arange(n, dtype=F32)
    row = jnp.floor(t / GRID_W)
    col = t - row * GRID_W
    inv = ROPE_BASE ** (-jnp.arange(axis // 2, dtype=F32) * (2.0 / axis))
    ar = row[:, None] * inv
    ac = col[:, None] * inv
    cos = jnp.concatenate([jnp.cos(ar), jnp.cos(ar), jnp.cos(ac), jnp.cos(ac)], axis=1)
    sin = jnp.concatenate([-jnp.sin(ar), jnp.sin(ar), -jnp.sin(ac), jnp.sin(ac)], axis=1)
    return cos, sin


def _pack_table(tab):
    e, d = tab.shape
    pairs = tab.astype(BF16).reshape(e // 2, 2, d).transpose(0, 2, 1)
    return lax.bitcast_convert_type(pairs, jnp.uint32).reshape(e // 2 * (d // LANES), LANES)


def _block(n, want):
    return want if n % want == 0 else n


def kernel(x, c, ctx, c_ctx, w_ada, b_ada, g_norm1, w_in, g_cq, w_uq, g_ckv, w_ukv, g_qn, g_kn, conv_qk, b_igate, b_fgate, g_mlstm, w_out, g_norm2, w_pq, sub_keys, expert_u, expert_v):
    B, N, D = x.shape
    NC = ctx.shape[1]
    assert w_ada.shape[0] == 1, "one layer"
    assert N % 256 == 0 and NC % ML_CHUNK == 0 and D == SUBLANES * LANES
    q_rank = g_cq.shape[1]
    kv_rank = g_ckv.shape[1]
    mla_cols = q_rank + kv_rank + MLA_ROPE
    assert (q_rank, kv_rank) == (C_KV0 - C_Q0, C_KR0 - C_KV0)
    swap = jnp.arange(MLA_ROPE) ^ (MLA_ROPE // 4)

    cc = jnp.concatenate([c, c_ctx[None, :], jnp.zeros((16 - B - 1, D), F32)], axis=0)
    mod = _ada(cc, w_ada[0].astype(BF16), b_ada)
    sh1, sc1, gt1, sh2, sc2, gt2 = [mod[:, i * D:(i + 1) * D] for i in range(6)]
    lat = lambda m: m[:B].reshape(B, 1, D)
    ctxm = lambda m: jnp.broadcast_to(m[B:B + 1].reshape(1, 1, D), (B, 1, D))

    wi = w_in[0]
    n_qk = 2 * ML_HEADS * ML_QK
    n_v = ML_HEADS * ML_V
    n_g = 4 * ML_HEADS
    m0 = mla_cols
    w_cols = jnp.concatenate([
        wi[:, 0:mla_cols],
        wi[:, q_rank + kv_rank + swap],
        wi[:, m0:m0 + n_qk + 2 * n_v + n_g],
        jnp.zeros((D, C_END - C_G0 - n_g), F32)], axis=1).astype(BF16)
    assert w_cols.shape[1] == C_END
    w_gates_t = wi[:, m0 + n_qk + 2 * n_v:m0 + n_qk + 2 * n_v + n_g].T.astype(BF16)
    cols_l, gates_l = _inproj(x, g_norm1, lat(sc1), lat(sh1), w_cols, w_gates_t, _block(N, 512))
    cols_c, gates_c = _inproj(ctx, g_norm1, ctxm(sc1), ctxm(sh1), w_cols, w_gates_t, _block(NC, 512))

    wq = w_uq[0].reshape(q_rank, MLA_HEADS, MLA_QK)
    wuq = jnp.concatenate([wq, wq[:, :, MLA_NOPE + swap]], axis=2).reshape(q_rank, MLA_HEADS * 256).astype(BF16)
    wukv = w_ukv[0].astype(BF16)
    gqs, gks = g_qn[:, MLA_NOPE + swap], g_kn[:, MLA_NOPE + swap]
    cos_l, sin_l = _rope_tables(N)
    cos_c, sin_c = jnp.ones((NC, MLA_ROPE), F32), jnp.zeros((NC, MLA_ROPE), F32)
    q_l, k_l, v_l = _mla_prep(cols_l, g_cq, wuq, g_ckv, wukv, g_qn, gqs, g_kn, gks, cos_l, sin_l, _block(N, 512))
    _, k_c, v_c = _mla_prep(cols_c, g_cq, wuq, g_ckv, wukv, g_qn, gqs, g_kn, gks, cos_c, sin_c, _block(NC, 512))
    k_all = jnp.concatenate([k_c, k_l], axis=2)
    v_all = jnp.concatenate([v_c, v_l], axis=2)
    mla = _attention(q_l, k_all, v_all, 256)

    bias16 = jnp.concatenate([b_igate[0].reshape(-1), b_fgate[0].reshape(-1)])
    cw = conv_qk[0]
    pl_ = _ml_prep(cols_l, gates_l, cw, bias16[None, :], bias16[:, None], _block(N, 512))
    pc_ = _ml_prep(cols_c, gates_c, cw, bias16[None, :], bias16[:, None], _block(NC, 512))
    hsum = _ml_scan(*pl_, *pc_)

    wo = w_out[0].astype(BF16)
    hw = MLA_HEADS * MLA_V
    sk = sub_keys[0].reshape(2 * PEER_HEADS, N_KEYS, -1).astype(BF16)
    x1, h2, scores = _outproj(x, mla, hsum, cols_l, g_mlstm, wo[:hw], wo[hw:], lat(gt1), g_norm2, lat(sc2), lat(sh2),
                              w_pq[0].astype(BF16), sk, 256)
    T = B * N
    nsel = PEER_HEADS * PEER_TOPK
    off, par, gw = [a.reshape(T, nsel) for a in _route(scores, 256)]

    tb = 32
    coef = _peer_u(off, h2.reshape(T, SUBLANES, LANES), _pack_table(expert_u[0]), par, gw, tb)
    out = _peer_v(off, coef, par, _pack_table(expert_v[0]), x1.reshape(T, SUBLANES, LANES),
                  lat(gt2).reshape(B, SUBLANES, LANES), tb, N)
    return out.reshape(B, N, D)
```

```python
import jax
import jax.numpy as jnp
from jax import lax
from jax.experimental import pallas as pl
from jax.experimental.pallas import tpu as pltpu

F32 = jnp.float32
BF16 = jnp.bfloat16
I32 = jnp.int32
EPS = 1e-6

GRID_W = 64
MLA_HEADS = 4
MLA_NOPE = 128
MLA_ROPE = 64
MLA_V = 128
MLA_QK = MLA_NOPE + MLA_ROPE
ROPE_BASE = 10000.0
ML_HEADS = 4
ML_QK = 64
ML_V = 128
ML_CHUNK = 64
PEER_HEADS = 8
N_KEYS = 128
PEER_TOPK = 16

LANES = 128
SUBLANES = 8
VMEM_LIMIT_BYTES = 56 * 1024 * 1024

C_Q0, C_KV0, C_KR0, C_KRS0, C_QK0, C_V0, C_O0, C_G0, C_END = 0, 256, 384, 448, 512, 1024, 1536, 2048, 2176
HIGHEST = lax.Precision.HIGHEST


def _params(sem):
    return pltpu.CompilerParams(dimension_semantics=sem, vmem_limit_bytes=VMEM_LIMIT_BYTES)


def _rms(x):
    return x * lax.rsqrt(jnp.mean(x * x, axis=-1, keepdims=True) + EPS)


def _ada_kernel(c_ref, w_ref, b_ref, o_ref):
    c = c_ref[...]
    s = c * jax.nn.sigmoid(c)
    o_ref[...] = jnp.dot(s.astype(BF16), w_ref[...], preferred_element_type=F32) + b_ref[...]


def _ada(cc, w, b):
    rows, d = cc.shape
    n = w.shape[1]
    bn = n // 4
    return pl.pallas_call(
        _ada_kernel,
        out_shape=jax.ShapeDtypeStruct((rows, n), F32),
        grid=(4,),
        in_specs=[pl.BlockSpec((rows, d), lambda j: (0, 0)),
                  pl.BlockSpec((d, bn), lambda j: (0, j)),
                  pl.BlockSpec((1, bn), lambda j: (0, j))],
        out_specs=pl.BlockSpec((rows, bn), lambda j: (0, j)),
        compiler_params=_params(("arbitrary",)),
        name="ada",
    )(cc, w, b)


def _inproj_kernel(x_ref, g_ref, sc_ref, sh_ref, w_ref, wg_ref, o_ref, gt_ref):
    x = x_ref[0]
    h = _rms(x) * g_ref[...] * (1.0 + sc_ref[0]) + sh_ref[0]
    hb = h.astype(BF16)
    o_ref[0] = jnp.dot(hb, w_ref[...], preferred_element_type=F32)
    gt_ref[0] = lax.dot_general(wg_ref[...], hb, (((1,), (1,)), ((), ())), preferred_element_type=F32)


def _inproj(x, g, sc, sh, w, wg, tm):
    b, n, d = x.shape
    nc = w.shape[1]
    return pl.pallas_call(
        _inproj_kernel,
        out_shape=(jax.ShapeDtypeStruct((b, n, nc), F32), jax.ShapeDtypeStruct((b, 16, n), F32)),
        grid=(b, n // tm),
        in_specs=[pl.BlockSpec((1, tm, d), lambda i, j: (i, j, 0)),
                  pl.BlockSpec((1, d), lambda i, j: (0, 0)),
                  pl.BlockSpec((1, 1, d), lambda i, j: (i, 0, 0)),
                  pl.BlockSpec((1, 1, d), lambda i, j: (i, 0, 0)),
                  pl.BlockSpec((d, nc), lambda i, j: (0, 0)),
                  pl.BlockSpec((16, d), lambda i, j: (0, 0))],
        out_specs=(pl.BlockSpec((1, tm, nc), lambda i, j: (i, j, 0)),
                   pl.BlockSpec((1, 16, tm), lambda i, j: (i, 0, j))),
        compiler_params=_params(("parallel", "parallel")),
        name="inproj",
    )(x, g, sc, sh, w, wg)


def _mla_prep_kernel(c_ref, gcq_ref, wuq_ref, gckv_ref, wukv_ref, gq_ref, gqs_ref, gk_ref, gks_ref,
                     cos_ref, sin_ref, q_ref, k_ref, v_ref):
    c = c_ref[0]
    cq = _rms(c[:, C_Q0:C_KV0]) * gcq_ref[...]
    ckv = _rms(c[:, C_KV0:C_KR0]) * gckv_ref[...]
    kr = c[:, C_KR0:C_KRS0]
    krs = c[:, C_KRS0:C_QK0]
    q_raw = jnp.dot(cq.astype(BF16), wuq_ref[...], preferred_element_type=F32)
    kv_raw = jnp.dot(ckv.astype(BF16), wukv_ref[...], preferred_element_type=F32)
    cos = cos_ref[...]
    sin = sin_ref[...]
    gq = gq_ref[...]
    gk = gk_ref[...]
    kr_ss = jnp.sum(kr * kr, axis=-1, keepdims=True)
    for h in range(MLA_HEADS):
        o = h * 256
        qn = q_raw[:, o:o + 128]
        qr = q_raw[:, o + 128:o + 192]
        qs = q_raw[:, o + 192:o + 256]
        ss = jnp.sum(qn * qn, axis=-1, keepdims=True) + jnp.sum(qr * qr, axis=-1, keepdims=True)
        r = lax.rsqrt(ss * (1.0 / MLA_QK) + EPS) * (MLA_QK ** -0.5)
        q_ref[0, h, :, 0:128] = (qn * r * gq[:, 0:128]).astype(BF16)
        q_ref[0, h, :, 128:192] = ((qr * r * gq[:, 128:192]) * cos + (qs * r * gqs_ref[...]) * sin).astype(BF16)
        kn = kv_raw[:, o:o + 128]
        ss = jnp.sum(kn * kn, axis=-1, keepdims=True) + kr_ss
        r = lax.rsqrt(ss * (1.0 / MLA_QK) + EPS)
        k_ref[0, h, :, 0:128] = (kn * r * gk[:, 0:128]).astype(BF16)
        k_ref[0, h, :, 128:192] = ((kr * r * gk[:, 128:192]) * cos + (krs * r * gks_ref[...]) * sin).astype(BF16)
        v_ref[0, h] = kv_raw[:, o + 128:o + 256].astype(BF16)


def _mla_prep(cols, gcq, wuq, gckv, wukv, gq, gqs, gk, gks, cos, sin, tm):
    b, n, _ = cols.shape
    full = lambda a: pl.BlockSpec(a.shape, lambda i, j: (0,) * a.ndim)
    return pl.pallas_call(
        _mla_prep_kernel,
        out_shape=(jax.ShapeDtypeStruct((b, MLA_HEADS, n, MLA_QK), BF16),
                   jax.ShapeDtypeStruct((b, MLA_HEADS, n, MLA_QK), BF16),
                   jax.ShapeDtypeStruct((b, MLA_HEADS, n, MLA_V), BF16)),
        grid=(b, n // tm),
        in_specs=[pl.BlockSpec((1, tm, 512), lambda i, j: (i, j, 0)),
                  full(gcq), full(wuq), full(gckv), full(wukv), full(gq), full(gqs), full(gk), full(gks),
                  pl.BlockSpec((tm, MLA_ROPE), lambda i, j: (j, 0)),
                  pl.BlockSpec((tm, MLA_ROPE), lambda i, j: (j, 0))],
        out_specs=(pl.BlockSpec((1, MLA_HEADS, tm, MLA_QK), lambda i, j: (i, 0, j, 0)),
                   pl.BlockSpec((1, MLA_HEADS, tm, MLA_QK), lambda i, j: (i, 0, j, 0)),
                   pl.BlockSpec((1, MLA_HEADS, tm, MLA_V), lambda i, j: (i, 0, j, 0))),
        compiler_params=_params(("parallel", "parallel")),
        name="mla_prep",
    )(cols, gcq, wuq, gckv, wukv, gq, gqs, gk, gks, cos, sin)


ATTN_TILES_PER_STEP = 2


def _attn_kernel(q_ref, k_ref, v_ref, o_ref):
    tq = q_ref.shape[2] // ATTN_TILES_PER_STEP
    k = k_ref[0, 0]
    v = v_ref[0, 0]
    for j in range(ATTN_TILES_PER_STEP):
        q = q_ref[0, 0, j * tq:(j + 1) * tq, :]
        s = lax.dot_general(q, k, (((1,), (1,)), ((), ())), preferred_element_type=F32)
        m = jnp.max(s, axis=-1, keepdims=True)
        p = jnp.exp(s - m)
        l = jnp.sum(p, axis=-1, keepdims=True)
        o = jnp.dot(p.astype(BF16), v, preferred_element_type=F32)
        o_ref[0, j * tq:(j + 1) * tq, :] = (o / l).astype(BF16)


def _attention(q, k, v, tq):
    b, h, n, dk = q.shape
    nk = k.shape[2]
    dv = v.shape[3]
    tb = tq * ATTN_TILES_PER_STEP
    return pl.pallas_call(
        _attn_kernel,
        out_shape=jax.ShapeDtypeStruct((b, n, h * dv), BF16),
        grid=(b, h, n // tb),
        in_specs=[pl.BlockSpec((1, 1, tb, dk), lambda i, j, t: (i, j, t, 0)),
                  pl.BlockSpec((1, 1, nk, dk), lambda i, j, t: (i, j, 0, 0)),
                  pl.BlockSpec((1, 1, nk, dv), lambda i, j, t: (i, j, 0, 0))],
        out_specs=pl.BlockSpec((1, tb, dv), lambda i, j, t: (i, t, j)),
        compiler_params=_params(("parallel", "parallel", "parallel")),
        name="attention",
    )(q, k, v)


def _ml_prep_kernel(qk_ref, prev_ref, next_ref, vin_ref, g_ref, gt_ref, cw_ref, bc_ref, br_ref,
                    q_ref, kt_ref, v_ref, a_ref, brow_ref):
    tn = qk_ref.shape[1]
    j = pl.program_id(1)
    nj = pl.num_programs(1)
    u = qk_ref[0]
    row = lax.broadcasted_iota(I32, (tn, 1), 0)
    before = jnp.where(j == 0, 0.0, prev_ref[0, SUBLANES - 1:SUBLANES, :])
    after = jnp.where(j == nj - 1, 0.0, next_ref[0, 0:1, :])
    up = jnp.where(row == 0, before, pltpu.roll(u, 1, 0))
    un = jnp.where(row == tn - 1, after, pltpu.roll(u, tn - 1, 0))
    cw = cw_ref[...]
    y = up * cw[0:1, :] + u * cw[1:2, :] + un * cw[2:3, :]
    y = y * jax.nn.sigmoid(y)
    hq = ML_HEADS * ML_QK
    for h in range(ML_HEADS):
        q_ref[0, h] = (y[:, h * ML_QK:(h + 1) * ML_QK] * (ML_QK ** -0.5)).astype(BF16)
    kt = y[:, hq:2 * hq].T
    L = ML_CHUNK
    for c in range(tn // L):
        kt_ref[0, c] = kt[:, c * L:(c + 1) * L].astype(BF16)
    v_ref[0] = vin_ref[0].astype(BF16)
    g = g_ref[0][:, 0:16] + bc_ref[...]
    lf_c = jax.nn.log_sigmoid(g[:, 8:16])
    gt = gt_ref[0] + br_ref[...]
    ig_r = gt[0:8, :]
    lf_r = jax.nn.log_sigmoid(gt[8:16, :])
    ti = lax.broadcasted_iota(I32, (L, L), 0)
    si = lax.broadcasted_iota(I32, (L, L), 1)
    lower = (si <= ti).astype(F32)
    upper = (si >= ti).astype(F32)
    lane_fwd = lax.broadcasted_iota(I32, (L, 8), 1) < ML_HEADS
    row_fwd = lax.broadcasted_iota(I32, (8, L), 0) < ML_HEADS
    for c in range(tn // L):
        lo = c * L
        lfc = lf_c[lo:lo + L, :]
        a_ref[0, lo:lo + L, :] = jnp.where(
            lane_fwd,
            jnp.dot(lower, lfc, precision=HIGHEST, preferred_element_type=F32),
            jnp.dot(upper, lfc, precision=HIGHEST, preferred_element_type=F32))
        lfr = lf_r[:, lo:lo + L]
        cf_r = jnp.where(
            row_fwd,
            jnp.dot(lfr, upper, precision=HIGHEST, preferred_element_type=F32),
            jnp.dot(lfr, lower, precision=HIGHEST, preferred_element_type=F32))
        brow_ref[0, c] = ig_r[:, lo:lo + L] - cf_r


def _ml_prep(cols, gates_t, cw, bias_col, bias_row, tn):
    b, n, _ = cols.shape
    r8 = tn // SUBLANES
    last8 = n // SUBLANES - 1
    L = ML_CHUNK
    hq = ML_HEADS * ML_QK
    return pl.pallas_call(
        _ml_prep_kernel,
        out_shape=(jax.ShapeDtypeStruct((b, ML_HEADS, n, ML_QK), BF16),
                   jax.ShapeDtypeStruct((b, n // L, hq, L), BF16),
                   jax.ShapeDtypeStruct((b, n, ML_HEADS * ML_V), BF16),
                   jax.ShapeDtypeStruct((b, n, 8), F32),
                   jax.ShapeDtypeStruct((b, n // L, 8, L), F32)),
        grid=(b, n // tn),
        in_specs=[pl.BlockSpec((1, tn, 512), lambda i, j: (i, j, C_QK0 // 512)),
                  pl.BlockSpec((1, SUBLANES, 512), lambda i, j: (i, jnp.maximum(j * r8 - 1, 0), C_QK0 // 512)),
                  pl.BlockSpec((1, SUBLANES, 512), lambda i, j: (i, jnp.minimum((j + 1) * r8, last8), C_QK0 // 512)),
                  pl.BlockSpec((1, tn, 512), lambda i, j: (i, j, C_V0 // 512)),
                  pl.BlockSpec((1, tn, 128), lambda i, j: (i, j, C_G0 // 128)),
                  pl.BlockSpec((1, 16, tn), lambda i, j: (i, 0, j)),
                  pl.BlockSpec((3, 512), lambda i, j: (0, 0)),
                  pl.BlockSpec((1, 16), lambda i, j: (0, 0)),
                  pl.BlockSpec((16, 1), lambda i, j: (0, 0))],
        out_specs=(pl.BlockSpec((1, ML_HEADS, tn, ML_QK), lambda i, j: (i, 0, j, 0)),
                   pl.BlockSpec((1, tn // L, hq, L), lambda i, j: (i, j, 0, 0)),
                   pl.BlockSpec((1, tn, ML_HEADS * ML_V), lambda i, j: (i, j, 0)),
                   pl.BlockSpec((1, tn, 8), lambda i, j: (i, j, 0)),
                   pl.BlockSpec((1, tn // L, 8, L), lambda i, j: (i, j, 0, 0))),
        compiler_params=_params(("parallel", "parallel")),
        name="ml_prep",
    )(cols, cols, cols, cols, cols, gates_t, cw, bias_col, bias_row)


def _ml_chunk(qb, kt, vaug, a_c, b_r, ct, m_prev, fwd, need_out):
    L = qb.shape[0]
    ti = lax.broadcasted_iota(I32, (L, L), 0)
    si = lax.broadcasted_iota(I32, (L, L), 1)
    mask = (si <= ti) if fwd else (si >= ti)
    dmat = jnp.where(mask, a_c + b_r, -jnp.inf)
    mloc = jnp.max(dmat, axis=1, keepdims=True)
    inter = a_c + m_prev
    m_t = jnp.maximum(inter, mloc)
    last = L - 1 if fwd else 0
    m_new = m_t[last:last + 1, :]
    a_last = a_c[last:last + 1, :]
    w_r = jnp.exp(a_last + b_r - m_new)
    decay = jnp.exp(a_last + m_prev - m_new)
    kw = (kt.astype(F32) * w_r).astype(BF16)
    if not need_out:
        return decay * ct + jnp.dot(kw, vaug, preferred_element_type=F32), m_new, None
    s = jnp.dot(qb, kt, preferred_element_type=F32)
    wmat = jnp.exp(dmat - m_t) * s
    sc = jnp.exp(inter - m_t)
    top = jnp.concatenate([wmat.astype(BF16), (sc * qb.astype(F32)).astype(BF16)], axis=1)
    bot = jnp.concatenate([kw, jnp.zeros(kw.shape, BF16)], axis=1)
    res = jnp.dot(jnp.concatenate([top, bot], axis=0), jnp.concatenate([vaug, ct.astype(BF16)], axis=0),
                  preferred_element_type=F32)
    nd = res[:L]
    num = nd[:, 0:ML_V]
    den = nd[:, ML_V:ML_V + 1]
    return decay * ct + res[L:], m_new, num / jnp.maximum(jnp.abs(den), jnp.exp(-m_t))


def _ml_scan_kernel(ql_ref, ktl_ref, vl_ref, al_ref, brl_ref, qc_ref, ktc_ref, vc_ref, ac_ref, brc_ref,
                    out_ref, st_ref, m_ref):
    L = ML_CHUNK
    ncl = ql_ref.shape[2] // L
    ncc = qc_ref.shape[2] // L
    ones_col = (lax.broadcasted_iota(I32, (L, ML_V), 1) == 0).astype(BF16)
    st_ref[...] = jnp.zeros(st_ref.shape, F32)
    m_ref[...] = jnp.zeros(m_ref.shape, F32)

    def step(refs, c, d, fwd, need_out):
        q_ref, kt_ref, v_ref, a_ref, br_ref = refs
        sl = pl.ds(pl.multiple_of(c * L, L), L)
        a_all = a_ref[0, sl, :]
        br_all = br_ref[0, c]
        v_all = v_ref[0, sl, :]
        kt_all = kt_ref[0, c]
        hs = []
        for h in range(ML_HEADS):
            j = d * ML_HEADS + h
            vaug = jnp.concatenate([v_all[:, h * ML_V:(h + 1) * ML_V], ones_col], axis=1)
            ct_new, m_new, hh = _ml_chunk(
                q_ref[0, h, sl, :], kt_all[h * ML_QK:(h + 1) * ML_QK, :], vaug,
                a_all[:, j:j + 1], br_all[j:j + 1, :], st_ref[j], m_ref[j][0:1, 0:1], fwd, need_out)
            st_ref[j] = ct_new
            m_ref[j] = jnp.broadcast_to(m_new, (SUBLANES, LANES))
            hs.append(hh)
        return hs, sl

    ctx_refs = (qc_ref, ktc_ref, vc_ref, ac_ref, brc_ref)
    lat_refs = (ql_ref, ktl_ref, vl_ref, al_ref, brl_ref)

    out_ref[...] = jnp.zeros(out_ref.shape, F32)

    def ctx_body(i, carry):
        step(ctx_refs, i, 0, True, False)
        step(ctx_refs, ncc - 1 - i, 1, False, False)
        return carry

    def lat_body(i, carry):
        for d, fwd in ((0, True), (1, False)):
            hs, sl = step(lat_refs, i if fwd else ncl - 1 - i, d, fwd, True)
            out_ref[0, sl, :] = out_ref[0, sl, :] + jnp.concatenate(hs, axis=1)
        return carry

    lax.fori_loop(0, ncc, ctx_body, 0)
    lax.fori_loop(0, ncl, lat_body, 0)


def _ml_scan(ql, ktl, vl, al, brl, qc, ktc, vc, ac, brc):
    b, _, n, _ = ql.shape
    nctx = qc.shape[2]
    L = ML_CHUNK
    hv = ML_HEADS * ML_V
    hq = ML_HEADS * ML_QK
    qspec = lambda nn: pl.BlockSpec((1, ML_HEADS, nn, ML_QK), lambda i: (i, 0, 0, 0))
    ktspec = lambda nn: pl.BlockSpec((1, nn // L, hq, L), lambda i: (i, 0, 0, 0))
    vspec = lambda nn: pl.BlockSpec((1, nn, hv), lambda i: (i, 0, 0))
    aspec = lambda nn: pl.BlockSpec((1, nn, 8), lambda i: (i, 0, 0))
    rspec = lambda nn: pl.BlockSpec((1, nn // L, 8, L), lambda i: (i, 0, 0, 0))
    return pl.pallas_call(
        _ml_scan_kernel,
        out_shape=jax.ShapeDtypeStruct((b, n, hv), F32),
        grid=(b,),
        in_specs=[qspec(n), ktspec(n), vspec(n), aspec(n), rspec(n),
                  qspec(nctx), ktspec(nctx), vspec(nctx), aspec(nctx), rspec(nctx)],
        out_specs=pl.BlockSpec((1, n, hv), lambda i: (i, 0, 0)),
        scratch_shapes=[pltpu.VMEM((2 * ML_HEADS, ML_QK, 2 * ML_V), F32),
                        pltpu.VMEM((2 * ML_HEADS, SUBLANES, LANES), F32)],
        compiler_params=_params(("parallel",)),
        name="ml_scan",
    )(ql, ktl, vl, al, brl, qc, ktc, vc, ac, brc)


def _outproj_kernel(x_ref, mla_ref, hs_ref, o_ref, gm_ref, wa_ref, wb_ref, gt_ref, g2_ref, sc_ref, sh_ref,
                    wpq_ref, sk_ref, x1_ref, h2_ref, s_ref):
    hs = hs_ref[0]
    gm = gm_ref[...]
    hn = jnp.concatenate([_rms(hs[:, h * ML_V:(h + 1) * ML_V]) * gm[:, h * ML_V:(h + 1) * ML_V]
                          for h in range(ML_HEADS)], axis=1)
    ml = (jax.nn.sigmoid(o_ref[0]) * hn).astype(BF16)
    mix = (jnp.dot(mla_ref[0], wa_ref[...], preferred_element_type=F32)
           + jnp.dot(ml, wb_ref[...], preferred_element_type=F32))
    x1 = x_ref[0] + gt_ref[0] * mix
    x1_ref[0] = x1
    h2 = _rms(x1) * g2_ref[...] * (1.0 + sc_ref[0]) + sh_ref[0]
    h2_ref[0] = h2
    qp = jnp.dot(h2.astype(BF16), wpq_ref[...], preferred_element_type=F32).astype(BF16)
    for hp in range(2 * PEER_HEADS):
        s_ref[0, hp] = lax.dot_general(sk_ref[hp], qp[:, hp * N_KEYS:(hp + 1) * N_KEYS],
                                       (((1,), (1,)), ((), ())), preferred_element_type=F32)


def _outproj(x, mla, hsum, cols, gm, wa, wb, gt1, g2, sc2, sh2, wpq, sk, tm):
    b, n, d = x.shape
    hw = mla.shape[2]
    nq = wpq.shape[1]
    mod = pl.BlockSpec((1, 1, d), lambda i, j: (i, 0, 0))
    return pl.pallas_call(
        _outproj_kernel,
        out_shape=(jax.ShapeDtypeStruct((b, n, d), F32), jax.ShapeDtypeStruct((b, n, d), F32),
                   jax.ShapeDtypeStruct((b, 2 * PEER_HEADS, N_KEYS, n), F32)),
        grid=(b, n // tm),
        in_specs=[pl.BlockSpec((1, tm, d), lambda i, j: (i, j, 0)),
                  pl.BlockSpec((1, tm, hw), lambda i, j: (i, j, 0)),
                  pl.BlockSpec((1, tm, hw), lambda i, j: (i, j, 0)),
                  pl.BlockSpec((1, tm, 512), lambda i, j: (i, j, C_O0 // 512)),
                  pl.BlockSpec((1, hw), lambda i, j: (0, 0)),
                  pl.BlockSpec((hw, d), lambda i, j: (0, 0)),
                  pl.BlockSpec((hw, d), lambda i, j: (0, 0)),
                  mod, pl.BlockSpec((1, d), lambda i, j: (0, 0)), mod, mod,
                  pl.BlockSpec((d, nq), lambda i, j: (0, 0)),
                  pl.BlockSpec((2 * PEER_HEADS, N_KEYS, N_KEYS), lambda i, j: (0, 0, 0))],
        out_specs=(pl.BlockSpec((1, tm, d), lambda i, j: (i, j, 0)),
                   pl.BlockSpec((1, tm, d), lambda i, j: (i, j, 0)),
                   pl.BlockSpec((1, 2 * PEER_HEADS, N_KEYS, tm), lambda i, j: (i, 0, 0, j))),
        compiler_params=_params(("parallel", "parallel")),
        name="outproj",
    )(x, mla, hsum, cols, gm, wa, wb, gt1, g2, sc2, sh2, wpq, sk)


def _topk_rows(s, k, rid=None):
    rows, t = s.shape
    if rid is None:
        rid = lax.broadcasted_iota(I32, (rows, t), 0)
    kid = lax.broadcasted_iota(I32, (k, t), 0)
    big = jnp.iinfo(jnp.int32).max

    def body(r, carry):
        s, vals, idxs = carry
        m = jnp.max(s, axis=0, keepdims=True)
        i = jnp.min(jnp.where(s == m, rid, big), axis=0, keepdims=True)
        vals = jnp.where(kid == r, m, vals)
        idxs = jnp.where(kid == r, i, idxs)
        s = jnp.where(rid == i, -jnp.inf, s)
        return s, vals, idxs

    _, vals, idxs = lax.fori_loop(0, k, body, (s, jnp.zeros((k, t), F32), jnp.zeros((k, t), I32)))
    return vals, idxs


def _pair_candidates(sv0, sv1):
    K, t = sv0.shape
    h = K // 2
    iid = lax.broadcasted_iota(I32, (h, t), 0)
    vals = [sv0[0:h] + sv1[0:1], sv0[h:K] + sv1[0:1]]
    ids = [iid * K, (iid + h) * K]
    for j in range(1, h):
        vals.append(sv0[0:h] + sv1[j:j + 1])
        ids.append(iid * K + j)
    vals.append(sv0[0:1] + sv1[h:K])
    ids.append(iid + h)
    return jnp.concatenate(vals, axis=0), jnp.concatenate(ids, axis=0)


def _route_kernel(s_ref, off_ref, par_ref, gw_ref, et_ref, gt_ref):
    K = PEER_TOPK

    def head(h, carry):
        sv0, si0 = _topk_rows(s_ref[0, 2 * h], K)
        sv1, si1 = _topk_rows(s_ref[0, 2 * h + 1], K)
        cand, cid = _pair_candidates(sv0, sv1)
        best, pos = _topk_rows(cand, K, cid)
        isel = pos >> 4
        jsel = pos & (K - 1)
        e0 = jnp.zeros_like(pos)
        e1 = jnp.zeros_like(pos)
        for i in range(K):
            e0 = jnp.where(isel == i, si0[i:i + 1, :], e0)
            e1 = jnp.where(jsel == i, si1[i:i + 1, :], e1)
        ex = jnp.exp(best - best[0:1, :])
        sl = pl.ds(pl.multiple_of(h * K, K), K)
        et_ref[sl, :] = e0 * N_KEYS + e1
        gt_ref[sl, :] = ex / jnp.sum(ex, axis=0, keepdims=True)
        return carry

    lax.fori_loop(0, PEER_HEADS, head, 0)
    e = et_ref[...].T
    off_ref[0] = (e >> 1) * SUBLANES
    par_ref[0] = (e & 1).astype(F32)
    gw_ref[0] = gt_ref[...].T


def _route(scores, tt):
    b, hp, nk, n = scores.shape
    nsel = PEER_HEADS * PEER_TOPK
    ospec = pl.BlockSpec((1, tt, nsel), lambda i, j: (i, j, 0))
    return pl.pallas_call(
        _route_kernel,
        out_shape=(jax.ShapeDtypeStruct((b, n, nsel), I32), jax.ShapeDtypeStruct((b, n, nsel), F32),
                   jax.ShapeDtypeStruct((b, n, nsel), F32)),
        grid=(b, n // tt),
        in_specs=[pl.BlockSpec((1, hp, nk, tt), lambda i, j: (i, 0, 0, j))],
        out_specs=(ospec, ospec, ospec),
        scratch_shapes=[pltpu.VMEM((nsel, tt), I32), pltpu.VMEM((nsel, tt), F32)],
        compiler_params=_params(("parallel", "parallel")),
        name="route",
    )(scores)


TILE_ROWS = 2 * SUBLANES
TOKENS_PER_ITER = 8


def _gathered_tiles(tab_ref, off_ref, t, nsel):
    return jnp.concatenate(
        [pltpu.bitcast(tab_ref[pl.ds(pl.multiple_of(off_ref[t, k], SUBLANES), SUBLANES), :], BF16)
         for k in range(nsel)], axis=0)


def _row_masks(nsel):
    width = nsel * TILE_ROWS
    lane = lax.broadcasted_iota(I32, (nsel, width), 1)
    spread = (lane // TILE_ROWS == lax.broadcasted_iota(I32, (nsel, width), 0)).astype(BF16)
    half = (lax.broadcasted_iota(I32, (1, width), 1) % 2).astype(F32)
    lane8 = lax.broadcasted_iota(I32, (SUBLANES, width), 1)
    diag = ((lane8 % TILE_ROWS) // 2 == lax.broadcasted_iota(I32, (SUBLANES, width), 0)).astype(F32)
    return spread, half, diag


def _row_to_tile(row):
    return jnp.concatenate([row[:, s * LANES:(s + 1) * LANES] for s in range(SUBLANES)], axis=0)


def _split_bf16(v):
    hi = v.astype(BF16)
    return hi, (v - hi.astype(F32)).astype(BF16)


def _peer_u_kernel(off_ref, h_ref, tab_ref, par_ref, g_ref, c_ref, r_ref):
    tb, nsel = g_ref.shape
    spread, half, diag = _row_masks(nsel)

    def one(t):
        x_hi, x_lo = _split_bf16(_row_to_tile(h_ref[pl.ds(t, 1), :]))
        res = lax.dot_general(jnp.concatenate([x_hi, x_lo], axis=0), _gathered_tiles(tab_ref, off_ref, t, nsel),
                              (((1,), (1,)), ((), ())), preferred_element_type=F32)
        r_ref[pl.ds(t, 1), :] = jnp.sum((res[:SUBLANES] + res[SUBLANES:]) * diag, axis=0, keepdims=True)

    def tok(i, carry):
        for u in range(TOKENS_PER_ITER):
            one(i * TOKENS_PER_ITER + u)
        return carry

    lax.fori_loop(0, tb // TOKENS_PER_ITER, tok, 0)
    mine = jnp.dot(par_ref[...].astype(BF16), spread, preferred_element_type=F32) == half
    r_hi, r_lo = _split_bf16(jnp.where(mine, r_ref[...], 0.0))
    dn = (((1,), (1,)), ((), ()))
    pre = (lax.dot_general(r_hi, spread, dn, preferred_element_type=F32)
           + lax.dot_general(r_lo, spread, dn, preferred_element_type=F32))
    c_ref[...] = g_ref[...] * (0.5 * pre * (1.0 + lax.erf(pre * (2.0 ** -0.5))))


def _smem_block(tb, nsel):
    return pl.BlockSpec((tb, nsel), lambda i: (i, 0), memory_space=pltpu.SMEM)


def _peer_u(off, h2, tab, par, gw, tb):
    t, nsel = gw.shape
    vspec = pl.BlockSpec((tb, nsel), lambda i: (i, 0))
    return pl.pallas_call(
        _peer_u_kernel,
        out_shape=jax.ShapeDtypeStruct((t, nsel), F32),
        grid=(t // tb,),
        in_specs=[_smem_block(tb, nsel),
                  pl.BlockSpec((tb, h2.shape[1]), lambda i: (i, 0)),
                  pl.BlockSpec(tab.shape, lambda i: (0, 0), pipeline_mode=pl.Buffered(1)),
                  vspec, vspec],
        out_specs=vspec,
        scratch_shapes=[pltpu.VMEM((tb, nsel * TILE_ROWS), F32)],
        compiler_params=_params(("arbitrary",)),
        name="peer_u",
    )(off, h2, tab, par, gw)


def _peer_v_kernel(off_ref, c_ref, par_ref, tab_ref, x1_ref, gt_ref, o_ref, lhi_ref, llo_ref):
    tb, nsel = c_ref.shape
    spread, half, diag = _row_masks(nsel)
    c_hi, c_lo = _split_bf16(c_ref[...])
    mine = jnp.dot(par_ref[...].astype(BF16), spread, preferred_element_type=F32) == half
    lhi_ref[...] = jnp.where(mine, jnp.dot(c_hi, spread, preferred_element_type=F32), 0.0)
    llo_ref[...] = jnp.where(mine, jnp.dot(c_lo, spread, preferred_element_type=F32), 0.0)
    gt = gt_ref[0]

    def one(t):
        left = jnp.concatenate([lhi_ref[pl.ds(t, 1), :] * diag, llo_ref[pl.ds(t, 1), :] * diag], axis=0).astype(BF16)
        acc = jnp.dot(left, _gathered_tiles(tab_ref, off_ref, t, nsel), preferred_element_type=F32)
        y = acc[:SUBLANES] + acc[SUBLANES:]
        y_row = jnp.concatenate([y[s:s + 1, :] for s in range(SUBLANES)], axis=1)
        o_ref[pl.ds(t, 1), :] = x1_ref[pl.ds(t, 1), :] + gt * y_row

    def tok(i, carry):
        for u in range(TOKENS_PER_ITER):
            one(i * TOKENS_PER_ITER + u)
        return carry

    lax.fori_loop(0, tb // TOKENS_PER_ITER, tok, 0)


def _peer_v(off, coef, par, tab, x1, gt2, tb, tok_per_batch):
    t, nsel = coef.shape
    d = x1.shape[1]
    bpb = tok_per_batch // tb
    vspec = pl.BlockSpec((tb, nsel), lambda i: (i, 0))
    return pl.pallas_call(
        _peer_v_kernel,
        out_shape=jax.ShapeDtypeStruct(x1.shape, F32),
        grid=(t // tb,),
        in_specs=[_smem_block(tb, nsel), vspec, vspec,
                  pl.BlockSpec(tab.shape, lambda i: (0, 0), pipeline_mode=pl.Buffered(1)),
                  pl.BlockSpec((tb, d), lambda i: (i, 0)),
                  pl.BlockSpec((1, 1, d), lambda i: (i // bpb, 0, 0))],
        out_specs=pl.BlockSpec((tb, d), lambda i: (i, 0)),
        scratch_shapes=[pltpu.VMEM((tb, nsel * TILE_ROWS), F32), pltpu.VMEM((tb, nsel * TILE_ROWS), F32)],
        compiler_params=_params(("arbitrary",)),
        name="peer_v",
    )(off, coef, par, tab, x1, gt2)


def _rope_tables(n):
    axis = MLA_ROPE // 2
    t = jnp.arange(n, dtype=F32)
    row = jnp.floor(t / GRID_W)
    col = t - row * GRID_W
    inv = ROPE_BASE ** (-jnp.arange(axis // 2, dtype=F32) * (2.0 / axis))
    ar = row[:, None] * inv
    ac = col[:, None] * inv
    cos = jnp.concatenate([jnp.cos(ar), jnp.cos(ar), jnp.cos(ac), jnp.cos(ac)], axis=1)
    sin = jnp.concatenate([-jnp.sin(ar), jnp.sin(ar), -jnp.sin(ac), jnp.sin(ac)], axis=1)
    return cos, sin


def _pack_table(tab):
    e, d = tab.shape
    pairs = tab.astype(BF16).reshape(e // 2, 2, d).transpose(0, 2, 1)
    return lax.bitcast_convert_type(pairs, jnp.uint32).reshape(e // 2 * (d // LANES), LANES)


def _block(n, want):
    return want if n % want == 0 else n


def kernel(x, c, ctx, c_ctx, w_ada, b_ada, g_norm1, w_in, g_cq, w_uq, g_ckv, w_ukv, g_qn, g_kn, conv_qk, b_igate, b_fgate, g_mlstm, w_out, g_norm2, w_pq, sub_keys, expert_u, expert_v):
    B, N, D = x.shape
    NC = ctx.shape[1]
    assert w_ada.shape[0] == 1, "one layer"
    assert N % (256 * ATTN_TILES_PER_STEP) == 0 and NC % ML_CHUNK == 0 and D == SUBLANES * LANES
    q_rank = g_cq.shape[1]
    kv_rank = g_ckv.shape[1]
    mla_cols = q_rank + kv_rank + MLA_ROPE
    assert (q_rank, kv_rank) == (C_KV0 - C_Q0, C_KR0 - C_KV0)
    swap = jnp.arange(MLA_ROPE) ^ (MLA_ROPE // 4)

    cc = jnp.concatenate([c, c_ctx[None, :], jnp.zeros((16 - B - 1, D), F32)], axis=0)
    mod = _ada(cc, w_ada[0].astype(BF16), b_ada)
    sh1, sc1, gt1, sh2, sc2, gt2 = [mod[:, i * D:(i + 1) * D] for i in range(6)]
    lat = lambda m: m[:B].reshape(B, 1, D)
    ctxm = lambda m: jnp.broadcast_to(m[B:B + 1].reshape(1, 1, D), (B, 1, D))

    wi = w_in[0]
    n_qk = 2 * ML_HEADS * ML_QK
    n_v = ML_HEADS * ML_V
    n_g = 4 * ML_HEADS
    m0 = mla_cols
    w_cols = jnp.concatenate([
        wi[:, 0:mla_cols],
        wi[:, q_rank + kv_rank + swap],
        wi[:, m0:m0 + n_qk + 2 * n_v + n_g],
        jnp.zeros((D, C_END - C_G0 - n_g), F32)], axis=1).astype(BF16)
    assert w_cols.shape[1] == C_END
    w_gates_t = wi[:, m0 + n_qk + 2 * n_v:m0 + n_qk + 2 * n_v + n_g].T.astype(BF16)
    cols_l, gates_l = _inproj(x, g_norm1, lat(sc1), lat(sh1), w_cols, w_gates_t, _block(N, 512))
    cols_c, gates_c = _inproj(ctx, g_norm1, ctxm(sc1), ctxm(sh1), w_cols, w_gates_t, _block(NC, 512))

    wq = w_uq[0].reshape(q_rank, MLA_HEADS, MLA_QK)
    wuq = jnp.concatenate([wq, wq[:, :, MLA_NOPE + swap]], axis=2).reshape(q_rank, MLA_HEADS * 256).astype(BF16)
    wukv = w_ukv[0].astype(BF16)
    gqs, gks = g_qn[:, MLA_NOPE + swap], g_kn[:, MLA_NOPE + swap]
    cos_l, sin_l = _rope_tables(N)
    cos_c, sin_c = jnp.ones((NC, MLA_ROPE), F32), jnp.zeros((NC, MLA_ROPE), F32)
    q_l, k_l, v_l = _mla_prep(cols_l, g_cq, wuq, g_ckv, wukv, g_qn, gqs, g_kn, gks, cos_l, sin_l, _block(N, 512))
    _, k_c, v_c = _mla_prep(cols_c, g_cq, wuq, g_ckv, wukv, g_qn, gqs, g_kn, gks, cos_c, sin_c, _block(NC, 512))
    k_all = jnp.concatenate([k_c, k_l], axis=2)
    v_all = jnp.concatenate([v_c, v_l], axis=2)
    mla = _attention(q_l, k_all, v_all, 256)

    bias16 = jnp.concatenate([b_igate[0].reshape(-1), b_fgate[0].reshape(-1)])
    cw = conv_qk[0]
    pl_ = _ml_prep(cols_l, gates_l, cw, bias16[None, :], bias16[:, None], _block(N, 512))
    pc_ = _ml_prep(cols_c, gates_c, cw, bias16[None, :], bias16[:, None], _block(NC, 512))
    hsum = _ml_scan(*pl_, *pc_)

    wo = w_out[0].astype(BF16)
    hw = MLA_HEADS * MLA_V
    sk = sub_keys[0].reshape(2 * PEER_HEADS, N_KEYS, -1).astype(BF16)
    x1, h2, scores = _outproj(x, mla, hsum, cols_l, g_mlstm, wo[:hw], wo[hw:], lat(gt1), g_norm2, lat(sc2), lat(sh2),
                              w_pq[0].astype(BF16), sk, 256)
    T = B * N
    nsel = PEER_HEADS * PEER_TOPK
    off, par, gw = [a.reshape(T, nsel) for a in _route(scores, 256)]

    tb = 32
    coef = _peer_u(off, h2.reshape(T, D), _pack_table(expert_u[0]), par, gw, tb)
    out = _peer_v(off, coef, par, _pack_table(expert_v[0]), x1.reshape(T, D), lat(gt2), tb, N)
    return out.reshape(B, N, D)
```

```python
import jax
import jax.numpy as jnp
from jax import lax
from jax.experimental import pallas as pl
from jax.experimental.pallas import tpu as pltpu

F32 = jnp.float32
BF16 = jnp.bfloat16
I32 = jnp.int32
EPS = 1e-6

GRID_W = 64
MLA_HEADS = 4
MLA_NOPE = 128
MLA_ROPE = 64
MLA_V = 128
MLA_QK = MLA_NOPE + MLA_ROPE
ROPE_BASE = 10000.0
ML_HEADS = 4
ML_QK = 64
ML_V = 128
ML_CHUNK = 64
PEER_HEADS = 8
N_KEYS = 128
PEER_TOPK = 16

LANES = 128
SUBLANES = 8
VMEM_LIMIT_BYTES = 56 * 1024 * 1024

C_Q0, C_KV0, C_KR0, C_KRS0, C_QK0, C_V0, C_O0, C_G0, C_END = 0, 256, 384, 448, 512, 1024, 1536, 2048, 2176
HIGHEST = lax.Precision.HIGHEST


def _params(sem):
    return pltpu.CompilerParams(dimension_semantics=sem, vmem_limit_bytes=VMEM_LIMIT_BYTES)


def _rms(x):
    return x * lax.rsqrt(jnp.mean(x * x, axis=-1, keepdims=True) + EPS)


def _ada_kernel(c_ref, w_ref, b_ref, o_ref):
    c = c_ref[...]
    s = c * jax.nn.sigmoid(c)
    o_ref[...] = jnp.dot(s.astype(BF16), w_ref[...], preferred_element_type=F32) + b_ref[...]


def _ada(cc, w, b):
    rows, d = cc.shape
    n = w.shape[1]
    bn = n // 4
    return pl.pallas_call(
        _ada_kernel,
        out_shape=jax.ShapeDtypeStruct((rows, n), F32),
        grid=(4,),
        in_specs=[pl.BlockSpec((rows, d), lambda j: (0, 0)),
                  pl.BlockSpec((d, bn), lambda j: (0, j)),
                  pl.BlockSpec((1, bn), lambda j: (0, j))],
        out_specs=pl.BlockSpec((rows, bn), lambda j: (0, j)),
        compiler_params=_params(("arbitrary",)),
        name="ada",
    )(cc, w, b)


def _inproj_kernel(x_ref, g_ref, sc_ref, sh_ref, w_ref, wg_ref, o_ref, gt_ref):
    x = x_ref[0]
    h = _rms(x) * g_ref[...] * (1.0 + sc_ref[0]) + sh_ref[0]
    hb = h.astype(BF16)
    o_ref[0] = jnp.dot(hb, w_ref[...], preferred_element_type=F32)
    gt_ref[0] = lax.dot_general(wg_ref[...], hb, (((1,), (1,)), ((), ())), preferred_element_type=F32)


def _inproj(x, g, sc, sh, w, wg, tm):
    b, n, d = x.shape
    nc = w.shape[1]
    return pl.pallas_call(
        _inproj_kernel,
        out_shape=(jax.ShapeDtypeStruct((b, n, nc), F32), jax.ShapeDtypeStruct((b, 16, n), F32)),
        grid=(b, n // tm),
        in_specs=[pl.BlockSpec((1, tm, d), lambda i, j: (i, j, 0)),
                  pl.BlockSpec((1, d), lambda i, j: (0, 0)),
                  pl.BlockSpec((1, 1, d), lambda i, j: (i, 0, 0)),
                  pl.BlockSpec((1, 1, d), lambda i, j: (i, 0, 0)),
                  pl.BlockSpec((d, nc), lambda i, j: (0, 0)),
                  pl.BlockSpec((16, d), lambda i, j: (0, 0))],
        out_specs=(pl.BlockSpec((1, tm, nc), lambda i, j: (i, j, 0)),
                   pl.BlockSpec((1, 16, tm), lambda i, j: (i, 0, j))),
        compiler_params=_params(("parallel", "parallel")),
        name="inproj",
    )(x, g, sc, sh, w, wg)


def _mla_prep_kernel(c_ref, gcq_ref, wuq_ref, gckv_ref, wukv_ref, gq_ref, gqs_ref, gk_ref, gks_ref,
                     cos_ref, sin_ref, q_ref, k_ref, v_ref):
    c = c_ref[0]
    cq = _rms(c[:, C_Q0:C_KV0]) * gcq_ref[...]
    ckv = _rms(c[:, C_KV0:C_KR0]) * gckv_ref[...]
    kr = c[:, C_KR0:C_KRS0]
    krs = c[:, C_KRS0:C_QK0]
    q_raw = jnp.dot(cq.astype(BF16), wuq_ref[...], preferred_element_type=F32)
    kv_raw = jnp.dot(ckv.astype(BF16), wukv_ref[...], preferred_element_type=F32)
    cos = cos_ref[...]
    sin = sin_ref[...]
    gq = gq_ref[...]
    gk = gk_ref[...]
    kr_ss = jnp.sum(kr * kr, axis=-1, keepdims=True)
    for h in range(MLA_HEADS):
        o = h * 256
        qn = q_raw[:, o:o + 128]
        qr = q_raw[:, o + 128:o + 192]
        qs = q_raw[:, o + 192:o + 256]
        ss = jnp.sum(qn * qn, axis=-1, keepdims=True) + jnp.sum(qr * qr, axis=-1, keepdims=True)
        r = lax.rsqrt(ss * (1.0 / MLA_QK) + EPS) * (MLA_QK ** -0.5)
        q_ref[0, h, :, 0:128] = (qn * r * gq[:, 0:128]).astype(BF16)
        q_ref[0, h, :, 128:192] = ((qr * r * gq[:, 128:192]) * cos + (qs * r * gqs_ref[...]) * sin).astype(BF16)
        kn = kv_raw[:, o:o + 128]
        ss = jnp.sum(kn * kn, axis=-1, keepdims=True) + kr_ss
        r = lax.rsqrt(ss * (1.0 / MLA_QK) + EPS)
        k_ref[0, h, :, 0:128] = (kn * r * gk[:, 0:128]).astype(BF16)
        k_ref[0, h, :, 128:192] = ((kr * r * gk[:, 128:192]) * cos + (krs * r * gks_ref[...]) * sin).astype(BF16)
        v_ref[0, h] = kv_raw[:, o + 128:o + 256].astype(BF16)


def _mla_prep(cols, gcq, wuq, gckv, wukv, gq, gqs, gk, gks, cos, sin, tm):
    b, n, _ = cols.shape
    full = lambda a: pl.BlockSpec(a.shape, lambda i, j: (0,) * a.ndim)
    return pl.pallas_call(
        _mla_prep_kernel,
        out_shape=(jax.ShapeDtypeStruct((b, MLA_HEADS, n, MLA_QK), BF16),
                   jax.ShapeDtypeStruct((b, MLA_HEADS, n, MLA_QK), BF16),
                   jax.ShapeDtypeStruct((b, MLA_HEADS, n, MLA_V), BF16)),
        grid=(b, n // tm),
        in_specs=[pl.BlockSpec((1, tm, 512), lambda i, j: (i, j, 0)),
                  full(gcq), full(wuq), full(gckv), full(wukv), full(gq), full(gqs), full(gk), full(gks),
                  pl.BlockSpec((tm, MLA_ROPE), lambda i, j: (j, 0)),
                  pl.BlockSpec((tm, MLA_ROPE), lambda i, j: (j, 0))],
        out_specs=(pl.BlockSpec((1, MLA_HEADS, tm, MLA_QK), lambda i, j: (i, 0, j, 0)),
                   pl.BlockSpec((1, MLA_HEADS, tm, MLA_QK), lambda i, j: (i, 0, j, 0)),
                   pl.BlockSpec((1, MLA_HEADS, tm, MLA_V), lambda i, j: (i, 0, j, 0))),
        compiler_params=_params(("parallel", "parallel")),
        name="mla_prep",
    )(cols, gcq, wuq, gckv, wukv, gq, gqs, gk, gks, cos, sin)


ATTN_TILES_PER_STEP = 2


def _attn_kernel(q_ref, kc_ref, kl_ref, vc_ref, vl_ref, o_ref):
    tq = q_ref.shape[2] // ATTN_TILES_PER_STEP
    nc = kc_ref.shape[2]
    kc, kl, vc, vl = kc_ref[0, 0], kl_ref[0, 0], vc_ref[0, 0], vl_ref[0, 0]
    nt = (((1,), (1,)), ((), ()))
    for j in range(ATTN_TILES_PER_STEP):
        q = q_ref[0, 0, j * tq:(j + 1) * tq, :]
        s = jnp.concatenate([lax.dot_general(q, kc, nt, preferred_element_type=F32),
                             lax.dot_general(q, kl, nt, preferred_element_type=F32)], axis=1)
        m = jnp.max(s, axis=-1, keepdims=True)
        p = jnp.exp(s - m)
        l = jnp.sum(p, axis=-1, keepdims=True)
        pb = p.astype(BF16)
        o = (jnp.dot(pb[:, :nc], vc, preferred_element_type=F32)
             + jnp.dot(pb[:, nc:], vl, preferred_element_type=F32))
        o_ref[0, j * tq:(j + 1) * tq, :] = (o / l).astype(BF16)


def _attention(q, k_c, k_l, v_c, v_l, tq):
    b, h, n, dk = q.shape
    nc = k_c.shape[2]
    dv = v_l.shape[3]
    tb = tq * ATTN_TILES_PER_STEP
    whole = lambda nn, dd: pl.BlockSpec((1, 1, nn, dd), lambda i, j, t: (i, j, 0, 0))
    return pl.pallas_call(
        _attn_kernel,
        out_shape=jax.ShapeDtypeStruct((b, n, h * dv), BF16),
        grid=(b, h, n // tb),
        in_specs=[pl.BlockSpec((1, 1, tb, dk), lambda i, j, t: (i, j, t, 0)),
                  whole(nc, dk), whole(n, dk), whole(nc, dv), whole(n, dv)],
        out_specs=pl.BlockSpec((1, tb, dv), lambda i, j, t: (i, t, j)),
        compiler_params=_params(("parallel", "parallel", "parallel")),
        name="attention",
    )(q, k_c, k_l, v_c, v_l)


def _ml_prep_kernel(qk_ref, prev_ref, next_ref, vin_ref, g_ref, gt_ref, cw_ref, bc_ref, br_ref,
                    q_ref, kt_ref, v_ref, a_ref, brow_ref):
    tn = qk_ref.shape[1]
    j = pl.program_id(1)
    nj = pl.num_programs(1)
    u = qk_ref[0]
    row = lax.broadcasted_iota(I32, (tn, 1), 0)
    before = jnp.where(j == 0, 0.0, prev_ref[0, SUBLANES - 1:SUBLANES, :])
    after = jnp.where(j == nj - 1, 0.0, next_ref[0, 0:1, :])
    up = jnp.where(row == 0, before, pltpu.roll(u, 1, 0))
    un = jnp.where(row == tn - 1, after, pltpu.roll(u, tn - 1, 0))
    cw = cw_ref[...]
    y = up * cw[0:1, :] + u * cw[1:2, :] + un * cw[2:3, :]
    y = y * jax.nn.sigmoid(y)
    hq = ML_HEADS * ML_QK
    for h in range(ML_HEADS):
        q_ref[0, h] = (y[:, h * ML_QK:(h + 1) * ML_QK] * (ML_QK ** -0.5)).astype(BF16)
    kt = y[:, hq:2 * hq].T
    L = ML_CHUNK
    for c in range(tn // L):
        kt_ref[0, c] = kt[:, c * L:(c + 1) * L].astype(BF16)
    v_ref[0] = vin_ref[0].astype(BF16)
    g = g_ref[0][:, 0:16] + bc_ref[...]
    lf_c = jax.nn.log_sigmoid(g[:, 8:16])
    gt = gt_ref[0] + br_ref[...]
    ig_r = gt[0:8, :]
    lf_r = jax.nn.log_sigmoid(gt[8:16, :])
    ti = lax.broadcasted_iota(I32, (L, L), 0)
    si = lax.broadcasted_iota(I32, (L, L), 1)
    lower = (si <= ti).astype(F32)
    upper = (si >= ti).astype(F32)
    lane_fwd = lax.broadcasted_iota(I32, (L, 8), 1) < ML_HEADS
    row_fwd = lax.broadcasted_iota(I32, (8, L), 0) < ML_HEADS
    for c in range(tn // L):
        lo = c * L
        lfc = lf_c[lo:lo + L, :]
        a_ref[0, lo:lo + L, :] = jnp.where(
            lane_fwd,
            jnp.dot(lower, lfc, precision=HIGHEST, preferred_element_type=F32),
            jnp.dot(upper, lfc, precision=HIGHEST, preferred_element_type=F32))
        lfr = lf_r[:, lo:lo + L]
        cf_r = jnp.where(
            row_fwd,
            jnp.dot(lfr, upper, precision=HIGHEST, preferred_element_type=F32),
            jnp.dot(lfr, lower, precision=HIGHEST, preferred_element_type=F32))
        brow_ref[0, c] = ig_r[:, lo:lo + L] - cf_r


def _ml_prep(cols, gates_t, cw, bias_col, bias_row, tn):
    b, n, _ = cols.shape
    r8 = tn // SUBLANES
    last8 = n // SUBLANES - 1
    L = ML_CHUNK
    hq = ML_HEADS * ML_QK
    return pl.pallas_call(
        _ml_prep_kernel,
        out_shape=(jax.ShapeDtypeStruct((b, ML_HEADS, n, ML_QK), BF16),
                   jax.ShapeDtypeStruct((b, n // L, hq, L), BF16),
                   jax.ShapeDtypeStruct((b, n, ML_HEADS * ML_V), BF16),
                   jax.ShapeDtypeStruct((b, n, 8), F32),
                   jax.ShapeDtypeStruct((b, n // L, 8, L), F32)),
        grid=(b, n // tn),
        in_specs=[pl.BlockSpec((1, tn, 512), lambda i, j: (i, j, C_QK0 // 512)),
                  pl.BlockSpec((1, SUBLANES, 512), lambda i, j: (i, jnp.maximum(j * r8 - 1, 0), C_QK0 // 512)),
                  pl.BlockSpec((1, SUBLANES, 512), lambda i, j: (i, jnp.minimum((j + 1) * r8, last8), C_QK0 // 512)),
                  pl.BlockSpec((1, tn, 512), lambda i, j: (i, j, C_V0 // 512)),
                  pl.BlockSpec((1, tn, 128), lambda i, j: (i, j, C_G0 // 128)),
                  pl.BlockSpec((1, 16, tn), lambda i, j: (i, 0, j)),
                  pl.BlockSpec((3, 512), lambda i, j: (0, 0)),
                  pl.BlockSpec((1, 16), lambda i, j: (0, 0)),
                  pl.BlockSpec((16, 1), lambda i, j: (0, 0))],
        out_specs=(pl.BlockSpec((1, ML_HEADS, tn, ML_QK), lambda i, j: (i, 0, j, 0)),
                   pl.BlockSpec((1, tn // L, hq, L), lambda i, j: (i, j, 0, 0)),
                   pl.BlockSpec((1, tn, ML_HEADS * ML_V), lambda i, j: (i, j, 0)),
                   pl.BlockSpec((1, tn, 8), lambda i, j: (i, j, 0)),
                   pl.BlockSpec((1, tn // L, 8, L), lambda i, j: (i, j, 0, 0))),
        compiler_params=_params(("parallel", "parallel")),
        name="ml_prep",
    )(cols, cols, cols, cols, cols, gates_t, cw, bias_col, bias_row)


def _ml_chunk(qb, kt, vaug, a_c, b_r, ct, m_prev, fwd, need_out):
    L = qb.shape[0]
    ti = lax.broadcasted_iota(I32, (L, L), 0)
    si = lax.broadcasted_iota(I32, (L, L), 1)
    mask = (si <= ti) if fwd else (si >= ti)
    dmat = jnp.where(mask, a_c + b_r, -jnp.inf)
    mloc = jnp.max(dmat, axis=1, keepdims=True)
    inter = a_c + m_prev
    m_t = jnp.maximum(inter, mloc)
    last = L - 1 if fwd else 0
    m_new = m_t[last:last + 1, :]
    a_last = a_c[last:last + 1, :]
    w_r = jnp.exp(a_last + b_r - m_new)
    decay = jnp.exp(a_last + m_prev - m_new)
    kw = (kt.astype(F32) * w_r).astype(BF16)
    if not need_out:
        return decay * ct + jnp.dot(kw, vaug, preferred_element_type=F32), m_new, None
    s = jnp.dot(qb, kt, preferred_element_type=F32)
    wmat = jnp.exp(dmat - m_t) * s
    sc = jnp.exp(inter - m_t)
    top = jnp.concatenate([wmat.astype(BF16), (sc * qb.astype(F32)).astype(BF16)], axis=1)
    bot = jnp.concatenate([kw, jnp.zeros(kw.shape, BF16)], axis=1)
    res = jnp.dot(jnp.concatenate([top, bot], axis=0), jnp.concatenate([vaug, ct.astype(BF16)], axis=0),
                  preferred_element_type=F32)
    nd = res[:L]
    num = nd[:, 0:ML_V]
    den = nd[:, ML_V:ML_V + 1]
    return decay * ct + res[L:], m_new, num / jnp.maximum(jnp.abs(den), jnp.exp(-m_t))


def _ml_scan_kernel(ql_ref, ktl_ref, vl_ref, al_ref, brl_ref, qc_ref, ktc_ref, vc_ref, ac_ref, brc_ref,
                    out_ref, st_ref, m_ref):
    L = ML_CHUNK
    ncl = ql_ref.shape[2] // L
    ncc = qc_ref.shape[2] // L
    ones_col = (lax.broadcasted_iota(I32, (L, ML_V), 1) == 0).astype(BF16)
    st_ref[...] = jnp.zeros(st_ref.shape, F32)
    m_ref[...] = jnp.zeros(m_ref.shape, F32)

    def step(refs, c, d, fwd, need_out):
        q_ref, kt_ref, v_ref, a_ref, br_ref = refs
        sl = pl.ds(pl.multiple_of(c * L, L), L)
        a_all = a_ref[0, sl, :]
        br_all = br_ref[0, c]
        v_all = v_ref[0, sl, :]
        kt_all = kt_ref[0, c]
        hs = []
        for h in range(ML_HEADS):
            j = d * ML_HEADS + h
            vaug = jnp.concatenate([v_all[:, h * ML_V:(h + 1) * ML_V], ones_col], axis=1)
            ct_new, m_new, hh = _ml_chunk(
                q_ref[0, h, sl, :], kt_all[h * ML_QK:(h + 1) * ML_QK, :], vaug,
                a_all[:, j:j + 1], br_all[j:j + 1, :], st_ref[j], m_ref[j][0:1, 0:1], fwd, need_out)
            st_ref[j] = ct_new
            m_ref[j] = jnp.broadcast_to(m_new, (SUBLANES, LANES))
            hs.append(hh)
        return hs, sl

    ctx_refs = (qc_ref, ktc_ref, vc_ref, ac_ref, brc_ref)
    lat_refs = (ql_ref, ktl_ref, vl_ref, al_ref, brl_ref)

    out_ref[...] = jnp.zeros(out_ref.shape, F32)

    def ctx_body(i, carry):
        step(ctx_refs, i, 0, True, False)
        step(ctx_refs, ncc - 1 - i, 1, False, False)
        return carry

    def lat_body(i, carry):
        for d, fwd in ((0, True), (1, False)):
            hs, sl = step(lat_refs, i if fwd else ncl - 1 - i, d, fwd, True)
            out_ref[0, sl, :] = out_ref[0, sl, :] + jnp.concatenate(hs, axis=1)
        return carry

    lax.fori_loop(0, ncc, ctx_body, 0)
    lax.fori_loop(0, ncl, lat_body, 0)


def _ml_scan(ql, ktl, vl, al, brl, qc, ktc, vc, ac, brc):
    b, _, n, _ = ql.shape
    nctx = qc.shape[2]
    L = ML_CHUNK
    hv = ML_HEADS * ML_V
    hq = ML_HEADS * ML_QK
    qspec = lambda nn: pl.BlockSpec((1, ML_HEADS, nn, ML_QK), lambda i: (i, 0, 0, 0))
    ktspec = lambda nn: pl.BlockSpec((1, nn // L, hq, L), lambda i: (i, 0, 0, 0))
    vspec = lambda nn: pl.BlockSpec((1, nn, hv), lambda i: (i, 0, 0))
    aspec = lambda nn: pl.BlockSpec((1, nn, 8), lambda i: (i, 0, 0))
    rspec = lambda nn: pl.BlockSpec((1, nn // L, 8, L), lambda i: (i, 0, 0, 0))
    return pl.pallas_call(
        _ml_scan_kernel,
        out_shape=jax.ShapeDtypeStruct((b, n, hv), F32),
        grid=(b,),
        in_specs=[qspec(n), ktspec(n), vspec(n), aspec(n), rspec(n),
                  qspec(nctx), ktspec(nctx), vspec(nctx), aspec(nctx), rspec(nctx)],
        out_specs=pl.BlockSpec((1, n, hv), lambda i: (i, 0, 0)),
        scratch_shapes=[pltpu.VMEM((2 * ML_HEADS, ML_QK, 2 * ML_V), F32),
                        pltpu.VMEM((2 * ML_HEADS, SUBLANES, LANES), F32)],
        compiler_params=_params(("parallel",)),
        name="ml_scan",
    )(ql, ktl, vl, al, brl, qc, ktc, vc, ac, brc)


def _outproj_kernel(x_ref, mla_ref, hs_ref, o_ref, gm_ref, wa_ref, wb_ref, gt_ref, g2_ref, sc_ref, sh_ref,
                    wpq_ref, sk_ref, x1_ref, h2_ref, s_ref):
    hs = hs_ref[0]
    gm = gm_ref[...]
    hn = jnp.concatenate([_rms(hs[:, h * ML_V:(h + 1) * ML_V]) * gm[:, h * ML_V:(h + 1) * ML_V]
                          for h in range(ML_HEADS)], axis=1)
    ml = (jax.nn.sigmoid(o_ref[0]) * hn).astype(BF16)
    mix = (jnp.dot(mla_ref[0], wa_ref[...], preferred_element_type=F32)
           + jnp.dot(ml, wb_ref[...], preferred_element_type=F32))
    x1 = x_ref[0] + gt_ref[0] * mix
    x1_ref[0] = x1
    h2 = _rms(x1) * g2_ref[...] * (1.0 + sc_ref[0]) + sh_ref[0]
    h2_ref[0] = h2
    qp = jnp.dot(h2.astype(BF16), wpq_ref[...], preferred_element_type=F32).astype(BF16)
    for hp in range(2 * PEER_HEADS):
        s_ref[0, hp] = lax.dot_general(sk_ref[hp], qp[:, hp * N_KEYS:(hp + 1) * N_KEYS],
                                       (((1,), (1,)), ((), ())), preferred_element_type=F32)


def _outproj(x, mla, hsum, cols, gm, wa, wb, gt1, g2, sc2, sh2, wpq, sk, tm):
    b, n, d = x.shape
    hw = mla.shape[2]
    nq = wpq.shape[1]
    mod = pl.BlockSpec((1, 1, d), lambda i, j: (i, 0, 0))
    return pl.pallas_call(
        _outproj_kernel,
        out_shape=(jax.ShapeDtypeStruct((b, n, d), F32), jax.ShapeDtypeStruct((b, n, d), F32),
                   jax.ShapeDtypeStruct((b, 2 * PEER_HEADS, N_KEYS, n), F32)),
        grid=(b, n // tm),
        in_specs=[pl.BlockSpec((1, tm, d), lambda i, j: (i, j, 0)),
                  pl.BlockSpec((1, tm, hw), lambda i, j: (i, j, 0)),
                  pl.BlockSpec((1, tm, hw), lambda i, j: (i, j, 0)),
                  pl.BlockSpec((1, tm, 512), lambda i, j: (i, j, C_O0 // 512)),
                  pl.BlockSpec((1, hw), lambda i, j: (0, 0)),
                  pl.BlockSpec((hw, d), lambda i, j: (0, 0)),
                  pl.BlockSpec((hw, d), lambda i, j: (0, 0)),
                  mod, pl.BlockSpec((1, d), lambda i, j: (0, 0)), mod, mod,
                  pl.BlockSpec((d, nq), lambda i, j: (0, 0)),
                  pl.BlockSpec((2 * PEER_HEADS, N_KEYS, N_KEYS), lambda i, j: (0, 0, 0))],
        out_specs=(pl.BlockSpec((1, tm, d), lambda i, j: (i, j, 0)),
                   pl.BlockSpec((1, tm, d), lambda i, j: (i, j, 0)),
                   pl.BlockSpec((1, 2 * PEER_HEADS, N_KEYS, tm), lambda i, j: (i, 0, 0, j))),
        compiler_params=_params(("parallel", "parallel")),
        name="outproj",
    )(x, mla, hsum, cols, gm, wa, wb, gt1, g2, sc2, sh2, wpq, sk)


def _topk_rows(s, k, rid=None):
    rows, t = s.shape
    if rid is None:
        rid = lax.broadcasted_iota(I32, (rows, t), 0)
    kid = lax.broadcasted_iota(I32, (k, t), 0)
    big = jnp.iinfo(jnp.int32).max

    def body(r, carry):
        s, vals, idxs = carry
        m = jnp.max(s, axis=0, keepdims=True)
        i = jnp.min(jnp.where(s == m, rid, big), axis=0, keepdims=True)
        vals = jnp.where(kid == r, m, vals)
        idxs = jnp.where(kid == r, i, idxs)
        s = jnp.where(rid == i, -jnp.inf, s)
        return s, vals, idxs

    _, vals, idxs = lax.fori_loop(0, k, body, (s, jnp.zeros((k, t), F32), jnp.zeros((k, t), I32)))
    return vals, idxs


def _pair_candidates(sv0, sv1):
    K, t = sv0.shape
    h = K // 2
    iid = lax.broadcasted_iota(I32, (h, t), 0)
    vals = [sv0[0:h] + sv1[0:1], sv0[h:K] + sv1[0:1]]
    ids = [iid * K, (iid + h) * K]
    for j in range(1, h):
        vals.append(sv0[0:h] + sv1[j:j + 1])
        ids.append(iid * K + j)
    vals.append(sv0[0:1] + sv1[h:K])
    ids.append(iid + h)
    return jnp.concatenate(vals, axis=0), jnp.concatenate(ids, axis=0)


def _route_kernel(s_ref, off_ref, par_ref, gw_ref, et_ref, gt_ref):
    K = PEER_TOPK

    def head(h, carry):
        sv0, si0 = _topk_rows(s_ref[0, 2 * h], K)
        sv1, si1 = _topk_rows(s_ref[0, 2 * h + 1], K)
        cand, cid = _pair_candidates(sv0, sv1)
        best, pos = _topk_rows(cand, K, cid)
        isel = pos >> 4
        jsel = pos & (K - 1)
        e0 = jnp.zeros_like(pos)
        e1 = jnp.zeros_like(pos)
        for i in range(K):
            e0 = jnp.where(isel == i, si0[i:i + 1, :], e0)
            e1 = jnp.where(jsel == i, si1[i:i + 1, :], e1)
        ex = jnp.exp(best - best[0:1, :])
        sl = pl.ds(pl.multiple_of(h * K, K), K)
        et_ref[sl, :] = e0 * N_KEYS + e1
        gt_ref[sl, :] = ex / jnp.sum(ex, axis=0, keepdims=True)
        return carry

    lax.fori_loop(0, PEER_HEADS, head, 0)
    e = et_ref[...].T
    off_ref[0] = (e >> 1) * SUBLANES
    par_ref[0] = (e & 1).astype(F32)
    gw_ref[0] = gt_ref[...].T


def _route(scores, tt):
    b, hp, nk, n = scores.shape
    nsel = PEER_HEADS * PEER_TOPK
    ospec = pl.BlockSpec((1, tt, nsel), lambda i, j: (i, j, 0))
    return pl.pallas_call(
        _route_kernel,
        out_shape=(jax.ShapeDtypeStruct((b, n, nsel), I32), jax.ShapeDtypeStruct((b, n, nsel), F32),
                   jax.ShapeDtypeStruct((b, n, nsel), F32)),
        grid=(b, n // tt),
        in_specs=[pl.BlockSpec((1, hp, nk, tt), lambda i, j: (i, 0, 0, j))],
        out_specs=(ospec, ospec, ospec),
        scratch_shapes=[pltpu.VMEM((nsel, tt), I32), pltpu.VMEM((nsel, tt), F32)],
        compiler_params=_params(("parallel", "parallel")),
        name="route",
    )(scores)


TILE_ROWS = 2 * SUBLANES
TOKENS_PER_ITER = 8


def _gathered_tiles(tab_ref, off_ref, t, nsel):
    return jnp.concatenate(
        [pltpu.bitcast(tab_ref[pl.ds(pl.multiple_of(off_ref[t, k], SUBLANES), SUBLANES), :], BF16)
         for k in range(nsel)], axis=0)


def _row_masks(nsel):
    width = nsel * TILE_ROWS
    lane = lax.broadcasted_iota(I32, (nsel, width), 1)
    spread = (lane // TILE_ROWS == lax.broadcasted_iota(I32, (nsel, width), 0)).astype(BF16)
    half = (lax.broadcasted_iota(I32, (1, width), 1) % 2).astype(F32)
    lane8 = lax.broadcasted_iota(I32, (SUBLANES, width), 1)
    diag = ((lane8 % TILE_ROWS) // 2 == lax.broadcasted_iota(I32, (SUBLANES, width), 0)).astype(F32)
    return spread, half, diag


def _row_to_tile(row):
    return jnp.concatenate([row[:, s * LANES:(s + 1) * LANES] for s in range(SUBLANES)], axis=0)


def _split_bf16(v):
    hi = v.astype(BF16)
    return hi, (v - hi.astype(F32)).astype(BF16)


def _peer_u_kernel(off_ref, h_ref, tab_ref, par_ref, g_ref, c_ref, r_ref):
    tb, nsel = g_ref.shape
    spread, half, diag = _row_masks(nsel)

    def one(t):
        x_hi, x_lo = _split_bf16(_row_to_tile(h_ref[pl.ds(t, 1), :]))
        res = lax.dot_general(jnp.concatenate([x_hi, x_lo], axis=0), _gathered_tiles(tab_ref, off_ref, t, nsel),
                              (((1,), (1,)), ((), ())), preferred_element_type=F32)
        r_ref[pl.ds(t, 1), :] = jnp.sum((res[:SUBLANES] + res[SUBLANES:]) * diag, axis=0, keepdims=True)

    def tok(i, carry):
        for u in range(TOKENS_PER_ITER):
            one(i * TOKENS_PER_ITER + u)
        return carry

    lax.fori_loop(0, tb // TOKENS_PER_ITER, tok, 0)
    mine = jnp.dot(par_ref[...].astype(BF16), spread, preferred_element_type=F32) == half
    r_hi, r_lo = _split_bf16(jnp.where(mine, r_ref[...], 0.0))
    dn = (((1,), (1,)), ((), ()))
    pre = (lax.dot_general(r_hi, spread, dn, preferred_element_type=F32)
           + lax.dot_general(r_lo, spread, dn, preferred_element_type=F32))
    c_ref[...] = g_ref[...] * (0.5 * pre * (1.0 + lax.erf(pre * (2.0 ** -0.5))))


def _smem_block(tb, nsel):
    return pl.BlockSpec((tb, nsel), lambda i: (i, 0), memory_space=pltpu.SMEM)


def _peer_u(off, h2, tab, par, gw, tb):
    t, nsel = gw.shape
    vspec = pl.BlockSpec((tb, nsel), lambda i: (i, 0))
    return pl.pallas_call(
        _peer_u_kernel,
        out_shape=jax.ShapeDtypeStruct((t, nsel), F32),
        grid=(t // tb,),
        in_specs=[_smem_block(tb, nsel),
                  pl.BlockSpec((tb, h2.shape[1]), lambda i: (i, 0)),
                  pl.BlockSpec(tab.shape, lambda i: (0, 0), pipeline_mode=pl.Buffered(1)),
                  vspec, vspec],
        out_specs=vspec,
        scratch_shapes=[pltpu.VMEM((tb, nsel * TILE_ROWS), F32)],
        compiler_params=_params(("arbitrary",)),
        name="peer_u",
    )(off, h2, tab, par, gw)


def _peer_v_kernel(off_ref, c_ref, par_ref, tab_ref, x1_ref, gt_ref, o_ref, lhi_ref, llo_ref):
    tb, nsel = c_ref.shape
    spread, half, diag = _row_masks(nsel)
    c_hi, c_lo = _split_bf16(c_ref[...])
    mine = jnp.dot(par_ref[...].astype(BF16), spread, preferred_element_type=F32) == half
    lhi_ref[...] = jnp.where(mine, jnp.dot(c_hi, spread, preferred_element_type=F32), 0.0)
    llo_ref[...] = jnp.where(mine, jnp.dot(c_lo, spread, preferred_element_type=F32), 0.0)
    gt = gt_ref[0]

    def one(t):
        left = jnp.concatenate([lhi_ref[pl.ds(t, 1), :] * diag, llo_ref[pl.ds(t, 1), :] * diag], axis=0).astype(BF16)
        acc = jnp.dot(left, _gathered_tiles(tab_ref, off_ref, t, nsel), preferred_element_type=F32)
        y = acc[:SUBLANES] + acc[SUBLANES:]
        y_row = jnp.concatenate([y[s:s + 1, :] for s in range(SUBLANES)], axis=1)
        o_ref[pl.ds(t, 1), :] = x1_ref[pl.ds(t, 1), :] + gt * y_row

    def tok(i, carry):
        for u in range(TOKENS_PER_ITER):
            one(i * TOKENS_PER_ITER + u)
        return carry

    lax.fori_loop(0, tb // TOKENS_PER_ITER, tok, 0)


def _peer_v(off, coef, par, tab, x1, gt2, tb, tok_per_batch):
    t, nsel = coef.shape
    d = x1.shape[1]
    bpb = tok_per_batch // tb
    vspec = pl.BlockSpec((tb, nsel), lambda i: (i, 0))
    return pl.pallas_call(
        _peer_v_kernel,
        out_shape=jax.ShapeDtypeStruct(x1.shape, F32),
        grid=(t // tb,),
        in_specs=[_smem_block(tb, nsel), vspec, vspec,
                  pl.BlockSpec(tab.shape, lambda i: (0, 0), pipeline_mode=pl.Buffered(1)),
                  pl.BlockSpec((tb, d), lambda i: (i, 0)),
                  pl.BlockSpec((1, 1, d), lambda i: (i // bpb, 0, 0))],
        out_specs=pl.BlockSpec((tb, d), lambda i: (i, 0)),
        scratch_shapes=[pltpu.VMEM((tb, nsel * TILE_ROWS), F32), pltpu.VMEM((tb, nsel * TILE_ROWS), F32)],
        compiler_params=_params(("arbitrary",)),
        name="peer_v",
    )(off, coef, par, tab, x1, gt2)


def _rope_tables(n):
    axis = MLA_ROPE // 2
    t = jnp.arange(n, dtype=F32)
    row = jnp.floor(t / GRID_W)
    col = t - row * GRID_W
    inv = ROPE_BASE ** (-jnp.arange(axis // 2, dtype=F32) * (2.0 / axis))
    ar = row[:, None] * inv
    ac = col[:, None] * inv
    cos = jnp.concatenate([jnp.cos(ar), jnp.cos(ar), jnp.cos(ac), jnp.cos(ac)], axis=1)
    sin = jnp.concatenate([-jnp.sin(ar), jnp.sin(ar), -jnp.sin(ac), jnp.sin(ac)], axis=1)
    return cos, sin


PACK_ROWS = 256


def _pack_kernel(t_ref, o_ref):
    n = t_ref.shape[0]
    rows = n // 2
    xb = t_ref[...].astype(BF16)
    col = lax.broadcasted_iota(I32, (rows, n), 1)
    row2 = 2 * lax.broadcasted_iota(I32, (rows, n), 0)
    even = pltpu.bitcast(jnp.dot((col == row2).astype(BF16), xb, preferred_element_type=F32), jnp.uint32)
    odd = pltpu.bitcast(jnp.dot((col == row2 + 1).astype(BF16), xb, preferred_element_type=F32), jnp.uint32)
    word = odd | (even >> 16)
    for s in range(SUBLANES):
        o_ref[pl.ds(s, rows, stride=SUBLANES), :] = word[:, s * LANES:(s + 1) * LANES]


def _pack_table(tab):
    e, d = tab.shape
    out_rows = PACK_ROWS // 2 * (d // LANES)
    return pl.pallas_call(
        _pack_kernel,
        out_shape=jax.ShapeDtypeStruct((e // 2 * (d // LANES), LANES), jnp.uint32),
        grid=(e // PACK_ROWS,),
        in_specs=[pl.BlockSpec((PACK_ROWS, d), lambda i: (i, 0))],
        out_specs=pl.BlockSpec((out_rows, LANES), lambda i: (i, 0)),
        compiler_params=_params(("parallel",)),
        name="pack_table",
    )(tab)


def _block(n, want):
    return want if n % want == 0 else n


def kernel(x, c, ctx, c_ctx, w_ada, b_ada, g_norm1, w_in, g_cq, w_uq, g_ckv, w_ukv, g_qn, g_kn, conv_qk, b_igate, b_fgate, g_mlstm, w_out, g_norm2, w_pq, sub_keys, expert_u, expert_v):
    B, N, D = x.shape
    NC = ctx.shape[1]
    assert w_ada.shape[0] == 1, "one layer"
    assert N % (256 * ATTN_TILES_PER_STEP) == 0 and NC % ML_CHUNK == 0 and D == SUBLANES * LANES
    q_rank = g_cq.shape[1]
    kv_rank = g_ckv.shape[1]
    mla_cols = q_rank + kv_rank + MLA_ROPE
    assert (q_rank, kv_rank) == (C_KV0 - C_Q0, C_KR0 - C_KV0)
    swap = jnp.arange(MLA_ROPE) ^ (MLA_ROPE // 4)

    cc = jnp.concatenate([c, c_ctx[None, :], jnp.zeros((16 - B - 1, D), F32)], axis=0)
    mod = _ada(cc, w_ada[0].astype(BF16), b_ada)
    sh1, sc1, gt1, sh2, sc2, gt2 = [mod[:, i * D:(i + 1) * D] for i in range(6)]
    lat = lambda m: m[:B].reshape(B, 1, D)
    ctxm = lambda m: jnp.broadcast_to(m[B:B + 1].reshape(1, 1, D), (B, 1, D))

    wi = w_in[0]
    n_qk = 2 * ML_HEADS * ML_QK
    n_v = ML_HEADS * ML_V
    n_g = 4 * ML_HEADS
    m0 = mla_cols
    w_cols = jnp.concatenate([
        wi[:, 0:mla_cols],
        wi[:, q_rank + kv_rank + swap],
        wi[:, m0:m0 + n_qk + 2 * n_v + n_g],
        jnp.zeros((D, C_END - C_G0 - n_g), F32)], axis=1).astype(BF16)
    assert w_cols.shape[1] == C_END
    w_gates_t = wi[:, m0 + n_qk + 2 * n_v:m0 + n_qk + 2 * n_v + n_g].T.astype(BF16)
    cols_l, gates_l = _inproj(x, g_norm1, lat(sc1), lat(sh1), w_cols, w_gates_t, _block(N, 512))
    cols_c, gates_c = _inproj(ctx, g_norm1, ctxm(sc1), ctxm(sh1), w_cols, w_gates_t, _block(NC, 512))

    wq = w_uq[0].reshape(q_rank, MLA_HEADS, MLA_QK)
    wuq = jnp.concatenate([wq, wq[:, :, MLA_NOPE + swap]], axis=2).reshape(q_rank, MLA_HEADS * 256).astype(BF16)
    wukv = w_ukv[0].astype(BF16)
    gqs, gks = g_qn[:, MLA_NOPE + swap], g_kn[:, MLA_NOPE + swap]
    cos_l, sin_l = _rope_tables(N)
    cos_c, sin_c = jnp.ones((NC, MLA_ROPE), F32), jnp.zeros((NC, MLA_ROPE), F32)
    q_l, k_l, v_l = _mla_prep(cols_l, g_cq, wuq, g_ckv, wukv, g_qn, gqs, g_kn, gks, cos_l, sin_l, _block(N, 512))
    _, k_c, v_c = _mla_prep(cols_c, g_cq, wuq, g_ckv, wukv, g_qn, gqs, g_kn, gks, cos_c, sin_c, _block(NC, 512))
    mla = _attention(q_l, k_c, k_l, v_c, v_l, 256)

    bias16 = jnp.concatenate([b_igate[0].reshape(-1), b_fgate[0].reshape(-1)])
    cw = conv_qk[0]
    pl_ = _ml_prep(cols_l, gates_l, cw, bias16[None, :], bias16[:, None], _block(N, 512))
    pc_ = _ml_prep(cols_c, gates_c, cw, bias16[None, :], bias16[:, None], _block(NC, 512))
    hsum = _ml_scan(*pl_, *pc_)

    wo = w_out[0].astype(BF16)
    hw = MLA_HEADS * MLA_V
    sk = sub_keys[0].reshape(2 * PEER_HEADS, N_KEYS, -1).astype(BF16)
    x1, h2, scores = _outproj(x, mla, hsum, cols_l, g_mlstm, wo[:hw], wo[hw:], lat(gt1), g_norm2, lat(sc2), lat(sh2),
                              w_pq[0].astype(BF16), sk, 256)
    T = B * N
    nsel = PEER_HEADS * PEER_TOPK
    off, par, gw = [a.reshape(T, nsel) for a in _route(scores, 256)]

    tb = 32
    coef = _peer_u(off, h2.reshape(T, D), _pack_table(expert_u[0]), par, gw, tb)
    out = _peer_v(off, coef, par, _pack_table(expert_v[0]), x1.reshape(T, D), lat(gt2), tb, N)
    return out.reshape(B, N, D)
```

```python
import jax
import jax.numpy as jnp
from jax import lax
from jax.experimental import pallas as pl
from jax.experimental.pallas import tpu as pltpu

F32 = jnp.float32
BF16 = jnp.bfloat16
I32 = jnp.int32
EPS = 1e-6

GRID_W = 64
MLA_HEADS = 4
MLA_NOPE = 128
MLA_ROPE = 64
MLA_V = 128
MLA_QK = MLA_NOPE + MLA_ROPE
ROPE_BASE = 10000.0
ML_HEADS = 4
ML_QK = 64
ML_V = 128
ML_CHUNK = 64
PEER_HEADS = 8
N_KEYS = 128
PEER_TOPK = 16

LANES = 128
SUBLANES = 8
VMEM_LIMIT_BYTES = 56 * 1024 * 1024

C_Q0, C_KV0, C_KR0, C_KRS0, C_QK0, C_V0, C_O0, C_G0, C_END = 0, 256, 384, 448, 512, 1024, 1536, 2048, 2176
HIGHEST = lax.Precision.HIGHEST


def _params(sem):
    return pltpu.CompilerParams(dimension_semantics=sem, vmem_limit_bytes=VMEM_LIMIT_BYTES)


def _rms(x):
    return x * lax.rsqrt(jnp.mean(x * x, axis=-1, keepdims=True) + EPS)


def _ada_kernel(c_ref, w_ref, b_ref, o_ref):
    c = c_ref[...]
    s = c * jax.nn.sigmoid(c)
    o_ref[...] = jnp.dot(s.astype(BF16), w_ref[...], preferred_element_type=F32) + b_ref[...]


def _ada(cc, w, b):
    rows, d = cc.shape
    n = w.shape[1]
    bn = n // 4
    return pl.pallas_call(
        _ada_kernel,
        out_shape=jax.ShapeDtypeStruct((rows, n), F32),
        grid=(4,),
        in_specs=[pl.BlockSpec((rows, d), lambda j: (0, 0)),
                  pl.BlockSpec((d, bn), lambda j: (0, j)),
                  pl.BlockSpec((1, bn), lambda j: (0, j))],
        out_specs=pl.BlockSpec((rows, bn), lambda j: (0, j)),
        compiler_params=_params(("arbitrary",)),
        name="ada",
    )(cc, w, b)


def _inproj_kernel(x_ref, g_ref, sc_ref, sh_ref, w_ref, wg_ref, o_ref, gt_ref):
    x = x_ref[0]
    h = _rms(x) * g_ref[...] * (1.0 + sc_ref[0]) + sh_ref[0]
    hb = h.astype(BF16)
    o_ref[0] = jnp.dot(hb, w_ref[...], preferred_element_type=F32)
    gt_ref[0] = lax.dot_general(wg_ref[...], hb, (((1,), (1,)), ((), ())), preferred_element_type=F32)


def _inproj(x, g, sc, sh, w, wg, tm):
    b, n, d = x.shape
    nc = w.shape[1]
    return pl.pallas_call(
        _inproj_kernel,
        out_shape=(jax.ShapeDtypeStruct((b, n, nc), F32), jax.ShapeDtypeStruct((b, 16, n), F32)),
        grid=(b, n // tm),
        in_specs=[pl.BlockSpec((1, tm, d), lambda i, j: (i, j, 0)),
                  pl.BlockSpec((1, d), lambda i, j: (0, 0)),
                  pl.BlockSpec((1, 1, d), lambda i, j: (i, 0, 0)),
                  pl.BlockSpec((1, 1, d), lambda i, j: (i, 0, 0)),
                  pl.BlockSpec((d, nc), lambda i, j: (0, 0)),
                  pl.BlockSpec((16, d), lambda i, j: (0, 0))],
        out_specs=(pl.BlockSpec((1, tm, nc), lambda i, j: (i, j, 0)),
                   pl.BlockSpec((1, 16, tm), lambda i, j: (i, 0, j))),
        compiler_params=_params(("parallel", "parallel")),
        name="inproj",
    )(x, g, sc, sh, w, wg)


def _mla_prep_kernel(c_ref, gcq_ref, wuq_ref, gckv_ref, wukv_ref, gq_ref, gqs_ref, gk_ref, gks_ref,
                     cos_ref, sin_ref, q_ref, k_ref, v_ref):
    c = c_ref[0]
    cq = _rms(c[:, C_Q0:C_KV0]) * gcq_ref[...]
    ckv = _rms(c[:, C_KV0:C_KR0]) * gckv_ref[...]
    kr = c[:, C_KR0:C_KRS0]
    krs = c[:, C_KRS0:C_QK0]
    q_raw = jnp.dot(cq.astype(BF16), wuq_ref[...], preferred_element_type=F32)
    kv_raw = jnp.dot(ckv.astype(BF16), wukv_ref[...], preferred_element_type=F32)
    cos = cos_ref[...]
    sin = sin_ref[...]
    gq = gq_ref[...]
    gk = gk_ref[...]
    kr_ss = jnp.sum(kr * kr, axis=-1, keepdims=True)
    for h in range(MLA_HEADS):
        o = h * 256
        qn = q_raw[:, o:o + 128]
        qr = q_raw[:, o + 128:o + 192]
        qs = q_raw[:, o + 192:o + 256]
        ss = jnp.sum(qn * qn, axis=-1, keepdims=True) + jnp.sum(qr * qr, axis=-1, keepdims=True)
        r = lax.rsqrt(ss * (1.0 / MLA_QK) + EPS) * (MLA_QK ** -0.5)
        q_ref[0, h, :, 0:128] = (qn * r * gq[:, 0:128]).astype(BF16)
        q_ref[0, h, :, 128:192] = ((qr * r * gq[:, 128:192]) * cos + (qs * r * gqs_ref[...]) * sin).astype(BF16)
        kn = kv_raw[:, o:o + 128]
        ss = jnp.sum(kn * kn, axis=-1, keepdims=True) + kr_ss
        r = lax.rsqrt(ss * (1.0 / MLA_QK) + EPS)
        k_ref[0, h, :, 0:128] = (kn * r * gk[:, 0:128]).astype(BF16)
        k_ref[0, h, :, 128:192] = ((kr * r * gk[:, 128:192]) * cos + (krs * r * gks_ref[...]) * sin).astype(BF16)
        v_ref[0, h] = kv_raw[:, o + 128:o + 256].astype(BF16)


def _mla_prep(cols, gcq, wuq, gckv, wukv, gq, gqs, gk, gks, cos, sin, tm):
    b, n, _ = cols.shape
    full = lambda a: pl.BlockSpec(a.shape, lambda i, j: (0,) * a.ndim)
    return pl.pallas_call(
        _mla_prep_kernel,
        out_shape=(jax.ShapeDtypeStruct((b, MLA_HEADS, n, MLA_QK), BF16),
                   jax.ShapeDtypeStruct((b, MLA_HEADS, n, MLA_QK), BF16),
                   jax.ShapeDtypeStruct((b, MLA_HEADS, n, MLA_V), BF16)),
        grid=(b, n // tm),
        in_specs=[pl.BlockSpec((1, tm, 512), lambda i, j: (i, j, 0)),
                  full(gcq), full(wuq), full(gckv), full(wukv), full(gq), full(gqs), full(gk), full(gks),
                  pl.BlockSpec((tm, MLA_ROPE), lambda i, j: (j, 0)),
                  pl.BlockSpec((tm, MLA_ROPE), lambda i, j: (j, 0))],
        out_specs=(pl.BlockSpec((1, MLA_HEADS, tm, MLA_QK), lambda i, j: (i, 0, j, 0)),
                   pl.BlockSpec((1, MLA_HEADS, tm, MLA_QK), lambda i, j: (i, 0, j, 0)),
                   pl.BlockSpec((1, MLA_HEADS, tm, MLA_V), lambda i, j: (i, 0, j, 0))),
        compiler_params=_params(("parallel", "parallel")),
        name="mla_prep",
    )(cols, gcq, wuq, gckv, wukv, gq, gqs, gk, gks, cos, sin)


ATTN_TILES_PER_STEP = 2


def _attn_kernel(q_ref, kc_ref, kl_ref, vc_ref, vl_ref, o_ref):
    tq = q_ref.shape[2] // ATTN_TILES_PER_STEP
    nc = kc_ref.shape[2]
    kc, kl, vc, vl = kc_ref[0, 0], kl_ref[0, 0], vc_ref[0, 0], vl_ref[0, 0]
    nt = (((1,), (1,)), ((), ()))
    for j in range(ATTN_TILES_PER_STEP):
        q = q_ref[0, 0, j * tq:(j + 1) * tq, :]
        s = jnp.concatenate([lax.dot_general(q, kc, nt, preferred_element_type=F32),
                             lax.dot_general(q, kl, nt, preferred_element_type=F32)], axis=1)
        m = jnp.max(s, axis=-1, keepdims=True)
        p = jnp.exp(s - m)
        l = jnp.sum(p, axis=-1, keepdims=True)
        pb = p.astype(BF16)
        o = (jnp.dot(pb[:, :nc], vc, preferred_element_type=F32)
             + jnp.dot(pb[:, nc:], vl, preferred_element_type=F32))
        o_ref[0, j * tq:(j + 1) * tq, :] = (o / l).astype(BF16)


def _attention(q, k_c, k_l, v_c, v_l, tq):
    b, h, n, dk = q.shape
    nc = k_c.shape[2]
    dv = v_l.shape[3]
    tb = tq * ATTN_TILES_PER_STEP
    whole = lambda nn, dd: pl.BlockSpec((1, 1, nn, dd), lambda i, j, t: (i, j, 0, 0))
    return pl.pallas_call(
        _attn_kernel,
        out_shape=jax.ShapeDtypeStruct((b, n, h * dv), BF16),
        grid=(b, h, n // tb),
        in_specs=[pl.BlockSpec((1, 1, tb, dk), lambda i, j, t: (i, j, t, 0)),
                  whole(nc, dk), whole(n, dk), whole(nc, dv), whole(n, dv)],
        out_specs=pl.BlockSpec((1, tb, dv), lambda i, j, t: (i, t, j)),
        compiler_params=_params(("parallel", "parallel", "parallel")),
        name="attention",
    )(q, k_c, k_l, v_c, v_l)


def _ml_prep_kernel(qk_ref, prev_ref, next_ref, vin_ref, g_ref, gt_ref, cw_ref, bc_ref, br_ref,
                    q_ref, kt_ref, v_ref, a_ref, brow_ref):
    tn = qk_ref.shape[1]
    j = pl.program_id(1)
    nj = pl.num_programs(1)
    u = qk_ref[0]
    row = lax.broadcasted_iota(I32, (tn, 1), 0)
    before = jnp.where(j == 0, 0.0, prev_ref[0, SUBLANES - 1:SUBLANES, :])
    after = jnp.where(j == nj - 1, 0.0, next_ref[0, 0:1, :])
    up = jnp.where(row == 0, before, pltpu.roll(u, 1, 0))
    un = jnp.where(row == tn - 1, after, pltpu.roll(u, tn - 1, 0))
    cw = cw_ref[...]
    y = up * cw[0:1, :] + u * cw[1:2, :] + un * cw[2:3, :]
    y = y * jax.nn.sigmoid(y)
    hq = ML_HEADS * ML_QK
    for h in range(ML_HEADS):
        q_ref[0, h] = (y[:, h * ML_QK:(h + 1) * ML_QK] * (ML_QK ** -0.5)).astype(BF16)
    kt = y[:, hq:2 * hq].T
    L = ML_CHUNK
    for c in range(tn // L):
        kt_ref[0, c] = kt[:, c * L:(c + 1) * L].astype(BF16)
    v_ref[0] = vin_ref[0].astype(BF16)
    g = g_ref[0][:, 0:16] + bc_ref[...]
    lf_c = jax.nn.log_sigmoid(g[:, 8:16])
    gt = gt_ref[0] + br_ref[...]
    ig_r = gt[0:8, :]
    lf_r = jax.nn.log_sigmoid(gt[8:16, :])
    ti = lax.broadcasted_iota(I32, (L, L), 0)
    si = lax.broadcasted_iota(I32, (L, L), 1)
    lower = (si <= ti).astype(F32)
    upper = (si >= ti).astype(F32)
    lane_fwd = lax.broadcasted_iota(I32, (L, 8), 1) < ML_HEADS
    row_fwd = lax.broadcasted_iota(I32, (8, L), 0) < ML_HEADS
    for c in range(tn // L):
        lo = c * L
        lfc = lf_c[lo:lo + L, :]
        a_ref[0, lo:lo + L, :] = jnp.where(
            lane_fwd,
            jnp.dot(lower, lfc, precision=HIGHEST, preferred_element_type=F32),
            jnp.dot(upper, lfc, precision=HIGHEST, preferred_element_type=F32))
        lfr = lf_r[:, lo:lo + L]
        cf_r = jnp.where(
            row_fwd,
            jnp.dot(lfr, upper, precision=HIGHEST, preferred_element_type=F32),
            jnp.dot(lfr, lower, precision=HIGHEST, preferred_element_type=F32))
        brow_ref[0, c] = ig_r[:, lo:lo + L] - cf_r


def _ml_prep(cols, gates_t, cw, bias_col, bias_row, tn):
    b, n, _ = cols.shape
    r8 = tn // SUBLANES
    last8 = n // SUBLANES - 1
    L = ML_CHUNK
    hq = ML_HEADS * ML_QK
    return pl.pallas_call(
        _ml_prep_kernel,
        out_shape=(jax.ShapeDtypeStruct((b, ML_HEADS, n, ML_QK), BF16),
                   jax.ShapeDtypeStruct((b, n // L, hq, L), BF16),
                   jax.ShapeDtypeStruct((b, n, ML_HEADS * ML_V), BF16),
                   jax.ShapeDtypeStruct((b, n, 8), F32),
                   jax.ShapeDtypeStruct((b, n // L, 8, L), F32)),
        grid=(b, n // tn),
        in_specs=[pl.BlockSpec((1, tn, 512), lambda i, j: (i, j, C_QK0 // 512)),
                  pl.BlockSpec((1, SUBLANES, 512), lambda i, j: (i, jnp.maximum(j * r8 - 1, 0), C_QK0 // 512)),
                  pl.BlockSpec((1, SUBLANES, 512), lambda i, j: (i, jnp.minimum((j + 1) * r8, last8), C_QK0 // 512)),
                  pl.BlockSpec((1, tn, 512), lambda i, j: (i, j, C_V0 // 512)),
                  pl.BlockSpec((1, tn, 128), lambda i, j: (i, j, C_G0 // 128)),
                  pl.BlockSpec((1, 16, tn), lambda i, j: (i, 0, j)),
                  pl.BlockSpec((3, 512), lambda i, j: (0, 0)),
                  pl.BlockSpec((1, 16), lambda i, j: (0, 0)),
                  pl.BlockSpec((16, 1), lambda i, j: (0, 0))],
        out_specs=(pl.BlockSpec((1, ML_HEADS, tn, ML_QK), lambda i, j: (i, 0, j, 0)),
                   pl.BlockSpec((1, tn // L, hq, L), lambda i, j: (i, j, 0, 0)),
                   pl.BlockSpec((1, tn, ML_HEADS * ML_V), lambda i, j: (i, j, 0)),
                   pl.BlockSpec((1, tn, 8), lambda i, j: (i, j, 0)),
                   pl.BlockSpec((1, tn // L, 8, L), lambda i, j: (i, j, 0, 0))),
        compiler_params=_params(("parallel", "parallel")),
        name="ml_prep",
    )(cols, cols, cols, cols, cols, gates_t, cw, bias_col, bias_row)


def _ml_chunk(qb, kt, vaug, a_c, b_r, ct, m_prev, fwd, need_out):
    L = qb.shape[0]
    ti = lax.broadcasted_iota(I32, (L, L), 0)
    si = lax.broadcasted_iota(I32, (L, L), 1)
    mask = (si <= ti) if fwd else (si >= ti)
    dmat = jnp.where(mask, a_c + b_r, -jnp.inf)
    mloc = jnp.max(dmat, axis=1, keepdims=True)
    inter = a_c + m_prev
    m_t = jnp.maximum(inter, mloc)
    last = L - 1 if fwd else 0
    m_new = m_t[last:last + 1, :]
    a_last = a_c[last:last + 1, :]
    w_r = jnp.exp(a_last + b_r - m_new)
    decay = jnp.exp(a_last + m_prev - m_new)
    kw = (kt.astype(F32) * w_r).astype(BF16)
    if not need_out:
        return decay * ct + jnp.dot(kw, vaug, preferred_element_type=F32), m_new, None
    s = jnp.dot(qb, kt, preferred_element_type=F32)
    wmat = jnp.exp(dmat - m_t) * s
    sc = jnp.exp(inter - m_t)
    top = jnp.concatenate([wmat.astype(BF16), (sc * qb.astype(F32)).astype(BF16)], axis=1)
    bot = jnp.concatenate([kw, jnp.zeros(kw.shape, BF16)], axis=1)
    res = jnp.dot(jnp.concatenate([top, bot], axis=0), jnp.concatenate([vaug, ct.astype(BF16)], axis=0),
                  preferred_element_type=F32)
    nd = res[:L]
    num = nd[:, 0:ML_V]
    den = nd[:, ML_V:ML_V + 1]
    return decay * ct + res[L:], m_new, num / jnp.maximum(jnp.abs(den), jnp.exp(-m_t))


def _ml_scan_kernel(ql_ref, ktl_ref, vl_ref, al_ref, brl_ref, qc_ref, ktc_ref, vc_ref, ac_ref, brc_ref,
                    out_ref, st_ref, m_ref):
    L = ML_CHUNK
    ncl = ql_ref.shape[2] // L
    ncc = qc_ref.shape[2] // L
    ones_col = (lax.broadcasted_iota(I32, (L, ML_V), 1) == 0).astype(BF16)
    st_ref[...] = jnp.zeros(st_ref.shape, F32)
    m_ref[...] = jnp.zeros(m_ref.shape, F32)

    def step(refs, c, d, fwd, need_out):
        q_ref, kt_ref, v_ref, a_ref, br_ref = refs
        sl = pl.ds(pl.multiple_of(c * L, L), L)
        a_all = a_ref[0, sl, :]
        br_all = br_ref[0, c]
        v_all = v_ref[0, sl, :]
        kt_all = kt_ref[0, c]
        hs = []
        for h in range(ML_HEADS):
            j = d * ML_HEADS + h
            vaug = jnp.concatenate([v_all[:, h * ML_V:(h + 1) * ML_V], ones_col], axis=1)
            ct_new, m_new, hh = _ml_chunk(
                q_ref[0, h, sl, :], kt_all[h * ML_QK:(h + 1) * ML_QK, :], vaug,
                a_all[:, j:j + 1], br_all[j:j + 1, :], st_ref[j], m_ref[j][0:1, 0:1], fwd, need_out)
            st_ref[j] = ct_new
            m_ref[j] = jnp.broadcast_to(m_new, (SUBLANES, LANES))
            hs.append(hh)
        return hs, sl

    ctx_refs = (qc_ref, ktc_ref, vc_ref, ac_ref, brc_ref)
    lat_refs = (ql_ref, ktl_ref, vl_ref, al_ref, brl_ref)

    out_ref[...] = jnp.zeros(out_ref.shape, F32)

    def ctx_body(i, carry):
        step(ctx_refs, i, 0, True, False)
        step(ctx_refs, ncc - 1 - i, 1, False, False)
        return carry

    def lat_body(i, carry):
        for d, fwd in ((0, True), (1, False)):
            hs, sl = step(lat_refs, i if fwd else ncl - 1 - i, d, fwd, True)
            out_ref[0, sl, :] = out_ref[0, sl, :] + jnp.concatenate(hs, axis=1)
        return carry

    lax.fori_loop(0, ncc, ctx_body, 0)
    lax.fori_loop(0, ncl, lat_body, 0)


def _ml_scan(ql, ktl, vl, al, brl, qc, ktc, vc, ac, brc):
    b, _, n, _ = ql.shape
    nctx = qc.shape[2]
    L = ML_CHUNK
    hv = ML_HEADS * ML_V
    hq = ML_HEADS * ML_QK
    qspec = lambda nn: pl.BlockSpec((1, ML_HEADS, nn, ML_QK), lambda i: (i, 0, 0, 0))
    ktspec = lambda nn: pl.BlockSpec((1, nn // L, hq, L), lambda i: (i, 0, 0, 0))
    vspec = lambda nn: pl.BlockSpec((1, nn, hv), lambda i: (i, 0, 0))
    aspec = lambda nn: pl.BlockSpec((1, nn, 8), lambda i: (i, 0, 0))
    rspec = lambda nn: pl.BlockSpec((1, nn // L, 8, L), lambda i: (i, 0, 0, 0))
    return pl.pallas_call(
        _ml_scan_kernel,
        out_shape=jax.ShapeDtypeStruct((b, n, hv), F32),
        grid=(b,),
        in_specs=[qspec(n), ktspec(n), vspec(n), aspec(n), rspec(n),
                  qspec(nctx), ktspec(nctx), vspec(nctx), aspec(nctx), rspec(nctx)],
        out_specs=pl.BlockSpec((1, n, hv), lambda i: (i, 0, 0)),
        scratch_shapes=[pltpu.VMEM((2 * ML_HEADS, ML_QK, 2 * ML_V), F32),
                        pltpu.VMEM((2 * ML_HEADS, SUBLANES, LANES), F32)],
        compiler_params=_params(("parallel",)),
        name="ml_scan",
    )(ql, ktl, vl, al, brl, qc, ktc, vc, ac, brc)


def _outproj_kernel(x_ref, mla_ref, hs_ref, o_ref, gm_ref, wa_ref, wb_ref, gt_ref, g2_ref, sc_ref, sh_ref,
                    wpq_ref, sk_ref, x1_ref, h2_ref, s_ref):
    hs = hs_ref[0]
    gm = gm_ref[...]
    hn = jnp.concatenate([_rms(hs[:, h * ML_V:(h + 1) * ML_V]) * gm[:, h * ML_V:(h + 1) * ML_V]
                          for h in range(ML_HEADS)], axis=1)
    ml = (jax.nn.sigmoid(o_ref[0]) * hn).astype(BF16)
    mix = (jnp.dot(mla_ref[0], wa_ref[...], preferred_element_type=F32)
           + jnp.dot(ml, wb_ref[...], preferred_element_type=F32))
    x1 = x_ref[0] + gt_ref[0] * mix
    x1_ref[0] = x1
    h2 = _rms(x1) * g2_ref[...] * (1.0 + sc_ref[0]) + sh_ref[0]
    h2_ref[0] = h2
    qp = jnp.dot(h2.astype(BF16), wpq_ref[...], preferred_element_type=F32).astype(BF16)
    for hp in range(2 * PEER_HEADS):
        s_ref[0, hp] = lax.dot_general(sk_ref[hp], qp[:, hp * N_KEYS:(hp + 1) * N_KEYS],
                                       (((1,), (1,)), ((), ())), preferred_element_type=F32)


def _outproj(x, mla, hsum, cols, gm, wa, wb, gt1, g2, sc2, sh2, wpq, sk, tm):
    b, n, d = x.shape
    hw = mla.shape[2]
    nq = wpq.shape[1]
    mod = pl.BlockSpec((1, 1, d), lambda i, j: (i, 0, 0))
    return pl.pallas_call(
        _outproj_kernel,
        out_shape=(jax.ShapeDtypeStruct((b, n, d), F32), jax.ShapeDtypeStruct((b, n, d), F32),
                   jax.ShapeDtypeStruct((b, 2 * PEER_HEADS, N_KEYS, n), F32)),
        grid=(b, n // tm),
        in_specs=[pl.BlockSpec((1, tm, d), lambda i, j: (i, j, 0)),
                  pl.BlockSpec((1, tm, hw), lambda i, j: (i, j, 0)),
                  pl.BlockSpec((1, tm, hw), lambda i, j: (i, j, 0)),
                  pl.BlockSpec((1, tm, 512), lambda i, j: (i, j, C_O0 // 512)),
                  pl.BlockSpec((1, hw), lambda i, j: (0, 0)),
                  pl.BlockSpec((hw, d), lambda i, j: (0, 0)),
                  pl.BlockSpec((hw, d), lambda i, j: (0, 0)),
                  mod, pl.BlockSpec((1, d), lambda i, j: (0, 0)), mod, mod,
                  pl.BlockSpec((d, nq), lambda i, j: (0, 0)),
                  pl.BlockSpec((2 * PEER_HEADS, N_KEYS, N_KEYS), lambda i, j: (0, 0, 0))],
        out_specs=(pl.BlockSpec((1, tm, d), lambda i, j: (i, j, 0)),
                   pl.BlockSpec((1, tm, d), lambda i, j: (i, j, 0)),
                   pl.BlockSpec((1, 2 * PEER_HEADS, N_KEYS, tm), lambda i, j: (i, 0, 0, j))),
        compiler_params=_params(("parallel", "parallel")),
        name="outproj",
    )(x, mla, hsum, cols, gm, wa, wb, gt1, g2, sc2, sh2, wpq, sk)


def _topk_rows(s, k, rid=None):
    rows, t = s.shape
    if rid is None:
        rid = lax.broadcasted_iota(I32, (rows, t), 0)
    kid = lax.broadcasted_iota(I32, (k, t), 0)
    big = jnp.iinfo(jnp.int32).max

    def body(r, carry):
        s, vals, idxs = carry
        m = jnp.max(s, axis=0, keepdims=True)
        i = jnp.min(jnp.where(s == m, rid, big), axis=0, keepdims=True)
        vals = jnp.where(kid == r, m, vals)
        idxs = jnp.where(kid == r, i, idxs)
        s = jnp.where(rid == i, -jnp.inf, s)
        return s, vals, idxs

    _, vals, idxs = lax.fori_loop(0, k, body, (s, jnp.zeros((k, t), F32), jnp.zeros((k, t), I32)))
    return vals, idxs


def _pair_candidates(sv0, sv1):
    K, t = sv0.shape
    h = K // 2
    iid = lax.broadcasted_iota(I32, (h, t), 0)
    vals = [sv0[0:h] + sv1[0:1], sv0[h:K] + sv1[0:1]]
    ids = [iid * K, (iid + h) * K]
    for j in range(1, h):
        vals.append(sv0[0:h] + sv1[j:j + 1])
        ids.append(iid * K + j)
    vals.append(sv0[0:1] + sv1[h:K])
    ids.append(iid + h)
    return jnp.concatenate(vals, axis=0), jnp.concatenate(ids, axis=0)


def _route_kernel(s_ref, off_ref, par_ref, gw_ref, et_ref, gt_ref):
    K = PEER_TOPK

    def head(h, carry):
        sv0, si0 = _topk_rows(s_ref[0, 2 * h], K)
        sv1, si1 = _topk_rows(s_ref[0, 2 * h + 1], K)
        cand, cid = _pair_candidates(sv0, sv1)
        best, pos = _topk_rows(cand, K, cid)
        isel = pos >> 4
        jsel = pos & (K - 1)
        e0 = jnp.zeros_like(pos)
        e1 = jnp.zeros_like(pos)
        for i in range(K):
            e0 = jnp.where(isel == i, si0[i:i + 1, :], e0)
            e1 = jnp.where(jsel == i, si1[i:i + 1, :], e1)
        ex = jnp.exp(best - best[0:1, :])
        sl = pl.ds(pl.multiple_of(h * K, K), K)
        et_ref[sl, :] = e0 * N_KEYS + e1
        gt_ref[sl, :] = ex / jnp.sum(ex, axis=0, keepdims=True)
        return carry

    lax.fori_loop(0, PEER_HEADS, head, 0)
    e = et_ref[...].T
    off_ref[0] = (e >> 1) * SUBLANES
    par_ref[0] = (e & 1).astype(F32)
    gw_ref[0] = gt_ref[...].T


def _route(scores, tt):
    b, hp, nk, n = scores.shape
    nsel = PEER_HEADS * PEER_TOPK
    ospec = pl.BlockSpec((1, tt, nsel), lambda i, j: (i, j, 0))
    return pl.pallas_call(
        _route_kernel,
        out_shape=(jax.ShapeDtypeStruct((b, n, nsel), I32), jax.ShapeDtypeStruct((b, n, nsel), F32),
                   jax.ShapeDtypeStruct((b, n, nsel), F32)),
        grid=(b, n // tt),
        in_specs=[pl.BlockSpec((1, hp, nk, tt), lambda i, j: (i, 0, 0, j))],
        out_specs=(ospec, ospec, ospec),
        scratch_shapes=[pltpu.VMEM((nsel, tt), I32), pltpu.VMEM((nsel, tt), F32)],
        compiler_params=_params(("parallel", "parallel")),
        name="route",
    )(scores)


TILE_ROWS = 2 * SUBLANES
TOKENS_PER_ITER = 16


CHUNK_SEL = 16


def _chunk_tiles(tab_ref, row_ref, c):
    return jnp.concatenate(
        [pltpu.bitcast(tab_ref[pl.ds(pl.multiple_of(row_ref[c * CHUNK_SEL + kk], SUBLANES), SUBLANES), :], BF16)
         for kk in range(CHUNK_SEL)], axis=0)


def _init_spread(spread_ref, gather_ref=None):
    @pl.when(pl.program_id(0) == 0)
    def _():
        nsel, width = spread_ref.shape
        spread_ref[...] = (lax.broadcasted_iota(I32, (nsel, width), 1) // TILE_ROWS
                           == lax.broadcasted_iota(I32, (nsel, width), 0)).astype(BF16)
        if gather_ref is not None:
            gather_ref[...] = (lax.broadcasted_iota(I32, (width, nsel), 0) // TILE_ROWS
                               == lax.broadcasted_iota(I32, (width, nsel), 1)).astype(BF16)


def _row_masks(width):
    half = (lax.broadcasted_iota(I32, (1, width), 1) % 2).astype(F32)
    lane8 = lax.broadcasted_iota(I32, (SUBLANES, width), 1)
    diag = ((lane8 % TILE_ROWS) // 2 == lax.broadcasted_iota(I32, (SUBLANES, width), 0)).astype(F32)
    return half, diag


def _row_to_tile(row):
    return jnp.concatenate([row[:, s * LANES:(s + 1) * LANES] for s in range(SUBLANES)], axis=0)


def _split_bf16(v):
    hi = v.astype(BF16)
    return hi, (v - hi.astype(F32)).astype(BF16)


def _peer_u_kernel(off_ref, h_ref, tab_ref, par_ref, g_ref, c_ref, r_ref, spread_ref, gather_ref):
    tb, nsel = g_ref.shape
    _init_spread(spread_ref, gather_ref)
    half, diag = _row_masks(spread_ref.shape[1])

    cw = CHUNK_SEL * TILE_ROWS
    diag_c = diag[:, :cw]
    nt = (((1,), (1,)), ((), ()))

    def tok(i, carry):
        ts = [i * TOKENS_PER_ITER + u for u in range(TOKENS_PER_ITER)]
        rows = [off_ref.at[t] for t in ts]
        xs = [jnp.concatenate(_split_bf16(_row_to_tile(h_ref[pl.ds(t, 1), :])), axis=0) for t in ts]
        for c in range(nsel // CHUNK_SEL):
            for u, t in enumerate(ts):
                res = lax.dot_general(xs[u], _chunk_tiles(tab_ref, rows[u], c), nt, preferred_element_type=F32)
                r_ref[pl.ds(t, 1), c * cw:(c + 1) * cw] = jnp.sum(
                    (res[:SUBLANES] + res[SUBLANES:]) * diag_c, axis=0, keepdims=True)
        return carry

    lax.fori_loop(0, tb // TOKENS_PER_ITER, tok, 0)
    mine = jnp.dot(par_ref[...].astype(BF16), spread_ref[...], preferred_element_type=F32) == half
    r_hi, r_lo = _split_bf16(jnp.where(mine, r_ref[...], 0.0))
    both = jnp.dot(jnp.concatenate([r_hi, r_lo], axis=0), gather_ref[...], preferred_element_type=F32)
    pre = both[:tb] + both[tb:]
    c_ref[...] = g_ref[...] * (0.5 * pre * (1.0 + lax.erf(pre * (2.0 ** -0.5))))


def _smem_block(tb, nsel):
    return pl.BlockSpec((tb, nsel), lambda i: (i, 0), memory_space=pltpu.SMEM)


def _peer_u(off, h2, tab, par, gw, tb):
    t, nsel = gw.shape
    vspec = pl.BlockSpec((tb, nsel), lambda i: (i, 0))
    return pl.pallas_call(
        _peer_u_kernel,
        out_shape=jax.ShapeDtypeStruct((t, nsel), F32),
        grid=(t // tb,),
        in_specs=[_smem_block(tb, nsel),
                  pl.BlockSpec((tb, h2.shape[1]), lambda i: (i, 0)),
                  pl.BlockSpec(tab.shape, lambda i: (0, 0), pipeline_mode=pl.Buffered(1)),
                  vspec, vspec],
        out_specs=vspec,
        scratch_shapes=[pltpu.VMEM((tb, nsel * TILE_ROWS), F32),
                        pltpu.VMEM((nsel, nsel * TILE_ROWS), BF16), pltpu.VMEM((nsel * TILE_ROWS, nsel), BF16)],
        compiler_params=_params(("arbitrary",)),
        name="peer_u",
    )(off, h2, tab, par, gw)


def _peer_v_kernel(off_ref, c_ref, par_ref, tab_ref, x1_ref, gt_ref, o_ref, lhi_ref, llo_ref, spread_ref):
    tb, nsel = c_ref.shape
    _init_spread(spread_ref)
    half, diag = _row_masks(spread_ref.shape[1])
    c_hi, c_lo = _split_bf16(c_ref[...])
    ex = jnp.dot(jnp.concatenate([par_ref[...].astype(BF16), c_hi, c_lo], axis=0), spread_ref[...],
                 preferred_element_type=F32)
    mine = ex[:tb] == half
    lhi_ref[...] = jnp.where(mine, ex[tb:2 * tb], 0.0)
    llo_ref[...] = jnp.where(mine, ex[2 * tb:], 0.0)
    gt = gt_ref[0]

    cw = CHUNK_SEL * TILE_ROWS
    diag_c = diag[:, :cw]

    def tok(i, carry):
        ts = [i * TOKENS_PER_ITER + u for u in range(TOKENS_PER_ITER)]
        rows = [off_ref.at[t] for t in ts]
        accs = [jnp.zeros((TILE_ROWS, LANES), F32) for _ in ts]
        for c in range(nsel // CHUNK_SEL):
            cols = slice(c * cw, (c + 1) * cw)
            for u, t in enumerate(ts):
                left = jnp.concatenate([lhi_ref[pl.ds(t, 1), cols] * diag_c, llo_ref[pl.ds(t, 1), cols] * diag_c],
                                       axis=0).astype(BF16)
                accs[u] = accs[u] + jnp.dot(left, _chunk_tiles(tab_ref, rows[u], c), preferred_element_type=F32)
        for u, t in enumerate(ts):
            y = accs[u][:SUBLANES] + accs[u][SUBLANES:]
            y_row = jnp.concatenate([y[s:s + 1, :] for s in range(SUBLANES)], axis=1)
            o_ref[pl.ds(t, 1), :] = x1_ref[pl.ds(t, 1), :] + gt * y_row
        return carry

    lax.fori_loop(0, tb // TOKENS_PER_ITER, tok, 0)


def _peer_v(off, coef, par, tab, x1, gt2, tb, tok_per_batch):
    t, nsel = coef.shape
    d = x1.shape[1]
    bpb = tok_per_batch // tb
    vspec = pl.BlockSpec((tb, nsel), lambda i: (i, 0))
    return pl.pallas_call(
        _peer_v_kernel,
        out_shape=jax.ShapeDtypeStruct(x1.shape, F32),
        grid=(t // tb,),
        in_specs=[_smem_block(tb, nsel), vspec, vspec,
                  pl.BlockSpec(tab.shape, lambda i: (0, 0), pipeline_mode=pl.Buffered(1)),
                  pl.BlockSpec((tb, d), lambda i: (i, 0)),
                  pl.BlockSpec((1, 1, d), lambda i: (i // bpb, 0, 0))],
        out_specs=pl.BlockSpec((tb, d), lambda i: (i, 0)),
        scratch_shapes=[pltpu.VMEM((tb, nsel * TILE_ROWS), F32), pltpu.VMEM((tb, nsel * TILE_ROWS), F32),
                        pltpu.VMEM((nsel, nsel * TILE_ROWS), BF16)],
        compiler_params=_params(("arbitrary",)),
        name="peer_v",
    )(off, coef, par, tab, x1, gt2)


def _rope_tables(n):
    axis = MLA_ROPE // 2
    t = jnp.arange(n, dtype=F32)
    row = jnp.floor(t / GRID_W)
    col = t - row * GRID_W
    inv = ROPE_BASE ** (-jnp.arange(axis // 2, dtype=F32) * (2.0 / axis))
    ar = row[:, None] * inv
    ac = col[:, None] * inv
    cos = jnp.concatenate([jnp.cos(ar), jnp.cos(ar), jnp.cos(ac), jnp.cos(ac)], axis=1)
    sin = jnp.concatenate([-jnp.sin(ar), jnp.sin(ar), -jnp.sin(ac), jnp.sin(ac)], axis=1)
    return cos, sin


PACK_ROWS = 256


def _pack_kernel(t_ref, o_ref):
    n = t_ref.shape[0]
    rows = n // 2
    xb = t_ref[...].astype(BF16)
    col = lax.broadcasted_iota(I32, (rows, n), 1)
    row2 = 2 * lax.broadcasted_iota(I32, (rows, n), 0)
    even = pltpu.bitcast(jnp.dot((col == row2).astype(BF16), xb, preferred_element_type=F32), jnp.uint32)
    odd = pltpu.bitcast(jnp.dot((col == row2 + 1).astype(BF16), xb, preferred_element_type=F32), jnp.uint32)
    word = odd | (even >> 16)
    for s in range(SUBLANES):
        o_ref[pl.ds(s, rows, stride=SUBLANES), :] = word[:, s * LANES:(s + 1) * LANES]


def _pack_table(tab):
    e, d = tab.shape
    out_rows = PACK_ROWS // 2 * (d // LANES)
    return pl.pallas_call(
        _pack_kernel,
        out_shape=jax.ShapeDtypeStruct((e // 2 * (d // LANES), LANES), jnp.uint32),
        grid=(e // PACK_ROWS,),
        in_specs=[pl.BlockSpec((PACK_ROWS, d), lambda i: (i, 0))],
        out_specs=pl.BlockSpec((out_rows, LANES), lambda i: (i, 0)),
        compiler_params=_params(("parallel",)),
        name="pack_table",
    )(tab)


def _block(n, want):
    return want if n % want == 0 else n


def kernel(x, c, ctx, c_ctx, w_ada, b_ada, g_norm1, w_in, g_cq, w_uq, g_ckv, w_ukv, g_qn, g_kn, conv_qk, b_igate, b_fgate, g_mlstm, w_out, g_norm2, w_pq, sub_keys, expert_u, expert_v):
    B, N, D = x.shape
    NC = ctx.shape[1]
    assert w_ada.shape[0] == 1, "one layer"
    assert N % (256 * ATTN_TILES_PER_STEP) == 0 and NC % ML_CHUNK == 0 and D == SUBLANES * LANES
    q_rank = g_cq.shape[1]
    kv_rank = g_ckv.shape[1]
    mla_cols = q_rank + kv_rank + MLA_ROPE
    assert (q_rank, kv_rank) == (C_KV0 - C_Q0, C_KR0 - C_KV0)
    swap = jnp.arange(MLA_ROPE) ^ (MLA_ROPE // 4)

    cc = jnp.concatenate([c, c_ctx[None, :], jnp.zeros((16 - B - 1, D), F32)], axis=0)
    mod = _ada(cc, w_ada[0].astype(BF16), b_ada)
    sh1, sc1, gt1, sh2, sc2, gt2 = [mod[:, i * D:(i + 1) * D] for i in range(6)]
    lat = lambda m: m[:B].reshape(B, 1, D)
    ctxm = lambda m: jnp.broadcast_to(m[B:B + 1].reshape(1, 1, D), (B, 1, D))

    wi = w_in[0]
    n_qk = 2 * ML_HEADS * ML_QK
    n_v = ML_HEADS * ML_V
    n_g = 4 * ML_HEADS
    m0 = mla_cols
    w_cols = jnp.concatenate([
        wi[:, 0:mla_cols],
        wi[:, q_rank + kv_rank + swap],
        wi[:, m0:m0 + n_qk + 2 * n_v + n_g],
        jnp.zeros((D, C_END - C_G0 - n_g), F32)], axis=1).astype(BF16)
    assert w_cols.shape[1] == C_END
    w_gates_t = wi[:, m0 + n_qk + 2 * n_v:m0 + n_qk + 2 * n_v + n_g].T.astype(BF16)
    cols_l, gates_l = _inproj(x, g_norm1, lat(sc1), lat(sh1), w_cols, w_gates_t, _block(N, 512))
    cols_c, gates_c = _inproj(ctx, g_norm1, ctxm(sc1), ctxm(sh1), w_cols, w_gates_t, _block(NC, 512))

    wq = w_uq[0].reshape(q_rank, MLA_HEADS, MLA_QK)
    wuq = jnp.concatenate([wq, wq[:, :, MLA_NOPE + swap]], axis=2).reshape(q_rank, MLA_HEADS * 256).astype(BF16)
    wukv = w_ukv[0].astype(BF16)
    gqs, gks = g_qn[:, MLA_NOPE + swap], g_kn[:, MLA_NOPE + swap]
    cos_l, sin_l = _rope_tables(N)
    cos_c, sin_c = jnp.ones((NC, MLA_ROPE), F32), jnp.zeros((NC, MLA_ROPE), F32)
    q_l, k_l, v_l = _mla_prep(cols_l, g_cq, wuq, g_ckv, wukv, g_qn, gqs, g_kn, gks, cos_l, sin_l, _block(N, 512))
    _, k_c, v_c = _mla_prep(cols_c, g_cq, wuq, g_ckv, wukv, g_qn, gqs, g_kn, gks, cos_c, sin_c, _block(NC, 512))
    mla = _attention(q_l, k_c, k_l, v_c, v_l, 256)

    bias16 = jnp.concatenate([b_igate[0].reshape(-1), b_fgate[0].reshape(-1)])
    cw = conv_qk[0]
    pl_ = _ml_prep(cols_l, gates_l, cw, bias16[None, :], bias16[:, None], _block(N, 512))
    pc_ = _ml_prep(cols_c, gates_c, cw, bias16[None, :], bias16[:, None], _block(NC, 512))
    hsum = _ml_scan(*pl_, *pc_)

    wo = w_out[0].astype(BF16)
    hw = MLA_HEADS * MLA_V
    sk = sub_keys[0].reshape(2 * PEER_HEADS, N_KEYS, -1).astype(BF16)
    x1, h2, scores = _outproj(x, mla, hsum, cols_l, g_mlstm, wo[:hw], wo[hw:], lat(gt1), g_norm2, lat(sc2), lat(sh2),
                              w_pq[0].astype(BF16), sk, 256)
    T = B * N
    nsel = PEER_HEADS * PEER_TOPK
    off, par, gw = [a.reshape(T, nsel) for a in _route(scores, 256)]

    tb = 32
    coef = _peer_u(off, h2.reshape(T, D), _pack_table(expert_u[0]), par, gw, tb)
    out = _peer_v(off, coef, par, _pack_table(expert_v[0]), x1.reshape(T, D), lat(gt2), tb, N)
    return out.reshape(B, N, D)
```

```python
import jax
import jax.numpy as jnp
from jax import lax
from jax.experimental import pallas as pl
from jax.experimental.pallas import tpu as pltpu

F32 = jnp.float32
BF16 = jnp.bfloat16
I32 = jnp.int32
EPS = 1e-6

GRID_W = 64
MLA_HEADS = 4
MLA_NOPE = 128
MLA_ROPE = 64
MLA_V = 128
MLA_QK = MLA_NOPE + MLA_ROPE
ROPE_BASE = 10000.0
ML_HEADS = 4
ML_QK = 64
ML_V = 128
ML_CHUNK = 64
PEER_HEADS = 8
N_KEYS = 128
PEER_TOPK = 16

LANES = 128
SUBLANES = 8
VMEM_LIMIT_BYTES = 56 * 1024 * 1024

C_Q0, C_KV0, C_KR0, C_KRS0, C_QK0, C_V0, C_O0, C_G0, C_END = 0, 256, 384, 448, 512, 1024, 1536, 2048, 2176
HIGHEST = lax.Precision.HIGHEST


def _params(sem):
    return pltpu.CompilerParams(dimension_semantics=sem, vmem_limit_bytes=VMEM_LIMIT_BYTES)


def _rms(x):
    return x * lax.rsqrt(jnp.mean(x * x, axis=-1, keepdims=True) + EPS)


def _ada_kernel(c_ref, w_ref, b_ref, o_ref):
    c = c_ref[...]
    s = c * jax.nn.sigmoid(c)
    o_ref[...] = jnp.dot(s.astype(BF16), w_ref[...], preferred_element_type=F32) + b_ref[...]


def _ada(cc, w, b):
    rows, d = cc.shape
    n = w.shape[1]
    bn = n // 4
    return pl.pallas_call(
        _ada_kernel,
        out_shape=jax.ShapeDtypeStruct((rows, n), F32),
        grid=(4,),
        in_specs=[pl.BlockSpec((rows, d), lambda j: (0, 0)),
                  pl.BlockSpec((d, bn), lambda j: (0, j)),
                  pl.BlockSpec((1, bn), lambda j: (0, j))],
        out_specs=pl.BlockSpec((rows, bn), lambda j: (0, j)),
        compiler_params=_params(("arbitrary",)),
        name="ada",
    )(cc, w, b)


def _inproj_kernel(x_ref, g_ref, sc_ref, sh_ref, w_ref, wg_ref, o_ref, gt_ref):
    x = x_ref[0]
    h = _rms(x) * g_ref[...] * (1.0 + sc_ref[0]) + sh_ref[0]
    hb = h.astype(BF16)
    o_ref[0] = jnp.dot(hb, w_ref[...], preferred_element_type=F32)
    gt_ref[0] = lax.dot_general(wg_ref[...], hb, (((1,), (1,)), ((), ())), preferred_element_type=F32)


def _inproj(x, g, sc, sh, w, wg, tm):
    b, n, d = x.shape
    nc = w.shape[1]
    return pl.pallas_call(
        _inproj_kernel,
        out_shape=(jax.ShapeDtypeStruct((b, n, nc), F32), jax.ShapeDtypeStruct((b, 16, n), F32)),
        grid=(b, n // tm),
        in_specs=[pl.BlockSpec((1, tm, d), lambda i, j: (i, j, 0)),
                  pl.BlockSpec((1, d), lambda i, j: (0, 0)),
                  pl.BlockSpec((1, 1, d), lambda i, j: (i, 0, 0)),
                  pl.BlockSpec((1, 1, d), lambda i, j: (i, 0, 0)),
                  pl.BlockSpec((d, nc), lambda i, j: (0, 0)),
                  pl.BlockSpec((16, d), lambda i, j: (0, 0))],
        out_specs=(pl.BlockSpec((1, tm, nc), lambda i, j: (i, j, 0)),
                   pl.BlockSpec((1, 16, tm), lambda i, j: (i, 0, j))),
        compiler_params=_params(("parallel", "parallel")),
        name="inproj",
    )(x, g, sc, sh, w, wg)


def _mla_prep_kernel(c_ref, gcq_ref, wuq_ref, gckv_ref, wukv_ref, gq_ref, gqs_ref, gk_ref, gks_ref,
                     cos_ref, sin_ref, q_ref, k_ref, v_ref):
    c = c_ref[0]
    cq = _rms(c[:, C_Q0:C_KV0]) * gcq_ref[...]
    ckv = _rms(c[:, C_KV0:C_KR0]) * gckv_ref[...]
    kr = c[:, C_KR0:C_KRS0]
    krs = c[:, C_KRS0:C_QK0]
    q_raw = jnp.dot(cq.astype(BF16), wuq_ref[...], preferred_element_type=F32)
    kv_raw = jnp.dot(ckv.astype(BF16), wukv_ref[...], preferred_element_type=F32)
    cos = cos_ref[...]
    sin = sin_ref[...]
    gq = gq_ref[...]
    gk = gk_ref[...]
    kr_ss = jnp.sum(kr * kr, axis=-1, keepdims=True)
    for h in range(MLA_HEADS):
        o = h * 256
        qn = q_raw[:, o:o + 128]
        qr = q_raw[:, o + 128:o + 192]
        qs = q_raw[:, o + 192:o + 256]
        ss = jnp.sum(qn * qn, axis=-1, keepdims=True) + jnp.sum(qr * qr, axis=-1, keepdims=True)
        r = lax.rsqrt(ss * (1.0 / MLA_QK) + EPS) * (MLA_QK ** -0.5)
        q_ref[0, h, :, 0:128] = (qn * r * gq[:, 0:128]).astype(BF16)
        q_ref[0, h, :, 128:192] = ((qr * r * gq[:, 128:192]) * cos + (qs * r * gqs_ref[...]) * sin).astype(BF16)
        kn = kv_raw[:, o:o + 128]
        ss = jnp.sum(kn * kn, axis=-1, keepdims=True) + kr_ss
        r = lax.rsqrt(ss * (1.0 / MLA_QK) + EPS)
        k_ref[0, h, :, 0:128] = (kn * r * gk[:, 0:128]).astype(BF16)
        k_ref[0, h, :, 128:192] = ((kr * r * gk[:, 128:192]) * cos + (krs * r * gks_ref[...]) * sin).astype(BF16)
        v_ref[0, h] = kv_raw[:, o + 128:o + 256].astype(BF16)


def _mla_prep(cols, gcq, wuq, gckv, wukv, gq, gqs, gk, gks, cos, sin, tm):
    b, n, _ = cols.shape
    full = lambda a: pl.BlockSpec(a.shape, lambda i, j: (0,) * a.ndim)
    return pl.pallas_call(
        _mla_prep_kernel,
        out_shape=(jax.ShapeDtypeStruct((b, MLA_HEADS, n, MLA_QK), BF16),
                   jax.ShapeDtypeStruct((b, MLA_HEADS, n, MLA_QK), BF16),
                   jax.ShapeDtypeStruct((b, MLA_HEADS, n, MLA_V), BF16)),
        grid=(b, n // tm),
        in_specs=[pl.BlockSpec((1, tm, 512), lambda i, j: (i, j, 0)),
                  full(gcq), full(wuq), full(gckv), full(wukv), full(gq), full(gqs), full(gk), full(gks),
                  pl.BlockSpec((tm, MLA_ROPE), lambda i, j: (j, 0)),
                  pl.BlockSpec((tm, MLA_ROPE), lambda i, j: (j, 0))],
        out_specs=(pl.BlockSpec((1, MLA_HEADS, tm, MLA_QK), lambda i, j: (i, 0, j, 0)),
                   pl.BlockSpec((1, MLA_HEADS, tm, MLA_QK), lambda i, j: (i, 0, j, 0)),
                   pl.BlockSpec((1, MLA_HEADS, tm, MLA_V), lambda i, j: (i, 0, j, 0))),
        compiler_params=_params(("parallel", "parallel")),
        name="mla_prep",
    )(cols, gcq, wuq, gckv, wukv, gq, gqs, gk, gks, cos, sin)


ATTN_TILES_PER_STEP = 4


def _attn_kernel(q_ref, kc_ref, kl_ref, vc_ref, vl_ref, o_ref):
    tq = q_ref.shape[2] // ATTN_TILES_PER_STEP
    nc = kc_ref.shape[2]
    kc, kl, vc, vl = kc_ref[0, 0], kl_ref[0, 0], vc_ref[0, 0], vl_ref[0, 0]
    nt = (((1,), (1,)), ((), ()))
    for j in range(ATTN_TILES_PER_STEP):
        q = q_ref[0, 0, j * tq:(j + 1) * tq, :]
        s = jnp.concatenate([lax.dot_general(q, kc, nt, preferred_element_type=F32),
                             lax.dot_general(q, kl, nt, preferred_element_type=F32)], axis=1)
        m = jnp.max(s, axis=-1, keepdims=True)
        p = jnp.exp(s - m)
        l = jnp.sum(p, axis=-1, keepdims=True)
        pb = p.astype(BF16)
        o = (jnp.dot(pb[:, :nc], vc, preferred_element_type=F32)
             + jnp.dot(pb[:, nc:], vl, preferred_element_type=F32))
        o_ref[0, j * tq:(j + 1) * tq, :] = (o / l).astype(BF16)


def _attention(q, k_c, k_l, v_c, v_l, tq):
    b, h, n, dk = q.shape
    nc = k_c.shape[2]
    dv = v_l.shape[3]
    tb = tq * ATTN_TILES_PER_STEP
    whole = lambda nn, dd: pl.BlockSpec((1, 1, nn, dd), lambda i, j, t: (i, j, 0, 0))
    return pl.pallas_call(
        _attn_kernel,
        out_shape=jax.ShapeDtypeStruct((b, n, h * dv), BF16),
        grid=(b, h, n // tb),
        in_specs=[pl.BlockSpec((1, 1, tb, dk), lambda i, j, t: (i, j, t, 0)),
                  whole(nc, dk), whole(n, dk), whole(nc, dv), whole(n, dv)],
        out_specs=pl.BlockSpec((1, tb, dv), lambda i, j, t: (i, t, j)),
        compiler_params=_params(("parallel", "parallel", "parallel")),
        name="attention",
    )(q, k_c, k_l, v_c, v_l)


def _ml_prep_kernel(qk_ref, prev_ref, next_ref, vin_ref, g_ref, gt_ref, cw_ref, bc_ref, br_ref,
                    q_ref, kt_ref, v_ref, a_ref, brow_ref):
    tn = qk_ref.shape[1]
    j = pl.program_id(1)
    nj = pl.num_programs(1)
    u = qk_ref[0]
    row = lax.broadcasted_iota(I32, (tn, 1), 0)
    before = jnp.where(j == 0, 0.0, prev_ref[0, SUBLANES - 1:SUBLANES, :])
    after = jnp.where(j == nj - 1, 0.0, next_ref[0, 0:1, :])
    up = jnp.where(row == 0, before, pltpu.roll(u, 1, 0))
    un = jnp.where(row == tn - 1, after, pltpu.roll(u, tn - 1, 0))
    cw = cw_ref[...]
    y = up * cw[0:1, :] + u * cw[1:2, :] + un * cw[2:3, :]
    y = y * jax.nn.sigmoid(y)
    hq = ML_HEADS * ML_QK
    for h in range(ML_HEADS):
        q_ref[0, h] = (y[:, h * ML_QK:(h + 1) * ML_QK] * (ML_QK ** -0.5)).astype(BF16)
    kt = y[:, hq:2 * hq].T
    L = ML_CHUNK
    for c in range(tn // L):
        kt_ref[0, c] = kt[:, c * L:(c + 1) * L].astype(BF16)
    v_ref[0] = vin_ref[0].astype(BF16)
    g = g_ref[0][:, 0:16] + bc_ref[...]
    lf_c = jax.nn.log_sigmoid(g[:, 8:16])
    gt = gt_ref[0] + br_ref[...]
    ig_r = gt[0:8, :]
    lf_r = jax.nn.log_sigmoid(gt[8:16, :])
    ti = lax.broadcasted_iota(I32, (L, L), 0)
    si = lax.broadcasted_iota(I32, (L, L), 1)
    lower = (si <= ti).astype(F32)
    upper = (si >= ti).astype(F32)
    lane_fwd = lax.broadcasted_iota(I32, (L, 8), 1) < ML_HEADS
    row_fwd = lax.broadcasted_iota(I32, (8, L), 0) < ML_HEADS
    for c in range(tn // L):
        lo = c * L
        lfc = lf_c[lo:lo + L, :]
        a_ref[0, lo:lo + L, :] = jnp.where(
            lane_fwd,
            jnp.dot(lower, lfc, precision=HIGHEST, preferred_element_type=F32),
            jnp.dot(upper, lfc, precision=HIGHEST, preferred_element_type=F32))
        lfr = lf_r[:, lo:lo + L]
        cf_r = jnp.where(
            row_fwd,
            jnp.dot(lfr, upper, precision=HIGHEST, preferred_element_type=F32),
            jnp.dot(lfr, lower, precision=HIGHEST, preferred_element_type=F32))
        brow_ref[0, c] = ig_r[:, lo:lo + L] - cf_r


def _ml_prep(cols, gates_t, cw, bias_col, bias_row, tn):
    b, n, _ = cols.shape
    r8 = tn // SUBLANES
    last8 = n // SUBLANES - 1
    L = ML_CHUNK
    hq = ML_HEADS * ML_QK
    return pl.pallas_call(
        _ml_prep_kernel,
        out_shape=(jax.ShapeDtypeStruct((b, ML_HEADS, n, ML_QK), BF16),
                   jax.ShapeDtypeStruct((b, n // L, hq, L), BF16),
                   jax.ShapeDtypeStruct((b, n, ML_HEADS * ML_V), BF16),
                   jax.ShapeDtypeStruct((b, n, 8), F32),
                   jax.ShapeDtypeStruct((b, n // L, 8, L), F32)),
        grid=(b, n // tn),
        in_specs=[pl.BlockSpec((1, tn, 512), lambda i, j: (i, j, C_QK0 // 512)),
                  pl.BlockSpec((1, SUBLANES, 512), lambda i, j: (i, jnp.maximum(j * r8 - 1, 0), C_QK0 // 512)),
                  pl.BlockSpec((1, SUBLANES, 512), lambda i, j: (i, jnp.minimum((j + 1) * r8, last8), C_QK0 // 512)),
                  pl.BlockSpec((1, tn, 512), lambda i, j: (i, j, C_V0 // 512)),
                  pl.BlockSpec((1, tn, 128), lambda i, j: (i, j, C_G0 // 128)),
                  pl.BlockSpec((1, 16, tn), lambda i, j: (i, 0, j)),
                  pl.BlockSpec((3, 512), lambda i, j: (0, 0)),
                  pl.BlockSpec((1, 16), lambda i, j: (0, 0)),
                  pl.BlockSpec((16, 1), lambda i, j: (0, 0))],
        out_specs=(pl.BlockSpec((1, ML_HEADS, tn, ML_QK), lambda i, j: (i, 0, j, 0)),
                   pl.BlockSpec((1, tn // L, hq, L), lambda i, j: (i, j, 0, 0)),
                   pl.BlockSpec((1, tn, ML_HEADS * ML_V), lambda i, j: (i, j, 0)),
                   pl.BlockSpec((1, tn, 8), lambda i, j: (i, j, 0)),
                   pl.BlockSpec((1, tn // L, 8, L), lambda i, j: (i, j, 0, 0))),
        compiler_params=_params(("parallel", "parallel")),
        name="ml_prep",
    )(cols, cols, cols, cols, cols, gates_t, cw, bias_col, bias_row)


def _ml_chunk(qb, kt, vaug, a_c, b_r, ct, m_prev, fwd, need_out):
    L = qb.shape[0]
    ti = lax.broadcasted_iota(I32, (L, L), 0)
    si = lax.broadcasted_iota(I32, (L, L), 1)
    mask = (si <= ti) if fwd else (si >= ti)
    dmat = jnp.where(mask, a_c + b_r, -jnp.inf)
    mloc = jnp.max(dmat, axis=1, keepdims=True)
    inter = a_c + m_prev
    m_t = jnp.maximum(inter, mloc)
    last = L - 1 if fwd else 0
    m_new = m_t[last:last + 1, :]
    a_last = a_c[last:last + 1, :]
    w_r = jnp.exp(a_last + b_r - m_new)
    decay = jnp.exp(a_last + m_prev - m_new)
    kw = (kt.astype(F32) * w_r).astype(BF16)
    if not need_out:
        return decay * ct + jnp.dot(kw, vaug, preferred_element_type=F32), m_new, None
    s = jnp.dot(qb, kt, preferred_element_type=F32)
    wmat = jnp.exp(dmat - m_t) * s
    sc = jnp.exp(inter - m_t)
    top = jnp.concatenate([wmat.astype(BF16), (sc * qb.astype(F32)).astype(BF16)], axis=1)
    bot = jnp.concatenate([kw, jnp.zeros(kw.shape, BF16)], axis=1)
    res = jnp.dot(jnp.concatenate([top, bot], axis=0), jnp.concatenate([vaug, ct.astype(BF16)], axis=0),
                  preferred_element_type=F32)
    nd = res[:L]
    num = nd[:, 0:ML_V]
    den = nd[:, ML_V:ML_V + 1]
    return decay * ct + res[L:], m_new, num / jnp.maximum(jnp.abs(den), jnp.exp(-m_t))


def _ml_scan_kernel(ql_ref, ktl_ref, vl_ref, al_ref, brl_ref, qc_ref, ktc_ref, vc_ref, ac_ref, brc_ref,
                    out_ref, st_ref, m_ref):
    L = ML_CHUNK
    ncl = ql_ref.shape[2] // L
    ncc = qc_ref.shape[2] // L
    ones_col = (lax.broadcasted_iota(I32, (L, ML_V), 1) == 0).astype(BF16)
    st_ref[...] = jnp.zeros(st_ref.shape, F32)
    m_ref[...] = jnp.zeros(m_ref.shape, F32)

    def step(refs, c, d, fwd, need_out):
        q_ref, kt_ref, v_ref, a_ref, br_ref = refs
        sl = pl.ds(pl.multiple_of(c * L, L), L)
        a_all = a_ref[0, sl, :]
        br_all = br_ref[0, c]
        v_all = v_ref[0, sl, :]
        kt_all = kt_ref[0, c]
        hs = []
        for h in range(ML_HEADS):
            j = d * ML_HEADS + h
            vaug = jnp.concatenate([v_all[:, h * ML_V:(h + 1) * ML_V], ones_col], axis=1)
            ct_new, m_new, hh = _ml_chunk(
                q_ref[0, h, sl, :], kt_all[h * ML_QK:(h + 1) * ML_QK, :], vaug,
                a_all[:, j:j + 1], br_all[j:j + 1, :], st_ref[j], m_ref[j][0:1, 0:1], fwd, need_out)
            st_ref[j] = ct_new
            m_ref[j] = jnp.broadcast_to(m_new, (SUBLANES, LANES))
            hs.append(hh)
        return hs, sl

    ctx_refs = (qc_ref, ktc_ref, vc_ref, ac_ref, brc_ref)
    lat_refs = (ql_ref, ktl_ref, vl_ref, al_ref, brl_ref)

    out_ref[...] = jnp.zeros(out_ref.shape, F32)

    def ctx_body(i, carry):
        step(ctx_refs, i, 0, True, False)
        step(ctx_refs, ncc - 1 - i, 1, False, False)
        return carry

    def lat_body(i, carry):
        for d, fwd in ((0, True), (1, False)):
            hs, sl = step(lat_refs, i if fwd else ncl - 1 - i, d, fwd, True)
            out_ref[0, sl, :] = out_ref[0, sl, :] + jnp.concatenate(hs, axis=1)
        return carry

    lax.fori_loop(0, ncc, ctx_body, 0)
    lax.fori_loop(0, ncl, lat_body, 0)


def _ml_scan(ql, ktl, vl, al, brl, qc, ktc, vc, ac, brc):
    b, _, n, _ = ql.shape
    nctx = qc.shape[2]
    L = ML_CHUNK
    hv = ML_HEADS * ML_V
    hq = ML_HEADS * ML_QK
    qspec = lambda nn: pl.BlockSpec((1, ML_HEADS, nn, ML_QK), lambda i: (i, 0, 0, 0))
    ktspec = lambda nn: pl.BlockSpec((1, nn // L, hq, L), lambda i: (i, 0, 0, 0))
    vspec = lambda nn: pl.BlockSpec((1, nn, hv), lambda i: (i, 0, 0))
    aspec = lambda nn: pl.BlockSpec((1, nn, 8), lambda i: (i, 0, 0))
    rspec = lambda nn: pl.BlockSpec((1, nn // L, 8, L), lambda i: (i, 0, 0, 0))
    return pl.pallas_call(
        _ml_scan_kernel,
        out_shape=jax.ShapeDtypeStruct((b, n, hv), F32),
        grid=(b,),
        in_specs=[qspec(n), ktspec(n), vspec(n), aspec(n), rspec(n),
                  qspec(nctx), ktspec(nctx), vspec(nctx), aspec(nctx), rspec(nctx)],
        out_specs=pl.BlockSpec((1, n, hv), lambda i: (i, 0, 0)),
        scratch_shapes=[pltpu.VMEM((2 * ML_HEADS, ML_QK, 2 * ML_V), F32),
                        pltpu.VMEM((2 * ML_HEADS, SUBLANES, LANES), F32)],
        compiler_params=_params(("parallel",)),
        name="ml_scan",
    )(ql, ktl, vl, al, brl, qc, ktc, vc, ac, brc)


def _outproj_kernel(x_ref, mla_ref, hs_ref, o_ref, gm_ref, wa_ref, wb_ref, gt_ref, g2_ref, sc_ref, sh_ref,
                    wpq_ref, sk_ref, x1_ref, h2_ref, s_ref):
    hs = hs_ref[0]
    gm = gm_ref[...]
    hn = jnp.concatenate([_rms(hs[:, h * ML_V:(h + 1) * ML_V]) * gm[:, h * ML_V:(h + 1) * ML_V]
                          for h in range(ML_HEADS)], axis=1)
    ml = (jax.nn.sigmoid(o_ref[0]) * hn).astype(BF16)
    mix = (jnp.dot(mla_ref[0], wa_ref[...], preferred_element_type=F32)
           + jnp.dot(ml, wb_ref[...], preferred_element_type=F32))
    x1 = x_ref[0] + gt_ref[0] * mix
    x1_ref[0] = x1
    h2 = _rms(x1) * g2_ref[...] * (1.0 + sc_ref[0]) + sh_ref[0]
    h2_ref[0] = h2
    qp = jnp.dot(h2.astype(BF16), wpq_ref[...], preferred_element_type=F32).astype(BF16)
    for hp in range(2 * PEER_HEADS):
        s_ref[0, hp] = lax.dot_general(sk_ref[hp], qp[:, hp * N_KEYS:(hp + 1) * N_KEYS],
                                       (((1,), (1,)), ((), ())), preferred_element_type=F32)


def _outproj(x, mla, hsum, cols, gm, wa, wb, gt1, g2, sc2, sh2, wpq, sk, tm):
    b, n, d = x.shape
    hw = mla.shape[2]
    nq = wpq.shape[1]
    mod = pl.BlockSpec((1, 1, d), lambda i, j: (i, 0, 0))
    return pl.pallas_call(
        _outproj_kernel,
        out_shape=(jax.ShapeDtypeStruct((b, n, d), F32), jax.ShapeDtypeStruct((b, n, d), F32),
                   jax.ShapeDtypeStruct((b, 2 * PEER_HEADS, N_KEYS, n), F32)),
        grid=(b, n // tm),
        in_specs=[pl.BlockSpec((1, tm, d), lambda i, j: (i, j, 0)),
                  pl.BlockSpec((1, tm, hw), lambda i, j: (i, j, 0)),
                  pl.BlockSpec((1, tm, hw), lambda i, j: (i, j, 0)),
                  pl.BlockSpec((1, tm, 512), lambda i, j: (i, j, C_O0 // 512)),
                  pl.BlockSpec((1, hw), lambda i, j: (0, 0)),
                  pl.BlockSpec((hw, d), lambda i, j: (0, 0)),
                  pl.BlockSpec((hw, d), lambda i, j: (0, 0)),
                  mod, pl.BlockSpec((1, d), lambda i, j: (0, 0)), mod, mod,
                  pl.BlockSpec((d, nq), lambda i, j: (0, 0)),
                  pl.BlockSpec((2 * PEER_HEADS, N_KEYS, N_KEYS), lambda i, j: (0, 0, 0))],
        out_specs=(pl.BlockSpec((1, tm, d), lambda i, j: (i, j, 0)),
                   pl.BlockSpec((1, tm, d), lambda i, j: (i, j, 0)),
                   pl.BlockSpec((1, 2 * PEER_HEADS, N_KEYS, tm), lambda i, j: (i, 0, 0, j))),
        compiler_params=_params(("parallel", "parallel")),
        name="outproj",
    )(x, mla, hsum, cols, gm, wa, wb, gt1, g2, sc2, sh2, wpq, sk)


def _topk_rows(s, k, rid=None):
    rows, t = s.shape
    if rid is None:
        rid = lax.broadcasted_iota(I32, (rows, t), 0)
    kid = lax.broadcasted_iota(I32, (k, t), 0)
    big = jnp.iinfo(jnp.int32).max

    def body(r, carry):
        s, vals, idxs = carry
        m = jnp.max(s, axis=0, keepdims=True)
        i = jnp.min(jnp.where(s == m, rid, big), axis=0, keepdims=True)
        vals = jnp.where(kid == r, m, vals)
        idxs = jnp.where(kid == r, i, idxs)
        s = jnp.where(rid == i, -jnp.inf, s)
        return s, vals, idxs

    _, vals, idxs = lax.fori_loop(0, k, body, (s, jnp.zeros((k, t), F32), jnp.zeros((k, t), I32)))
    return vals, idxs


def _pair_candidates(sv0, sv1):
    K, t = sv0.shape
    h = K // 2
    iid = lax.broadcasted_iota(I32, (h, t), 0)
    vals = [sv0[0:h] + sv1[0:1], sv0[h:K] + sv1[0:1]]
    ids = [iid * K, (iid + h) * K]
    for j in range(1, h):
        vals.append(sv0[0:h] + sv1[j:j + 1])
        ids.append(iid * K + j)
    vals.append(sv0[0:1] + sv1[h:K])
    ids.append(iid + h)
    return jnp.concatenate(vals, axis=0), jnp.concatenate(ids, axis=0)


def _route_kernel(s_ref, off_ref, par_ref, gw_ref, et_ref, gt_ref):
    K = PEER_TOPK

    def head(h, carry):
        sv0, si0 = _topk_rows(s_ref[0, 2 * h], K)
        sv1, si1 = _topk_rows(s_ref[0, 2 * h + 1], K)
        cand, cid = _pair_candidates(sv0, sv1)
        best, pos = _topk_rows(cand, K, cid)
        isel = pos >> 4
        jsel = pos & (K - 1)
        e0 = jnp.zeros_like(pos)
        e1 = jnp.zeros_like(pos)
        for i in range(K):
            e0 = jnp.where(isel == i, si0[i:i + 1, :], e0)
            e1 = jnp.where(jsel == i, si1[i:i + 1, :], e1)
        ex = jnp.exp(best - best[0:1, :])
        sl = pl.ds(pl.multiple_of(h * K, K), K)
        et_ref[sl, :] = e0 * N_KEYS + e1
        gt_ref[sl, :] = ex / jnp.sum(ex, axis=0, keepdims=True)
        return carry

    lax.fori_loop(0, PEER_HEADS, head, 0)
    e = et_ref[...].T
    off_ref[0] = (e >> 1) * SUBLANES
    par_ref[0] = (e & 1).astype(F32)
    gw_ref[0] = gt_ref[...].T


def _route(scores, tt):
    b, hp, nk, n = scores.shape
    nsel = PEER_HEADS * PEER_TOPK
    ospec = pl.BlockSpec((1, tt, nsel), lambda i, j: (i, j, 0))
    return pl.pallas_call(
        _route_kernel,
        out_shape=(jax.ShapeDtypeStruct((b, n, nsel), I32), jax.ShapeDtypeStruct((b, n, nsel), F32),
                   jax.ShapeDtypeStruct((b, n, nsel), F32)),
        grid=(b, n // tt),
        in_specs=[pl.BlockSpec((1, hp, nk, tt), lambda i, j: (i, 0, 0, j))],
        out_specs=(ospec, ospec, ospec),
        scratch_shapes=[pltpu.VMEM((nsel, tt), I32), pltpu.VMEM((nsel, tt), F32)],
        compiler_params=_params(("parallel", "parallel")),
        name="route",
    )(scores)


TILE_ROWS = 2 * SUBLANES
TOKENS_PER_ITER = 16


CHUNK_SEL = 16


def _chunk_tiles(tab_ref, row_ref, c):
    return jnp.concatenate(
        [pltpu.bitcast(tab_ref[pl.ds(pl.multiple_of(row_ref[c * CHUNK_SEL + kk], SUBLANES), SUBLANES), :], BF16)
         for kk in range(CHUNK_SEL)], axis=0)


def _init_spread(spread_ref, gather_ref=None):
    @pl.when(pl.program_id(0) == 0)
    def _():
        nsel, width = spread_ref.shape
        spread_ref[...] = (lax.broadcasted_iota(I32, (nsel, width), 1) // TILE_ROWS
                           == lax.broadcasted_iota(I32, (nsel, width), 0)).astype(BF16)
        if gather_ref is not None:
            gather_ref[...] = (lax.broadcasted_iota(I32, (width, nsel), 0) // TILE_ROWS
                               == lax.broadcasted_iota(I32, (width, nsel), 1)).astype(BF16)


def _row_masks(width):
    half = (lax.broadcasted_iota(I32, (1, width), 1) % 2).astype(F32)
    lane8 = lax.broadcasted_iota(I32, (SUBLANES, width), 1)
    diag = ((lane8 % TILE_ROWS) // 2 == lax.broadcasted_iota(I32, (SUBLANES, width), 0)).astype(F32)
    return half, diag


def _row_to_tile(row):
    return jnp.concatenate([row[:, s * LANES:(s + 1) * LANES] for s in range(SUBLANES)], axis=0)


def _split_bf16(v):
    hi = v.astype(BF16)
    return hi, (v - hi.astype(F32)).astype(BF16)


def _peer_u_kernel(off_ref, h_ref, tab_ref, par_ref, g_ref, c_ref, r_ref, spread_ref, gather_ref):
    tb, nsel = g_ref.shape
    _init_spread(spread_ref, gather_ref)
    half, diag = _row_masks(spread_ref.shape[1])

    cw = CHUNK_SEL * TILE_ROWS
    diag_c = diag[:, :cw]
    nt = (((1,), (1,)), ((), ()))

    def tok(i, carry):
        ts = [i * TOKENS_PER_ITER + u for u in range(TOKENS_PER_ITER)]
        rows = [off_ref.at[t] for t in ts]
        xs = [jnp.concatenate(_split_bf16(_row_to_tile(h_ref[pl.ds(t, 1), :])), axis=0) for t in ts]
        for c in range(nsel // CHUNK_SEL):
            for u, t in enumerate(ts):
                res = lax.dot_general(xs[u], _chunk_tiles(tab_ref, rows[u], c), nt, preferred_element_type=F32)
                r_ref[pl.ds(t, 1), c * cw:(c + 1) * cw] = jnp.sum(
                    (res[:SUBLANES] + res[SUBLANES:]) * diag_c, axis=0, keepdims=True)
        return carry

    lax.fori_loop(0, tb // TOKENS_PER_ITER, tok, 0)
    mine = jnp.dot(par_ref[...].astype(BF16), spread_ref[...], preferred_element_type=F32) == half
    r_hi, r_lo = _split_bf16(jnp.where(mine, r_ref[...], 0.0))
    both = jnp.dot(jnp.concatenate([r_hi, r_lo], axis=0), gather_ref[...], preferred_element_type=F32)
    pre = both[:tb] + both[tb:]
    c_ref[...] = g_ref[...] * (0.5 * pre * (1.0 + lax.erf(pre * (2.0 ** -0.5))))


def _smem_block(tb, nsel):
    return pl.BlockSpec((tb, nsel), lambda i: (i, 0), memory_space=pltpu.SMEM)


def _peer_u(off, h2, tab, par, gw, tb):
    t, nsel = gw.shape
    vspec = pl.BlockSpec((tb, nsel), lambda i: (i, 0))
    return pl.pallas_call(
        _peer_u_kernel,
        out_shape=jax.ShapeDtypeStruct((t, nsel), F32),
        grid=(t // tb,),
        in_specs=[_smem_block(tb, nsel),
                  pl.BlockSpec((tb, h2.shape[1]), lambda i: (i, 0)),
                  pl.BlockSpec(tab.shape, lambda i: (0, 0), pipeline_mode=pl.Buffered(1)),
                  vspec, vspec],
        out_specs=vspec,
        scratch_shapes=[pltpu.VMEM((tb, nsel * TILE_ROWS), F32),
                        pltpu.VMEM((nsel, nsel * TILE_ROWS), BF16), pltpu.VMEM((nsel * TILE_ROWS, nsel), BF16)],
        compiler_params=_params(("arbitrary",)),
        name="peer_u",
    )(off, h2, tab, par, gw)


def _peer_v_kernel(off_ref, c_ref, par_ref, tab_ref, x1_ref, gt_ref, o_ref, lhi_ref, llo_ref, spread_ref):
    tb, nsel = c_ref.shape
    _init_spread(spread_ref)
    half, diag = _row_masks(spread_ref.shape[1])
    c_hi, c_lo = _split_bf16(c_ref[...])
    ex = jnp.dot(jnp.concatenate([par_ref[...].astype(BF16), c_hi, c_lo], axis=0), spread_ref[...],
                 preferred_element_type=F32)
    mine = ex[:tb] == half
    lhi_ref[...] = jnp.where(mine, ex[tb:2 * tb], 0.0)
    llo_ref[...] = jnp.where(mine, ex[2 * tb:], 0.0)
    gt = gt_ref[0]

    cw = CHUNK_SEL * TILE_ROWS
    diag_c = diag[:, :cw]

    def tok(i, carry):
        ts = [i * TOKENS_PER_ITER + u for u in range(TOKENS_PER_ITER)]
        rows = [off_ref.at[t] for t in ts]
        accs = [jnp.zeros((TILE_ROWS, LANES), F32) for _ in ts]
        for c in range(nsel // CHUNK_SEL):
            cols = slice(c * cw, (c + 1) * cw)
            for u, t in enumerate(ts):
                left = jnp.concatenate([lhi_ref[pl.ds(t, 1), cols] * diag_c, llo_ref[pl.ds(t, 1), cols] * diag_c],
                                       axis=0).astype(BF16)
                accs[u] = accs[u] + jnp.dot(left, _chunk_tiles(tab_ref, rows[u], c), preferred_element_type=F32)
        for u, t in enumerate(ts):
            y = accs[u][:SUBLANES] + accs[u][SUBLANES:]
            y_row = jnp.concatenate([y[s:s + 1, :] for s in range(SUBLANES)], axis=1)
            o_ref[pl.ds(t, 1), :] = x1_ref[pl.ds(t, 1), :] + gt * y_row
        return carry

    lax.fori_loop(0, tb // TOKENS_PER_ITER, tok, 0)


def _peer_v(off, coef, par, tab, x1, gt2, tb, tok_per_batch):
    t, nsel = coef.shape
    d = x1.shape[1]
    bpb = tok_per_batch // tb
    vspec = pl.BlockSpec((tb, nsel), lambda i: (i, 0))
    return pl.pallas_call(
        _peer_v_kernel,
        out_shape=jax.ShapeDtypeStruct(x1.shape, F32),
        grid=(t // tb,),
        in_specs=[_smem_block(tb, nsel), vspec, vspec,
                  pl.BlockSpec(tab.shape, lambda i: (0, 0), pipeline_mode=pl.Buffered(1)),
                  pl.BlockSpec((tb, d), lambda i: (i, 0)),
                  pl.BlockSpec((1, 1, d), lambda i: (i // bpb, 0, 0))],
        out_specs=pl.BlockSpec((tb, d), lambda i: (i, 0)),
        scratch_shapes=[pltpu.VMEM((tb, nsel * TILE_ROWS), F32), pltpu.VMEM((tb, nsel * TILE_ROWS), F32),
                        pltpu.VMEM((nsel, nsel * TILE_ROWS), BF16)],
        compiler_params=_params(("arbitrary",)),
        name="peer_v",
    )(off, coef, par, tab, x1, gt2)


def _rope_tables(n):
    axis = MLA_ROPE // 2
    t = jnp.arange(n, dtype=F32)
    row = jnp.floor(t / GRID_W)
    col = t - row * GRID_W
    inv = ROPE_BASE ** (-jnp.arange(axis // 2, dtype=F32) * (2.0 / axis))
    ar = row[:, None] * inv
    ac = col[:, None] * inv
    cos = jnp.concatenate([jnp.cos(ar), jnp.cos(ar), jnp.cos(ac), jnp.cos(ac)], axis=1)
    sin = jnp.concatenate([-jnp.sin(ar), jnp.sin(ar), -jnp.sin(ac), jnp.sin(ac)], axis=1)
    return cos, sin


PACK_ROWS = 256


def _pack_kernel(t_ref, o_ref):
    n = t_ref.shape[0]
    rows = n // 2
    xb = t_ref[...].astype(BF16)
    col = lax.broadcasted_iota(I32, (rows, n), 1)
    row2 = 2 * lax.broadcasted_iota(I32, (rows, n), 0)
    even = pltpu.bitcast(jnp.dot((col == row2).astype(BF16), xb, preferred_element_type=F32), jnp.uint32)
    odd = pltpu.bitcast(jnp.dot((col == row2 + 1).astype(BF16), xb, preferred_element_type=F32), jnp.uint32)
    word = odd | (even >> 16)
    for s in range(SUBLANES):
        o_ref[pl.ds(s, rows, stride=SUBLANES), :] = word[:, s * LANES:(s + 1) * LANES]


def _pack_table(tab):
    e, d = tab.shape
    out_rows = PACK_ROWS // 2 * (d // LANES)
    return pl.pallas_call(
        _pack_kernel,
        out_shape=jax.ShapeDtypeStruct((e // 2 * (d // LANES), LANES), jnp.uint32),
        grid=(e // PACK_ROWS,),
        in_specs=[pl.BlockSpec((PACK_ROWS, d), lambda i: (i, 0))],
        out_specs=pl.BlockSpec((out_rows, LANES), lambda i: (i, 0)),
        compiler_params=_params(("parallel",)),
        name="pack_table",
    )(tab)


def _block(n, want):
    return want if n % want == 0 else n


def kernel(x, c, ctx, c_ctx, w_ada, b_ada, g_norm1, w_in, g_cq, w_uq, g_ckv, w_ukv, g_qn, g_kn, conv_qk, b_igate, b_fgate, g_mlstm, w_out, g_norm2, w_pq, sub_keys, expert_u, expert_v):
    B, N, D = x.shape
    NC = ctx.shape[1]
    assert w_ada.shape[0] == 1, "one layer"
    assert N % (256 * ATTN_TILES_PER_STEP) == 0 and NC % ML_CHUNK == 0 and D == SUBLANES * LANES
    q_rank = g_cq.shape[1]
    kv_rank = g_ckv.shape[1]
    mla_cols = q_rank + kv_rank + MLA_ROPE
    assert (q_rank, kv_rank) == (C_KV0 - C_Q0, C_KR0 - C_KV0)
    swap = jnp.arange(MLA_ROPE) ^ (MLA_ROPE // 4)

    cc = jnp.concatenate([c, c_ctx[None, :], jnp.zeros((16 - B - 1, D), F32)], axis=0)
    mod = _ada(cc, w_ada[0].astype(BF16), b_ada)
    sh1, sc1, gt1, sh2, sc2, gt2 = [mod[:, i * D:(i + 1) * D] for i in range(6)]
    lat = lambda m: m[:B].reshape(B, 1, D)
    ctxm = lambda m: jnp.broadcast_to(m[B:B + 1].reshape(1, 1, D), (B, 1, D))

    wi = w_in[0]
    n_qk = 2 * ML_HEADS * ML_QK
    n_v = ML_HEADS * ML_V
    n_g = 4 * ML_HEADS
    m0 = mla_cols
    w_cols = jnp.concatenate([
        wi[:, 0:mla_cols],
        wi[:, q_rank + kv_rank + swap],
        wi[:, m0:m0 + n_qk + 2 * n_v + n_g],
        jnp.zeros((D, C_END - C_G0 - n_g), F32)], axis=1).astype(BF16)
    assert w_cols.shape[1] == C_END
    w_gates_t = wi[:, m0 + n_qk + 2 * n_v:m0 + n_qk + 2 * n_v + n_g].T.astype(BF16)
    cols_l, gates_l = _inproj(x, g_norm1, lat(sc1), lat(sh1), w_cols, w_gates_t, _block(N, 512))
    cols_c, gates_c = _inproj(ctx, g_norm1, ctxm(sc1), ctxm(sh1), w_cols, w_gates_t, _block(NC, 512))

    wq = w_uq[0].reshape(q_rank, MLA_HEADS, MLA_QK)
    wuq = jnp.concatenate([wq, wq[:, :, MLA_NOPE + swap]], axis=2).reshape(q_rank, MLA_HEADS * 256).astype(BF16)
    wukv = w_ukv[0].astype(BF16)
    gqs, gks = g_qn[:, MLA_NOPE + swap], g_kn[:, MLA_NOPE + swap]
    cos_l, sin_l = _rope_tables(N)
    cos_c, sin_c = jnp.ones((NC, MLA_ROPE), F32), jnp.zeros((NC, MLA_ROPE), F32)
    q_l, k_l, v_l = _mla_prep(cols_l, g_cq, wuq, g_ckv, wukv, g_qn, gqs, g_kn, gks, cos_l, sin_l, _block(N, 512))
    _, k_c, v_c = _mla_prep(cols_c, g_cq, wuq, g_ckv, wukv, g_qn, gqs, g_kn, gks, cos_c, sin_c, _block(NC, 512))
    mla = _attention(q_l, k_c, k_l, v_c, v_l, 256)

    bias16 = jnp.concatenate([b_igate[0].reshape(-1), b_fgate[0].reshape(-1)])
    cw = conv_qk[0]
    pl_ = _ml_prep(cols_l, gates_l, cw, bias16[None, :], bias16[:, None], _block(N, 512))
    pc_ = _ml_prep(cols_c, gates_c, cw, bias16[None, :], bias16[:, None], _block(NC, 512))
    hsum = _ml_scan(*pl_, *pc_)

    wo = w_out[0].astype(BF16)
    hw = MLA_HEADS * MLA_V
    sk = sub_keys[0].reshape(2 * PEER_HEADS, N_KEYS, -1).astype(BF16)
    x1, h2, scores = _outproj(x, mla, hsum, cols_l, g_mlstm, wo[:hw], wo[hw:], lat(gt1), g_norm2, lat(sc2), lat(sh2),
                              w_pq[0].astype(BF16), sk, 256)
    T = B * N
    nsel = PEER_HEADS * PEER_TOPK
    off, par, gw = [a.reshape(T, nsel) for a in _route(scores, 256)]

    tb = 64
    coef = _peer_u(off, h2.reshape(T, D), _pack_table(expert_u[0]), par, gw, tb)
    out = _peer_v(off, coef, par, _pack_table(expert_v[0]), x1.reshape(T, D), lat(gt2), tb, N)
    return out.reshape(B, N, D)
```

```python
import jax
import jax.numpy as jnp
from jax import lax
from jax.experimental import pallas as pl
from jax.experimental.pallas import tpu as pltpu

F32 = jnp.float32
BF16 = jnp.bfloat16
I32 = jnp.int32
EPS = 1e-6

GRID_W = 64
MLA_HEADS = 4
MLA_NOPE = 128
MLA_ROPE = 64
MLA_V = 128
MLA_QK = MLA_NOPE + MLA_ROPE
ROPE_BASE = 10000.0
ML_HEADS = 4
ML_QK = 64
ML_V = 128
ML_CHUNK = 64
PEER_HEADS = 8
N_KEYS = 128
PEER_TOPK = 16
N_PAIR_CANDIDATES = 80

LANES = 128
SUBLANES = 8
VMEM_LIMIT_BYTES = 56 * 1024 * 1024

ROW_TILE = 512
ATTN_TQ = 256
OUT_TILE = 256
ROUTE_TILE = 256
PEER_TOKENS = 64

C_Q0, C_KV0, C_KR0, C_KRS0, C_QK0, C_V0, C_O0, C_G0, C_END = 0, 256, 384, 448, 512, 1024, 1536, 2048, 2176
HIGHEST = lax.Precision.HIGHEST


def _params(sem):
    return pltpu.CompilerParams(dimension_semantics=sem, vmem_limit_bytes=VMEM_LIMIT_BYTES)


def _rms(x):
    return x * lax.rsqrt(jnp.mean(x * x, axis=-1, keepdims=True) + EPS)


def _ada_kernel(c_ref, w_ref, b_ref, o_ref):
    c = c_ref[...]
    s = c * jax.nn.sigmoid(c)
    o_ref[...] = jnp.dot(s.astype(BF16), w_ref[...], preferred_element_type=F32) + b_ref[...]


def _ada(cc, w, b):
    rows, d = cc.shape
    n = w.shape[1]
    bn = n // 4
    return pl.pallas_call(
        _ada_kernel,
        out_shape=jax.ShapeDtypeStruct((rows, n), F32),
        grid=(4,),
        in_specs=[pl.BlockSpec((rows, d), lambda j: (0, 0)),
                  pl.BlockSpec((d, bn), lambda j: (0, j)),
                  pl.BlockSpec((1, bn), lambda j: (0, j))],
        out_specs=pl.BlockSpec((rows, bn), lambda j: (0, j)),
        compiler_params=_params(("arbitrary",)),
        name="ada",
    )(cc, w, b)


def _inproj_kernel(x_ref, g_ref, sc_ref, sh_ref, w_ref, wg_ref, o_ref, gt_ref):
    x = x_ref[0]
    h = _rms(x) * g_ref[...] * (1.0 + sc_ref[0]) + sh_ref[0]
    hb = h.astype(BF16)
    o_ref[0] = jnp.dot(hb, w_ref[...], preferred_element_type=F32)
    gt_ref[0] = lax.dot_general(wg_ref[...], hb, (((1,), (1,)), ((), ())), preferred_element_type=F32)


def _inproj(x, g, sc, sh, w, wg, tm):
    b, n, d = x.shape
    nc = w.shape[1]
    return pl.pallas_call(
        _inproj_kernel,
        out_shape=(jax.ShapeDtypeStruct((b, n, nc), F32), jax.ShapeDtypeStruct((b, 16, n), F32)),
        grid=(b, n // tm),
        in_specs=[pl.BlockSpec((1, tm, d), lambda i, j: (i, j, 0)),
                  pl.BlockSpec((1, d), lambda i, j: (0, 0)),
                  pl.BlockSpec((1, 1, d), lambda i, j: (i, 0, 0)),
                  pl.BlockSpec((1, 1, d), lambda i, j: (i, 0, 0)),
                  pl.BlockSpec((d, nc), lambda i, j: (0, 0)),
                  pl.BlockSpec((16, d), lambda i, j: (0, 0))],
        out_specs=(pl.BlockSpec((1, tm, nc), lambda i, j: (i, j, 0)),
                   pl.BlockSpec((1, 16, tm), lambda i, j: (i, 0, j))),
        compiler_params=_params(("parallel", "parallel")),
        name="inproj",
    )(x, g, sc, sh, w, wg)


def _mla_prep_kernel(c_ref, gcq_ref, wuq_ref, gckv_ref, wukv_ref, gq_ref, gqs_ref, gk_ref, gks_ref,
                     cos_ref, sin_ref, q_ref, k_ref, v_ref):
    c = c_ref[0]
    cq = _rms(c[:, C_Q0:C_KV0]) * gcq_ref[...]
    ckv = _rms(c[:, C_KV0:C_KR0]) * gckv_ref[...]
    kr = c[:, C_KR0:C_KRS0]
    krs = c[:, C_KRS0:C_QK0]
    q_raw = jnp.dot(cq.astype(BF16), wuq_ref[...], preferred_element_type=F32)
    kv_raw = jnp.dot(ckv.astype(BF16), wukv_ref[...], preferred_element_type=F32)
    cos = cos_ref[...]
    sin = sin_ref[...]
    gq = gq_ref[...]
    gk = gk_ref[...]
    kr_ss = jnp.sum(kr * kr, axis=-1, keepdims=True)
    for h in range(MLA_HEADS):
        o = h * 256
        qn = q_raw[:, o:o + 128]
        qr = q_raw[:, o + 128:o + 192]
        qs = q_raw[:, o + 192:o + 256]
        ss = jnp.sum(qn * qn, axis=-1, keepdims=True) + jnp.sum(qr * qr, axis=-1, keepdims=True)
        r = lax.rsqrt(ss * (1.0 / MLA_QK) + EPS) * (MLA_QK ** -0.5)
        q_ref[0, h, :, 0:128] = (qn * r * gq[:, 0:128]).astype(BF16)
        q_ref[0, h, :, 128:192] = ((qr * r * gq[:, 128:192]) * cos + (qs * r * gqs_ref[...]) * sin).astype(BF16)
        kn = kv_raw[:, o:o + 128]
        ss = jnp.sum(kn * kn, axis=-1, keepdims=True) + kr_ss
        r = lax.rsqrt(ss * (1.0 / MLA_QK) + EPS)
        k_ref[0, h, :, 0:128] = (kn * r * gk[:, 0:128]).astype(BF16)
        k_ref[0, h, :, 128:192] = ((kr * r * gk[:, 128:192]) * cos + (krs * r * gks_ref[...]) * sin).astype(BF16)
        v_ref[0, h] = kv_raw[:, o + 128:o + 256].astype(BF16)


def _mla_prep(cols, gcq, wuq, gckv, wukv, gq, gqs, gk, gks, cos, sin, tm):
    b, n, _ = cols.shape
    full = lambda a: pl.BlockSpec(a.shape, lambda i, j: (0,) * a.ndim)
    return pl.pallas_call(
        _mla_prep_kernel,
        out_shape=(jax.ShapeDtypeStruct((b, MLA_HEADS, n, MLA_QK), BF16),
                   jax.ShapeDtypeStruct((b, MLA_HEADS, n, MLA_QK), BF16),
                   jax.ShapeDtypeStruct((b, MLA_HEADS, n, MLA_V), BF16)),
        grid=(b, n // tm),
        in_specs=[pl.BlockSpec((1, tm, 512), lambda i, j: (i, j, 0)),
                  full(gcq), full(wuq), full(gckv), full(wukv), full(gq), full(gqs), full(gk), full(gks),
                  pl.BlockSpec((tm, MLA_ROPE), lambda i, j: (j, 0)),
                  pl.BlockSpec((tm, MLA_ROPE), lambda i, j: (j, 0))],
        out_specs=(pl.BlockSpec((1, MLA_HEADS, tm, MLA_QK), lambda i, j: (i, 0, j, 0)),
                   pl.BlockSpec((1, MLA_HEADS, tm, MLA_QK), lambda i, j: (i, 0, j, 0)),
                   pl.BlockSpec((1, MLA_HEADS, tm, MLA_V), lambda i, j: (i, 0, j, 0))),
        compiler_params=_params(("parallel", "parallel")),
        name="mla_prep",
    )(cols, gcq, wuq, gckv, wukv, gq, gqs, gk, gks, cos, sin)


ATTN_TILES_PER_STEP = 4


def _attn_kernel(q_ref, kc_ref, kl_ref, vc_ref, vl_ref, o_ref):
    tq = q_ref.shape[2] // ATTN_TILES_PER_STEP
    nc = kc_ref.shape[2]
    kc, kl, vc, vl = kc_ref[0, 0], kl_ref[0, 0], vc_ref[0, 0], vl_ref[0, 0]
    nt = (((1,), (1,)), ((), ()))
    for j in range(ATTN_TILES_PER_STEP):
        q = q_ref[0, 0, j * tq:(j + 1) * tq, :]
        s = jnp.concatenate([lax.dot_general(q, kc, nt, preferred_element_type=F32),
                             lax.dot_general(q, kl, nt, preferred_element_type=F32)], axis=1)
        m = jnp.max(s, axis=-1, keepdims=True)
        p = jnp.exp(s - m)
        l = jnp.sum(p, axis=-1, keepdims=True)
        pb = p.astype(BF16)
        o = (jnp.dot(pb[:, :nc], vc, preferred_element_type=F32)
             + jnp.dot(pb[:, nc:], vl, preferred_element_type=F32))
        o_ref[0, j * tq:(j + 1) * tq, :] = (o / l).astype(BF16)


def _attention(q, k_c, k_l, v_c, v_l, tq):
    b, h, n, dk = q.shape
    nc = k_c.shape[2]
    dv = v_l.shape[3]
    tb = tq * ATTN_TILES_PER_STEP
    whole = lambda nn, dd: pl.BlockSpec((1, 1, nn, dd), lambda i, j, t: (i, j, 0, 0))
    return pl.pallas_call(
        _attn_kernel,
        out_shape=jax.ShapeDtypeStruct((b, n, h * dv), BF16),
        grid=(b, h, n // tb),
        in_specs=[pl.BlockSpec((1, 1, tb, dk), lambda i, j, t: (i, j, t, 0)),
                  whole(nc, dk), whole(n, dk), whole(nc, dv), whole(n, dv)],
        out_specs=pl.BlockSpec((1, tb, dv), lambda i, j, t: (i, t, j)),
        compiler_params=_params(("parallel", "parallel", "parallel")),
        name="attention",
    )(q, k_c, k_l, v_c, v_l)


def _ml_prep_kernel(qk_ref, prev_ref, next_ref, vin_ref, g_ref, gt_ref, cw_ref, bc_ref, br_ref,
                    q_ref, kt_ref, v_ref, a_ref, brow_ref):
    tn = qk_ref.shape[1]
    j = pl.program_id(1)
    nj = pl.num_programs(1)
    u = qk_ref[0]
    row = lax.broadcasted_iota(I32, (tn, 1), 0)
    before = jnp.where(j == 0, 0.0, prev_ref[0, SUBLANES - 1:SUBLANES, :])
    after = jnp.where(j == nj - 1, 0.0, next_ref[0, 0:1, :])
    up = jnp.where(row == 0, before, pltpu.roll(u, 1, 0))
    un = jnp.where(row == tn - 1, after, pltpu.roll(u, tn - 1, 0))
    cw = cw_ref[...]
    y = up * cw[0:1, :] + u * cw[1:2, :] + un * cw[2:3, :]
    y = y * jax.nn.sigmoid(y)
    hq = ML_HEADS * ML_QK
    for h in range(ML_HEADS):
        q_ref[0, h] = (y[:, h * ML_QK:(h + 1) * ML_QK] * (ML_QK ** -0.5)).astype(BF16)
    kt = y[:, hq:2 * hq].T
    L = ML_CHUNK
    for c in range(tn // L):
        kt_ref[0, c] = kt[:, c * L:(c + 1) * L].astype(BF16)
    v_ref[0] = vin_ref[0].astype(BF16)
    g = g_ref[0][:, 0:16] + bc_ref[...]
    lf_c = jax.nn.log_sigmoid(g[:, 8:16])
    gt = gt_ref[0] + br_ref[...]
    ig_r = gt[0:8, :]
    lf_r = jax.nn.log_sigmoid(gt[8:16, :])
    ti = lax.broadcasted_iota(I32, (L, L), 0)
    si = lax.broadcasted_iota(I32, (L, L), 1)
    lower = (si <= ti).astype(F32)
    upper = (si >= ti).astype(F32)
    lane_fwd = lax.broadcasted_iota(I32, (L, 8), 1) < ML_HEADS
    row_fwd = lax.broadcasted_iota(I32, (8, L), 0) < ML_HEADS
    for c in range(tn // L):
        lo = c * L
        lfc = lf_c[lo:lo + L, :]
        a_ref[0, lo:lo + L, :] = jnp.where(
            lane_fwd,
            jnp.dot(lower, lfc, precision=HIGHEST, preferred_element_type=F32),
            jnp.dot(upper, lfc, precision=HIGHEST, preferred_element_type=F32))
        lfr = lf_r[:, lo:lo + L]
        cf_r = jnp.where(
            row_fwd,
            jnp.dot(lfr, upper, precision=HIGHEST, preferred_element_type=F32),
            jnp.dot(lfr, lower, precision=HIGHEST, preferred_element_type=F32))
        brow_ref[0, c] = ig_r[:, lo:lo + L] - cf_r


def _ml_prep(cols, gates_t, cw, bias_col, bias_row, tn):
    b, n, _ = cols.shape
    r8 = tn // SUBLANES
    last8 = n // SUBLANES - 1
    L = ML_CHUNK
    hq = ML_HEADS * ML_QK
    return pl.pallas_call(
        _ml_prep_kernel,
        out_shape=(jax.ShapeDtypeStruct((b, ML_HEADS, n, ML_QK), BF16),
                   jax.ShapeDtypeStruct((b, n // L, hq, L), BF16),
                   jax.ShapeDtypeStruct((b, n, ML_HEADS * ML_V), BF16),
                   jax.ShapeDtypeStruct((b, n, 8), F32),
                   jax.ShapeDtypeStruct((b, n // L, 8, L), F32)),
        grid=(b, n // tn),
        in_specs=[pl.BlockSpec((1, tn, 512), lambda i, j: (i, j, C_QK0 // 512)),
                  pl.BlockSpec((1, SUBLANES, 512), lambda i, j: (i, jnp.maximum(j * r8 - 1, 0), C_QK0 // 512)),
                  pl.BlockSpec((1, SUBLANES, 512), lambda i, j: (i, jnp.minimum((j + 1) * r8, last8), C_QK0 // 512)),
                  pl.BlockSpec((1, tn, 512), lambda i, j: (i, j, C_V0 // 512)),
                  pl.BlockSpec((1, tn, 128), lambda i, j: (i, j, C_G0 // 128)),
                  pl.BlockSpec((1, 16, tn), lambda i, j: (i, 0, j)),
                  pl.BlockSpec((3, 512), lambda i, j: (0, 0)),
                  pl.BlockSpec((1, 16), lambda i, j: (0, 0)),
                  pl.BlockSpec((16, 1), lambda i, j: (0, 0))],
        out_specs=(pl.BlockSpec((1, ML_HEADS, tn, ML_QK), lambda i, j: (i, 0, j, 0)),
                   pl.BlockSpec((1, tn // L, hq, L), lambda i, j: (i, j, 0, 0)),
                   pl.BlockSpec((1, tn, ML_HEADS * ML_V), lambda i, j: (i, j, 0)),
                   pl.BlockSpec((1, tn, 8), lambda i, j: (i, j, 0)),
                   pl.BlockSpec((1, tn // L, 8, L), lambda i, j: (i, j, 0, 0))),
        compiler_params=_params(("parallel", "parallel")),
        name="ml_prep",
    )(cols, cols, cols, cols, cols, gates_t, cw, bias_col, bias_row)


def _ml_chunk(qb, kt, vaug, a_c, b_r, ct, m_prev, fwd, need_out):
    L = qb.shape[0]
    ti = lax.broadcasted_iota(I32, (L, L), 0)
    si = lax.broadcasted_iota(I32, (L, L), 1)
    mask = (si <= ti) if fwd else (si >= ti)
    dmat = jnp.where(mask, a_c + b_r, -jnp.inf)
    mloc = jnp.max(dmat, axis=1, keepdims=True)
    inter = a_c + m_prev
    m_t = jnp.maximum(inter, mloc)
    last = L - 1 if fwd else 0
    m_new = m_t[last:last + 1, :]
    a_last = a_c[last:last + 1, :]
    w_r = jnp.exp(a_last + b_r - m_new)
    decay = jnp.exp(a_last + m_prev - m_new)
    kw = (kt.astype(F32) * w_r).astype(BF16)
    if not need_out:
        return decay * ct + jnp.dot(kw, vaug, preferred_element_type=F32), m_new, None
    s = jnp.dot(qb, kt, preferred_element_type=F32)
    wmat = jnp.exp(dmat - m_t) * s
    sc = jnp.exp(inter - m_t)
    top = jnp.concatenate([wmat.astype(BF16), (sc * qb.astype(F32)).astype(BF16)], axis=1)
    bot = jnp.concatenate([kw, jnp.zeros(kw.shape, BF16)], axis=1)
    res = jnp.dot(jnp.concatenate([top, bot], axis=0), jnp.concatenate([vaug, ct.astype(BF16)], axis=0),
                  preferred_element_type=F32)
    nd = res[:L]
    num = nd[:, 0:ML_V]
    den = nd[:, ML_V:ML_V + 1]
    return decay * ct + res[L:], m_new, num / jnp.maximum(jnp.abs(den), jnp.exp(-m_t))


def _ml_scan_kernel(ql_ref, ktl_ref, vl_ref, al_ref, brl_ref, qc_ref, ktc_ref, vc_ref, ac_ref, brc_ref,
                    out_ref, st_ref, m_ref):
    L = ML_CHUNK
    ncl = ql_ref.shape[2] // L
    ncc = qc_ref.shape[2] // L
    ones_col = (lax.broadcasted_iota(I32, (L, ML_V), 1) == 0).astype(BF16)
    st_ref[...] = jnp.zeros(st_ref.shape, F32)
    m_ref[...] = jnp.zeros(m_ref.shape, F32)

    def step(refs, c, d, fwd, need_out):
        q_ref, kt_ref, v_ref, a_ref, br_ref = refs
        sl = pl.ds(pl.multiple_of(c * L, L), L)
        a_all = a_ref[0, sl, :]
        br_all = br_ref[0, c]
        v_all = v_ref[0, sl, :]
        kt_all = kt_ref[0, c]
        hs = []
        for h in range(ML_HEADS):
            j = d * ML_HEADS + h
            vaug = jnp.concatenate([v_all[:, h * ML_V:(h + 1) * ML_V], ones_col], axis=1)
            ct_new, m_new, hh = _ml_chunk(
                q_ref[0, h, sl, :], kt_all[h * ML_QK:(h + 1) * ML_QK, :], vaug,
                a_all[:, j:j + 1], br_all[j:j + 1, :], st_ref[j], m_ref[j][0:1, 0:1], fwd, need_out)
            st_ref[j] = ct_new
            m_ref[j] = jnp.broadcast_to(m_new, (SUBLANES, LANES))
            hs.append(hh)
        return hs, sl

    ctx_refs = (qc_ref, ktc_ref, vc_ref, ac_ref, brc_ref)
    lat_refs = (ql_ref, ktl_ref, vl_ref, al_ref, brl_ref)

    out_ref[...] = jnp.zeros(out_ref.shape, F32)

    def ctx_body(i, carry):
        step(ctx_refs, i, 0, True, False)
        step(ctx_refs, ncc - 1 - i, 1, False, False)
        return carry

    def lat_body(i, carry):
        for d, fwd in ((0, True), (1, False)):
            hs, sl = step(lat_refs, i if fwd else ncl - 1 - i, d, fwd, True)
            out_ref[0, sl, :] = out_ref[0, sl, :] + jnp.concatenate(hs, axis=1)
        return carry

    lax.fori_loop(0, ncc, ctx_body, 0)
    lax.fori_loop(0, ncl, lat_body, 0)


def _ml_scan(ql, ktl, vl, al, brl, qc, ktc, vc, ac, brc):
    b, _, n, _ = ql.shape
    nctx = qc.shape[2]
    L = ML_CHUNK
    hv = ML_HEADS * ML_V
    hq = ML_HEADS * ML_QK
    qspec = lambda nn: pl.BlockSpec((1, ML_HEADS, nn, ML_QK), lambda i: (i, 0, 0, 0))
    ktspec = lambda nn: pl.BlockSpec((1, nn // L, hq, L), lambda i: (i, 0, 0, 0))
    vspec = lambda nn: pl.BlockSpec((1, nn, hv), lambda i: (i, 0, 0))
    aspec = lambda nn: pl.BlockSpec((1, nn, 8), lambda i: (i, 0, 0))
    rspec = lambda nn: pl.BlockSpec((1, nn // L, 8, L), lambda i: (i, 0, 0, 0))
    return pl.pallas_call(
        _ml_scan_kernel,
        out_shape=jax.ShapeDtypeStruct((b, n, hv), F32),
        grid=(b,),
        in_specs=[qspec(n), ktspec(n), vspec(n), aspec(n), rspec(n),
                  qspec(nctx), ktspec(nctx), vspec(nctx), aspec(nctx), rspec(nctx)],
        out_specs=pl.BlockSpec((1, n, hv), lambda i: (i, 0, 0)),
        scratch_shapes=[pltpu.VMEM((2 * ML_HEADS, ML_QK, 2 * ML_V), F32),
                        pltpu.VMEM((2 * ML_HEADS, SUBLANES, LANES), F32)],
        compiler_params=_params(("parallel",)),
        name="ml_scan",
    )(ql, ktl, vl, al, brl, qc, ktc, vc, ac, brc)


def _outproj_kernel(x_ref, mla_ref, hs_ref, o_ref, gm_ref, wa_ref, wb_ref, gt_ref, g2_ref, sc_ref, sh_ref,
                    wpq_ref, sk_ref, x1_ref, h2_ref, s_ref):
    hs = hs_ref[0]
    gm = gm_ref[...]
    hn = jnp.concatenate([_rms(hs[:, h * ML_V:(h + 1) * ML_V]) * gm[:, h * ML_V:(h + 1) * ML_V]
                          for h in range(ML_HEADS)], axis=1)
    ml = (jax.nn.sigmoid(o_ref[0]) * hn).astype(BF16)
    mix = (jnp.dot(mla_ref[0], wa_ref[...], preferred_element_type=F32)
           + jnp.dot(ml, wb_ref[...], preferred_element_type=F32))
    x1 = x_ref[0] + gt_ref[0] * mix
    x1_ref[0] = x1
    h2 = _rms(x1) * g2_ref[...] * (1.0 + sc_ref[0]) + sh_ref[0]
    h2_ref[0] = h2
    qp = jnp.dot(h2.astype(BF16), wpq_ref[...], preferred_element_type=F32).astype(BF16)
    for hp in range(2 * PEER_HEADS):
        s_ref[0, hp] = lax.dot_general(sk_ref[hp], qp[:, hp * N_KEYS:(hp + 1) * N_KEYS],
                                       (((1,), (1,)), ((), ())), preferred_element_type=F32)


def _outproj(x, mla, hsum, cols, gm, wa, wb, gt1, g2, sc2, sh2, wpq, sk, tm):
    b, n, d = x.shape
    hw = mla.shape[2]
    nq = wpq.shape[1]
    mod = pl.BlockSpec((1, 1, d), lambda i, j: (i, 0, 0))
    return pl.pallas_call(
        _outproj_kernel,
        out_shape=(jax.ShapeDtypeStruct((b, n, d), F32), jax.ShapeDtypeStruct((b, n, d), F32),
                   jax.ShapeDtypeStruct((b, 2 * PEER_HEADS, N_KEYS, n), F32)),
        grid=(b, n // tm),
        in_specs=[pl.BlockSpec((1, tm, d), lambda i, j: (i, j, 0)),
                  pl.BlockSpec((1, tm, hw), lambda i, j: (i, j, 0)),
                  pl.BlockSpec((1, tm, hw), lambda i, j: (i, j, 0)),
                  pl.BlockSpec((1, tm, 512), lambda i, j: (i, j, C_O0 // 512)),
                  pl.BlockSpec((1, hw), lambda i, j: (0, 0)),
                  pl.BlockSpec((hw, d), lambda i, j: (0, 0)),
                  pl.BlockSpec((hw, d), lambda i, j: (0, 0)),
                  mod, pl.BlockSpec((1, d), lambda i, j: (0, 0)), mod, mod,
                  pl.BlockSpec((d, nq), lambda i, j: (0, 0)),
                  pl.BlockSpec((2 * PEER_HEADS, N_KEYS, N_KEYS), lambda i, j: (0, 0, 0))],
        out_specs=(pl.BlockSpec((1, tm, d), lambda i, j: (i, j, 0)),
                   pl.BlockSpec((1, tm, d), lambda i, j: (i, j, 0)),
                   pl.BlockSpec((1, 2 * PEER_HEADS, N_KEYS, tm), lambda i, j: (i, 0, 0, j))),
        compiler_params=_params(("parallel", "parallel")),
        name="outproj",
    )(x, mla, hsum, cols, gm, wa, wb, gt1, g2, sc2, sh2, wpq, sk)


def _topk_rows(work_refs, rid_ref, rows, k):
    t = rid_ref.shape[1]
    kid = lax.broadcasted_iota(I32, (k, t), 0)
    big = jnp.float32(3.0e38)

    def body(r, carry):
        out = []
        for work_ref, (vals, idxs) in zip(work_refs, carry):
            s = work_ref[0:rows, :]
            rid = rid_ref[0:rows, :]
            m = jnp.max(s, axis=0, keepdims=True)
            i = jnp.min(jnp.where(s == m, rid, big), axis=0, keepdims=True)
            work_ref[0:rows, :] = jnp.where(rid == i, -jnp.inf, s)
            out.append((jnp.where(kid == r, m, vals), jnp.where(kid == r, i, idxs)))
        return tuple(out)

    init = tuple((jnp.zeros((k, t), F32), jnp.zeros((k, t), F32)) for _ in work_refs)
    return lax.fori_loop(0, k, body, init)


def _pair_candidates(sv0, sv1):
    K, t = sv0.shape
    h = K // 2
    iid = lax.broadcasted_iota(I32, (h, t), 0)
    vals = [sv0[0:h] + sv1[0:1], sv0[h:K] + sv1[0:1]]
    ids = [iid * K, (iid + h) * K]
    for j in range(1, h):
        vals.append(sv0[0:h] + sv1[j:j + 1])
        ids.append(iid * K + j)
    vals.append(sv0[0:1] + sv1[h:K])
    ids.append(iid + h)
    return jnp.concatenate(vals, axis=0), jnp.concatenate(ids, axis=0).astype(F32)


def _route_kernel(s_ref, off_ref, par_ref, gw_ref, et_ref, gt_ref, work_ref, work2_ref, rid_ref, cid_ref):
    K = PEER_TOPK
    nk, tt = rid_ref.shape
    rid_ref[...] = lax.broadcasted_iota(I32, (nk, tt), 0).astype(F32)

    def head(h, carry):
        work_ref[...] = s_ref[0, 2 * h]
        work2_ref[...] = s_ref[0, 2 * h + 1]
        (sv0, si0), (sv1, si1) = _topk_rows((work_ref, work2_ref), rid_ref, nk, K)
        cand, cid = _pair_candidates(sv0, sv1)
        nc = cand.shape[0]
        work_ref[0:nc, :] = cand
        cid_ref[...] = cid
        (best, pos), = _topk_rows((work_ref,), cid_ref, nc, K)
        pos = pos.astype(I32)
        si0 = si0.astype(I32)
        si1 = si1.astype(I32)
        isel = pos >> 4
        jsel = pos & (K - 1)
        e0 = jnp.zeros_like(pos)
        e1 = jnp.zeros_like(pos)
        for i in range(K):
            e0 = jnp.where(isel == i, si0[i:i + 1, :], e0)
            e1 = jnp.where(jsel == i, si1[i:i + 1, :], e1)
        ex = jnp.exp(best - best[0:1, :])
        sl = pl.ds(pl.multiple_of(h * K, K), K)
        et_ref[sl, :] = e0 * N_KEYS + e1
        gt_ref[sl, :] = ex / jnp.sum(ex, axis=0, keepdims=True)
        return carry

    lax.fori_loop(0, PEER_HEADS, head, 0)
    e = et_ref[...].T
    off_ref[0] = (e >> 1) * SUBLANES
    par_ref[0] = (e & 1).astype(F32)
    gw_ref[0] = gt_ref[...].T


def _route(scores, tt):
    b, hp, nk, n = scores.shape
    nsel = PEER_HEADS * PEER_TOPK
    ospec = pl.BlockSpec((1, tt, nsel), lambda i, j: (i, j, 0))
    return pl.pallas_call(
        _route_kernel,
        out_shape=(jax.ShapeDtypeStruct((b, n, nsel), I32), jax.ShapeDtypeStruct((b, n, nsel), F32),
                   jax.ShapeDtypeStruct((b, n, nsel), F32)),
        grid=(b, n // tt),
        in_specs=[pl.BlockSpec((1, hp, nk, tt), lambda i, j: (i, 0, 0, j))],
        out_specs=(ospec, ospec, ospec),
        scratch_shapes=[pltpu.VMEM((nsel, tt), I32), pltpu.VMEM((nsel, tt), F32),
                        pltpu.VMEM((nk, tt), F32), pltpu.VMEM((nk, tt), F32), pltpu.VMEM((nk, tt), F32),
                        pltpu.VMEM((N_PAIR_CANDIDATES, tt), F32)],
        compiler_params=_params(("parallel", "parallel")),
        name="route",
    )(scores)


TILE_ROWS = 2 * SUBLANES
TOKENS_PER_ITER = 16


CHUNK_SEL = 16


def _chunk_tiles(tab_ref, row_ref, c):
    return jnp.concatenate(
        [pltpu.bitcast(tab_ref[pl.ds(pl.multiple_of(row_ref[c * CHUNK_SEL + kk], SUBLANES), SUBLANES), :], BF16)
         for kk in range(CHUNK_SEL)], axis=0)


def _init_spread(spread_ref, gather_ref=None):
    @pl.when(pl.program_id(0) == 0)
    def _():
        nsel, width = spread_ref.shape
        spread_ref[...] = (lax.broadcasted_iota(I32, (nsel, width), 1) // TILE_ROWS
                           == lax.broadcasted_iota(I32, (nsel, width), 0)).astype(BF16)
        if gather_ref is not None:
            gather_ref[...] = (lax.broadcasted_iota(I32, (width, nsel), 0) // TILE_ROWS
                               == lax.broadcasted_iota(I32, (width, nsel), 1)).astype(BF16)


def _row_masks(width):
    half = (lax.broadcasted_iota(I32, (1, width), 1) % 2).astype(F32)
    lane8 = lax.broadcasted_iota(I32, (SUBLANES, width), 1)
    diag = ((lane8 % TILE_ROWS) // 2 == lax.broadcasted_iota(I32, (SUBLANES, width), 0)).astype(F32)
    return half, diag


def _row_to_tile(row):
    return jnp.concatenate([row[:, s * LANES:(s + 1) * LANES] for s in range(SUBLANES)], axis=0)


def _split_bf16(v):
    hi = v.astype(BF16)
    return hi, (v - hi.astype(F32)).astype(BF16)


def _peer_u_kernel(off_ref, h_ref, tab_ref, par_ref, g_ref, c_ref, r_ref, spread_ref, gather_ref):
    tb, nsel = g_ref.shape
    _init_spread(spread_ref, gather_ref)
    half, diag = _row_masks(spread_ref.shape[1])

    cw = CHUNK_SEL * TILE_ROWS
    diag_c = diag[:, :cw]
    nt = (((1,), (1,)), ((), ()))

    def tok(i, carry):
        ts = [i * TOKENS_PER_ITER + u for u in range(TOKENS_PER_ITER)]
        rows = [off_ref.at[t] for t in ts]
        xs = [jnp.concatenate(_split_bf16(_row_to_tile(h_ref[pl.ds(t, 1), :])), axis=0) for t in ts]
        for c in range(nsel // CHUNK_SEL):
            for u, t in enumerate(ts):
                res = lax.dot_general(xs[u], _chunk_tiles(tab_ref, rows[u], c), nt, preferred_element_type=F32)
                r_ref[pl.ds(t, 1), c * cw:(c + 1) * cw] = jnp.sum(
                    (res[:SUBLANES] + res[SUBLANES:]) * diag_c, axis=0, keepdims=True)
        return carry

    lax.fori_loop(0, tb // TOKENS_PER_ITER, tok, 0)
    mine = jnp.dot(par_ref[...].astype(BF16), spread_ref[...], preferred_element_type=F32) == half
    r_hi, r_lo = _split_bf16(jnp.where(mine, r_ref[...], 0.0))
    both = jnp.dot(jnp.concatenate([r_hi, r_lo], axis=0), gather_ref[...], preferred_element_type=F32)
    pre = both[:tb] + both[tb:]
    c_ref[...] = g_ref[...] * (0.5 * pre * (1.0 + lax.erf(pre * (2.0 ** -0.5))))


def _smem_block(tb, nsel):
    return pl.BlockSpec((tb, nsel), lambda i: (i, 0), memory_space=pltpu.SMEM)


def _peer_u(off, h2, tab, par, gw, tb):
    t, nsel = gw.shape
    vspec = pl.BlockSpec((tb, nsel), lambda i: (i, 0))
    return pl.pallas_call(
        _peer_u_kernel,
        out_shape=jax.ShapeDtypeStruct((t, nsel), F32),
        grid=(t // tb,),
        in_specs=[_smem_block(tb, nsel),
                  pl.BlockSpec((tb, h2.shape[1]), lambda i: (i, 0)),
                  pl.BlockSpec(tab.shape, lambda i: (0, 0), pipeline_mode=pl.Buffered(1)),
                  vspec, vspec],
        out_specs=vspec,
        scratch_shapes=[pltpu.VMEM((tb, nsel * TILE_ROWS), F32),
                        pltpu.VMEM((nsel, nsel * TILE_ROWS), BF16), pltpu.VMEM((nsel * TILE_ROWS, nsel), BF16)],
        compiler_params=_params(("arbitrary",)),
        name="peer_u",
    )(off, h2, tab, par, gw)


def _peer_v_kernel(off_ref, c_ref, par_ref, tab_ref, x1_ref, gt_ref, o_ref, lhi_ref, llo_ref, spread_ref):
    tb, nsel = c_ref.shape
    _init_spread(spread_ref)
    half, diag = _row_masks(spread_ref.shape[1])
    c_hi, c_lo = _split_bf16(c_ref[...])
    ex = jnp.dot(jnp.concatenate([par_ref[...].astype(BF16), c_hi, c_lo], axis=0), spread_ref[...],
                 preferred_element_type=F32)
    mine = ex[:tb] == half
    lhi_ref[...] = jnp.where(mine, ex[tb:2 * tb], 0.0)
    llo_ref[...] = jnp.where(mine, ex[2 * tb:], 0.0)
    gt = gt_ref[0]

    cw = CHUNK_SEL * TILE_ROWS
    diag_c = diag[:, :cw]

    def tok(i, carry):
        ts = [i * TOKENS_PER_ITER + u for u in range(TOKENS_PER_ITER)]
        rows = [off_ref.at[t] for t in ts]
        accs = [jnp.zeros((TILE_ROWS, LANES), F32) for _ in ts]
        for c in range(nsel // CHUNK_SEL):
            cols = slice(c * cw, (c + 1) * cw)
            for u, t in enumerate(ts):
                left = jnp.concatenate([lhi_ref[pl.ds(t, 1), cols] * diag_c, llo_ref[pl.ds(t, 1), cols] * diag_c],
                                       axis=0).astype(BF16)
                accs[u] = accs[u] + jnp.dot(left, _chunk_tiles(tab_ref, rows[u], c), preferred_element_type=F32)
        for u, t in enumerate(ts):
            y = accs[u][:SUBLANES] + accs[u][SUBLANES:]
            y_row = jnp.concatenate([y[s:s + 1, :] for s in range(SUBLANES)], axis=1)
            o_ref[pl.ds(t, 1), :] = x1_ref[pl.ds(t, 1), :] + gt * y_row
        return carry

    lax.fori_loop(0, tb // TOKENS_PER_ITER, tok, 0)


def _peer_v(off, coef, par, tab, x1, gt2, tb, tok_per_batch):
    t, nsel = coef.shape
    d = x1.shape[1]
    bpb = tok_per_batch // tb
    vspec = pl.BlockSpec((tb, nsel), lambda i: (i, 0))
    return pl.pallas_call(
        _peer_v_kernel,
        out_shape=jax.ShapeDtypeStruct(x1.shape, F32),
        grid=(t // tb,),
        in_specs=[_smem_block(tb, nsel), vspec, vspec,
                  pl.BlockSpec(tab.shape, lambda i: (0, 0), pipeline_mode=pl.Buffered(1)),
                  pl.BlockSpec((tb, d), lambda i: (i, 0)),
                  pl.BlockSpec((1, 1, d), lambda i: (i // bpb, 0, 0))],
        out_specs=pl.BlockSpec((tb, d), lambda i: (i, 0)),
        scratch_shapes=[pltpu.VMEM((tb, nsel * TILE_ROWS), F32), pltpu.VMEM((tb, nsel * TILE_ROWS), F32),
                        pltpu.VMEM((nsel, nsel * TILE_ROWS), BF16)],
        compiler_params=_params(("arbitrary",)),
        name="peer_v",
    )(off, coef, par, tab, x1, gt2)


def _rope_tables(n):
    axis = MLA_ROPE // 2
    t = jnp.arange(n, dtype=F32)
    row = jnp.floor(t / GRID_W)
    col = t - row * GRID_W
    inv = ROPE_BASE ** (-jnp.arange(axis // 2, dtype=F32) * (2.0 / axis))
    ar = row[:, None] * inv
    ac = col[:, None] * inv
    cos = jnp.concatenate([jnp.cos(ar), jnp.cos(ar), jnp.cos(ac), jnp.cos(ac)], axis=1)
    sin = jnp.concatenate([-jnp.sin(ar), jnp.sin(ar), -jnp.sin(ac), jnp.sin(ac)], axis=1)
    return cos, sin


PACK_ROWS = 256


def _pack_kernel(t_ref, o_ref):
    n = t_ref.shape[0]
    rows = n // 2
    xb = t_ref[...].astype(BF16)
    col = lax.broadcasted_iota(I32, (rows, n), 1)
    row2 = 2 * lax.broadcasted_iota(I32, (rows, n), 0)
    even = pltpu.bitcast(jnp.dot((col == row2).astype(BF16), xb, preferred_element_type=F32), jnp.uint32)
    odd = pltpu.bitcast(jnp.dot((col == row2 + 1).astype(BF16), xb, preferred_element_type=F32), jnp.uint32)
    word = odd | (even >> 16)
    for s in range(SUBLANES):
        o_ref[pl.ds(s, rows, stride=SUBLANES), :] = word[:, s * LANES:(s + 1) * LANES]


def _pack_table(tab):
    e, d = tab.shape
    out_rows = PACK_ROWS // 2 * (d // LANES)
    return pl.pallas_call(
        _pack_kernel,
        out_shape=jax.ShapeDtypeStruct((e // 2 * (d // LANES), LANES), jnp.uint32),
        grid=(e // PACK_ROWS,),
        in_specs=[pl.BlockSpec((PACK_ROWS, d), lambda i: (i, 0))],
        out_specs=pl.BlockSpec((out_rows, LANES), lambda i: (i, 0)),
        compiler_params=_params(("parallel",)),
        name="pack_table",
    )(tab)


def _block(n, want):
    return want if n % want == 0 else n


def kernel(x, c, ctx, c_ctx, w_ada, b_ada, g_norm1, w_in, g_cq, w_uq, g_ckv, w_ukv, g_qn, g_kn, conv_qk, b_igate, b_fgate, g_mlstm, w_out, g_norm2, w_pq, sub_keys, expert_u, expert_v):
    B, N, D = x.shape
    NC = ctx.shape[1]
    assert w_ada.shape[0] == 1, "one layer"
    assert N % (ATTN_TQ * ATTN_TILES_PER_STEP) == 0 and NC % ML_CHUNK == 0 and D == SUBLANES * LANES
    q_rank = g_cq.shape[1]
    kv_rank = g_ckv.shape[1]
    mla_cols = q_rank + kv_rank + MLA_ROPE
    assert (q_rank, kv_rank) == (C_KV0 - C_Q0, C_KR0 - C_KV0)
    swap = jnp.arange(MLA_ROPE) ^ (MLA_ROPE // 4)

    cc = jnp.concatenate([c, c_ctx[None, :], jnp.zeros((16 - B - 1, D), F32)], axis=0)
    mod = _ada(cc, w_ada[0].astype(BF16), b_ada)
    sh1, sc1, gt1, sh2, sc2, gt2 = [mod[:, i * D:(i + 1) * D] for i in range(6)]
    lat = lambda m: m[:B].reshape(B, 1, D)
    ctxm = lambda m: jnp.broadcast_to(m[B:B + 1].reshape(1, 1, D), (B, 1, D))

    wi = w_in[0]
    n_qk = 2 * ML_HEADS * ML_QK
    n_v = ML_HEADS * ML_V
    n_g = 4 * ML_HEADS
    m0 = mla_cols
    w_cols = jnp.concatenate([
        wi[:, 0:mla_cols],
        wi[:, q_rank + kv_rank + swap],
        wi[:, m0:m0 + n_qk + 2 * n_v + n_g],
        jnp.zeros((D, C_END - C_G0 - n_g), F32)], axis=1).astype(BF16)
    assert w_cols.shape[1] == C_END
    w_gates_t = wi[:, m0 + n_qk + 2 * n_v:m0 + n_qk + 2 * n_v + n_g].T.astype(BF16)
    cols_l, gates_l = _inproj(x, g_norm1, lat(sc1), lat(sh1), w_cols, w_gates_t, _block(N, ROW_TILE))
    cols_c, gates_c = _inproj(ctx, g_norm1, ctxm(sc1), ctxm(sh1), w_cols, w_gates_t, _block(NC, ROW_TILE))

    wq = w_uq[0].reshape(q_rank, MLA_HEADS, MLA_QK)
    wuq = jnp.concatenate([wq, wq[:, :, MLA_NOPE + swap]], axis=2).reshape(q_rank, MLA_HEADS * 256).astype(BF16)
    wukv = w_ukv[0].astype(BF16)
    gqs, gks = g_qn[:, MLA_NOPE + swap], g_kn[:, MLA_NOPE + swap]
    cos_l, sin_l = _rope_tables(N)
    cos_c, sin_c = jnp.ones((NC, MLA_ROPE), F32), jnp.zeros((NC, MLA_ROPE), F32)
    q_l, k_l, v_l = _mla_prep(cols_l, g_cq, wuq, g_ckv, wukv, g_qn, gqs, g_kn, gks, cos_l, sin_l, _block(N, ROW_TILE))
    _, k_c, v_c = _mla_prep(cols_c, g_cq, wuq, g_ckv, wukv, g_qn, gqs, g_kn, gks, cos_c, sin_c, _block(NC, ROW_TILE))
    mla = _attention(q_l, k_c, k_l, v_c, v_l, ATTN_TQ)

    bias16 = jnp.concatenate([b_igate[0].reshape(-1), b_fgate[0].reshape(-1)])
    cw = conv_qk[0]
    pl_ = _ml_prep(cols_l, gates_l, cw, bias16[None, :], bias16[:, None], _block(N, ROW_TILE))
    pc_ = _ml_prep(cols_c, gates_c, cw, bias16[None, :], bias16[:, None], _block(NC, ROW_TILE))
    hsum = _ml_scan(*pl_, *pc_)

    wo = w_out[0].astype(BF16)
    hw = MLA_HEADS * MLA_V
    sk = sub_keys[0].reshape(2 * PEER_HEADS, N_KEYS, -1).astype(BF16)
    x1, h2, scores = _outproj(x, mla, hsum, cols_l, g_mlstm, wo[:hw], wo[hw:], lat(gt1), g_norm2, lat(sc2), lat(sh2),
                              w_pq[0].astype(BF16), sk, OUT_TILE)
    T = B * N
    nsel = PEER_HEADS * PEER_TOPK
    off, par, gw = [a.reshape(T, nsel) for a in _route(scores, ROUTE_TILE)]

    tb = PEER_TOKENS
    coef = _peer_u(off, h2.reshape(T, D), _pack_table(expert_u[0]), par, gw, tb)
    out = _peer_v(off, coef, par, _pack_table(expert_v[0]), x1.reshape(T, D), lat(gt2), tb, N)
    return out.reshape(B, N, D)
```

```python
import jax
import jax.numpy as jnp
from jax import lax
from jax.experimental import pallas as pl
from jax.experimental.pallas import tpu as pltpu

F32 = jnp.float32
BF16 = jnp.bfloat16
I32 = jnp.int32
EPS = 1e-6

GRID_W = 64
MLA_HEADS = 4
MLA_NOPE = 128
MLA_ROPE = 64
MLA_V = 128
MLA_QK = MLA_NOPE + MLA_ROPE
ROPE_BASE = 10000.0
ML_HEADS = 4
ML_QK = 64
ML_V = 128
ML_CHUNK = 64
PEER_HEADS = 8
N_KEYS = 128
PEER_TOPK = 16
N_PAIR_CANDIDATES = 80

LANES = 128
SUBLANES = 8
VMEM_LIMIT_BYTES = 56 * 1024 * 1024

ROW_TILE = 512
ATTN_TQ = 256
OUT_TILE = 256
ROUTE_TILE = 256
PEER_TOKENS = 64

C_Q0, C_KV0, C_KR0, C_KRS0, C_QK0, C_V0, C_O0, C_G0, C_END = 0, 256, 384, 448, 512, 1024, 1536, 2048, 2176
HIGHEST = lax.Precision.HIGHEST


def _params(sem):
    return pltpu.CompilerParams(dimension_semantics=sem, vmem_limit_bytes=VMEM_LIMIT_BYTES)


def _rms(x):
    return x * lax.rsqrt(jnp.mean(x * x, axis=-1, keepdims=True) + EPS)


def _ada_kernel(c_ref, w_ref, b_ref, o_ref):
    c = c_ref[...]
    s = c * jax.nn.sigmoid(c)
    o_ref[...] = jnp.dot(s.astype(BF16), w_ref[...], preferred_element_type=F32) + b_ref[...]


def _ada(cc, w, b):
    rows, d = cc.shape
    n = w.shape[1]
    bn = n // 4
    return pl.pallas_call(
        _ada_kernel,
        out_shape=jax.ShapeDtypeStruct((rows, n), F32),
        grid=(4,),
        in_specs=[pl.BlockSpec((rows, d), lambda j: (0, 0)),
                  pl.BlockSpec((d, bn), lambda j: (0, j)),
                  pl.BlockSpec((1, bn), lambda j: (0, j))],
        out_specs=pl.BlockSpec((rows, bn), lambda j: (0, j)),
        compiler_params=_params(("arbitrary",)),
        name="ada",
    )(cc, w, b)


def _inproj_kernel(x_ref, g_ref, sc_ref, sh_ref, w_ref, wg_ref, o_ref, gt_ref):
    x = x_ref[0]
    h = _rms(x) * g_ref[...] * (1.0 + sc_ref[0]) + sh_ref[0]
    hb = h.astype(BF16)
    o_ref[0] = jnp.dot(hb, w_ref[...], preferred_element_type=F32)
    gt_ref[0] = lax.dot_general(wg_ref[...], hb, (((1,), (1,)), ((), ())), preferred_element_type=F32)


def _inproj(x, g, sc, sh, w, wg, tm):
    b, n, d = x.shape
    nc = w.shape[1]
    return pl.pallas_call(
        _inproj_kernel,
        out_shape=(jax.ShapeDtypeStruct((b, n, nc), F32), jax.ShapeDtypeStruct((b, 16, n), F32)),
        grid=(b, n // tm),
        in_specs=[pl.BlockSpec((1, tm, d), lambda i, j: (i, j, 0)),
                  pl.BlockSpec((1, d), lambda i, j: (0, 0)),
                  pl.BlockSpec((1, 1, d), lambda i, j: (i, 0, 0)),
                  pl.BlockSpec((1, 1, d), lambda i, j: (i, 0, 0)),
                  pl.BlockSpec((d, nc), lambda i, j: (0, 0)),
                  pl.BlockSpec((16, d), lambda i, j: (0, 0))],
        out_specs=(pl.BlockSpec((1, tm, nc), lambda i, j: (i, j, 0)),
                   pl.BlockSpec((1, 16, tm), lambda i, j: (i, 0, j))),
        compiler_params=_params(("parallel", "parallel")),
        name="inproj",
    )(x, g, sc, sh, w, wg)


def _mla_prep_kernel(c_ref, gcq_ref, wuq_ref, gckv_ref, wukv_ref, gq_ref, gqs_ref, gk_ref, gks_ref,
                     cos_ref, sin_ref, q_ref, k_ref, v_ref):
    c = c_ref[0]
    cq = _rms(c[:, C_Q0:C_KV0]) * gcq_ref[...]
    ckv = _rms(c[:, C_KV0:C_KR0]) * gckv_ref[...]
    kr = c[:, C_KR0:C_KRS0]
    krs = c[:, C_KRS0:C_QK0]
    q_raw = jnp.dot(cq.astype(BF16), wuq_ref[...], preferred_element_type=F32)
    kv_raw = jnp.dot(ckv.astype(BF16), wukv_ref[...], preferred_element_type=F32)
    cos = cos_ref[...]
    sin = sin_ref[...]
    gq = gq_ref[...]
    gk = gk_ref[...]
    kr_ss = jnp.sum(kr * kr, axis=-1, keepdims=True)
    for h in range(MLA_HEADS):
        o = h * 256
        qn = q_raw[:, o:o + 128]
        qr = q_raw[:, o + 128:o + 192]
        qs = q_raw[:, o + 192:o + 256]
        ss = jnp.sum(qn * qn, axis=-1, keepdims=True) + jnp.sum(qr * qr, axis=-1, keepdims=True)
        r = lax.rsqrt(ss * (1.0 / MLA_QK) + EPS) * (MLA_QK ** -0.5)
        q_ref[0, h, :, 0:128] = (qn * r * gq[:, 0:128]).astype(BF16)
        q_ref[0, h, :, 128:192] = ((qr * r * gq[:, 128:192]) * cos + (qs * r * gqs_ref[...]) * sin).astype(BF16)
        kn = kv_raw[:, o:o + 128]
        ss = jnp.sum(kn * kn, axis=-1, keepdims=True) + kr_ss
        r = lax.rsqrt(ss * (1.0 / MLA_QK) + EPS)
        k_ref[0, h, :, 0:128] = (kn * r * gk[:, 0:128]).astype(BF16)
        k_ref[0, h, :, 128:192] = ((kr * r * gk[:, 128:192]) * cos + (krs * r * gks_ref[...]) * sin).astype(BF16)
        v_ref[0, h] = kv_raw[:, o + 128:o + 256].astype(BF16)


def _mla_prep(cols, gcq, wuq, gckv, wukv, gq, gqs, gk, gks, cos, sin, tm):
    b, n, _ = cols.shape
    full = lambda a: pl.BlockSpec(a.shape, lambda i, j: (0,) * a.ndim)
    return pl.pallas_call(
        _mla_prep_kernel,
        out_shape=(jax.ShapeDtypeStruct((b, MLA_HEADS, n, MLA_QK), BF16),
                   jax.ShapeDtypeStruct((b, MLA_HEADS, n, MLA_QK), BF16),
                   jax.ShapeDtypeStruct((b, MLA_HEADS, n, MLA_V), BF16)),
        grid=(b, n // tm),
        in_specs=[pl.BlockSpec((1, tm, 512), lambda i, j: (i, j, 0)),
                  full(gcq), full(wuq), full(gckv), full(wukv), full(gq), full(gqs), full(gk), full(gks),
                  pl.BlockSpec((tm, MLA_ROPE), lambda i, j: (j, 0)),
                  pl.BlockSpec((tm, MLA_ROPE), lambda i, j: (j, 0))],
        out_specs=(pl.BlockSpec((1, MLA_HEADS, tm, MLA_QK), lambda i, j: (i, 0, j, 0)),
                   pl.BlockSpec((1, MLA_HEADS, tm, MLA_QK), lambda i, j: (i, 0, j, 0)),
                   pl.BlockSpec((1, MLA_HEADS, tm, MLA_V), lambda i, j: (i, 0, j, 0))),
        compiler_params=_params(("parallel", "parallel")),
        name="mla_prep",
    )(cols, gcq, wuq, gckv, wukv, gq, gqs, gk, gks, cos, sin)


ATTN_TILES_PER_STEP = 4


def _attn_kernel(q_ref, kc_ref, kl_ref, vc_ref, vl_ref, o_ref):
    tq = q_ref.shape[2] // ATTN_TILES_PER_STEP
    nc = kc_ref.shape[2]
    kc, kl, vc, vl = kc_ref[0, 0], kl_ref[0, 0], vc_ref[0, 0], vl_ref[0, 0]
    nt = (((1,), (1,)), ((), ()))
    for j in range(ATTN_TILES_PER_STEP):
        q = q_ref[0, 0, j * tq:(j + 1) * tq, :]
        s = jnp.concatenate([lax.dot_general(q, kc, nt, preferred_element_type=F32),
                             lax.dot_general(q, kl, nt, preferred_element_type=F32)], axis=1)
        m = jnp.max(s, axis=-1, keepdims=True)
        p = jnp.exp(s - m)
        l = jnp.sum(p, axis=-1, keepdims=True)
        pb = p.astype(BF16)
        o = (jnp.dot(pb[:, :nc], vc, preferred_element_type=F32)
             + jnp.dot(pb[:, nc:], vl, preferred_element_type=F32))
        o_ref[0, j * tq:(j + 1) * tq, :] = (o / l).astype(BF16)


def _attention(q, k_c, k_l, v_c, v_l, tq):
    b, h, n, dk = q.shape
    nc = k_c.shape[2]
    dv = v_l.shape[3]
    tb = tq * ATTN_TILES_PER_STEP
    whole = lambda nn, dd: pl.BlockSpec((1, 1, nn, dd), lambda i, j, t: (i, j, 0, 0))
    return pl.pallas_call(
        _attn_kernel,
        out_shape=jax.ShapeDtypeStruct((b, n, h * dv), BF16),
        grid=(b, h, n // tb),
        in_specs=[pl.BlockSpec((1, 1, tb, dk), lambda i, j, t: (i, j, t, 0)),
                  whole(nc, dk), whole(n, dk), whole(nc, dv), whole(n, dv)],
        out_specs=pl.BlockSpec((1, tb, dv), lambda i, j, t: (i, t, j)),
        compiler_params=_params(("parallel", "parallel", "parallel")),
        name="attention",
    )(q, k_c, k_l, v_c, v_l)


def _ml_prep_kernel(qk_ref, prev_ref, next_ref, vin_ref, g_ref, gt_ref, cw_ref, bc_ref, br_ref,
                    q_ref, kt_ref, v_ref, a_ref, brow_ref):
    tn = qk_ref.shape[1]
    j = pl.program_id(1)
    nj = pl.num_programs(1)
    u = qk_ref[0]
    row = lax.broadcasted_iota(I32, (tn, 1), 0)
    before = jnp.where(j == 0, 0.0, prev_ref[0, SUBLANES - 1:SUBLANES, :])
    after = jnp.where(j == nj - 1, 0.0, next_ref[0, 0:1, :])
    up = jnp.where(row == 0, before, pltpu.roll(u, 1, 0))
    un = jnp.where(row == tn - 1, after, pltpu.roll(u, tn - 1, 0))
    cw = cw_ref[...]
    y = up * cw[0:1, :] + u * cw[1:2, :] + un * cw[2:3, :]
    y = y * jax.nn.sigmoid(y)
    hq = ML_HEADS * ML_QK
    for h in range(ML_HEADS):
        q_ref[0, h] = (y[:, h * ML_QK:(h + 1) * ML_QK] * (ML_QK ** -0.5)).astype(BF16)
    kt = y[:, hq:2 * hq].T
    L = ML_CHUNK
    for c in range(tn // L):
        kt_ref[0, c] = kt[:, c * L:(c + 1) * L].astype(BF16)
    v_ref[0] = vin_ref[0].astype(BF16)
    g = g_ref[0][:, 0:16] + bc_ref[...]
    lf_c = jax.nn.log_sigmoid(g[:, 8:16])
    gt = gt_ref[0] + br_ref[...]
    ig_r = gt[0:8, :]
    lf_r = jax.nn.log_sigmoid(gt[8:16, :])
    ti = lax.broadcasted_iota(I32, (L, L), 0)
    si = lax.broadcasted_iota(I32, (L, L), 1)
    lower = (si <= ti).astype(F32)
    upper = (si >= ti).astype(F32)
    lane_fwd = lax.broadcasted_iota(I32, (L, 8), 1) < ML_HEADS
    row_fwd = lax.broadcasted_iota(I32, (8, L), 0) < ML_HEADS
    for c in range(tn // L):
        lo = c * L
        lfc = lf_c[lo:lo + L, :]
        a_ref[0, lo:lo + L, :] = jnp.where(
            lane_fwd,
            jnp.dot(lower, lfc, precision=HIGHEST, preferred_element_type=F32),
            jnp.dot(upper, lfc, precision=HIGHEST, preferred_element_type=F32))
        lfr = lf_r[:, lo:lo + L]
        cf_r = jnp.where(
            row_fwd,
            jnp.dot(lfr, upper, precision=HIGHEST, preferred_element_type=F32),
            jnp.dot(lfr, lower, precision=HIGHEST, preferred_element_type=F32))
        brow_ref[0, c] = ig_r[:, lo:lo + L] - cf_r


def _ml_prep(cols, gates_t, cw, bias_col, bias_row, tn):
    b, n, _ = cols.shape
    r8 = tn // SUBLANES
    last8 = n // SUBLANES - 1
    L = ML_CHUNK
    hq = ML_HEADS * ML_QK
    return pl.pallas_call(
        _ml_prep_kernel,
        out_shape=(jax.ShapeDtypeStruct((b, ML_HEADS, n, ML_QK), BF16),
                   jax.ShapeDtypeStruct((b, n // L, hq, L), BF16),
                   jax.ShapeDtypeStruct((b, n, ML_HEADS * ML_V), BF16),
                   jax.ShapeDtypeStruct((b, n, 8), F32),
                   jax.ShapeDtypeStruct((b, n // L, 8, L), F32)),
        grid=(b, n // tn),
        in_specs=[pl.BlockSpec((1, tn, 512), lambda i, j: (i, j, C_QK0 // 512)),
                  pl.BlockSpec((1, SUBLANES, 512), lambda i, j: (i, jnp.maximum(j * r8 - 1, 0), C_QK0 // 512)),
                  pl.BlockSpec((1, SUBLANES, 512), lambda i, j: (i, jnp.minimum((j + 1) * r8, last8), C_QK0 // 512)),
                  pl.BlockSpec((1, tn, 512), lambda i, j: (i, j, C_V0 // 512)),
                  pl.BlockSpec((1, tn, 128), lambda i, j: (i, j, C_G0 // 128)),
                  pl.BlockSpec((1, 16, tn), lambda i, j: (i, 0, j)),
                  pl.BlockSpec((3, 512), lambda i, j: (0, 0)),
                  pl.BlockSpec((1, 16), lambda i, j: (0, 0)),
                  pl.BlockSpec((16, 1), lambda i, j: (0, 0))],
        out_specs=(pl.BlockSpec((1, ML_HEADS, tn, ML_QK), lambda i, j: (i, 0, j, 0)),
                   pl.BlockSpec((1, tn // L, hq, L), lambda i, j: (i, j, 0, 0)),
                   pl.BlockSpec((1, tn, ML_HEADS * ML_V), lambda i, j: (i, j, 0)),
                   pl.BlockSpec((1, tn, 8), lambda i, j: (i, j, 0)),
                   pl.BlockSpec((1, tn // L, 8, L), lambda i, j: (i, j, 0, 0))),
        compiler_params=_params(("parallel", "parallel")),
        name="ml_prep",
    )(cols, cols, cols, cols, cols, gates_t, cw, bias_col, bias_row)


def _ml_chunk(qb, kt, vaug, a_c, b_r, ct, m_prev, fwd, need_out):
    L = qb.shape[0]
    ti = lax.broadcasted_iota(I32, (L, L), 0)
    si = lax.broadcasted_iota(I32, (L, L), 1)
    mask = (si <= ti) if fwd else (si >= ti)
    dmat = jnp.where(mask, a_c + b_r, -jnp.inf)
    mloc = jnp.max(dmat, axis=1, keepdims=True)
    inter = a_c + m_prev
    m_t = jnp.maximum(inter, mloc)
    last = L - 1 if fwd else 0
    m_new = m_t[last:last + 1, :]
    a_last = a_c[last:last + 1, :]
    w_r = jnp.exp(a_last + b_r - m_new)
    decay = jnp.exp(a_last + m_prev - m_new)
    kw = (kt.astype(F32) * w_r).astype(BF16)
    if not need_out:
        return decay * ct + jnp.dot(kw, vaug, preferred_element_type=F32), m_new, None
    s = jnp.dot(qb, kt, preferred_element_type=F32)
    wmat = jnp.exp(dmat - m_t) * s
    sc = jnp.exp(inter - m_t)
    top = jnp.concatenate([wmat.astype(BF16), (sc * qb.astype(F32)).astype(BF16)], axis=1)
    bot = jnp.concatenate([kw, jnp.zeros(kw.shape, BF16)], axis=1)
    res = jnp.dot(jnp.concatenate([top, bot], axis=0), jnp.concatenate([vaug, ct.astype(BF16)], axis=0),
                  preferred_element_type=F32)
    nd = res[:L]
    num = nd[:, 0:ML_V]
    den = nd[:, ML_V:ML_V + 1]
    return decay * ct + res[L:], m_new, num / jnp.maximum(jnp.abs(den), jnp.exp(-m_t))


def _ml_scan_kernel(ql_ref, ktl_ref, vl_ref, al_ref, brl_ref, qc_ref, ktc_ref, vc_ref, ac_ref, brc_ref,
                    out_ref, st_ref, m_ref):
    L = ML_CHUNK
    ncl = ql_ref.shape[2] // L
    ncc = qc_ref.shape[2] // L
    ones_col = (lax.broadcasted_iota(I32, (L, ML_V), 1) == 0).astype(BF16)
    st_ref[...] = jnp.zeros(st_ref.shape, F32)
    m_ref[...] = jnp.zeros(m_ref.shape, F32)

    def step(refs, c, d, fwd, need_out):
        q_ref, kt_ref, v_ref, a_ref, br_ref = refs
        sl = pl.ds(pl.multiple_of(c * L, L), L)
        a_all = a_ref[0, sl, :]
        br_all = br_ref[0, c]
        v_all = v_ref[0, sl, :]
        kt_all = kt_ref[0, c]
        hs = []
        for h in range(ML_HEADS):
            j = d * ML_HEADS + h
            vaug = jnp.concatenate([v_all[:, h * ML_V:(h + 1) * ML_V], ones_col], axis=1)
            ct_new, m_new, hh = _ml_chunk(
                q_ref[0, h, sl, :], kt_all[h * ML_QK:(h + 1) * ML_QK, :], vaug,
                a_all[:, j:j + 1], br_all[j:j + 1, :], st_ref[j], m_ref[j][0:1, 0:1], fwd, need_out)
            st_ref[j] = ct_new
            m_ref[j] = jnp.broadcast_to(m_new, (SUBLANES, LANES))
            hs.append(hh)
        return hs, sl

    ctx_refs = (qc_ref, ktc_ref, vc_ref, ac_ref, brc_ref)
    lat_refs = (ql_ref, ktl_ref, vl_ref, al_ref, brl_ref)

    out_ref[...] = jnp.zeros(out_ref.shape, F32)

    def ctx_body(i, carry):
        step(ctx_refs, i, 0, True, False)
        step(ctx_refs, ncc - 1 - i, 1, False, False)
        return carry

    def lat_body(i, carry):
        for d, fwd in ((0, True), (1, False)):
            hs, sl = step(lat_refs, i if fwd else ncl - 1 - i, d, fwd, True)
            out_ref[0, sl, :] = out_ref[0, sl, :] + jnp.concatenate(hs, axis=1)
        return carry

    lax.fori_loop(0, ncc, ctx_body, 0)
    lax.fori_loop(0, ncl, lat_body, 0)


def _ml_scan(ql, ktl, vl, al, brl, qc, ktc, vc, ac, brc):
    b, _, n, _ = ql.shape
    nctx = qc.shape[2]
    L = ML_CHUNK
    hv = ML_HEADS * ML_V
    hq = ML_HEADS * ML_QK
    qspec = lambda nn: pl.BlockSpec((1, ML_HEADS, nn, ML_QK), lambda i: (i, 0, 0, 0))
    ktspec = lambda nn: pl.BlockSpec((1, nn // L, hq, L), lambda i: (i, 0, 0, 0))
    vspec = lambda nn: pl.BlockSpec((1, nn, hv), lambda i: (i, 0, 0))
    aspec = lambda nn: pl.BlockSpec((1, nn, 8), lambda i: (i, 0, 0))
    rspec = lambda nn: pl.BlockSpec((1, nn // L, 8, L), lambda i: (i, 0, 0, 0))
    return pl.pallas_call(
        _ml_scan_kernel,
        out_shape=jax.ShapeDtypeStruct((b, n, hv), F32),
        grid=(b,),
        in_specs=[qspec(n), ktspec(n), vspec(n), aspec(n), rspec(n),
                  qspec(nctx), ktspec(nctx), vspec(nctx), aspec(nctx), rspec(nctx)],
        out_specs=pl.BlockSpec((1, n, hv), lambda i: (i, 0, 0)),
        scratch_shapes=[pltpu.VMEM((2 * ML_HEADS, ML_QK, 2 * ML_V), F32),
                        pltpu.VMEM((2 * ML_HEADS, SUBLANES, LANES), F32)],
        compiler_params=_params(("parallel",)),
        name="ml_scan",
    )(ql, ktl, vl, al, brl, qc, ktc, vc, ac, brc)


def _outproj_kernel(x_ref, mla_ref, hs_ref, o_ref, gm_ref, wa_ref, wb_ref, gt_ref, g2_ref, sc_ref, sh_ref,
                    wpq_ref, sk_ref, x1_ref, h2_ref, s_ref):
    hs = hs_ref[0]
    gm = gm_ref[...]
    hn = jnp.concatenate([_rms(hs[:, h * ML_V:(h + 1) * ML_V]) * gm[:, h * ML_V:(h + 1) * ML_V]
                          for h in range(ML_HEADS)], axis=1)
    ml = (jax.nn.sigmoid(o_ref[0]) * hn).astype(BF16)
    mix = (jnp.dot(mla_ref[0], wa_ref[...], preferred_element_type=F32)
           + jnp.dot(ml, wb_ref[...], preferred_element_type=F32))
    x1 = x_ref[0] + gt_ref[0] * mix
    x1_ref[0] = x1
    h2 = _rms(x1) * g2_ref[...] * (1.0 + sc_ref[0]) + sh_ref[0]
    h2_ref[0] = h2
    qp = jnp.dot(h2.astype(BF16), wpq_ref[...], preferred_element_type=F32).astype(BF16)
    for hp in range(2 * PEER_HEADS):
        s_ref[0, hp] = lax.dot_general(sk_ref[hp], qp[:, hp * N_KEYS:(hp + 1) * N_KEYS],
                                       (((1,), (1,)), ((), ())), preferred_element_type=F32)


def _outproj(x, mla, hsum, cols, gm, wa, wb, gt1, g2, sc2, sh2, wpq, sk, tm):
    b, n, d = x.shape
    hw = mla.shape[2]
    nq = wpq.shape[1]
    mod = pl.BlockSpec((1, 1, d), lambda i, j: (i, 0, 0))
    return pl.pallas_call(
        _outproj_kernel,
        out_shape=(jax.ShapeDtypeStruct((b, n, d), F32), jax.ShapeDtypeStruct((b, n, d), F32),
                   jax.ShapeDtypeStruct((b, 2 * PEER_HEADS, N_KEYS, n), F32)),
        grid=(b, n // tm),
        in_specs=[pl.BlockSpec((1, tm, d), lambda i, j: (i, j, 0)),
                  pl.BlockSpec((1, tm, hw), lambda i, j: (i, j, 0)),
                  pl.BlockSpec((1, tm, hw), lambda i, j: (i, j, 0)),
                  pl.BlockSpec((1, tm, 512), lambda i, j: (i, j, C_O0 // 512)),
                  pl.BlockSpec((1, hw), lambda i, j: (0, 0)),
                  pl.BlockSpec((hw, d), lambda i, j: (0, 0)),
                  pl.BlockSpec((hw, d), lambda i, j: (0, 0)),
                  mod, pl.BlockSpec((1, d), lambda i, j: (0, 0)), mod, mod,
                  pl.BlockSpec((d, nq), lambda i, j: (0, 0)),
                  pl.BlockSpec((2 * PEER_HEADS, N_KEYS, N_KEYS), lambda i, j: (0, 0, 0))],
        out_specs=(pl.BlockSpec((1, tm, d), lambda i, j: (i, j, 0)),
                   pl.BlockSpec((1, tm, d), lambda i, j: (i, j, 0)),
                   pl.BlockSpec((1, 2 * PEER_HEADS, N_KEYS, tm), lambda i, j: (i, 0, 0, j))),
        compiler_params=_params(("parallel", "parallel")),
        name="outproj",
    )(x, mla, hsum, cols, gm, wa, wb, gt1, g2, sc2, sh2, wpq, sk)


def _topk_rows(work_refs, rid_ref, rows, k):
    t = rid_ref.shape[1]
    kid = lax.broadcasted_iota(I32, (k, t), 0)
    big = jnp.float32(3.0e38)

    def body(r, carry):
        out = []
        for work_ref, (vals, idxs) in zip(work_refs, carry):
            s = work_ref[0:rows, :]
            rid = rid_ref[0:rows, :]
            m = jnp.max(s, axis=0, keepdims=True)
            i = jnp.min(jnp.where(s == m, rid, big), axis=0, keepdims=True)
            work_ref[0:rows, :] = jnp.where(rid == i, -jnp.inf, s)
            out.append((jnp.where(kid == r, m, vals), jnp.where(kid == r, i, idxs)))
        return tuple(out)

    init = tuple((jnp.zeros((k, t), F32), jnp.zeros((k, t), F32)) for _ in work_refs)
    return lax.fori_loop(0, k, body, init)


def _pair_candidates(sv0, sv1):
    K, t = sv0.shape
    h = K // 2
    iid = lax.broadcasted_iota(I32, (h, t), 0)
    vals = [sv0[0:h] + sv1[0:1], sv0[h:K] + sv1[0:1]]
    ids = [iid * K, (iid + h) * K]
    for j in range(1, h):
        vals.append(sv0[0:h] + sv1[j:j + 1])
        ids.append(iid * K + j)
    vals.append(sv0[0:1] + sv1[h:K])
    ids.append(iid + h)
    return jnp.concatenate(vals, axis=0), jnp.concatenate(ids, axis=0).astype(F32)


HEADS_PER_ROUND = 4


def _route_kernel(s_ref, off_ref, par_ref, gw_ref, et_ref, gt_ref, work_ref, rid_ref, cid_ref):
    K = PEER_TOPK
    nk, tt = rid_ref.shape
    rid_ref[...] = lax.broadcasted_iota(I32, (nk, tt), 0).astype(F32)
    hr = HEADS_PER_ROUND

    def round_(g, carry):
        heads = [g * hr + a for a in range(hr)]
        for a, h in enumerate(heads):
            work_ref[2 * a] = s_ref[0, 2 * h]
            work_ref[2 * a + 1] = s_ref[0, 2 * h + 1]
        first = _topk_rows([work_ref.at[b] for b in range(2 * hr)], rid_ref, nk, K)
        ncand = N_PAIR_CANDIDATES
        for a in range(hr):
            cand, cid = _pair_candidates(first[2 * a][0], first[2 * a + 1][0])
            work_ref[a, 0:ncand, :] = cand
            cid_ref[...] = cid
        second = _topk_rows([work_ref.at[a] for a in range(hr)], cid_ref, ncand, K)
        for a, h in enumerate(heads):
            best, pos = second[a]
            pos = pos.astype(I32)
            si0 = first[2 * a][1].astype(I32)
            si1 = first[2 * a + 1][1].astype(I32)
            isel = pos >> 4
            jsel = pos & (K - 1)
            e0 = jnp.zeros_like(pos)
            e1 = jnp.zeros_like(pos)
            for i in range(K):
                e0 = jnp.where(isel == i, si0[i:i + 1, :], e0)
                e1 = jnp.where(jsel == i, si1[i:i + 1, :], e1)
            ex = jnp.exp(best - best[0:1, :])
            sl = pl.ds(pl.multiple_of(h * K, K), K)
            et_ref[sl, :] = e0 * N_KEYS + e1
            gt_ref[sl, :] = ex / jnp.sum(ex, axis=0, keepdims=True)
        return carry

    lax.fori_loop(0, PEER_HEADS // hr, round_, 0)
    e = et_ref[...].T
    off_ref[0] = (e >> 1) * SUBLANES
    par_ref[0] = (e & 1).astype(F32)
    gw_ref[0] = gt_ref[...].T


def _route(scores, tt):
    b, hp, nk, n = scores.shape
    nsel = PEER_HEADS * PEER_TOPK
    ospec = pl.BlockSpec((1, tt, nsel), lambda i, j: (i, j, 0))
    return pl.pallas_call(
        _route_kernel,
        out_shape=(jax.ShapeDtypeStruct((b, n, nsel), I32), jax.ShapeDtypeStruct((b, n, nsel), F32),
                   jax.ShapeDtypeStruct((b, n, nsel), F32)),
        grid=(b, n // tt),
        in_specs=[pl.BlockSpec((1, hp, nk, tt), lambda i, j: (i, 0, 0, j))],
        out_specs=(ospec, ospec, ospec),
        scratch_shapes=[pltpu.VMEM((nsel, tt), I32), pltpu.VMEM((nsel, tt), F32),
                        pltpu.VMEM((2 * HEADS_PER_ROUND, nk, tt), F32), pltpu.VMEM((nk, tt), F32),
                        pltpu.VMEM((N_PAIR_CANDIDATES, tt), F32)],
        compiler_params=_params(("parallel", "parallel")),
        name="route",
    )(scores)


TILE_ROWS = 2 * SUBLANES
TOKENS_PER_ITER = 16


CHUNK_SEL = 16


def _chunk_tiles(tab_ref, row_ref, c):
    return jnp.concatenate(
        [pltpu.bitcast(tab_ref[pl.ds(pl.multiple_of(row_ref[c * CHUNK_SEL + kk], SUBLANES), SUBLANES), :], BF16)
         for kk in range(CHUNK_SEL)], axis=0)


def _init_spread(spread_ref, gather_ref=None):
    @pl.when(pl.program_id(0) == 0)
    def _():
        nsel, width = spread_ref.shape
        spread_ref[...] = (lax.broadcasted_iota(I32, (nsel, width), 1) // TILE_ROWS
                           == lax.broadcasted_iota(I32, (nsel, width), 0)).astype(BF16)
        if gather_ref is not None:
            gather_ref[...] = (lax.broadcasted_iota(I32, (width, nsel), 0) // TILE_ROWS
                               == lax.broadcasted_iota(I32, (width, nsel), 1)).astype(BF16)


def _row_masks(width):
    half = (lax.broadcasted_iota(I32, (1, width), 1) % 2).astype(F32)
    lane8 = lax.broadcasted_iota(I32, (SUBLANES, width), 1)
    diag = ((lane8 % TILE_ROWS) // 2 == lax.broadcasted_iota(I32, (SUBLANES, width), 0)).astype(F32)
    return half, diag


def _row_to_tile(row):
    return jnp.concatenate([row[:, s * LANES:(s + 1) * LANES] for s in range(SUBLANES)], axis=0)


def _split_bf16(v):
    hi = v.astype(BF16)
    return hi, (v - hi.astype(F32)).astype(BF16)


def _peer_u_kernel(off_ref, h_ref, tab_ref, par_ref, g_ref, c_ref, r_ref, spread_ref, gather_ref):
    tb, nsel = g_ref.shape
    _init_spread(spread_ref, gather_ref)
    half, diag = _row_masks(spread_ref.shape[1])

    cw = CHUNK_SEL * TILE_ROWS
    diag_c = diag[:, :cw]
    nt = (((1,), (1,)), ((), ()))

    def tok(i, carry):
        ts = [i * TOKENS_PER_ITER + u for u in range(TOKENS_PER_ITER)]
        rows = [off_ref.at[t] for t in ts]
        xs = [jnp.concatenate(_split_bf16(_row_to_tile(h_ref[pl.ds(t, 1), :])), axis=0) for t in ts]
        for c in range(nsel // CHUNK_SEL):
            for u, t in enumerate(ts):
                res = lax.dot_general(xs[u], _chunk_tiles(tab_ref, rows[u], c), nt, preferred_element_type=F32)
                r_ref[pl.ds(t, 1), c * cw:(c + 1) * cw] = jnp.sum(
                    (res[:SUBLANES] + res[SUBLANES:]) * diag_c, axis=0, keepdims=True)
        return carry

    lax.fori_loop(0, tb // TOKENS_PER_ITER, tok, 0)
    mine = jnp.dot(par_ref[...].astype(BF16), spread_ref[...], preferred_element_type=F32) == half
    r_hi, r_lo = _split_bf16(jnp.where(mine, r_ref[...], 0.0))
    both = jnp.dot(jnp.concatenate([r_hi, r_lo], axis=0), gather_ref[...], preferred_element_type=F32)
    pre = both[:tb] + both[tb:]
    c_ref[...] = g_ref[...] * (0.5 * pre * (1.0 + lax.erf(pre * (2.0 ** -0.5))))


def _smem_block(tb, nsel):
    return pl.BlockSpec((tb, nsel), lambda i: (i, 0), memory_space=pltpu.SMEM)


def _peer_u(off, h2, tab, par, gw, tb):
    t, nsel = gw.shape
    vspec = pl.BlockSpec((tb, nsel), lambda i: (i, 0))
    return pl.pallas_call(
        _peer_u_kernel,
        out_shape=jax.ShapeDtypeStruct((t, nsel), F32),
        grid=(t // tb,),
        in_specs=[_smem_block(tb, nsel),
                  pl.BlockSpec((tb, h2.shape[1]), lambda i: (i, 0)),
                  pl.BlockSpec(tab.shape, lambda i: (0, 0), pipeline_mode=pl.Buffered(1)),
                  vspec, vspec],
        out_specs=vspec,
        scratch_shapes=[pltpu.VMEM((tb, nsel * TILE_ROWS), F32),
                        pltpu.VMEM((nsel, nsel * TILE_ROWS), BF16), pltpu.VMEM((nsel * TILE_ROWS, nsel), BF16)],
        compiler_params=_params(("arbitrary",)),
        name="peer_u",
    )(off, h2, tab, par, gw)


def _peer_v_kernel(off_ref, c_ref, par_ref, tab_ref, x1_ref, gt_ref, o_ref, lhi_ref, llo_ref, spread_ref):
    tb, nsel = c_ref.shape
    _init_spread(spread_ref)
    half, diag = _row_masks(spread_ref.shape[1])
    c_hi, c_lo = _split_bf16(c_ref[...])
    ex = jnp.dot(jnp.concatenate([par_ref[...].astype(BF16), c_hi, c_lo], axis=0), spread_ref[...],
                 preferred_element_type=F32)
    mine = ex[:tb] == half
    lhi_ref[...] = jnp.where(mine, ex[tb:2 * tb], 0.0)
    llo_ref[...] = jnp.where(mine, ex[2 * tb:], 0.0)
    gt = gt_ref[0]

    cw = CHUNK_SEL * TILE_ROWS
    diag_c = diag[:, :cw]

    def tok(i, carry):
        ts = [i * TOKENS_PER_ITER + u for u in range(TOKENS_PER_ITER)]
        rows = [off_ref.at[t] for t in ts]
        accs = [jnp.zeros((TILE_ROWS, LANES), F32) for _ in ts]
        for c in range(nsel // CHUNK_SEL):
            cols = slice(c * cw, (c + 1) * cw)
            for u, t in enumerate(ts):
                left = jnp.concatenate([lhi_ref[pl.ds(t, 1), cols] * diag_c, llo_ref[pl.ds(t, 1), cols] * diag_c],
                                       axis=0).astype(BF16)
                accs[u] = accs[u] + jnp.dot(left, _chunk_tiles(tab_ref, rows[u], c), preferred_element_type=F32)
        for u, t in enumerate(ts):
            y = accs[u][:SUBLANES] + accs[u][SUBLANES:]
            y_row = jnp.concatenate([y[s:s + 1, :] for s in range(SUBLANES)], axis=1)
            o_ref[pl.ds(t, 1), :] = x1_ref[pl.ds(t, 1), :] + gt * y_row
        return carry

    lax.fori_loop(0, tb // TOKENS_PER_ITER, tok, 0)


def _peer_v(off, coef, par, tab, x1, gt2, tb, tok_per_batch):
    t, nsel = coef.shape
    d = x1.shape[1]
    bpb = tok_per_batch // tb
    vspec = pl.BlockSpec((tb, nsel), lambda i: (i, 0))
    return pl.pallas_call(
        _peer_v_kernel,
        out_shape=jax.ShapeDtypeStruct(x1.shape, F32),
        grid=(t // tb,),
        in_specs=[_smem_block(tb, nsel), vspec, vspec,
                  pl.BlockSpec(tab.shape, lambda i: (0, 0), pipeline_mode=pl.Buffered(1)),
                  pl.BlockSpec((tb, d), lambda i: (i, 0)),
                  pl.BlockSpec((1, 1, d), lambda i: (i // bpb, 0, 0))],
        out_specs=pl.BlockSpec((tb, d), lambda i: (i, 0)),
        scratch_shapes=[pltpu.VMEM((tb, nsel * TILE_ROWS), F32), pltpu.VMEM((tb, nsel * TILE_ROWS), F32),
                        pltpu.VMEM((nsel, nsel * TILE_ROWS), BF16)],
        compiler_params=_params(("arbitrary",)),
        name="peer_v",
    )(off, coef, par, tab, x1, gt2)


def _rope_tables(n):
    axis = MLA_ROPE // 2
    t = jnp.arange(n, dtype=F32)
    row = jnp.floor(t / GRID_W)
    col = t - row * GRID_W
    inv = ROPE_BASE ** (-jnp.arange(axis // 2, dtype=F32) * (2.0 / axis))
    ar = row[:, None] * inv
    ac = col[:, None] * inv
    cos = jnp.concatenate([jnp.cos(ar), jnp.cos(ar), jnp.cos(ac), jnp.cos(ac)], axis=1)
    sin = jnp.concatenate([-jnp.sin(ar), jnp.sin(ar), -jnp.sin(ac), jnp.sin(ac)], axis=1)
    return cos, sin


PACK_ROWS = 256


def _pack_kernel(t_ref, o_ref):
    n = t_ref.shape[0]
    rows = n // 2
    xb = t_ref[...].astype(BF16)
    col = lax.broadcasted_iota(I32, (rows, n), 1)
    row2 = 2 * lax.broadcasted_iota(I32, (rows, n), 0)
    even = pltpu.bitcast(jnp.dot((col == row2).astype(BF16), xb, preferred_element_type=F32), jnp.uint32)
    odd = pltpu.bitcast(jnp.dot((col == row2 + 1).astype(BF16), xb, preferred_element_type=F32), jnp.uint32)
    word = odd | (even >> 16)
    for s in range(SUBLANES):
        o_ref[pl.ds(s, rows, stride=SUBLANES), :] = word[:, s * LANES:(s + 1) * LANES]


def _pack_table(tab):
    e, d = tab.shape
    out_rows = PACK_ROWS // 2 * (d // LANES)
    return pl.pallas_call(
        _pack_kernel,
        out_shape=jax.ShapeDtypeStruct((e // 2 * (d // LANES), LANES), jnp.uint32),
        grid=(e // PACK_ROWS,),
        in_specs=[pl.BlockSpec((PACK_ROWS, d), lambda i: (i, 0))],
        out_specs=pl.BlockSpec((out_rows, LANES), lambda i: (i, 0)),
        compiler_params=_params(("parallel",)),
        name="pack_table",
    )(tab)


def _block(n, want):
    return want if n % want == 0 else n


def kernel(x, c, ctx, c_ctx, w_ada, b_ada, g_norm1, w_in, g_cq, w_uq, g_ckv, w_ukv, g_qn, g_kn, conv_qk, b_igate, b_fgate, g_mlstm, w_out, g_norm2, w_pq, sub_keys, expert_u, expert_v):
    B, N, D = x.shape
    NC = ctx.shape[1]
    assert w_ada.shape[0] == 1, "one layer"
    assert N % (ATTN_TQ * ATTN_TILES_PER_STEP) == 0 and NC % ML_CHUNK == 0 and D == SUBLANES * LANES
    q_rank = g_cq.shape[1]
    kv_rank = g_ckv.shape[1]
    mla_cols = q_rank + kv_rank + MLA_ROPE
    assert (q_rank, kv_rank) == (C_KV0 - C_Q0, C_KR0 - C_KV0)
    swap = jnp.arange(MLA_ROPE) ^ (MLA_ROPE // 4)

    cc = jnp.concatenate([c, c_ctx[None, :], jnp.zeros((16 - B - 1, D), F32)], axis=0)
    mod = _ada(cc, w_ada[0].astype(BF16), b_ada)
    sh1, sc1, gt1, sh2, sc2, gt2 = [mod[:, i * D:(i + 1) * D] for i in range(6)]
    lat = lambda m: m[:B].reshape(B, 1, D)
    ctxm = lambda m: jnp.broadcast_to(m[B:B + 1].reshape(1, 1, D), (B, 1, D))

    wi = w_in[0]
    n_qk = 2 * ML_HEADS * ML_QK
    n_v = ML_HEADS * ML_V
    n_g = 4 * ML_HEADS
    m0 = mla_cols
    w_cols = jnp.concatenate([
        wi[:, 0:mla_cols],
        wi[:, q_rank + kv_rank + swap],
        wi[:, m0:m0 + n_qk + 2 * n_v + n_g],
        jnp.zeros((D, C_END - C_G0 - n_g), F32)], axis=1).astype(BF16)
    assert w_cols.shape[1] == C_END
    w_gates_t = wi[:, m0 + n_qk + 2 * n_v:m0 + n_qk + 2 * n_v + n_g].T.astype(BF16)
    cols_l, gates_l = _inproj(x, g_norm1, lat(sc1), lat(sh1), w_cols, w_gates_t, _block(N, ROW_TILE))
    cols_c, gates_c = _inproj(ctx, g_norm1, ctxm(sc1), ctxm(sh1), w_cols, w_gates_t, _block(NC, ROW_TILE))

    wq = w_uq[0].reshape(q_rank, MLA_HEADS, MLA_QK)
    wuq = jnp.concatenate([wq, wq[:, :, MLA_NOPE + swap]], axis=2).reshape(q_rank, MLA_HEADS * 256).astype(BF16)
    wukv = w_ukv[0].astype(BF16)
    gqs, gks = g_qn[:, MLA_NOPE + swap], g_kn[:, MLA_NOPE + swap]
    cos_l, sin_l = _rope_tables(N)
    cos_c, sin_c = jnp.ones((NC, MLA_ROPE), F32), jnp.zeros((NC, MLA_ROPE), F32)
    q_l, k_l, v_l = _mla_prep(cols_l, g_cq, wuq, g_ckv, wukv, g_qn, gqs, g_kn, gks, cos_l, sin_l, _block(N, ROW_TILE))
    _, k_c, v_c = _mla_prep(cols_c, g_cq, wuq, g_ckv, wukv, g_qn, gqs, g_kn, gks, cos_c, sin_c, _block(NC, ROW_TILE))
    mla = _attention(q_l, k_c, k_l, v_c, v_l, ATTN_TQ)

    bias16 = jnp.concatenate([b_igate[0].reshape(-1), b_fgate[0].reshape(-1)])
    cw = conv_qk[0]
    pl_ = _ml_prep(cols_l, gates_l, cw, bias16[None, :], bias16[:, None], _block(N, ROW_TILE))
    pc_ = _ml_prep(cols_c, gates_c, cw, bias16[None, :], bias16[:, None], _block(NC, ROW_TILE))
    hsum = _ml_scan(*pl_, *pc_)

    wo = w_out[0].astype(BF16)
    hw = MLA_HEADS * MLA_V
    sk = sub_keys[0].reshape(2 * PEER_HEADS, N_KEYS, -1).astype(BF16)
    x1, h2, scores = _outproj(x, mla, hsum, cols_l, g_mlstm, wo[:hw], wo[hw:], lat(gt1), g_norm2, lat(sc2), lat(sh2),
                              w_pq[0].astype(BF16), sk, OUT_TILE)
    T = B * N
    nsel = PEER_HEADS * PEER_TOPK
    off, par, gw = [a.reshape(T, nsel) for a in _route(scores, ROUTE_TILE)]

    tb = PEER_TOKENS
    coef = _peer_u(off, h2.reshape(T, D), _pack_table(expert_u[0]), par, gw, tb)
    out = _peer_v(off, coef, par, _pack_table(expert_v[0]), x1.reshape(T, D), lat(gt2), tb, N)
    return out.reshape(B, N, D)
```

```python
import jax
import jax.numpy as jnp
from jax import lax
from jax.experimental import pallas as pl
from jax.experimental.pallas import tpu as pltpu

F32 = jnp.float32
BF16 = jnp.bfloat16
I32 = jnp.int32
EPS = 1e-6

GRID_W = 64
MLA_HEADS = 4
MLA_NOPE = 128
MLA_ROPE = 64
MLA_V = 128
MLA_QK = MLA_NOPE + MLA_ROPE
ROPE_BASE = 10000.0
ML_HEADS = 4
ML_QK = 64
ML_V = 128
ML_CHUNK = 64
PEER_HEADS = 8
N_KEYS = 128
PEER_TOPK = 16
N_PAIR_CANDIDATES = 80

LANES = 128
SUBLANES = 8
VMEM_LIMIT_BYTES = 56 * 1024 * 1024

ROW_TILE = 512
ATTN_TQ = 256
OUT_TILE = 256
ROUTE_TILE = 256
PEER_TOKENS = 64

C_Q0, C_KV0, C_KR0, C_KRS0, C_QK0, C_V0, C_O0, C_G0, C_END = 0, 256, 384, 448, 512, 1024, 1536, 2048, 2176
HIGHEST = lax.Precision.HIGHEST


def _params(sem):
    return pltpu.CompilerParams(dimension_semantics=sem, vmem_limit_bytes=VMEM_LIMIT_BYTES)


def _rms(x):
    return x * lax.rsqrt(jnp.mean(x * x, axis=-1, keepdims=True) + EPS)


def _ada_kernel(c_ref, w_ref, b_ref, o_ref):
    c = c_ref[...]
    s = c * jax.nn.sigmoid(c)
    o_ref[...] = jnp.dot(s.astype(BF16), w_ref[...], preferred_element_type=F32) + b_ref[...]


def _ada(cc, w, b):
    rows, d = cc.shape
    n = w.shape[1]
    bn = n // 4
    return pl.pallas_call(
        _ada_kernel,
        out_shape=jax.ShapeDtypeStruct((rows, n), F32),
        grid=(4,),
        in_specs=[pl.BlockSpec((rows, d), lambda j: (0, 0)),
                  pl.BlockSpec((d, bn), lambda j: (0, j)),
                  pl.BlockSpec((1, bn), lambda j: (0, j))],
        out_specs=pl.BlockSpec((rows, bn), lambda j: (0, j)),
        compiler_params=_params(("arbitrary",)),
        name="ada",
    )(cc, w, b)


def _inproj_kernel(x_ref, g_ref, sc_ref, sh_ref, w_ref, wg_ref, o_ref, gt_ref):
    x = x_ref[0]
    h = _rms(x) * g_ref[...] * (1.0 + sc_ref[0]) + sh_ref[0]
    hb = h.astype(BF16)
    o_ref[0] = jnp.dot(hb, w_ref[...], preferred_element_type=F32)
    gt_ref[0] = lax.dot_general(wg_ref[...], hb, (((1,), (1,)), ((), ())), preferred_element_type=F32)


def _inproj(x, g, sc, sh, w, wg, tm):
    b, n, d = x.shape
    nc = w.shape[1]
    return pl.pallas_call(
        _inproj_kernel,
        out_shape=(jax.ShapeDtypeStruct((b, n, nc), F32), jax.ShapeDtypeStruct((b, 16, n), F32)),
        grid=(b, n // tm),
        in_specs=[pl.BlockSpec((1, tm, d), lambda i, j: (i, j, 0)),
                  pl.BlockSpec((1, d), lambda i, j: (0, 0)),
                  pl.BlockSpec((1, 1, d), lambda i, j: (i, 0, 0)),
                  pl.BlockSpec((1, 1, d), lambda i, j: (i, 0, 0)),
                  pl.BlockSpec((d, nc), lambda i, j: (0, 0)),
                  pl.BlockSpec((16, d), lambda i, j: (0, 0))],
        out_specs=(pl.BlockSpec((1, tm, nc), lambda i, j: (i, j, 0)),
                   pl.BlockSpec((1, 16, tm), lambda i, j: (i, 0, j))),
        compiler_params=_params(("parallel", "parallel")),
        name="inproj",
    )(x, g, sc, sh, w, wg)


def _mla_prep_kernel(c_ref, gcq_ref, wuq_ref, gckv_ref, wukv_ref, gq_ref, gqs_ref, gk_ref, gks_ref,
                     cos_ref, sin_ref, q_ref, k_ref, v_ref):
    c = c_ref[0]
    cq = _rms(c[:, C_Q0:C_KV0]) * gcq_ref[...]
    ckv = _rms(c[:, C_KV0:C_KR0]) * gckv_ref[...]
    kr = c[:, C_KR0:C_KRS0]
    krs = c[:, C_KRS0:C_QK0]
    q_raw = jnp.dot(cq.astype(BF16), wuq_ref[...], preferred_element_type=F32)
    kv_raw = jnp.dot(ckv.astype(BF16), wukv_ref[...], preferred_element_type=F32)
    cos = cos_ref[...]
    sin = sin_ref[...]
    gq = gq_ref[...]
    gk = gk_ref[...]
    kr_ss = jnp.sum(kr * kr, axis=-1, keepdims=True)
    for h in range(MLA_HEADS):
        o = h * 256
        qn = q_raw[:, o:o + 128]
        qr = q_raw[:, o + 128:o + 192]
        qs = q_raw[:, o + 192:o + 256]
        ss = jnp.sum(qn * qn, axis=-1, keepdims=True) + jnp.sum(qr * qr, axis=-1, keepdims=True)
        r = lax.rsqrt(ss * (1.0 / MLA_QK) + EPS) * (MLA_QK ** -0.5)
        q_ref[0, h, :, 0:128] = (qn * r * gq[:, 0:128]).astype(BF16)
        q_ref[0, h, :, 128:192] = ((qr * r * gq[:, 128:192]) * cos + (qs * r * gqs_ref[...]) * sin).astype(BF16)
        kn = kv_raw[:, o:o + 128]
        ss = jnp.sum(kn * kn, axis=-1, keepdims=True) + kr_ss
        r = lax.rsqrt(ss * (1.0 / MLA_QK) + EPS)
        k_ref[0, h, :, 0:128] = (kn * r * gk[:, 0:128]).astype(BF16)
        k_ref[0, h, :, 128:192] = ((kr * r * gk[:, 128:192]) * cos + (krs * r * gks_ref[...]) * sin).astype(BF16)
        v_ref[0, h] = kv_raw[:, o + 128:o + 256].astype(BF16)


def _mla_prep(cols, gcq, wuq, gckv, wukv, gq, gqs, gk, gks, cos, sin, tm):
    b, n, _ = cols.shape
    full = lambda a: pl.BlockSpec(a.shape, lambda i, j: (0,) * a.ndim)
    return pl.pallas_call(
        _mla_prep_kernel,
        out_shape=(jax.ShapeDtypeStruct((b, MLA_HEADS, n, MLA_QK), BF16),
                   jax.ShapeDtypeStruct((b, MLA_HEADS, n, MLA_QK), BF16),
                   jax.ShapeDtypeStruct((b, MLA_HEADS, n, MLA_V), BF16)),
        grid=(b, n // tm),
        in_specs=[pl.BlockSpec((1, tm, 512), lambda i, j: (i, j, 0)),
                  full(gcq), full(wuq), full(gckv), full(wukv), full(gq), full(gqs), full(gk), full(gks),
                  pl.BlockSpec((tm, MLA_ROPE), lambda i, j: (j, 0)),
                  pl.BlockSpec((tm, MLA_ROPE), lambda i, j: (j, 0))],
        out_specs=(pl.BlockSpec((1, MLA_HEADS, tm, MLA_QK), lambda i, j: (i, 0, j, 0)),
                   pl.BlockSpec((1, MLA_HEADS, tm, MLA_QK), lambda i, j: (i, 0, j, 0)),
                   pl.BlockSpec((1, MLA_HEADS, tm, MLA_V), lambda i, j: (i, 0, j, 0))),
        compiler_params=_params(("parallel", "parallel")),
        name="mla_prep",
    )(cols, gcq, wuq, gckv, wukv, gq, gqs, gk, gks, cos, sin)


ATTN_TILES_PER_STEP = 4


def _attn_kernel(q_ref, kc_ref, kl_ref, vc_ref, vl_ref, o_ref):
    tq = q_ref.shape[2] // ATTN_TILES_PER_STEP
    nc = kc_ref.shape[2]
    kc, kl, vc, vl = kc_ref[0, 0], kl_ref[0, 0], vc_ref[0, 0], vl_ref[0, 0]
    nt = (((1,), (1,)), ((), ()))
    for j in range(ATTN_TILES_PER_STEP):
        q = q_ref[0, 0, j * tq:(j + 1) * tq, :]
        s = jnp.concatenate([lax.dot_general(q, kc, nt, preferred_element_type=F32),
                             lax.dot_general(q, kl, nt, preferred_element_type=F32)], axis=1)
        m = jnp.max(s, axis=-1, keepdims=True)
        p = jnp.exp(s - m)
        l = jnp.sum(p, axis=-1, keepdims=True)
        pb = p.astype(BF16)
        o = (jnp.dot(pb[:, :nc], vc, preferred_element_type=F32)
             + jnp.dot(pb[:, nc:], vl, preferred_element_type=F32))
        o_ref[0, j * tq:(j + 1) * tq, :] = (o / l).astype(BF16)


def _attention(q, k_c, k_l, v_c, v_l, tq):
    b, h, n, dk = q.shape
    nc = k_c.shape[2]
    dv = v_l.shape[3]
    tb = tq * ATTN_TILES_PER_STEP
    whole = lambda nn, dd: pl.BlockSpec((1, 1, nn, dd), lambda i, j, t: (i, j, 0, 0))
    return pl.pallas_call(
        _attn_kernel,
        out_shape=jax.ShapeDtypeStruct((b, n, h * dv), BF16),
        grid=(b, h, n // tb),
        in_specs=[pl.BlockSpec((1, 1, tb, dk), lambda i, j, t: (i, j, t, 0)),
                  whole(nc, dk), whole(n, dk), whole(nc, dv), whole(n, dv)],
        out_specs=pl.BlockSpec((1, tb, dv), lambda i, j, t: (i, t, j)),
        compiler_params=_params(("parallel", "parallel", "parallel")),
        name="attention",
    )(q, k_c, k_l, v_c, v_l)


def _ml_prep_kernel(qk_ref, prev_ref, next_ref, vin_ref, g_ref, gt_ref, cw_ref, bc_ref, br_ref,
                    q_ref, kt_ref, v_ref, a_ref, bm_ref, brow_ref):
    tn = qk_ref.shape[1]
    j = pl.program_id(1)
    nj = pl.num_programs(1)
    u = qk_ref[0]
    row = lax.broadcasted_iota(I32, (tn, 1), 0)
    before = jnp.where(j == 0, 0.0, prev_ref[0, SUBLANES - 1:SUBLANES, :])
    after = jnp.where(j == nj - 1, 0.0, next_ref[0, 0:1, :])
    up = jnp.where(row == 0, before, pltpu.roll(u, 1, 0))
    un = jnp.where(row == tn - 1, after, pltpu.roll(u, tn - 1, 0))
    cw = cw_ref[...]
    y = up * cw[0:1, :] + u * cw[1:2, :] + un * cw[2:3, :]
    y = y * jax.nn.sigmoid(y)
    hq = ML_HEADS * ML_QK
    for h in range(ML_HEADS):
        q_ref[0, h] = (y[:, h * ML_QK:(h + 1) * ML_QK] * (ML_QK ** -0.5)).astype(BF16)
    kt = y[:, hq:2 * hq].T
    L = ML_CHUNK
    for c in range(tn // L):
        kt_ref[0, c] = kt[:, c * L:(c + 1) * L].astype(BF16)
    v_ref[0] = vin_ref[0].astype(BF16)
    g = g_ref[0][:, 0:16] + bc_ref[...]
    ig_c = g[:, 0:8]
    lf_c = jax.nn.log_sigmoid(g[:, 8:16])
    gt = gt_ref[0] + br_ref[...]
    ig_r = gt[0:8, :]
    lf_r = jax.nn.log_sigmoid(gt[8:16, :])
    ti = lax.broadcasted_iota(I32, (L, L), 0)
    si = lax.broadcasted_iota(I32, (L, L), 1)
    lower = (si <= ti).astype(F32)
    upper = (si >= ti).astype(F32)
    lane_fwd = lax.broadcasted_iota(I32, (L, 8), 1) < ML_HEADS
    row_l = lax.broadcasted_iota(I32, (L, 8), 0)
    row_fwd = lax.broadcasted_iota(I32, (8, L), 0) < ML_HEADS
    for c in range(tn // L):
        lo = c * L
        lfc = lf_c[lo:lo + L, :]
        cf_c = jnp.where(
            lane_fwd,
            jnp.dot(lower, lfc, precision=HIGHEST, preferred_element_type=F32),
            jnp.dot(upper, lfc, precision=HIGHEST, preferred_element_type=F32))
        a_ref[0, lo:lo + L, :] = cf_c
        pre = suf = ig_c[lo:lo + L, :] - cf_c
        d = 1
        while d < L:
            pre = jnp.maximum(pre, jnp.where(row_l >= d, pltpu.roll(pre, d, 0), -jnp.inf))
            suf = jnp.maximum(suf, jnp.where(row_l < L - d, pltpu.roll(suf, L - d, 0), -jnp.inf))
            d *= 2
        bm_ref[0, lo:lo + L, :] = jnp.where(lane_fwd, pre, suf)
        lfr = lf_r[:, lo:lo + L]
        cf_r = jnp.where(
            row_fwd,
            jnp.dot(lfr, upper, precision=HIGHEST, preferred_element_type=F32),
            jnp.dot(lfr, lower, precision=HIGHEST, preferred_element_type=F32))
        brow_ref[0, c] = ig_r[:, lo:lo + L] - cf_r


def _ml_prep(cols, gates_t, cw, bias_col, bias_row, tn):
    b, n, _ = cols.shape
    r8 = tn // SUBLANES
    last8 = n // SUBLANES - 1
    L = ML_CHUNK
    hq = ML_HEADS * ML_QK
    return pl.pallas_call(
        _ml_prep_kernel,
        out_shape=(jax.ShapeDtypeStruct((b, ML_HEADS, n, ML_QK), BF16),
                   jax.ShapeDtypeStruct((b, n // L, hq, L), BF16),
                   jax.ShapeDtypeStruct((b, n, ML_HEADS * ML_V), BF16),
                   jax.ShapeDtypeStruct((b, n, 8), F32),
                   jax.ShapeDtypeStruct((b, n, 8), F32),
                   jax.ShapeDtypeStruct((b, n // L, 8, L), F32)),
        grid=(b, n // tn),
        in_specs=[pl.BlockSpec((1, tn, 512), lambda i, j: (i, j, C_QK0 // 512)),
                  pl.BlockSpec((1, SUBLANES, 512), lambda i, j: (i, jnp.maximum(j * r8 - 1, 0), C_QK0 // 512)),
                  pl.BlockSpec((1, SUBLANES, 512), lambda i, j: (i, jnp.minimum((j + 1) * r8, last8), C_QK0 // 512)),
                  pl.BlockSpec((1, tn, 512), lambda i, j: (i, j, C_V0 // 512)),
                  pl.BlockSpec((1, tn, 128), lambda i, j: (i, j, C_G0 // 128)),
                  pl.BlockSpec((1, 16, tn), lambda i, j: (i, 0, j)),
                  pl.BlockSpec((3, 512), lambda i, j: (0, 0)),
                  pl.BlockSpec((1, 16), lambda i, j: (0, 0)),
                  pl.BlockSpec((16, 1), lambda i, j: (0, 0))],
        out_specs=(pl.BlockSpec((1, ML_HEADS, tn, ML_QK), lambda i, j: (i, 0, j, 0)),
                   pl.BlockSpec((1, tn // L, hq, L), lambda i, j: (i, j, 0, 0)),
                   pl.BlockSpec((1, tn, ML_HEADS * ML_V), lambda i, j: (i, j, 0)),
                   pl.BlockSpec((1, tn, 8), lambda i, j: (i, j, 0)),
                   pl.BlockSpec((1, tn, 8), lambda i, j: (i, j, 0)),
                   pl.BlockSpec((1, tn // L, 8, L), lambda i, j: (i, j, 0, 0))),
        compiler_params=_params(("parallel", "parallel")),
        name="ml_prep",
    )(cols, cols, cols, cols, cols, gates_t, cw, bias_col, bias_row)


def _ml_chunk(qb, kt, vaug, a_c, bm_c, b_r, ct, m_prev, fwd, need_out):
    L = qb.shape[0]
    ti = lax.broadcasted_iota(I32, (L, L), 0)
    si = lax.broadcasted_iota(I32, (L, L), 1)
    mask = (si <= ti) if fwd else (si >= ti)
    a_b = jnp.broadcast_to(a_c, (L, LANES))
    inter = a_b + m_prev
    m_t = jnp.maximum(inter, a_b + jnp.broadcast_to(bm_c, (L, LANES)))
    last = L - 1 if fwd else 0
    m_new = m_t[last:last + 1, :]
    a_last = a_b[last:last + 1, :]
    w_r = jnp.exp(a_last[:, :L] + b_r - m_new[:, :L])
    decay = jnp.exp(a_last + m_prev - m_new)
    decay = jnp.concatenate([decay, decay], axis=1)
    kw = (kt.astype(F32) * w_r).astype(BF16)
    if not need_out:
        return decay * ct + jnp.dot(kw, vaug, preferred_element_type=F32), m_new, None
    s = jnp.dot(qb, kt, preferred_element_type=F32)
    dmat = jnp.where(mask, a_b[:, :L] + b_r, -jnp.inf)
    wmat = jnp.exp(dmat - m_t[:, :L]) * s
    sc = jnp.exp(inter - m_t)
    top = jnp.concatenate([wmat.astype(BF16), (sc[:, :qb.shape[1]] * qb.astype(F32)).astype(BF16)], axis=1)
    bot = jnp.concatenate([kw, jnp.zeros(kw.shape, BF16)], axis=1)
    res = jnp.dot(jnp.concatenate([top, bot], axis=0), jnp.concatenate([vaug, ct.astype(BF16)], axis=0),
                  preferred_element_type=F32)
    num = res[:L, 0:ML_V]
    den = res[:L, ML_V:2 * ML_V]
    return decay * ct + res[L:], m_new, num / jnp.maximum(jnp.abs(den), jnp.exp(-m_t))


def _ml_scan_kernel(ql_ref, ktl_ref, vl_ref, al_ref, bml_ref, brl_ref,
                    qc_ref, ktc_ref, vc_ref, ac_ref, bmc_ref, brc_ref, out_ref, st_ref, m_ref):
    L = ML_CHUNK
    ncl = ql_ref.shape[2] // L
    ncc = qc_ref.shape[2] // L
    ones_blk = jnp.ones((L, ML_V), BF16)
    st_ref[...] = jnp.zeros(st_ref.shape, F32)
    m_ref[...] = jnp.zeros(m_ref.shape, F32)

    def step(refs, c, d, fwd, need_out):
        q_ref, kt_ref, v_ref, a_ref, bm_ref, br_ref = refs
        sl = pl.ds(pl.multiple_of(c * L, L), L)
        a_all = a_ref[0, sl, :]
        bm_all = bm_ref[0, sl, :]
        br_all = br_ref[0, c]
        v_all = v_ref[0, sl, :]
        kt_all = kt_ref[0, c]
        hs = []
        for h in range(ML_HEADS):
            j = d * ML_HEADS + h
            vaug = jnp.concatenate([v_all[:, h * ML_V:(h + 1) * ML_V], ones_blk], axis=1)
            ct_new, m_new, hh = _ml_chunk(
                q_ref[0, h, sl, :], kt_all[h * ML_QK:(h + 1) * ML_QK, :], vaug,
                a_all[:, j:j + 1], bm_all[:, j:j + 1], br_all[j:j + 1, :], st_ref[j], m_ref[j][0:1, :],
                fwd, need_out)
            st_ref[j] = ct_new
            m_ref[j] = jnp.broadcast_to(m_new, (SUBLANES, LANES))
            hs.append(hh)
        return hs, sl

    ctx_refs = (qc_ref, ktc_ref, vc_ref, ac_ref, bmc_ref, brc_ref)
    lat_refs = (ql_ref, ktl_ref, vl_ref, al_ref, bml_ref, brl_ref)

    out_ref[...] = jnp.zeros(out_ref.shape, F32)

    def ctx_body(i, carry):
        step(ctx_refs, i, 0, True, False)
        step(ctx_refs, ncc - 1 - i, 1, False, False)
        return carry

    def lat_body(i, carry):
        for d, fwd in ((0, True), (1, False)):
            hs, sl = step(lat_refs, i if fwd else ncl - 1 - i, d, fwd, True)
            out_ref[0, sl, :] = out_ref[0, sl, :] + jnp.concatenate(hs, axis=1)
        return carry

    lax.fori_loop(0, ncc, ctx_body, 0)
    lax.fori_loop(0, ncl, lat_body, 0)


def _ml_scan(ql, ktl, vl, al, bml, brl, qc, ktc, vc, ac, bmc, brc):
    b, _, n, _ = ql.shape
    nctx = qc.shape[2]
    L = ML_CHUNK
    hv = ML_HEADS * ML_V
    hq = ML_HEADS * ML_QK
    qspec = lambda nn: pl.BlockSpec((1, ML_HEADS, nn, ML_QK), lambda i: (i, 0, 0, 0))
    ktspec = lambda nn: pl.BlockSpec((1, nn // L, hq, L), lambda i: (i, 0, 0, 0))
    vspec = lambda nn: pl.BlockSpec((1, nn, hv), lambda i: (i, 0, 0))
    aspec = lambda nn: pl.BlockSpec((1, nn, 8), lambda i: (i, 0, 0))
    rspec = lambda nn: pl.BlockSpec((1, nn // L, 8, L), lambda i: (i, 0, 0, 0))
    return pl.pallas_call(
        _ml_scan_kernel,
        out_shape=jax.ShapeDtypeStruct((b, n, hv), F32),
        grid=(b,),
        in_specs=[qspec(n), ktspec(n), vspec(n), aspec(n), aspec(n), rspec(n),
                  qspec(nctx), ktspec(nctx), vspec(nctx), aspec(nctx), aspec(nctx), rspec(nctx)],
        out_specs=pl.BlockSpec((1, n, hv), lambda i: (i, 0, 0)),
        scratch_shapes=[pltpu.VMEM((2 * ML_HEADS, ML_QK, 2 * ML_V), F32),
                        pltpu.VMEM((2 * ML_HEADS, SUBLANES, LANES), F32)],
        compiler_params=_params(("parallel",)),
        name="ml_scan",
    )(ql, ktl, vl, al, bml, brl, qc, ktc, vc, ac, bmc, brc)


def _outproj_kernel(x_ref, mla_ref, hs_ref, o_ref, gm_ref, wa_ref, wb_ref, gt_ref, g2_ref, sc_ref, sh_ref,
                    wpq_ref, sk_ref, x1_ref, h2_ref, s_ref):
    hs = hs_ref[0]
    gm = gm_ref[...]
    hn = jnp.concatenate([_rms(hs[:, h * ML_V:(h + 1) * ML_V]) * gm[:, h * ML_V:(h + 1) * ML_V]
                          for h in range(ML_HEADS)], axis=1)
    ml = (jax.nn.sigmoid(o_ref[0]) * hn).astype(BF16)
    mix = (jnp.dot(mla_ref[0], wa_ref[...], preferred_element_type=F32)
           + jnp.dot(ml, wb_ref[...], preferred_element_type=F32))
    x1 = x_ref[0] + gt_ref[0] * mix
    x1_ref[0] = x1
    h2 = _rms(x1) * g2_ref[...] * (1.0 + sc_ref[0]) + sh_ref[0]
    h2_ref[0] = h2
    qp = jnp.dot(h2.astype(BF16), wpq_ref[...], preferred_element_type=F32).astype(BF16)
    for hp in range(2 * PEER_HEADS):
        s_ref[0, hp] = lax.dot_general(sk_ref[hp], qp[:, hp * N_KEYS:(hp + 1) * N_KEYS],
                                       (((1,), (1,)), ((), ())), preferred_element_type=F32)


def _outproj(x, mla, hsum, cols, gm, wa, wb, gt1, g2, sc2, sh2, wpq, sk, tm):
    b, n, d = x.shape
    hw = mla.shape[2]
    nq = wpq.shape[1]
    mod = pl.BlockSpec((1, 1, d), lambda i, j: (i, 0, 0))
    return pl.pallas_call(
        _outproj_kernel,
        out_shape=(jax.ShapeDtypeStruct((b, n, d), F32), jax.ShapeDtypeStruct((b, n, d), F32),
                   jax.ShapeDtypeStruct((b, 2 * PEER_HEADS, N_KEYS, n), F32)),
        grid=(b, n // tm),
        in_specs=[pl.BlockSpec((1, tm, d), lambda i, j: (i, j, 0)),
                  pl.BlockSpec((1, tm, hw), lambda i, j: (i, j, 0)),
                  pl.BlockSpec((1, tm, hw), lambda i, j: (i, j, 0)),
                  pl.BlockSpec((1, tm, 512), lambda i, j: (i, j, C_O0 // 512)),
                  pl.BlockSpec((1, hw), lambda i, j: (0, 0)),
                  pl.BlockSpec((hw, d), lambda i, j: (0, 0)),
                  pl.BlockSpec((hw, d), lambda i, j: (0, 0)),
                  mod, pl.BlockSpec((1, d), lambda i, j: (0, 0)), mod, mod,
                  pl.BlockSpec((d, nq), lambda i, j: (0, 0)),
                  pl.BlockSpec((2 * PEER_HEADS, N_KEYS, N_KEYS), lambda i, j: (0, 0, 0))],
        out_specs=(pl.BlockSpec((1, tm, d), lambda i, j: (i, j, 0)),
                   pl.BlockSpec((1, tm, d), lambda i, j: (i, j, 0)),
                   pl.BlockSpec((1, 2 * PEER_HEADS, N_KEYS, tm), lambda i, j: (i, 0, 0, j))),
        compiler_params=_params(("parallel", "parallel")),
        name="outproj",
    )(x, mla, hsum, cols, gm, wa, wb, gt1, g2, sc2, sh2, wpq, sk)


def _topk_rows(work_refs, rid_ref, rows, k):
    t = rid_ref.shape[1]
    kid = lax.broadcasted_iota(I32, (k, t), 0)
    big = jnp.float32(3.0e38)

    def body(r, carry):
        out = []
        for work_ref, (vals, idxs) in zip(work_refs, carry):
            s = work_ref[0:rows, :]
            rid = rid_ref[0:rows, :]
            m = jnp.max(s, axis=0, keepdims=True)
            i = jnp.min(jnp.where(s == m, rid, big), axis=0, keepdims=True)
            work_ref[0:rows, :] = jnp.where(rid == i, -jnp.inf, s)
            out.append((jnp.where(kid == r, m, vals), jnp.where(kid == r, i, idxs)))
        return tuple(out)

    init = tuple((jnp.zeros((k, t), F32), jnp.zeros((k, t), F32)) for _ in work_refs)
    return lax.fori_loop(0, k, body, init)


def _pair_candidates(sv0, sv1):
    K, t = sv0.shape
    h = K // 2
    iid = lax.broadcasted_iota(I32, (h, t), 0)
    vals = [sv0[0:h] + sv1[0:1], sv0[h:K] + sv1[0:1]]
    ids = [iid * K, (iid + h) * K]
    for j in range(1, h):
        vals.append(sv0[0:h] + sv1[j:j + 1])
        ids.append(iid * K + j)
    vals.append(sv0[0:1] + sv1[h:K])
    ids.append(iid + h)
    return jnp.concatenate(vals, axis=0), jnp.concatenate(ids, axis=0).astype(F32)


HEADS_PER_ROUND = 4


def _route_kernel(s_ref, off_ref, par_ref, gw_ref, et_ref, gt_ref, work_ref, rid_ref, cid_ref):
    K = PEER_TOPK
    nk, tt = rid_ref.shape
    rid_ref[...] = lax.broadcasted_iota(I32, (nk, tt), 0).astype(F32)
    hr = HEADS_PER_ROUND

    def round_(g, carry):
        heads = [g * hr + a for a in range(hr)]
        for a, h in enumerate(heads):
            work_ref[2 * a] = s_ref[0, 2 * h]
            work_ref[2 * a + 1] = s_ref[0, 2 * h + 1]
        first = _topk_rows([work_ref.at[b] for b in range(2 * hr)], rid_ref, nk, K)
        ncand = N_PAIR_CANDIDATES
        for a in range(hr):
            cand, cid = _pair_candidates(first[2 * a][0], first[2 * a + 1][0])
            work_ref[a, 0:ncand, :] = cand
            cid_ref[...] = cid
        second = _topk_rows([work_ref.at[a] for a in range(hr)], cid_ref, ncand, K)
        for a, h in enumerate(heads):
            best, pos = second[a]
            pos = pos.astype(I32)
            si0 = first[2 * a][1].astype(I32)
            si1 = first[2 * a + 1][1].astype(I32)
            isel = pos >> 4
            jsel = pos & (K - 1)
            e0 = jnp.zeros_like(pos)
            e1 = jnp.zeros_like(pos)
            for i in range(K):
                e0 = jnp.where(isel == i, si0[i:i + 1, :], e0)
                e1 = jnp.where(jsel == i, si1[i:i + 1, :], e1)
            ex = jnp.exp(best - best[0:1, :])
            sl = pl.ds(pl.multiple_of(h * K, K), K)
            et_ref[sl, :] = e0 * N_KEYS + e1
            gt_ref[sl, :] = ex / jnp.sum(ex, axis=0, keepdims=True)
        return carry

    lax.fori_loop(0, PEER_HEADS // hr, round_, 0)
    e = et_ref[...].T
    off_ref[0] = (e >> 1) * SUBLANES
    par_ref[0] = (e & 1).astype(F32)
    gw_ref[0] = gt_ref[...].T


def _route(scores, tt):
    b, hp, nk, n = scores.shape
    nsel = PEER_HEADS * PEER_TOPK
    ospec = pl.BlockSpec((1, tt, nsel), lambda i, j: (i, j, 0))
    return pl.pallas_call(
        _route_kernel,
        out_shape=(jax.ShapeDtypeStruct((b, n, nsel), I32), jax.ShapeDtypeStruct((b, n, nsel), F32),
                   jax.ShapeDtypeStruct((b, n, nsel), F32)),
        grid=(b, n // tt),
        in_specs=[pl.BlockSpec((1, hp, nk, tt), lambda i, j: (i, 0, 0, j))],
        out_specs=(ospec, ospec, ospec),
        scratch_shapes=[pltpu.VMEM((nsel, tt), I32), pltpu.VMEM((nsel, tt), F32),
                        pltpu.VMEM((2 * HEADS_PER_ROUND, nk, tt), F32), pltpu.VMEM((nk, tt), F32),
                        pltpu.VMEM((N_PAIR_CANDIDATES, tt), F32)],
        compiler_params=_params(("parallel", "parallel")),
        name="route",
    )(scores)


TILE_ROWS = 2 * SUBLANES
TOKENS_PER_ITER = 16


CHUNK_SEL = 16


def _chunk_tiles(tab_ref, row_ref, c):
    return jnp.concatenate(
        [pltpu.bitcast(tab_ref[pl.ds(pl.multiple_of(row_ref[c * CHUNK_SEL + kk], SUBLANES), SUBLANES), :], BF16)
         for kk in range(CHUNK_SEL)], axis=0)


def _init_spread(spread_ref, gather_ref=None):
    @pl.when(pl.program_id(0) == 0)
    def _():
        nsel, width = spread_ref.shape
        spread_ref[...] = (lax.broadcasted_iota(I32, (nsel, width), 1) // TILE_ROWS
                           == lax.broadcasted_iota(I32, (nsel, width), 0)).astype(BF16)
        if gather_ref is not None:
            gather_ref[...] = (lax.broadcasted_iota(I32, (width, nsel), 0) // TILE_ROWS
                               == lax.broadcasted_iota(I32, (width, nsel), 1)).astype(BF16)


def _row_masks(width):
    half = (lax.broadcasted_iota(I32, (1, width), 1) % 2).astype(F32)
    lane8 = lax.broadcasted_iota(I32, (SUBLANES, width), 1)
    diag = ((lane8 % TILE_ROWS) // 2 == lax.broadcasted_iota(I32, (SUBLANES, width), 0)).astype(F32)
    return half, diag


def _row_to_tile(row):
    return jnp.concatenate([row[:, s * LANES:(s + 1) * LANES] for s in range(SUBLANES)], axis=0)


def _split_bf16(v):
    hi = v.astype(BF16)
    return hi, (v - hi.astype(F32)).astype(BF16)


def _peer_u_kernel(off_ref, h_ref, tab_ref, par_ref, g_ref, c_ref, r_ref, spread_ref, gather_ref):
    tb, nsel = g_ref.shape
    _init_spread(spread_ref, gather_ref)
    half, diag = _row_masks(spread_ref.shape[1])

    cw = CHUNK_SEL * TILE_ROWS
    diag_c = diag[:, :cw]
    nt = (((1,), (1,)), ((), ()))

    def tok(i, carry):
        ts = [i * TOKENS_PER_ITER + u for u in range(TOKENS_PER_ITER)]
        rows = [off_ref.at[t] for t in ts]
        xs = [jnp.concatenate(_split_bf16(_row_to_tile(h_ref[pl.ds(t, 1), :])), axis=0) for t in ts]
        for c in range(nsel // CHUNK_SEL):
            for u, t in enumerate(ts):
                res = lax.dot_general(xs[u], _chunk_tiles(tab_ref, rows[u], c), nt, preferred_element_type=F32)
                r_ref[pl.ds(t, 1), c * cw:(c + 1) * cw] = jnp.sum(
                    (res[:SUBLANES] + res[SUBLANES:]) * diag_c, axis=0, keepdims=True)
        return carry

    lax.fori_loop(0, tb // TOKENS_PER_ITER, tok, 0)
    mine = jnp.dot(par_ref[...].astype(BF16), spread_ref[...], preferred_element_type=F32) == half
    r_hi, r_lo = _split_bf16(jnp.where(mine, r_ref[...], 0.0))
    both = jnp.dot(jnp.concatenate([r_hi, r_lo], axis=0), gather_ref[...], preferred_element_type=F32)
    pre = both[:tb] + both[tb:]
    c_ref[...] = g_ref[...] * (0.5 * pre * (1.0 + lax.erf(pre * (2.0 ** -0.5))))


def _smem_block(tb, nsel):
    return pl.BlockSpec((tb, nsel), lambda i: (i, 0), memory_space=pltpu.SMEM)


def _peer_u(off, h2, tab, par, gw, tb):
    t, nsel = gw.shape
    vspec = pl.BlockSpec((tb, nsel), lambda i: (i, 0))
    return pl.pallas_call(
        _peer_u_kernel,
        out_shape=jax.ShapeDtypeStruct((t, nsel), F32),
        grid=(t // tb,),
        in_specs=[_smem_block(tb, nsel),
                  pl.BlockSpec((tb, h2.shape[1]), lambda i: (i, 0)),
                  pl.BlockSpec(tab.shape, lambda i: (0, 0), pipeline_mode=pl.Buffered(1)),
                  vspec, vspec],
        out_specs=vspec,
        scratch_shapes=[pltpu.VMEM((tb, nsel * TILE_ROWS), F32),
                        pltpu.VMEM((nsel, nsel * TILE_ROWS), BF16), pltpu.VMEM((nsel * TILE_ROWS, nsel), BF16)],
        compiler_params=_params(("arbitrary",)),
        name="peer_u",
    )(off, h2, tab, par, gw)


def _peer_v_kernel(off_ref, c_ref, par_ref, tab_ref, x1_ref, gt_ref, o_ref, lhi_ref, llo_ref, spread_ref):
    tb, nsel = c_ref.shape
    _init_spread(spread_ref)
    half, diag = _row_masks(spread_ref.shape[1])
    c_hi, c_lo = _split_bf16(c_ref[...])
    ex = jnp.dot(jnp.concatenate([par_ref[...].astype(BF16), c_hi, c_lo], axis=0), spread_ref[...],
                 preferred_element_type=F32)
    mine = ex[:tb] == half
    lhi_ref[...] = jnp.where(mine, ex[tb:2 * tb], 0.0)
    llo_ref[...] = jnp.where(mine, ex[2 * tb:], 0.0)
    gt = gt_ref[0]

    cw = CHUNK_SEL * TILE_ROWS
    diag_c = diag[:, :cw]

    def tok(i, carry):
        ts = [i * TOKENS_PER_ITER + u for u in range(TOKENS_PER_ITER)]
        rows = [off_ref.at[t] for t in ts]
        accs = [jnp.zeros((TILE_ROWS, LANES), F32) for _ in ts]
        for c in range(nsel // CHUNK_SEL):
            cols = slice(c * cw, (c + 1) * cw)
            for u, t in enumerate(ts):
                left = jnp.concatenate([lhi_ref[pl.ds(t, 1), cols] * diag_c, llo_ref[pl.ds(t, 1), cols] * diag_c],
                                       axis=0).astype(BF16)
                accs[u] = accs[u] + jnp.dot(left, _chunk_tiles(tab_ref, rows[u], c), preferred_element_type=F32)
        for u, t in enumerate(ts):
            y = accs[u][:SUBLANES] + accs[u][SUBLANES:]
            y_row = jnp.concatenate([y[s:s + 1, :] for s in range(SUBLANES)], axis=1)
            o_ref[pl.ds(t, 1), :] = x1_ref[pl.ds(t, 1), :] + gt * y_row
        return carry

    lax.fori_loop(0, tb // TOKENS_PER_ITER, tok, 0)


def _peer_v(off, coef, par, tab, x1, gt2, tb, tok_per_batch):
    t, nsel = coef.shape
    d = x1.shape[1]
    bpb = tok_per_batch // tb
    vspec = pl.BlockSpec((tb, nsel), lambda i: (i, 0))
    return pl.pallas_call(
        _peer_v_kernel,
        out_shape=jax.ShapeDtypeStruct(x1.shape, F32),
        grid=(t // tb,),
        in_specs=[_smem_block(tb, nsel), vspec, vspec,
                  pl.BlockSpec(tab.shape, lambda i: (0, 0), pipeline_mode=pl.Buffered(1)),
                  pl.BlockSpec((tb, d), lambda i: (i, 0)),
                  pl.BlockSpec((1, 1, d), lambda i: (i // bpb, 0, 0))],
        out_specs=pl.BlockSpec((tb, d), lambda i: (i, 0)),
        scratch_shapes=[pltpu.VMEM((tb, nsel * TILE_ROWS), F32), pltpu.VMEM((tb, nsel * TILE_ROWS), F32),
                        pltpu.VMEM((nsel, nsel * TILE_ROWS), BF16)],
        compiler_params=_params(("arbitrary",)),
        name="peer_v",
    )(off, coef, par, tab, x1, gt2)


def _rope_tables(n):
    axis = MLA_ROPE // 2
    t = jnp.arange(n, dtype=F32)
    row = jnp.floor(t / GRID_W)
    col = t - row * GRID_W
    inv = ROPE_BASE ** (-jnp.arange(axis // 2, dtype=F32) * (2.0 / axis))
    ar = row[:, None] * inv
    ac = col[:, None] * inv
    cos = jnp.concatenate([jnp.cos(ar), jnp.cos(ar), jnp.cos(ac), jnp.cos(ac)], axis=1)
    sin = jnp.concatenate([-jnp.sin(ar), jnp.sin(ar), -jnp.sin(ac), jnp.sin(ac)], axis=1)
    return cos, sin


PACK_ROWS = 256


def _pack_kernel(t_ref, o_ref):
    n = t_ref.shape[0]
    rows = n // 2
    xb = t_ref[...].astype(BF16)
    col = lax.broadcasted_iota(I32, (rows, n), 1)
    row2 = 2 * lax.broadcasted_iota(I32, (rows, n), 0)
    even = pltpu.bitcast(jnp.dot((col == row2).astype(BF16), xb, preferred_element_type=F32), jnp.uint32)
    odd = pltpu.bitcast(jnp.dot((col == row2 + 1).astype(BF16), xb, preferred_element_type=F32), jnp.uint32)
    word = odd | (even >> 16)
    for s in range(SUBLANES):
        o_ref[pl.ds(s, rows, stride=SUBLANES), :] = word[:, s * LANES:(s + 1) * LANES]


def _pack_table(tab):
    e, d = tab.shape
    out_rows = PACK_ROWS // 2 * (d // LANES)
    return pl.pallas_call(
        _pack_kernel,
        out_shape=jax.ShapeDtypeStruct((e // 2 * (d // LANES), LANES), jnp.uint32),
        grid=(e // PACK_ROWS,),
        in_specs=[pl.BlockSpec((PACK_ROWS, d), lambda i: (i, 0))],
        out_specs=pl.BlockSpec((out_rows, LANES), lambda i: (i, 0)),
        compiler_params=_params(("parallel",)),
        name="pack_table",
    )(tab)


def _block(n, want):
    return want if n % want == 0 else n


def kernel(x, c, ctx, c_ctx, w_ada, b_ada, g_norm1, w_in, g_cq, w_uq, g_ckv, w_ukv, g_qn, g_kn, conv_qk, b_igate, b_fgate, g_mlstm, w_out, g_norm2, w_pq, sub_keys, expert_u, expert_v):
    B, N, D = x.shape
    NC = ctx.shape[1]
    assert w_ada.shape[0] == 1, "one layer"
    assert N % (ATTN_TQ * ATTN_TILES_PER_STEP) == 0 and NC % ML_CHUNK == 0 and D == SUBLANES * LANES
    q_rank = g_cq.shape[1]
    kv_rank = g_ckv.shape[1]
    mla_cols = q_rank + kv_rank + MLA_ROPE
    assert (q_rank, kv_rank) == (C_KV0 - C_Q0, C_KR0 - C_KV0)
    swap = jnp.arange(MLA_ROPE) ^ (MLA_ROPE // 4)

    cc = jnp.concatenate([c, c_ctx[None, :], jnp.zeros((16 - B - 1, D), F32)], axis=0)
    mod = _ada(cc, w_ada[0].astype(BF16), b_ada)
    sh1, sc1, gt1, sh2, sc2, gt2 = [mod[:, i * D:(i + 1) * D] for i in range(6)]
    lat = lambda m: m[:B].reshape(B, 1, D)
    ctxm = lambda m: jnp.broadcast_to(m[B:B + 1].reshape(1, 1, D), (B, 1, D))

    wi = w_in[0]
    n_qk = 2 * ML_HEADS * ML_QK
    n_v = ML_HEADS * ML_V
    n_g = 4 * ML_HEADS
    m0 = mla_cols
    w_cols = jnp.concatenate([
        wi[:, 0:mla_cols],
        wi[:, q_rank + kv_rank + swap],
        wi[:, m0:m0 + n_qk + 2 * n_v + n_g],
        jnp.zeros((D, C_END - C_G0 - n_g), F32)], axis=1).astype(BF16)
    assert w_cols.shape[1] == C_END
    w_gates_t = wi[:, m0 + n_qk + 2 * n_v:m0 + n_qk + 2 * n_v + n_g].T.astype(BF16)
    cols_l, gates_l = _inproj(x, g_norm1, lat(sc1), lat(sh1), w_cols, w_gates_t, _block(N, ROW_TILE))
    cols_c, gates_c = _inproj(ctx, g_norm1, ctxm(sc1), ctxm(sh1), w_cols, w_gates_t, _block(NC, ROW_TILE))

    wq = w_uq[0].reshape(q_rank, MLA_HEADS, MLA_QK)
    wuq = jnp.concatenate([wq, wq[:, :, MLA_NOPE + swap]], axis=2).reshape(q_rank, MLA_HEADS * 256).astype(BF16)
    wukv = w_ukv[0].astype(BF16)
    gqs, gks = g_qn[:, MLA_NOPE + swap], g_kn[:, MLA_NOPE + swap]
    cos_l, sin_l = _rope_tables(N)
    cos_c, sin_c = jnp.ones((NC, MLA_ROPE), F32), jnp.zeros((NC, MLA_ROPE), F32)
    q_l, k_l, v_l = _mla_prep(cols_l, g_cq, wuq, g_ckv, wukv, g_qn, gqs, g_kn, gks, cos_l, sin_l, _block(N, ROW_TILE))
    _, k_c, v_c = _mla_prep(cols_c, g_cq, wuq, g_ckv, wukv, g_qn, gqs, g_kn, gks, cos_c, sin_c, _block(NC, ROW_TILE))
    mla = _attention(q_l, k_c, k_l, v_c, v_l, ATTN_TQ)

    bias16 = jnp.concatenate([b_igate[0].reshape(-1), b_fgate[0].reshape(-1)])
    cw = conv_qk[0]
    pl_ = _ml_prep(cols_l, gates_l, cw, bias16[None, :], bias16[:, None], _block(N, ROW_TILE))
    pc_ = _ml_prep(cols_c, gates_c, cw, bias16[None, :], bias16[:, None], _block(NC, ROW_TILE))
    hsum = _ml_scan(*pl_, *pc_)

    wo = w_out[0].astype(BF16)
    hw = MLA_HEADS * MLA_V
    sk = sub_keys[0].reshape(2 * PEER_HEADS, N_KEYS, -1).astype(BF16)
    x1, h2, scores = _outproj(x, mla, hsum, cols_l, g_mlstm, wo[:hw], wo[hw:], lat(gt1), g_norm2, lat(sc2), lat(sh2),
                              w_pq[0].astype(BF16), sk, OUT_TILE)
    T = B * N
    nsel = PEER_HEADS * PEER_TOPK
    off, par, gw = [a.reshape(T, nsel) for a in _route(scores, ROUTE_TILE)]

    tb = PEER_TOKENS
    coef = _peer_u(off, h2.reshape(T, D), _pack_table(expert_u[0]), par, gw, tb)
    out = _peer_v(off, coef, par, _pack_table(expert_v[0]), x1.reshape(T, D), lat(gt2), tb, N)
    return out.reshape(B, N, D)
```

```python
import jax
import jax.numpy as jnp
from jax import lax
from jax.experimental import pallas as pl
from jax.experimental.pallas import tpu as pltpu

F32 = jnp.float32
BF16 = jnp.bfloat16
I32 = jnp.int32
EPS = 1e-6

GRID_W = 64
MLA_HEADS = 4
MLA_NOPE = 128
MLA_ROPE = 64
MLA_V = 128
MLA_QK = MLA_NOPE + MLA_ROPE
ROPE_BASE = 10000.0
ML_HEADS = 4
ML_QK = 64
ML_V = 128
ML_CHUNK = 64
PEER_HEADS = 8
N_KEYS = 128
PEER_TOPK = 16
N_PAIR_CANDIDATES = 80

LANES = 128
SUBLANES = 8
VMEM_LIMIT_BYTES = 56 * 1024 * 1024

ROW_TILE = 512
ATTN_TQ = 256
OUT_TILE = 256
ROUTE_TILE = 256
PEER_TOKENS = 128

C_Q0, C_KV0, C_KR0, C_KRS0, C_QK0, C_V0, C_O0, C_G0, C_END = 0, 256, 384, 448, 512, 1024, 1536, 2048, 2176
HIGHEST = lax.Precision.HIGHEST


def _params(sem):
    return pltpu.CompilerParams(dimension_semantics=sem, vmem_limit_bytes=VMEM_LIMIT_BYTES)


def _rms(x):
    return x * lax.rsqrt(jnp.mean(x * x, axis=-1, keepdims=True) + EPS)


def _ada_kernel(c_ref, w_ref, b_ref, o_ref):
    c = c_ref[...]
    s = c * jax.nn.sigmoid(c)
    o_ref[...] = jnp.dot(s.astype(BF16), w_ref[...], preferred_element_type=F32) + b_ref[...]


def _ada(cc, w, b):
    rows, d = cc.shape
    n = w.shape[1]
    bn = n // 4
    return pl.pallas_call(
        _ada_kernel,
        out_shape=jax.ShapeDtypeStruct((rows, n), F32),
        grid=(4,),
        in_specs=[pl.BlockSpec((rows, d), lambda j: (0, 0)),
                  pl.BlockSpec((d, bn), lambda j: (0, j)),
                  pl.BlockSpec((1, bn), lambda j: (0, j))],
        out_specs=pl.BlockSpec((rows, bn), lambda j: (0, j)),
        compiler_params=_params(("arbitrary",)),
        name="ada",
    )(cc, w, b)


def _inproj_kernel(x_ref, g_ref, sc_ref, sh_ref, w_ref, wg_ref, o_ref, gt_ref):
    x = x_ref[0]
    h = _rms(x) * g_ref[...] * (1.0 + sc_ref[0]) + sh_ref[0]
    hb = h.astype(BF16)
    o_ref[0] = jnp.dot(hb, w_ref[...], preferred_element_type=F32)
    gt_ref[0] = lax.dot_general(wg_ref[...], hb, (((1,), (1,)), ((), ())), preferred_element_type=F32)


def _inproj(x, g, sc, sh, w, wg, tm):
    b, n, d = x.shape
    nc = w.shape[1]
    return pl.pallas_call(
        _inproj_kernel,
        out_shape=(jax.ShapeDtypeStruct((b, n, nc), F32), jax.ShapeDtypeStruct((b, 16, n), F32)),
        grid=(b, n // tm),
        in_specs=[pl.BlockSpec((1, tm, d), lambda i, j: (i, j, 0)),
                  pl.BlockSpec((1, d), lambda i, j: (0, 0)),
                  pl.BlockSpec((1, 1, d), lambda i, j: (i, 0, 0)),
                  pl.BlockSpec((1, 1, d), lambda i, j: (i, 0, 0)),
                  pl.BlockSpec((d, nc), lambda i, j: (0, 0)),
                  pl.BlockSpec((16, d), lambda i, j: (0, 0))],
        out_specs=(pl.BlockSpec((1, tm, nc), lambda i, j: (i, j, 0)),
                   pl.BlockSpec((1, 16, tm), lambda i, j: (i, 0, j))),
        compiler_params=_params(("parallel", "parallel")),
        name="inproj",
    )(x, g, sc, sh, w, wg)


def _mla_prep_kernel(c_ref, gcq_ref, wuq_ref, gckv_ref, wukv_ref, gq_ref, gqs_ref, gk_ref, gks_ref,
                     cos_ref, sin_ref, q_ref, k_ref, v_ref):
    c = c_ref[0]
    cq = _rms(c[:, C_Q0:C_KV0]) * gcq_ref[...]
    ckv = _rms(c[:, C_KV0:C_KR0]) * gckv_ref[...]
    kr = c[:, C_KR0:C_KRS0]
    krs = c[:, C_KRS0:C_QK0]
    q_raw = jnp.dot(cq.astype(BF16), wuq_ref[...], preferred_element_type=F32)
    kv_raw = jnp.dot(ckv.astype(BF16), wukv_ref[...], preferred_element_type=F32)
    cos = cos_ref[...]
    sin = sin_ref[...]
    gq = gq_ref[...]
    gk = gk_ref[...]
    kr_ss = jnp.sum(kr * kr, axis=-1, keepdims=True)
    for h in range(MLA_HEADS):
        o = h * 256
        qn = q_raw[:, o:o + 128]
        qr = q_raw[:, o + 128:o + 192]
        qs = q_raw[:, o + 192:o + 256]
        ss = jnp.sum(qn * qn, axis=-1, keepdims=True) + jnp.sum(qr * qr, axis=-1, keepdims=True)
        r = lax.rsqrt(ss * (1.0 / MLA_QK) + EPS) * (MLA_QK ** -0.5)
        q_ref[0, h, :, 0:128] = (qn * r * gq[:, 0:128]).astype(BF16)
        q_ref[0, h, :, 128:192] = ((qr * r * gq[:, 128:192]) * cos + (qs * r * gqs_ref[...]) * sin).astype(BF16)
        kn = kv_raw[:, o:o + 128]
        ss = jnp.sum(kn * kn, axis=-1, keepdims=True) + kr_ss
        r = lax.rsqrt(ss * (1.0 / MLA_QK) + EPS)
        k_ref[0, h, :, 0:128] = (kn * r * gk[:, 0:128]).astype(BF16)
        k_ref[0, h, :, 128:192] = ((kr * r * gk[:, 128:192]) * cos + (krs * r * gks_ref[...]) * sin).astype(BF16)
        v_ref[0, h] = kv_raw[:, o + 128:o + 256].astype(BF16)


def _mla_prep(cols, gcq, wuq, gckv, wukv, gq, gqs, gk, gks, cos, sin, tm):
    b, n, _ = cols.shape
    full = lambda a: pl.BlockSpec(a.shape, lambda i, j: (0,) * a.ndim)
    return pl.pallas_call(
        _mla_prep_kernel,
        out_shape=(jax.ShapeDtypeStruct((b, MLA_HEADS, n, MLA_QK), BF16),
                   jax.ShapeDtypeStruct((b, MLA_HEADS, n, MLA_QK), BF16),
                   jax.ShapeDtypeStruct((b, MLA_HEADS, n, MLA_V), BF16)),
        grid=(b, n // tm),
        in_specs=[pl.BlockSpec((1, tm, 512), lambda i, j: (i, j, 0)),
                  full(gcq), full(wuq), full(gckv), full(wukv), full(gq), full(gqs), full(gk), full(gks),
                  pl.BlockSpec((tm, MLA_ROPE), lambda i, j: (j, 0)),
                  pl.BlockSpec((tm, MLA_ROPE), lambda i, j: (j, 0))],
        out_specs=(pl.BlockSpec((1, MLA_HEADS, tm, MLA_QK), lambda i, j: (i, 0, j, 0)),
                   pl.BlockSpec((1, MLA_HEADS, tm, MLA_QK), lambda i, j: (i, 0, j, 0)),
                   pl.BlockSpec((1, MLA_HEADS, tm, MLA_V), lambda i, j: (i, 0, j, 0))),
        compiler_params=_params(("parallel", "parallel")),
        name="mla_prep",
    )(cols, gcq, wuq, gckv, wukv, gq, gqs, gk, gks, cos, sin)


ATTN_TILES_PER_STEP = 4


def _attn_kernel(q_ref, kc_ref, kl_ref, vc_ref, vl_ref, o_ref):
    tq = q_ref.shape[2] // ATTN_TILES_PER_STEP
    nc = kc_ref.shape[2]
    kc, kl, vc, vl = kc_ref[0, 0], kl_ref[0, 0], vc_ref[0, 0], vl_ref[0, 0]
    nt = (((1,), (1,)), ((), ()))
    for j in range(ATTN_TILES_PER_STEP):
        q = q_ref[0, 0, j * tq:(j + 1) * tq, :]
        s = jnp.concatenate([lax.dot_general(q, kc, nt, preferred_element_type=F32),
                             lax.dot_general(q, kl, nt, preferred_element_type=F32)], axis=1)
        m = jnp.max(s, axis=-1, keepdims=True)
        p = jnp.exp(s - m)
        l = jnp.sum(p, axis=-1, keepdims=True)
        pb = p.astype(BF16)
        o = (jnp.dot(pb[:, :nc], vc, preferred_element_type=F32)
             + jnp.dot(pb[:, nc:], vl, preferred_element_type=F32))
        o_ref[0, j * tq:(j + 1) * tq, :] = (o / l).astype(BF16)


def _attention(q, k_c, k_l, v_c, v_l, tq):
    b, h, n, dk = q.shape
    nc = k_c.shape[2]
    dv = v_l.shape[3]
    tb = tq * ATTN_TILES_PER_STEP
    whole = lambda nn, dd: pl.BlockSpec((1, 1, nn, dd), lambda i, j, t: (i, j, 0, 0))
    return pl.pallas_call(
        _attn_kernel,
        out_shape=jax.ShapeDtypeStruct((b, n, h * dv), BF16),
        grid=(b, h, n // tb),
        in_specs=[pl.BlockSpec((1, 1, tb, dk), lambda i, j, t: (i, j, t, 0)),
                  whole(nc, dk), whole(n, dk), whole(nc, dv), whole(n, dv)],
        out_specs=pl.BlockSpec((1, tb, dv), lambda i, j, t: (i, t, j)),
        compiler_params=_params(("parallel", "parallel", "parallel")),
        name="attention",
    )(q, k_c, k_l, v_c, v_l)


def _ml_prep_kernel(qk_ref, prev_ref, next_ref, vin_ref, g_ref, gt_ref, cw_ref, bc_ref, br_ref,
                    q_ref, kt_ref, v_ref, a_ref, bm_ref, brow_ref):
    tn = qk_ref.shape[1]
    j = pl.program_id(1)
    nj = pl.num_programs(1)
    u = qk_ref[0]
    row = lax.broadcasted_iota(I32, (tn, 1), 0)
    before = jnp.where(j == 0, 0.0, prev_ref[0, SUBLANES - 1:SUBLANES, :])
    after = jnp.where(j == nj - 1, 0.0, next_ref[0, 0:1, :])
    up = jnp.where(row == 0, before, pltpu.roll(u, 1, 0))
    un = jnp.where(row == tn - 1, after, pltpu.roll(u, tn - 1, 0))
    cw = cw_ref[...]
    y = up * cw[0:1, :] + u * cw[1:2, :] + un * cw[2:3, :]
    y = y * jax.nn.sigmoid(y)
    hq = ML_HEADS * ML_QK
    for h in range(ML_HEADS):
        q_ref[0, h] = (y[:, h * ML_QK:(h + 1) * ML_QK] * (ML_QK ** -0.5)).astype(BF16)
    kt = y[:, hq:2 * hq].T
    L = ML_CHUNK
    for c in range(tn // L):
        kt_ref[0, c] = kt[:, c * L:(c + 1) * L].astype(BF16)
    v_ref[0] = vin_ref[0].astype(BF16)
    g = g_ref[0][:, 0:16] + bc_ref[...]
    ig_c = g[:, 0:8]
    lf_c = jax.nn.log_sigmoid(g[:, 8:16])
    gt = gt_ref[0] + br_ref[...]
    ig_r = gt[0:8, :]
    lf_r = jax.nn.log_sigmoid(gt[8:16, :])
    ti = lax.broadcasted_iota(I32, (L, L), 0)
    si = lax.broadcasted_iota(I32, (L, L), 1)
    lower = (si <= ti).astype(F32)
    upper = (si >= ti).astype(F32)
    lane_fwd = lax.broadcasted_iota(I32, (L, 8), 1) < ML_HEADS
    row_l = lax.broadcasted_iota(I32, (L, 8), 0)
    row_fwd = lax.broadcasted_iota(I32, (8, L), 0) < ML_HEADS
    for c in range(tn // L):
        lo = c * L
        lfc = lf_c[lo:lo + L, :]
        cf_c = jnp.where(
            lane_fwd,
            jnp.dot(lower, lfc, precision=HIGHEST, preferred_element_type=F32),
            jnp.dot(upper, lfc, precision=HIGHEST, preferred_element_type=F32))
        a_ref[0, lo:lo + L, :] = cf_c
        pre = suf = ig_c[lo:lo + L, :] - cf_c
        d = 1
        while d < L:
            pre = jnp.maximum(pre, jnp.where(row_l >= d, pltpu.roll(pre, d, 0), -jnp.inf))
            suf = jnp.maximum(suf, jnp.where(row_l < L - d, pltpu.roll(suf, L - d, 0), -jnp.inf))
            d *= 2
        bm_ref[0, lo:lo + L, :] = jnp.where(lane_fwd, pre, suf)
        lfr = lf_r[:, lo:lo + L]
        cf_r = jnp.where(
            row_fwd,
            jnp.dot(lfr, upper, precision=HIGHEST, preferred_element_type=F32),
            jnp.dot(lfr, lower, precision=HIGHEST, preferred_element_type=F32))
        brow_ref[0, c] = ig_r[:, lo:lo + L] - cf_r


def _ml_prep(cols, gates_t, cw, bias_col, bias_row, tn):
    b, n, _ = cols.shape
    r8 = tn // SUBLANES
    last8 = n // SUBLANES - 1
    L = ML_CHUNK
    hq = ML_HEADS * ML_QK
    return pl.pallas_call(
        _ml_prep_kernel,
        out_shape=(jax.ShapeDtypeStruct((b, ML_HEADS, n, ML_QK), BF16),
                   jax.ShapeDtypeStruct((b, n // L, hq, L), BF16),
                   jax.ShapeDtypeStruct((b, n, ML_HEADS * ML_V), BF16),
                   jax.ShapeDtypeStruct((b, n, 8), F32),
                   jax.ShapeDtypeStruct((b, n, 8), F32),
                   jax.ShapeDtypeStruct((b, n // L, 8, L), F32)),
        grid=(b, n // tn),
        in_specs=[pl.BlockSpec((1, tn, 512), lambda i, j: (i, j, C_QK0 // 512)),
                  pl.BlockSpec((1, SUBLANES, 512), lambda i, j: (i, jnp.maximum(j * r8 - 1, 0), C_QK0 // 512)),
                  pl.BlockSpec((1, SUBLANES, 512), lambda i, j: (i, jnp.minimum((j + 1) * r8, last8), C_QK0 // 512)),
                  pl.BlockSpec((1, tn, 512), lambda i, j: (i, j, C_V0 // 512)),
                  pl.BlockSpec((1, tn, 128), lambda i, j: (i, j, C_G0 // 128)),
                  pl.BlockSpec((1, 16, tn), lambda i, j: (i, 0, j)),
                  pl.BlockSpec((3, 512), lambda i, j: (0, 0)),
                  pl.BlockSpec((1, 16), lambda i, j: (0, 0)),
                  pl.BlockSpec((16, 1), lambda i, j: (0, 0))],
        out_specs=(pl.BlockSpec((1, ML_HEADS, tn, ML_QK), lambda i, j: (i, 0, j, 0)),
                   pl.BlockSpec((1, tn // L, hq, L), lambda i, j: (i, j, 0, 0)),
                   pl.BlockSpec((1, tn, ML_HEADS * ML_V), lambda i, j: (i, j, 0)),
                   pl.BlockSpec((1, tn, 8), lambda i, j: (i, j, 0)),
                   pl.BlockSpec((1, tn, 8), lambda i, j: (i, j, 0)),
                   pl.BlockSpec((1, tn // L, 8, L), lambda i, j: (i, j, 0, 0))),
        compiler_params=_params(("parallel", "parallel")),
        name="ml_prep",
    )(cols, cols, cols, cols, cols, gates_t, cw, bias_col, bias_row)


def _ml_chunk(qb, kt, vaug, a_c, bm_c, b_r, ct, m_prev, fwd, need_out):
    L = qb.shape[0]
    ti = lax.broadcasted_iota(I32, (L, L), 0)
    si = lax.broadcasted_iota(I32, (L, L), 1)
    mask = (si <= ti) if fwd else (si >= ti)
    a_b = jnp.broadcast_to(a_c, (L, LANES))
    inter = a_b + m_prev
    m_t = jnp.maximum(inter, a_b + jnp.broadcast_to(bm_c, (L, LANES)))
    last = L - 1 if fwd else 0
    m_new = m_t[last:last + 1, :]
    a_last = a_b[last:last + 1, :]
    w_r = jnp.exp(a_last[:, :L] + b_r - m_new[:, :L])
    decay = jnp.exp(a_last + m_prev - m_new)
    decay = jnp.concatenate([decay, decay], axis=1)
    kw = (kt.astype(F32) * w_r).astype(BF16)
    if not need_out:
        return decay * ct + jnp.dot(kw, vaug, preferred_element_type=F32), m_new, None
    s = jnp.dot(qb, kt, preferred_element_type=F32)
    dmat = jnp.where(mask, a_b[:, :L] + b_r, -jnp.inf)
    wmat = jnp.exp(dmat - m_t[:, :L]) * s
    sc = jnp.exp(inter - m_t)
    top = jnp.concatenate([wmat.astype(BF16), (sc[:, :qb.shape[1]] * qb.astype(F32)).astype(BF16)], axis=1)
    bot = jnp.concatenate([kw, jnp.zeros(kw.shape, BF16)], axis=1)
    res = jnp.dot(jnp.concatenate([top, bot], axis=0), jnp.concatenate([vaug, ct.astype(BF16)], axis=0),
                  preferred_element_type=F32)
    num = res[:L, 0:ML_V]
    den = res[:L, ML_V:2 * ML_V]
    return decay * ct + res[L:], m_new, num / jnp.maximum(jnp.abs(den), jnp.exp(-m_t))


def _ml_scan_kernel(ql_ref, ktl_ref, vl_ref, al_ref, bml_ref, brl_ref,
                    qc_ref, ktc_ref, vc_ref, ac_ref, bmc_ref, brc_ref, out_ref, st_ref, m_ref):
    L = ML_CHUNK
    ncl = ql_ref.shape[2] // L
    ncc = qc_ref.shape[2] // L
    ones_blk = jnp.ones((L, ML_V), BF16)
    st_ref[...] = jnp.zeros(st_ref.shape, F32)
    m_ref[...] = jnp.zeros(m_ref.shape, F32)

    def step(refs, c, d, fwd, need_out):
        q_ref, kt_ref, v_ref, a_ref, bm_ref, br_ref = refs
        sl = pl.ds(pl.multiple_of(c * L, L), L)
        a_all = a_ref[0, sl, :]
        bm_all = bm_ref[0, sl, :]
        br_all = br_ref[0, c]
        v_all = v_ref[0, sl, :]
        kt_all = kt_ref[0, c]
        hs = []
        for h in range(ML_HEADS):
            j = d * ML_HEADS + h
            vaug = jnp.concatenate([v_all[:, h * ML_V:(h + 1) * ML_V], ones_blk], axis=1)
            ct_new, m_new, hh = _ml_chunk(
                q_ref[0, h, sl, :], kt_all[h * ML_QK:(h + 1) * ML_QK, :], vaug,
                a_all[:, j:j + 1], bm_all[:, j:j + 1], br_all[j:j + 1, :], st_ref[j], m_ref[j][0:1, :],
                fwd, need_out)
            st_ref[j] = ct_new
            m_ref[j] = jnp.broadcast_to(m_new, (SUBLANES, LANES))
            hs.append(hh)
        return hs, sl

    ctx_refs = (qc_ref, ktc_ref, vc_ref, ac_ref, bmc_ref, brc_ref)
    lat_refs = (ql_ref, ktl_ref, vl_ref, al_ref, bml_ref, brl_ref)

    out_ref[...] = jnp.zeros(out_ref.shape, F32)

    def ctx_body(i, carry):
        step(ctx_refs, i, 0, True, False)
        step(ctx_refs, ncc - 1 - i, 1, False, False)
        return carry

    def lat_body(i, carry):
        for d, fwd in ((0, True), (1, False)):
            hs, sl = step(lat_refs, i if fwd else ncl - 1 - i, d, fwd, True)
            out_ref[0, sl, :] = out_ref[0, sl, :] + jnp.concatenate(hs, axis=1)
        return carry

    lax.fori_loop(0, ncc, ctx_body, 0)
    lax.fori_loop(0, ncl, lat_body, 0)


def _ml_scan(ql, ktl, vl, al, bml, brl, qc, ktc, vc, ac, bmc, brc):
    b, _, n, _ = ql.shape
    nctx = qc.shape[2]
    L = ML_CHUNK
    hv = ML_HEADS * ML_V
    hq = ML_HEADS * ML_QK
    qspec = lambda nn: pl.BlockSpec((1, ML_HEADS, nn, ML_QK), lambda i: (i, 0, 0, 0))
    ktspec = lambda nn: pl.BlockSpec((1, nn // L, hq, L), lambda i: (i, 0, 0, 0))
    vspec = lambda nn: pl.BlockSpec((1, nn, hv), lambda i: (i, 0, 0))
    aspec = lambda nn: pl.BlockSpec((1, nn, 8), lambda i: (i, 0, 0))
    rspec = lambda nn: pl.BlockSpec((1, nn // L, 8, L), lambda i: (i, 0, 0, 0))
    return pl.pallas_call(
        _ml_scan_kernel,
        out_shape=jax.ShapeDtypeStruct((b, n, hv), F32),
        grid=(b,),
        in_specs=[qspec(n), ktspec(n), vspec(n), aspec(n), aspec(n), rspec(n),
                  qspec(nctx), ktspec(nctx), vspec(nctx), aspec(nctx), aspec(nctx), rspec(nctx)],
        out_specs=pl.BlockSpec((1, n, hv), lambda i: (i, 0, 0)),
        scratch_shapes=[pltpu.VMEM((2 * ML_HEADS, ML_QK, 2 * ML_V), F32),
                        pltpu.VMEM((2 * ML_HEADS, SUBLANES, LANES), F32)],
        compiler_params=_params(("parallel",)),
        name="ml_scan",
    )(ql, ktl, vl, al, bml, brl, qc, ktc, vc, ac, bmc, brc)


def _outproj_kernel(x_ref, mla_ref, hs_ref, o_ref, gm_ref, wa_ref, wb_ref, gt_ref, g2_ref, sc_ref, sh_ref,
                    wpq_ref, sk_ref, x1_ref, h2_ref, s_ref):
    hs = hs_ref[0]
    gm = gm_ref[...]
    hn = jnp.concatenate([_rms(hs[:, h * ML_V:(h + 1) * ML_V]) * gm[:, h * ML_V:(h + 1) * ML_V]
                          for h in range(ML_HEADS)], axis=1)
    ml = (jax.nn.sigmoid(o_ref[0]) * hn).astype(BF16)
    mix = (jnp.dot(mla_ref[0], wa_ref[...], preferred_element_type=F32)
           + jnp.dot(ml, wb_ref[...], preferred_element_type=F32))
    x1 = x_ref[0] + gt_ref[0] * mix
    x1_ref[0] = x1
    h2 = _rms(x1) * g2_ref[...] * (1.0 + sc_ref[0]) + sh_ref[0]
    h2_ref[0] = h2
    qp = jnp.dot(h2.astype(BF16), wpq_ref[...], preferred_element_type=F32).astype(BF16)
    for hp in range(2 * PEER_HEADS):
        s_ref[0, hp] = lax.dot_general(sk_ref[hp], qp[:, hp * N_KEYS:(hp + 1) * N_KEYS],
                                       (((1,), (1,)), ((), ())), preferred_element_type=F32)


def _outproj(x, mla, hsum, cols, gm, wa, wb, gt1, g2, sc2, sh2, wpq, sk, tm):
    b, n, d = x.shape
    hw = mla.shape[2]
    nq = wpq.shape[1]
    mod = pl.BlockSpec((1, 1, d), lambda i, j: (i, 0, 0))
    return pl.pallas_call(
        _outproj_kernel,
        out_shape=(jax.ShapeDtypeStruct((b, n, d), F32), jax.ShapeDtypeStruct((b, n, d), F32),
                   jax.ShapeDtypeStruct((b, 2 * PEER_HEADS, N_KEYS, n), F32)),
        grid=(b, n // tm),
        in_specs=[pl.BlockSpec((1, tm, d), lambda i, j: (i, j, 0)),
                  pl.BlockSpec((1, tm, hw), lambda i, j: (i, j, 0)),
                  pl.BlockSpec((1, tm, hw), lambda i, j: (i, j, 0)),
                  pl.BlockSpec((1, tm, 512), lambda i, j: (i, j, C_O0 // 512)),
                  pl.BlockSpec((1, hw), lambda i, j: (0, 0)),
                  pl.BlockSpec((hw, d), lambda i, j: (0, 0)),
                  pl.BlockSpec((hw, d), lambda i, j: (0, 0)),
                  mod, pl.BlockSpec((1, d), lambda i, j: (0, 0)), mod, mod,
                  pl.BlockSpec((d, nq), lambda i, j: (0, 0)),
                  pl.BlockSpec((2 * PEER_HEADS, N_KEYS, N_KEYS), lambda i, j: (0, 0, 0))],
        out_specs=(pl.BlockSpec((1, tm, d), lambda i, j: (i, j, 0)),
                   pl.BlockSpec((1, tm, d), lambda i, j: (i, j, 0)),
                   pl.BlockSpec((1, 2 * PEER_HEADS, N_KEYS, tm), lambda i, j: (i, 0, 0, j))),
        compiler_params=_params(("parallel", "parallel")),
        name="outproj",
    )(x, mla, hsum, cols, gm, wa, wb, gt1, g2, sc2, sh2, wpq, sk)


def _topk_rows(work_refs, rid_ref, rows, k):
    t = rid_ref.shape[1]
    kid = lax.broadcasted_iota(I32, (k, t), 0)
    big = jnp.float32(3.0e38)

    def body(r, carry):
        out = []
        for work_ref, (vals, idxs) in zip(work_refs, carry):
            s = work_ref[0:rows, :]
            rid = rid_ref[0:rows, :]
            m = jnp.max(s, axis=0, keepdims=True)
            i = jnp.min(jnp.where(s == m, rid, big), axis=0, keepdims=True)
            work_ref[0:rows, :] = jnp.where(rid == i, -jnp.inf, s)
            out.append((jnp.where(kid == r, m, vals), jnp.where(kid == r, i, idxs)))
        return tuple(out)

    init = tuple((jnp.zeros((k, t), F32), jnp.zeros((k, t), F32)) for _ in work_refs)
    return lax.fori_loop(0, k, body, init)


def _pair_candidates(sv0, sv1):
    K, t = sv0.shape
    h = K // 2
    iid = lax.broadcasted_iota(I32, (h, t), 0)
    vals = [sv0[0:h] + sv1[0:1], sv0[h:K] + sv1[0:1]]
    ids = [iid * K, (iid + h) * K]
    for j in range(1, h):
        vals.append(sv0[0:h] + sv1[j:j + 1])
        ids.append(iid * K + j)
    vals.append(sv0[0:1] + sv1[h:K])
    ids.append(iid + h)
    return jnp.concatenate(vals, axis=0), jnp.concatenate(ids, axis=0).astype(F32)


HEADS_PER_ROUND = 4


def _route_kernel(s_ref, off_ref, par_ref, gw_ref, et_ref, gt_ref, work_ref, rid_ref, cid_ref):
    K = PEER_TOPK
    nk, tt = rid_ref.shape
    rid_ref[...] = lax.broadcasted_iota(I32, (nk, tt), 0).astype(F32)
    hr = HEADS_PER_ROUND

    def round_(g, carry):
        heads = [g * hr + a for a in range(hr)]
        for a, h in enumerate(heads):
            work_ref[2 * a] = s_ref[0, 2 * h]
            work_ref[2 * a + 1] = s_ref[0, 2 * h + 1]
        first = _topk_rows([work_ref.at[b] for b in range(2 * hr)], rid_ref, nk, K)
        ncand = N_PAIR_CANDIDATES
        for a in range(hr):
            cand, cid = _pair_candidates(first[2 * a][0], first[2 * a + 1][0])
            work_ref[a, 0:ncand, :] = cand
            cid_ref[...] = cid
        second = _topk_rows([work_ref.at[a] for a in range(hr)], cid_ref, ncand, K)
        for a, h in enumerate(heads):
            best, pos = second[a]
            pos = pos.astype(I32)
            si0 = first[2 * a][1].astype(I32)
            si1 = first[2 * a + 1][1].astype(I32)
            isel = pos >> 4
            jsel = pos & (K - 1)
            e0 = jnp.zeros_like(pos)
            e1 = jnp.zeros_like(pos)
            for i in range(K):
                e0 = jnp.where(isel == i, si0[i:i + 1, :], e0)
                e1 = jnp.where(jsel == i, si1[i:i + 1, :], e1)
            ex = jnp.exp(best - best[0:1, :])
            sl = pl.ds(pl.multiple_of(h * K, K), K)
            et_ref[sl, :] = e0 * N_KEYS + e1
            gt_ref[sl, :] = ex / jnp.sum(ex, axis=0, keepdims=True)
        return carry

    lax.fori_loop(0, PEER_HEADS // hr, round_, 0)
    e = et_ref[...].T
    off_ref[0] = (e >> 1) * SUBLANES
    par_ref[0] = (e & 1).astype(F32)
    gw_ref[0] = gt_ref[...].T


def _route(scores, tt):
    b, hp, nk, n = scores.shape
    nsel = PEER_HEADS * PEER_TOPK
    ospec = pl.BlockSpec((1, tt, nsel), lambda i, j: (i, j, 0))
    return pl.pallas_call(
        _route_kernel,
        out_shape=(jax.ShapeDtypeStruct((b, n, nsel), I32), jax.ShapeDtypeStruct((b, n, nsel), F32),
                   jax.ShapeDtypeStruct((b, n, nsel), F32)),
        grid=(b, n // tt),
        in_specs=[pl.BlockSpec((1, hp, nk, tt), lambda i, j: (i, 0, 0, j))],
        out_specs=(ospec, ospec, ospec),
        scratch_shapes=[pltpu.VMEM((nsel, tt), I32), pltpu.VMEM((nsel, tt), F32),
                        pltpu.VMEM((2 * HEADS_PER_ROUND, nk, tt), F32), pltpu.VMEM((nk, tt), F32),
                        pltpu.VMEM((N_PAIR_CANDIDATES, tt), F32)],
        compiler_params=_params(("parallel", "parallel")),
        name="route",
    )(scores)


TILE_ROWS = 2 * SUBLANES
TOKENS_PER_ITER = 16


CHUNK_SEL = 16


def _chunk_tiles(tab_ref, row_ref, c):
    return jnp.concatenate(
        [pltpu.bitcast(tab_ref[pl.ds(pl.multiple_of(row_ref[c * CHUNK_SEL + kk], SUBLANES), SUBLANES), :], BF16)
         for kk in range(CHUNK_SEL)], axis=0)


def _init_spread(spread_ref, gather_ref=None):
    @pl.when(pl.program_id(0) == 0)
    def _():
        nsel, width = spread_ref.shape
        spread_ref[...] = (lax.broadcasted_iota(I32, (nsel, width), 1) // TILE_ROWS
                           == lax.broadcasted_iota(I32, (nsel, width), 0)).astype(BF16)
        if gather_ref is not None:
            gather_ref[...] = (lax.broadcasted_iota(I32, (width, nsel), 0) // TILE_ROWS
                               == lax.broadcasted_iota(I32, (width, nsel), 1)).astype(BF16)


def _row_masks(width):
    half = (lax.broadcasted_iota(I32, (1, width), 1) % 2).astype(F32)
    lane8 = lax.broadcasted_iota(I32, (SUBLANES, width), 1)
    diag = ((lane8 % TILE_ROWS) // 2 == lax.broadcasted_iota(I32, (SUBLANES, width), 0)).astype(F32)
    return half, diag


def _row_to_tile(row):
    return jnp.concatenate([row[:, s * LANES:(s + 1) * LANES] for s in range(SUBLANES)], axis=0)


def _split_bf16(v):
    hi = v.astype(BF16)
    return hi, (v - hi.astype(F32)).astype(BF16)


def _peer_u_kernel(off_ref, h_ref, tab_ref, par_ref, g_ref, c_ref, r_ref, spread_ref, gather_ref):
    tb, nsel = g_ref.shape
    _init_spread(spread_ref, gather_ref)
    half, diag = _row_masks(spread_ref.shape[1])

    cw = CHUNK_SEL * TILE_ROWS
    diag_c = diag[:, :cw]
    nt = (((1,), (1,)), ((), ()))

    def tok(i, carry):
        ts = [i * TOKENS_PER_ITER + u for u in range(TOKENS_PER_ITER)]
        rows = [off_ref.at[t] for t in ts]
        xs = [jnp.concatenate(_split_bf16(_row_to_tile(h_ref[pl.ds(t, 1), :])), axis=0) for t in ts]
        for c in range(nsel // CHUNK_SEL):
            for u, t in enumerate(ts):
                res = lax.dot_general(xs[u], _chunk_tiles(tab_ref, rows[u], c), nt, preferred_element_type=F32)
                r_ref[pl.ds(t, 1), c * cw:(c + 1) * cw] = jnp.sum(
                    (res[:SUBLANES] + res[SUBLANES:]) * diag_c, axis=0, keepdims=True)
        return carry

    lax.fori_loop(0, tb // TOKENS_PER_ITER, tok, 0)
    mine = jnp.dot(par_ref[...].astype(BF16), spread_ref[...], preferred_element_type=F32) == half
    r_hi, r_lo = _split_bf16(jnp.where(mine, r_ref[...], 0.0))
    both = jnp.dot(jnp.concatenate([r_hi, r_lo], axis=0), gather_ref[...], preferred_element_type=F32)
    pre = both[:tb] + both[tb:]
    c_ref[...] = g_ref[...] * (0.5 * pre * (1.0 + lax.erf(pre * (2.0 ** -0.5))))


def _smem_block(tb, nsel):
    return pl.BlockSpec((tb, nsel), lambda i: (i, 0), memory_space=pltpu.SMEM)


def _peer_u(off, h2, tab, par, gw, tb):
    t, nsel = gw.shape
    vspec = pl.BlockSpec((tb, nsel), lambda i: (i, 0))
    return pl.pallas_call(
        _peer_u_kernel,
        out_shape=jax.ShapeDtypeStruct((t, nsel), F32),
        grid=(t // tb,),
        in_specs=[_smem_block(tb, nsel),
                  pl.BlockSpec((tb, h2.shape[1]), lambda i: (i, 0)),
                  pl.BlockSpec(tab.shape, lambda i: (0, 0), pipeline_mode=pl.Buffered(1)),
                  vspec, vspec],
        out_specs=vspec,
        scratch_shapes=[pltpu.VMEM((tb, nsel * TILE_ROWS), F32),
                        pltpu.VMEM((nsel, nsel * TILE_ROWS), BF16), pltpu.VMEM((nsel * TILE_ROWS, nsel), BF16)],
        compiler_params=_params(("arbitrary",)),
        name="peer_u",
    )(off, h2, tab, par, gw)


def _peer_v_kernel(off_ref, c_ref, par_ref, tab_ref, x1_ref, gt_ref, o_ref, lhi_ref, llo_ref, spread_ref):
    tb, nsel = c_ref.shape
    _init_spread(spread_ref)
    half, diag = _row_masks(spread_ref.shape[1])
    c_hi, c_lo = _split_bf16(c_ref[...])
    ex = jnp.dot(jnp.concatenate([par_ref[...].astype(BF16), c_hi, c_lo], axis=0), spread_ref[...],
                 preferred_element_type=F32)
    mine = ex[:tb] == half
    lhi_ref[...] = jnp.where(mine, ex[tb:2 * tb], 0.0)
    llo_ref[...] = jnp.where(mine, ex[2 * tb:], 0.0)
    gt = gt_ref[0]

    cw = CHUNK_SEL * TILE_ROWS
    diag_c = diag[:, :cw]

    def tok(i, carry):
        ts = [i * TOKENS_PER_ITER + u for u in range(TOKENS_PER_ITER)]
        rows = [off_ref.at[t] for t in ts]
        accs = [jnp.zeros((TILE_ROWS, LANES), F32) for _ in ts]
        for c in range(nsel // CHUNK_SEL):
            cols = slice(c * cw, (c + 1) * cw)
            for u, t in enumerate(ts):
                left = jnp.concatenate([lhi_ref[pl.ds(t, 1), cols] * diag_c, llo_ref[pl.ds(t, 1), cols] * diag_c],
                                       axis=0).astype(BF16)
                accs[u] = accs[u] + jnp.dot(left, _chunk_tiles(tab_ref, rows[u], c), preferred_element_type=F32)
        for u, t in enumerate(ts):
            y = accs[u][:SUBLANES] + accs[u][SUBLANES:]
            y_row = jnp.concatenate([y[s:s + 1, :] for s in range(SUBLANES)], axis=1)
            o_ref[pl.ds(t, 1), :] = x1_ref[pl.ds(t, 1), :] + gt * y_row
        return carry

    lax.fori_loop(0, tb // TOKENS_PER_ITER, tok, 0)


def _peer_v(off, coef, par, tab, x1, gt2, tb, tok_per_batch):
    t, nsel = coef.shape
    d = x1.shape[1]
    bpb = tok_per_batch // tb
    vspec = pl.BlockSpec((tb, nsel), lambda i: (i, 0))
    return pl.pallas_call(
        _peer_v_kernel,
        out_shape=jax.ShapeDtypeStruct(x1.shape, F32),
        grid=(t // tb,),
        in_specs=[_smem_block(tb, nsel), vspec, vspec,
                  pl.BlockSpec(tab.shape, lambda i: (0, 0), pipeline_mode=pl.Buffered(1)),
                  pl.BlockSpec((tb, d), lambda i: (i, 0)),
                  pl.BlockSpec((1, 1, d), lambda i: (i // bpb, 0, 0))],
        out_specs=pl.BlockSpec((tb, d), lambda i: (i, 0)),
        scratch_shapes=[pltpu.VMEM((tb, nsel * TILE_ROWS), F32), pltpu.VMEM((tb, nsel * TILE_ROWS), F32),
                        pltpu.VMEM((nsel, nsel * TILE_ROWS), BF16)],
        compiler_params=_params(("arbitrary",)),
        name="peer_v",
    )(off, coef, par, tab, x1, gt2)


def _rope_tables(n):
    axis = MLA_ROPE // 2
    t = jnp.arange(n, dtype=F32)
    row = jnp.floor(t / GRID_W)
    col = t - row * GRID_W
    inv = ROPE_BASE ** (-jnp.arange(axis // 2, dtype=F32) * (2.0 / axis))
    ar = row[:, None] * inv
    ac = col[:, None] * inv
    cos = jnp.concatenate([jnp.cos(ar), jnp.cos(ar), jnp.cos(ac), jnp.cos(ac)], axis=1)
    sin = jnp.concatenate([-jnp.sin(ar), jnp.sin(ar), -jnp.sin(ac), jnp.sin(ac)], axis=1)
    return cos, sin


PACK_ROWS = 256


def _pack_kernel(t_ref, o_ref):
    n = t_ref.shape[0]
    rows = n // 2
    xb = t_ref[...].astype(BF16)
    col = lax.broadcasted_iota(I32, (rows, n), 1)
    row2 = 2 * lax.broadcasted_iota(I32, (rows, n), 0)
    even = pltpu.bitcast(jnp.dot((col == row2).astype(BF16), xb, preferred_element_type=F32), jnp.uint32)
    odd = pltpu.bitcast(jnp.dot((col == row2 + 1).astype(BF16), xb, preferred_element_type=F32), jnp.uint32)
    word = odd | (even >> 16)
    for s in range(SUBLANES):
        o_ref[pl.ds(s, rows, stride=SUBLANES), :] = word[:, s * LANES:(s + 1) * LANES]


def _pack_table(tab):
    e, d = tab.shape
    out_rows = PACK_ROWS // 2 * (d // LANES)
    return pl.pallas_call(
        _pack_kernel,
        out_shape=jax.ShapeDtypeStruct((e // 2 * (d // LANES), LANES), jnp.uint32),
        grid=(e // PACK_ROWS,),
        in_specs=[pl.BlockSpec((PACK_ROWS, d), lambda i: (i, 0))],
        out_specs=pl.BlockSpec((out_rows, LANES), lambda i: (i, 0)),
        compiler_params=_params(("parallel",)),
        name="pack_table",
    )(tab)


def _block(n, want):
    return want if n % want == 0 else n


def kernel(x, c, ctx, c_ctx, w_ada, b_ada, g_norm1, w_in, g_cq, w_uq, g_ckv, w_ukv, g_qn, g_kn, conv_qk, b_igate, b_fgate, g_mlstm, w_out, g_norm2, w_pq, sub_keys, expert_u, expert_v):
    B, N, D = x.shape
    NC = ctx.shape[1]
    assert w_ada.shape[0] == 1, "one layer"
    assert N % (ATTN_TQ * ATTN_TILES_PER_STEP) == 0 and NC % ML_CHUNK == 0 and D == SUBLANES * LANES
    q_rank = g_cq.shape[1]
    kv_rank = g_ckv.shape[1]
    mla_cols = q_rank + kv_rank + MLA_ROPE
    assert (q_rank, kv_rank) == (C_KV0 - C_Q0, C_KR0 - C_KV0)
    swap = jnp.arange(MLA_ROPE) ^ (MLA_ROPE // 4)

    cc = jnp.concatenate([c, c_ctx[None, :], jnp.zeros((16 - B - 1, D), F32)], axis=0)
    mod = _ada(cc, w_ada[0].astype(BF16), b_ada)
    sh1, sc1, gt1, sh2, sc2, gt2 = [mod[:, i * D:(i + 1) * D] for i in range(6)]
    lat = lambda m: m[:B].reshape(B, 1, D)
    ctxm = lambda m: jnp.broadcast_to(m[B:B + 1].reshape(1, 1, D), (B, 1, D))

    wi = w_in[0]
    n_qk = 2 * ML_HEADS * ML_QK
    n_v = ML_HEADS * ML_V
    n_g = 4 * ML_HEADS
    m0 = mla_cols
    w_cols = jnp.concatenate([
        wi[:, 0:mla_cols],
        wi[:, q_rank + kv_rank + swap],
        wi[:, m0:m0 + n_qk + 2 * n_v + n_g],
        jnp.zeros((D, C_END - C_G0 - n_g), F32)], axis=1).astype(BF16)
    assert w_cols.shape[1] == C_END
    w_gates_t = wi[:, m0 + n_qk + 2 * n_v:m0 + n_qk + 2 * n_v + n_g].T.astype(BF16)
    cols_l, gates_l = _inproj(x, g_norm1, lat(sc1), lat(sh1), w_cols, w_gates_t, _block(N, ROW_TILE))
    cols_c, gates_c = _inproj(ctx, g_norm1, ctxm(sc1), ctxm(sh1), w_cols, w_gates_t, _block(NC, ROW_TILE))

    wq = w_uq[0].reshape(q_rank, MLA_HEADS, MLA_QK)
    wuq = jnp.concatenate([wq, wq[:, :, MLA_NOPE + swap]], axis=2).reshape(q_rank, MLA_HEADS * 256).astype(BF16)
    wukv = w_ukv[0].astype(BF16)
    gqs, gks = g_qn[:, MLA_NOPE + swap], g_kn[:, MLA_NOPE + swap]
    cos_l, sin_l = _rope_tables(N)
    cos_c, sin_c = jnp.ones((NC, MLA_ROPE), F32), jnp.zeros((NC, MLA_ROPE), F32)
    q_l, k_l, v_l = _mla_prep(cols_l, g_cq, wuq, g_ckv, wukv, g_qn, gqs, g_kn, gks, cos_l, sin_l, _block(N, ROW_TILE))
    _, k_c, v_c = _mla_prep(cols_c, g_cq, wuq, g_ckv, wukv, g_qn, gqs, g_kn, gks, cos_c, sin_c, _block(NC, ROW_TILE))
    mla = _attention(q_l, k_c, k_l, v_c, v_l, ATTN_TQ)

    bias16 = jnp.concatenate([b_igate[0].reshape(-1), b_fgate[0].reshape(-1)])
    cw = conv_qk[0]
    pl_ = _ml_prep(cols_l, gates_l, cw, bias16[None, :], bias16[:, None], _block(N, ROW_TILE))
    pc_ = _ml_prep(cols_c, gates_c, cw, bias16[None, :], bias16[:, None], _block(NC, ROW_TILE))
    hsum = _ml_scan(*pl_, *pc_)

    wo = w_out[0].astype(BF16)
    hw = MLA_HEADS * MLA_V
    sk = sub_keys[0].reshape(2 * PEER_HEADS, N_KEYS, -1).astype(BF16)
    x1, h2, scores = _outproj(x, mla, hsum, cols_l, g_mlstm, wo[:hw], wo[hw:], lat(gt1), g_norm2, lat(sc2), lat(sh2),
                              w_pq[0].astype(BF16), sk, OUT_TILE)
    T = B * N
    nsel = PEER_HEADS * PEER_TOPK
    off, par, gw = [a.reshape(T, nsel) for a in _route(scores, ROUTE_TILE)]

    tb = PEER_TOKENS
    coef = _peer_u(off, h2.reshape(T, D), _pack_table(expert_u[0]), par, gw, tb)
    out = _peer_v(off, coef, par, _pack_table(expert_v[0]), x1.reshape(T, D), lat(gt2), tb, N)
    return out.reshape(B, N, D)
```

```python
import jax
import jax.numpy as jnp
from jax import lax
from jax.experimental import pallas as pl
from jax.experimental.pallas import tpu as pltpu

F32 = jnp.float32
BF16 = jnp.bfloat16
I32 = jnp.int32
EPS = 1e-6

GRID_W = 64
MLA_HEADS = 4
MLA_NOPE = 128
MLA_ROPE = 64
MLA_V = 128
MLA_QK = MLA_NOPE + MLA_ROPE
ROPE_BASE = 10000.0
ML_HEADS = 4
ML_QK = 64
ML_V = 128
ML_CHUNK = 64
PEER_HEADS = 8
N_KEYS = 128
PEER_TOPK = 16
N_PAIR_CANDIDATES = 80

LANES = 128
SUBLANES = 8
VMEM_LIMIT_BYTES = 56 * 1024 * 1024

ROW_TILE = 512
ATTN_TQ = 256
OUT_TILE = 256
ROUTE_TILE = 256
PEER_TOKENS = 128

C_Q0, C_KV0, C_KR0, C_KRS0, C_QK0, C_V0, C_O0, C_G0, C_END = 0, 256, 384, 448, 512, 1024, 1536, 2048, 2176
HIGHEST = lax.Precision.HIGHEST


def _params(sem):
    return pltpu.CompilerParams(dimension_semantics=sem, vmem_limit_bytes=VMEM_LIMIT_BYTES)


def _rms(x):
    return x * lax.rsqrt(jnp.mean(x * x, axis=-1, keepdims=True) + EPS)


def _ada_kernel(c_ref, w_ref, b_ref, o_ref):
    c = c_ref[...]
    s = c * jax.nn.sigmoid(c)
    o_ref[...] = jnp.dot(s.astype(BF16), w_ref[...], preferred_element_type=F32) + b_ref[...]


def _ada(cc, w, b):
    rows, d = cc.shape
    n = w.shape[1]
    bn = n // 4
    return pl.pallas_call(
        _ada_kernel,
        out_shape=jax.ShapeDtypeStruct((rows, n), F32),
        grid=(4,),
        in_specs=[pl.BlockSpec((rows, d), lambda j: (0, 0)),
                  pl.BlockSpec((d, bn), lambda j: (0, j)),
                  pl.BlockSpec((1, bn), lambda j: (0, j))],
        out_specs=pl.BlockSpec((rows, bn), lambda j: (0, j)),
        compiler_params=_params(("arbitrary",)),
        name="ada",
    )(cc, w, b)


def _inproj_kernel(x_ref, g_ref, sc_ref, sh_ref, w_ref, wg_ref, o_ref, gt_ref):
    x = x_ref[0]
    h = _rms(x) * g_ref[...] * (1.0 + sc_ref[0]) + sh_ref[0]
    hb = h.astype(BF16)
    o_ref[0] = jnp.dot(hb, w_ref[...], preferred_element_type=F32)
    gt_ref[0] = lax.dot_general(wg_ref[...], hb, (((1,), (1,)), ((), ())), preferred_element_type=F32)


def _inproj(x, g, sc, sh, w, wg, tm):
    b, n, d = x.shape
    nc = w.shape[1]
    return pl.pallas_call(
        _inproj_kernel,
        out_shape=(jax.ShapeDtypeStruct((b, n, nc), F32), jax.ShapeDtypeStruct((b, 16, n), F32)),
        grid=(b, n // tm),
        in_specs=[pl.BlockSpec((1, tm, d), lambda i, j: (i, j, 0)),
                  pl.BlockSpec((1, d), lambda i, j: (0, 0)),
                  pl.BlockSpec((1, 1, d), lambda i, j: (i, 0, 0)),
                  pl.BlockSpec((1, 1, d), lambda i, j: (i, 0, 0)),
                  pl.BlockSpec((d, nc), lambda i, j: (0, 0)),
                  pl.BlockSpec((16, d), lambda i, j: (0, 0))],
        out_specs=(pl.BlockSpec((1, tm, nc), lambda i, j: (i, j, 0)),
                   pl.BlockSpec((1, 16, tm), lambda i, j: (i, 0, j))),
        compiler_params=_params(("parallel", "parallel")),
        name="inproj",
    )(x, g, sc, sh, w, wg)


def _mla_prep_kernel(c_ref, gcq_ref, wuq_ref, gckv_ref, wukv_ref, gq_ref, gqs_ref, gk_ref, gks_ref,
                     cos_ref, sin_ref, q_ref, k_ref, v_ref):
    c = c_ref[0]
    cq = _rms(c[:, C_Q0:C_KV0]) * gcq_ref[...]
    ckv = _rms(c[:, C_KV0:C_KR0]) * gckv_ref[...]
    kr = c[:, C_KR0:C_KRS0]
    krs = c[:, C_KRS0:C_QK0]
    q_raw = jnp.dot(cq.astype(BF16), wuq_ref[...], preferred_element_type=F32)
    kv_raw = jnp.dot(ckv.astype(BF16), wukv_ref[...], preferred_element_type=F32)
    cos = cos_ref[...]
    sin = sin_ref[...]
    gq = gq_ref[...]
    gk = gk_ref[...]
    kr_ss = jnp.sum(kr * kr, axis=-1, keepdims=True)
    for h in range(MLA_HEADS):
        o = h * 256
        qn = q_raw[:, o:o + 128]
        qr = q_raw[:, o + 128:o + 192]
        qs = q_raw[:, o + 192:o + 256]
        ss = jnp.sum(qn * qn, axis=-1, keepdims=True) + jnp.sum(qr * qr, axis=-1, keepdims=True)
        r = lax.rsqrt(ss * (1.0 / MLA_QK) + EPS) * (MLA_QK ** -0.5)
        q_ref[0, h, :, 0:128] = (qn * r * gq[:, 0:128]).astype(BF16)
        q_ref[0, h, :, 128:192] = ((qr * r * gq[:, 128:192]) * cos + (qs * r * gqs_ref[...]) * sin).astype(BF16)
        kn = kv_raw[:, o:o + 128]
        ss = jnp.sum(kn * kn, axis=-1, keepdims=True) + kr_ss
        r = lax.rsqrt(ss * (1.0 / MLA_QK) + EPS)
        k_ref[0, h, :, 0:128] = (kn * r * gk[:, 0:128]).astype(BF16)
        k_ref[0, h, :, 128:192] = ((kr * r * gk[:, 128:192]) * cos + (krs * r * gks_ref[...]) * sin).astype(BF16)
        v_ref[0, h] = kv_raw[:, o + 128:o + 256].astype(BF16)


def _mla_prep(cols, gcq, wuq, gckv, wukv, gq, gqs, gk, gks, cos, sin, tm):
    b, n, _ = cols.shape
    full = lambda a: pl.BlockSpec(a.shape, lambda i, j: (0,) * a.ndim)
    return pl.pallas_call(
        _mla_prep_kernel,
        out_shape=(jax.ShapeDtypeStruct((b, MLA_HEADS, n, MLA_QK), BF16),
                   jax.ShapeDtypeStruct((b, MLA_HEADS, n, MLA_QK), BF16),
                   jax.ShapeDtypeStruct((b, MLA_HEADS, n, MLA_V), BF16)),
        grid=(b, n // tm),
        in_specs=[pl.BlockSpec((1, tm, 512), lambda i, j: (i, j, 0)),
                  full(gcq), full(wuq), full(gckv), full(wukv), full(gq), full(gqs), full(gk), full(gks),
                  pl.BlockSpec((tm, MLA_ROPE), lambda i, j: (j, 0)),
                  pl.BlockSpec((tm, MLA_ROPE), lambda i, j: (j, 0))],
        out_specs=(pl.BlockSpec((1, MLA_HEADS, tm, MLA_QK), lambda i, j: (i, 0, j, 0)),
                   pl.BlockSpec((1, MLA_HEADS, tm, MLA_QK), lambda i, j: (i, 0, j, 0)),
                   pl.BlockSpec((1, MLA_HEADS, tm, MLA_V), lambda i, j: (i, 0, j, 0))),
        compiler_params=_params(("parallel", "parallel")),
        name="mla_prep",
    )(cols, gcq, wuq, gckv, wukv, gq, gqs, gk, gks, cos, sin)


ATTN_TILES_PER_STEP = 4


def _attn_kernel(q_ref, kc_ref, kl_ref, vc_ref, vl_ref, o_ref):
    tq = q_ref.shape[2] // ATTN_TILES_PER_STEP
    nc = kc_ref.shape[2]
    kc, kl, vc, vl = kc_ref[0, 0], kl_ref[0, 0], vc_ref[0, 0], vl_ref[0, 0]
    nt = (((1,), (1,)), ((), ()))
    for j in range(ATTN_TILES_PER_STEP):
        q = q_ref[0, 0, j * tq:(j + 1) * tq, :]
        s = jnp.concatenate([lax.dot_general(q, kc, nt, preferred_element_type=F32),
                             lax.dot_general(q, kl, nt, preferred_element_type=F32)], axis=1)
        m = jnp.max(s, axis=-1, keepdims=True)
        p = jnp.exp(s - m)
        l = jnp.sum(p, axis=-1, keepdims=True)
        pb = p.astype(BF16)
        o = (jnp.dot(pb[:, :nc], vc, preferred_element_type=F32)
             + jnp.dot(pb[:, nc:], vl, preferred_element_type=F32))
        o_ref[0, j * tq:(j + 1) * tq, :] = (o / l).astype(BF16)


def _attention(q, k_c, k_l, v_c, v_l, tq):
    b, h, n, dk = q.shape
    nc = k_c.shape[2]
    dv = v_l.shape[3]
    tb = tq * ATTN_TILES_PER_STEP
    whole = lambda nn, dd: pl.BlockSpec((1, 1, nn, dd), lambda i, j, t: (i, j, 0, 0))
    return pl.pallas_call(
        _attn_kernel,
        out_shape=jax.ShapeDtypeStruct((b, n, h * dv), BF16),
        grid=(b, h, n // tb),
        in_specs=[pl.BlockSpec((1, 1, tb, dk), lambda i, j, t: (i, j, t, 0)),
                  whole(nc, dk), whole(n, dk), whole(nc, dv), whole(n, dv)],
        out_specs=pl.BlockSpec((1, tb, dv), lambda i, j, t: (i, t, j)),
        compiler_params=_params(("parallel", "parallel", "parallel")),
        name="attention",
    )(q, k_c, k_l, v_c, v_l)


def _ml_prep_kernel(qk_ref, prev_ref, next_ref, vin_ref, g_ref, gt_ref, cw_ref, bc_ref, br_ref,
                    q_ref, kt_ref, v_ref, a_ref, bm_ref, brow_ref):
    tn = qk_ref.shape[1]
    j = pl.program_id(1)
    nj = pl.num_programs(1)
    u = qk_ref[0]
    row = lax.broadcasted_iota(I32, (tn, 1), 0)
    before = jnp.where(j == 0, 0.0, prev_ref[0, SUBLANES - 1:SUBLANES, :])
    after = jnp.where(j == nj - 1, 0.0, next_ref[0, 0:1, :])
    up = jnp.where(row == 0, before, pltpu.roll(u, 1, 0))
    un = jnp.where(row == tn - 1, after, pltpu.roll(u, tn - 1, 0))
    cw = cw_ref[...]
    y = up * cw[0:1, :] + u * cw[1:2, :] + un * cw[2:3, :]
    y = y * jax.nn.sigmoid(y)
    hq = ML_HEADS * ML_QK
    for h in range(ML_HEADS):
        q_ref[0, h] = (y[:, h * ML_QK:(h + 1) * ML_QK] * (ML_QK ** -0.5)).astype(BF16)
    kt = y[:, hq:2 * hq].T
    L = ML_CHUNK
    for c in range(tn // L):
        kt_ref[0, c] = kt[:, c * L:(c + 1) * L].astype(BF16)
    v_ref[0] = vin_ref[0].astype(BF16)
    g = g_ref[0][:, 0:16] + bc_ref[...]
    ig_c = g[:, 0:8]
    lf_c = jax.nn.log_sigmoid(g[:, 8:16])
    gt = gt_ref[0] + br_ref[...]
    ig_r = gt[0:8, :]
    lf_r = jax.nn.log_sigmoid(gt[8:16, :])
    ti = lax.broadcasted_iota(I32, (L, L), 0)
    si = lax.broadcasted_iota(I32, (L, L), 1)
    lower = (si <= ti).astype(F32)
    upper = (si >= ti).astype(F32)
    lane_fwd = lax.broadcasted_iota(I32, (L, 8), 1) < ML_HEADS
    row_l = lax.broadcasted_iota(I32, (L, 8), 0)
    row_fwd = lax.broadcasted_iota(I32, (8, L), 0) < ML_HEADS
    for c in range(tn // L):
        lo = c * L
        lfc = lf_c[lo:lo + L, :]
        cf_c = jnp.where(
            lane_fwd,
            jnp.dot(lower, lfc, precision=HIGHEST, preferred_element_type=F32),
            jnp.dot(upper, lfc, precision=HIGHEST, preferred_element_type=F32))
        a_ref[0, lo:lo + L, :] = cf_c
        pre = suf = ig_c[lo:lo + L, :] - cf_c
        d = 1
        while d < L:
            pre = jnp.maximum(pre, jnp.where(row_l >= d, pltpu.roll(pre, d, 0), -jnp.inf))
            suf = jnp.maximum(suf, jnp.where(row_l < L - d, pltpu.roll(suf, L - d, 0), -jnp.inf))
            d *= 2
        bm_ref[0, lo:lo + L, :] = jnp.where(lane_fwd, pre, suf)
        lfr = lf_r[:, lo:lo + L]
        cf_r = jnp.where(
            row_fwd,
            jnp.dot(lfr, upper, precision=HIGHEST, preferred_element_type=F32),
            jnp.dot(lfr, lower, precision=HIGHEST, preferred_element_type=F32))
        brow_ref[0, c] = ig_r[:, lo:lo + L] - cf_r


def _ml_prep(cols, gates_t, cw, bias_col, bias_row, tn):
    b, n, _ = cols.shape
    r8 = tn // SUBLANES
    last8 = n // SUBLANES - 1
    L = ML_CHUNK
    hq = ML_HEADS * ML_QK
    return pl.pallas_call(
        _ml_prep_kernel,
        out_shape=(jax.ShapeDtypeStruct((b, ML_HEADS, n, ML_QK), BF16),
                   jax.ShapeDtypeStruct((b, n // L, hq, L), BF16),
                   jax.ShapeDtypeStruct((b, n, ML_HEADS * ML_V), BF16),
                   jax.ShapeDtypeStruct((b, n, 8), F32),
                   jax.ShapeDtypeStruct((b, n, 8), F32),
                   jax.ShapeDtypeStruct((b, n // L, 8, L), F32)),
        grid=(b, n // tn),
        in_specs=[pl.BlockSpec((1, tn, 512), lambda i, j: (i, j, C_QK0 // 512)),
                  pl.BlockSpec((1, SUBLANES, 512), lambda i, j: (i, jnp.maximum(j * r8 - 1, 0), C_QK0 // 512)),
                  pl.BlockSpec((1, SUBLANES, 512), lambda i, j: (i, jnp.minimum((j + 1) * r8, last8), C_QK0 // 512)),
                  pl.BlockSpec((1, tn, 512), lambda i, j: (i, j, C_V0 // 512)),
                  pl.BlockSpec((1, tn, 128), lambda i, j: (i, j, C_G0 // 128)),
                  pl.BlockSpec((1, 16, tn), lambda i, j: (i, 0, j)),
                  pl.BlockSpec((3, 512), lambda i, j: (0, 0)),
                  pl.BlockSpec((1, 16), lambda i, j: (0, 0)),
                  pl.BlockSpec((16, 1), lambda i, j: (0, 0))],
        out_specs=(pl.BlockSpec((1, ML_HEADS, tn, ML_QK), lambda i, j: (i, 0, j, 0)),
                   pl.BlockSpec((1, tn // L, hq, L), lambda i, j: (i, j, 0, 0)),
                   pl.BlockSpec((1, tn, ML_HEADS * ML_V), lambda i, j: (i, j, 0)),
                   pl.BlockSpec((1, tn, 8), lambda i, j: (i, j, 0)),
                   pl.BlockSpec((1, tn, 8), lambda i, j: (i, j, 0)),
                   pl.BlockSpec((1, tn // L, 8, L), lambda i, j: (i, j, 0, 0))),
        compiler_params=_params(("parallel", "parallel")),
        name="ml_prep",
    )(cols, cols, cols, cols, cols, gates_t, cw, bias_col, bias_row)


def _ml_chunk(qb, kt, vaug, a_c, bm_c, b_r, ct, m_prev, fwd, need_out):
    L = qb.shape[0]
    ti = lax.broadcasted_iota(I32, (L, L), 0)
    si = lax.broadcasted_iota(I32, (L, L), 1)
    mask = (si <= ti) if fwd else (si >= ti)
    a_b = jnp.broadcast_to(a_c, (L, LANES))
    inter = a_b + m_prev
    m_t = jnp.maximum(inter, a_b + jnp.broadcast_to(bm_c, (L, LANES)))
    last = L - 1 if fwd else 0
    m_new = m_t[last:last + 1, :]
    a_last = a_b[last:last + 1, :]
    w_r = jnp.exp(a_last[:, :L] + b_r - m_new[:, :L])
    decay = jnp.exp(a_last + m_prev - m_new)
    decay = jnp.concatenate([decay, decay], axis=1)
    kw = (kt.astype(F32) * w_r).astype(BF16)
    if not need_out:
        return decay * ct + jnp.dot(kw, vaug, preferred_element_type=F32), m_new, None
    s = jnp.dot(qb, kt, preferred_element_type=F32)
    dmat = jnp.where(mask, a_b[:, :L] + b_r, -jnp.inf)
    wmat = jnp.exp(dmat - m_t[:, :L]) * s
    sc = jnp.exp(inter - m_t)
    top = jnp.concatenate([wmat.astype(BF16), (sc[:, :qb.shape[1]] * qb.astype(F32)).astype(BF16)], axis=1)
    bot = jnp.concatenate([kw, jnp.zeros(kw.shape, BF16)], axis=1)
    res = jnp.dot(jnp.concatenate([top, bot], axis=0), jnp.concatenate([vaug, ct.astype(BF16)], axis=0),
                  preferred_element_type=F32)
    num = res[:L, 0:ML_V]
    den = res[:L, ML_V:2 * ML_V]
    return decay * ct + res[L:], m_new, num / jnp.maximum(jnp.abs(den), jnp.exp(-m_t))


def _ml_scan_kernel(ql_ref, ktl_ref, vl_ref, al_ref, bml_ref, brl_ref,
                    qc_ref, ktc_ref, vc_ref, ac_ref, bmc_ref, brc_ref, out_ref, st_ref, m_ref):
    L = ML_CHUNK
    ncl = ql_ref.shape[2] // L
    ncc = qc_ref.shape[2] // L
    ones_blk = jnp.ones((L, ML_V), BF16)
    st_ref[...] = jnp.zeros(st_ref.shape, F32)
    m_ref[...] = jnp.zeros(m_ref.shape, F32)

    def step(refs, c, d, fwd, need_out):
        q_ref, kt_ref, v_ref, a_ref, bm_ref, br_ref = refs
        sl = pl.ds(pl.multiple_of(c * L, L), L)
        a_all = a_ref[0, sl, :]
        bm_all = bm_ref[0, sl, :]
        br_all = br_ref[0, c]
        v_all = v_ref[0, sl, :]
        kt_all = kt_ref[0, c]
        hs = []
        for h in range(ML_HEADS):
            j = d * ML_HEADS + h
            vaug = jnp.concatenate([v_all[:, h * ML_V:(h + 1) * ML_V], ones_blk], axis=1)
            ct_new, m_new, hh = _ml_chunk(
                q_ref[0, h, sl, :], kt_all[h * ML_QK:(h + 1) * ML_QK, :], vaug,
                a_all[:, j:j + 1], bm_all[:, j:j + 1], br_all[j:j + 1, :], st_ref[j], m_ref[j][0:1, :],
                fwd, need_out)
            st_ref[j] = ct_new
            m_ref[j] = jnp.broadcast_to(m_new, (SUBLANES, LANES))
            hs.append(hh)
        return hs, sl

    ctx_refs = (qc_ref, ktc_ref, vc_ref, ac_ref, bmc_ref, brc_ref)
    lat_refs = (ql_ref, ktl_ref, vl_ref, al_ref, bml_ref, brl_ref)

    out_ref[...] = jnp.zeros(out_ref.shape, F32)

    def ctx_body(i, carry):
        step(ctx_refs, i, 0, True, False)
        step(ctx_refs, ncc - 1 - i, 1, False, False)
        return carry

    def lat_body(i, carry):
        for d, fwd in ((0, True), (1, False)):
            hs, sl = step(lat_refs, i if fwd else ncl - 1 - i, d, fwd, True)
            out_ref[0, sl, :] = out_ref[0, sl, :] + jnp.concatenate(hs, axis=1)
        return carry

    lax.fori_loop(0, ncc, ctx_body, 0)
    lax.fori_loop(0, ncl, lat_body, 0)


def _ml_scan(ql, ktl, vl, al, bml, brl, qc, ktc, vc, ac, bmc, brc):
    b, _, n, _ = ql.shape
    nctx = qc.shape[2]
    L = ML_CHUNK
    hv = ML_HEADS * ML_V
    hq = ML_HEADS * ML_QK
    qspec = lambda nn: pl.BlockSpec((1, ML_HEADS, nn, ML_QK), lambda i: (i, 0, 0, 0))
    ktspec = lambda nn: pl.BlockSpec((1, nn // L, hq, L), lambda i: (i, 0, 0, 0))
    vspec = lambda nn: pl.BlockSpec((1, nn, hv), lambda i: (i, 0, 0))
    aspec = lambda nn: pl.BlockSpec((1, nn, 8), lambda i: (i, 0, 0))
    rspec = lambda nn: pl.BlockSpec((1, nn // L, 8, L), lambda i: (i, 0, 0, 0))
    return pl.pallas_call(
        _ml_scan_kernel,
        out_shape=jax.ShapeDtypeStruct((b, n, hv), F32),
        grid=(b,),
        in_specs=[qspec(n), ktspec(n), vspec(n), aspec(n), aspec(n), rspec(n),
                  qspec(nctx), ktspec(nctx), vspec(nctx), aspec(nctx), aspec(nctx), rspec(nctx)],
        out_specs=pl.BlockSpec((1, n, hv), lambda i: (i, 0, 0)),
        scratch_shapes=[pltpu.VMEM((2 * ML_HEADS, ML_QK, 2 * ML_V), F32),
                        pltpu.VMEM((2 * ML_HEADS, SUBLANES, LANES), F32)],
        compiler_params=_params(("parallel",)),
        name="ml_scan",
    )(ql, ktl, vl, al, bml, brl, qc, ktc, vc, ac, bmc, brc)


def _outproj_kernel(x_ref, mla_ref, hs_ref, o_ref, gm_ref, wa_ref, wb_ref, gt_ref, g2_ref, sc_ref, sh_ref,
                    wpq_ref, sk_ref, x1_ref, h2_ref, s_ref):
    hs = hs_ref[0]
    gm = gm_ref[...]
    hn = jnp.concatenate([_rms(hs[:, h * ML_V:(h + 1) * ML_V]) * gm[:, h * ML_V:(h + 1) * ML_V]
                          for h in range(ML_HEADS)], axis=1)
    ml = (jax.nn.sigmoid(o_ref[0]) * hn).astype(BF16)
    mix = (jnp.dot(mla_ref[0], wa_ref[...], preferred_element_type=F32)
           + jnp.dot(ml, wb_ref[...], preferred_element_type=F32))
    x1 = x_ref[0] + gt_ref[0] * mix
    x1_ref[0] = x1
    h2 = _rms(x1) * g2_ref[...] * (1.0 + sc_ref[0]) + sh_ref[0]
    h2_ref[0] = h2
    qp = jnp.dot(h2.astype(BF16), wpq_ref[...], preferred_element_type=F32).astype(BF16)
    for hp in range(2 * PEER_HEADS):
        s_ref[0, hp] = lax.dot_general(sk_ref[hp], qp[:, hp * N_KEYS:(hp + 1) * N_KEYS],
                                       (((1,), (1,)), ((), ())), preferred_element_type=F32)


def _outproj(x, mla, hsum, cols, gm, wa, wb, gt1, g2, sc2, sh2, wpq, sk, tm):
    b, n, d = x.shape
    hw = mla.shape[2]
    nq = wpq.shape[1]
    mod = pl.BlockSpec((1, 1, d), lambda i, j: (i, 0, 0))
    return pl.pallas_call(
        _outproj_kernel,
        out_shape=(jax.ShapeDtypeStruct((b, n, d), F32), jax.ShapeDtypeStruct((b, n, d), F32),
                   jax.ShapeDtypeStruct((b, 2 * PEER_HEADS, N_KEYS, n), F32)),
        grid=(b, n // tm),
        in_specs=[pl.BlockSpec((1, tm, d), lambda i, j: (i, j, 0)),
                  pl.BlockSpec((1, tm, hw), lambda i, j: (i, j, 0)),
                  pl.BlockSpec((1, tm, hw), lambda i, j: (i, j, 0)),
                  pl.BlockSpec((1, tm, 512), lambda i, j: (i, j, C_O0 // 512)),
                  pl.BlockSpec((1, hw), lambda i, j: (0, 0)),
                  pl.BlockSpec((hw, d), lambda i, j: (0, 0)),
                  pl.BlockSpec((hw, d), lambda i, j: (0, 0)),
                  mod, pl.BlockSpec((1, d), lambda i, j: (0, 0)), mod, mod,
                  pl.BlockSpec((d, nq), lambda i, j: (0, 0)),
                  pl.BlockSpec((2 * PEER_HEADS, N_KEYS, N_KEYS), lambda i, j: (0, 0, 0))],
        out_specs=(pl.BlockSpec((1, tm, d), lambda i, j: (i, j, 0)),
                   pl.BlockSpec((1, tm, d), lambda i, j: (i, j, 0)),
                   pl.BlockSpec((1, 2 * PEER_HEADS, N_KEYS, tm), lambda i, j: (i, 0, 0, j))),
        compiler_params=_params(("parallel", "parallel")),
        name="outproj",
    )(x, mla, hsum, cols, gm, wa, wb, gt1, g2, sc2, sh2, wpq, sk)


def _topk_rows(work_refs, rid_ref, rows, k):
    t = rid_ref.shape[1]
    kid = lax.broadcasted_iota(I32, (k, t), 0)
    big = jnp.float32(3.0e38)

    def body(r, carry):
        out = []
        for work_ref, (vals, idxs) in zip(work_refs, carry):
            s = work_ref[0:rows, :]
            rid = rid_ref[0:rows, :]
            m = jnp.max(s, axis=0, keepdims=True)
            i = jnp.min(jnp.where(s == m, rid, big), axis=0, keepdims=True)
            work_ref[0:rows, :] = jnp.where(rid == i, -jnp.inf, s)
            out.append((jnp.where(kid == r, m, vals), jnp.where(kid == r, i, idxs)))
        return tuple(out)

    init = tuple((jnp.zeros((k, t), F32), jnp.zeros((k, t), F32)) for _ in work_refs)
    return lax.fori_loop(0, k, body, init)


def _pair_candidates(sv0, sv1):
    K, t = sv0.shape
    h = K // 2
    iid = lax.broadcasted_iota(I32, (h, t), 0)
    vals = [sv0[0:h] + sv1[0:1], sv0[h:K] + sv1[0:1]]
    ids = [iid * K, (iid + h) * K]
    for j in range(1, h):
        vals.append(sv0[0:h] + sv1[j:j + 1])
        ids.append(iid * K + j)
    vals.append(sv0[0:1] + sv1[h:K])
    ids.append(iid + h)
    return jnp.concatenate(vals, axis=0), jnp.concatenate(ids, axis=0).astype(F32)


HEADS_PER_ROUND = 4


def _route_kernel(s_ref, off_ref, par_ref, gw_ref, et_ref, gt_ref, work_ref, rid_ref, cid_ref):
    K = PEER_TOPK
    nk, tt = rid_ref.shape
    rid_ref[...] = lax.broadcasted_iota(I32, (nk, tt), 0).astype(F32)
    hr = HEADS_PER_ROUND

    def round_(g, carry):
        heads = [g * hr + a for a in range(hr)]
        for a, h in enumerate(heads):
            work_ref[2 * a] = s_ref[0, 2 * h]
            work_ref[2 * a + 1] = s_ref[0, 2 * h + 1]
        first = _topk_rows([work_ref.at[b] for b in range(2 * hr)], rid_ref, nk, K)
        ncand = N_PAIR_CANDIDATES
        for a in range(hr):
            cand, cid = _pair_candidates(first[2 * a][0], first[2 * a + 1][0])
            work_ref[a, 0:ncand, :] = cand
            cid_ref[...] = cid
        second = _topk_rows([work_ref.at[a] for a in range(hr)], cid_ref, ncand, K)
        for a, h in enumerate(heads):
            best, pos = second[a]
            pos = pos.astype(I32)
            si0 = first[2 * a][1].astype(I32)
            si1 = first[2 * a + 1][1].astype(I32)
            isel = pos >> 4
            jsel = pos & (K - 1)
            e0 = jnp.zeros_like(pos)
            e1 = jnp.zeros_like(pos)
            for i in range(K):
                e0 = jnp.where(isel == i, si0[i:i + 1, :], e0)
                e1 = jnp.where(jsel == i, si1[i:i + 1, :], e1)
            ex = jnp.exp(best - best[0:1, :])
            sl = pl.ds(pl.multiple_of(h * K, K), K)
            et_ref[sl, :] = e0 * N_KEYS + e1
            gt_ref[sl, :] = ex / jnp.sum(ex, axis=0, keepdims=True)
        return carry

    lax.fori_loop(0, PEER_HEADS // hr, round_, 0)
    e = et_ref[...].T
    off_ref[0] = (e >> 1) * SUBLANES
    par_ref[0] = (e & 1).astype(F32)
    gw_ref[0] = gt_ref[...].T


def _route(scores, tt):
    b, hp, nk, n = scores.shape
    nsel = PEER_HEADS * PEER_TOPK
    ospec = pl.BlockSpec((1, tt, nsel), lambda i, j: (i, j, 0))
    return pl.pallas_call(
        _route_kernel,
        out_shape=(jax.ShapeDtypeStruct((b, n, nsel), I32), jax.ShapeDtypeStruct((b, n, nsel), F32),
                   jax.ShapeDtypeStruct((b, n, nsel), F32)),
        grid=(b, n // tt),
        in_specs=[pl.BlockSpec((1, hp, nk, tt), lambda i, j: (i, 0, 0, j))],
        out_specs=(ospec, ospec, ospec),
        scratch_shapes=[pltpu.VMEM((nsel, tt), I32), pltpu.VMEM((nsel, tt), F32),
                        pltpu.VMEM((2 * HEADS_PER_ROUND, nk, tt), F32), pltpu.VMEM((nk, tt), F32),
                        pltpu.VMEM((N_PAIR_CANDIDATES, tt), F32)],
        compiler_params=_params(("parallel", "parallel")),
        name="route",
    )(scores)


TILE_ROWS = 2 * SUBLANES
TOKENS_PER_ITER = 32


CHUNK_SEL = 16


def _chunk_tiles(tab_ref, row_ref, c):
    return jnp.concatenate(
        [pltpu.bitcast(tab_ref[pl.ds(pl.multiple_of(row_ref[c * CHUNK_SEL + kk], SUBLANES), SUBLANES), :], BF16)
         for kk in range(CHUNK_SEL)], axis=0)


def _init_spread(spread_ref, gather_ref=None):
    @pl.when(pl.program_id(0) == 0)
    def _():
        nsel, width = spread_ref.shape
        spread_ref[...] = (lax.broadcasted_iota(I32, (nsel, width), 1) // TILE_ROWS
                           == lax.broadcasted_iota(I32, (nsel, width), 0)).astype(BF16)
        if gather_ref is not None:
            gather_ref[...] = (lax.broadcasted_iota(I32, (width, nsel), 0) // TILE_ROWS
                               == lax.broadcasted_iota(I32, (width, nsel), 1)).astype(BF16)


def _row_masks(width):
    half = (lax.broadcasted_iota(I32, (1, width), 1) % 2).astype(F32)
    lane8 = lax.broadcasted_iota(I32, (SUBLANES, width), 1)
    diag = ((lane8 % TILE_ROWS) // 2 == lax.broadcasted_iota(I32, (SUBLANES, width), 0)).astype(F32)
    return half, diag


def _row_to_tile(row):
    return jnp.concatenate([row[:, s * LANES:(s + 1) * LANES] for s in range(SUBLANES)], axis=0)


def _split_bf16(v):
    hi = v.astype(BF16)
    return hi, (v - hi.astype(F32)).astype(BF16)


def _peer_u_kernel(off_ref, h_ref, tab_ref, par_ref, g_ref, c_ref, r_ref, spread_ref, gather_ref):
    tb, nsel = g_ref.shape
    _init_spread(spread_ref, gather_ref)
    half, diag = _row_masks(spread_ref.shape[1])

    cw = CHUNK_SEL * TILE_ROWS
    diag_c = diag[:, :cw]
    nt = (((1,), (1,)), ((), ()))

    def tok(i, carry):
        ts = [i * TOKENS_PER_ITER + u for u in range(TOKENS_PER_ITER)]
        rows = [off_ref.at[t] for t in ts]
        xs = [jnp.concatenate(_split_bf16(_row_to_tile(h_ref[pl.ds(t, 1), :])), axis=0) for t in ts]
        for c in range(nsel // CHUNK_SEL):
            for u, t in enumerate(ts):
                res = lax.dot_general(xs[u], _chunk_tiles(tab_ref, rows[u], c), nt, preferred_element_type=F32)
                r_ref[pl.ds(t, 1), c * cw:(c + 1) * cw] = jnp.sum(
                    (res[:SUBLANES] + res[SUBLANES:]) * diag_c, axis=0, keepdims=True)
        return carry

    lax.fori_loop(0, tb // TOKENS_PER_ITER, tok, 0)
    mine = jnp.dot(par_ref[...].astype(BF16), spread_ref[...], preferred_element_type=F32) == half
    r_hi, r_lo = _split_bf16(jnp.where(mine, r_ref[...], 0.0))
    both = jnp.dot(jnp.concatenate([r_hi, r_lo], axis=0), gather_ref[...], preferred_element_type=F32)
    pre = both[:tb] + both[tb:]
    c_ref[...] = g_ref[...] * (0.5 * pre * (1.0 + lax.erf(pre * (2.0 ** -0.5))))


def _smem_block(tb, nsel):
    return pl.BlockSpec((tb, nsel), lambda i: (i, 0), memory_space=pltpu.SMEM)


def _peer_u(off, h2, tab, par, gw, tb):
    t, nsel = gw.shape
    vspec = pl.BlockSpec((tb, nsel), lambda i: (i, 0))
    return pl.pallas_call(
        _peer_u_kernel,
        out_shape=jax.ShapeDtypeStruct((t, nsel), F32),
        grid=(t // tb,),
        in_specs=[_smem_block(tb, nsel),
                  pl.BlockSpec((tb, h2.shape[1]), lambda i: (i, 0)),
                  pl.BlockSpec(tab.shape, lambda i: (0, 0), pipeline_mode=pl.Buffered(1)),
                  vspec, vspec],
        out_specs=vspec,
        scratch_shapes=[pltpu.VMEM((tb, nsel * TILE_ROWS), F32),
                        pltpu.VMEM((nsel, nsel * TILE_ROWS), BF16), pltpu.VMEM((nsel * TILE_ROWS, nsel), BF16)],
        compiler_params=_params(("arbitrary",)),
        name="peer_u",
    )(off, h2, tab, par, gw)


def _peer_v_kernel(off_ref, c_ref, par_ref, tab_ref, x1_ref, gt_ref, o_ref, lhi_ref, llo_ref, spread_ref):
    tb, nsel = c_ref.shape
    _init_spread(spread_ref)
    half, diag = _row_masks(spread_ref.shape[1])
    c_hi, c_lo = _split_bf16(c_ref[...])
    ex = jnp.dot(jnp.concatenate([par_ref[...].astype(BF16), c_hi, c_lo], axis=0), spread_ref[...],
                 preferred_element_type=F32)
    mine = ex[:tb] == half
    lhi_ref[...] = jnp.where(mine, ex[tb:2 * tb], 0.0)
    llo_ref[...] = jnp.where(mine, ex[2 * tb:], 0.0)
    gt = gt_ref[0]

    cw = CHUNK_SEL * TILE_ROWS
    diag_c = diag[:, :cw]

    def tok(i, carry):
        ts = [i * TOKENS_PER_ITER + u for u in range(TOKENS_PER_ITER)]
        rows = [off_ref.at[t] for t in ts]
        accs = [jnp.zeros((TILE_ROWS, LANES), F32) for _ in ts]
        for c in range(nsel // CHUNK_SEL):
            cols = slice(c * cw, (c + 1) * cw)
            for u, t in enumerate(ts):
                left = jnp.concatenate([lhi_ref[pl.ds(t, 1), cols] * diag_c, llo_ref[pl.ds(t, 1), cols] * diag_c],
                                       axis=0).astype(BF16)
                accs[u] = accs[u] + jnp.dot(left, _chunk_tiles(tab_ref, rows[u], c), preferred_element_type=F32)
        for u, t in enumerate(ts):
            y = accs[u][:SUBLANES] + accs[u][SUBLANES:]
            y_row = jnp.concatenate([y[s:s + 1, :] for s in range(SUBLANES)], axis=1)
            o_ref[pl.ds(t, 1), :] = x1_ref[pl.ds(t, 1), :] + gt * y_row
        return carry

    lax.fori_loop(0, tb // TOKENS_PER_ITER, tok, 0)


def _peer_v(off, coef, par, tab, x1, gt2, tb, tok_per_batch):
    t, nsel = coef.shape
    d = x1.shape[1]
    bpb = tok_per_batch // tb
    vspec = pl.BlockSpec((tb, nsel), lambda i: (i, 0))
    return pl.pallas_call(
        _peer_v_kernel,
        out_shape=jax.ShapeDtypeStruct(x1.shape, F32),
        grid=(t // tb,),
        in_specs=[_smem_block(tb, nsel), vspec, vspec,
                  pl.BlockSpec(tab.shape, lambda i: (0, 0), pipeline_mode=pl.Buffered(1)),
                  pl.BlockSpec((tb, d), lambda i: (i, 0)),
                  pl.BlockSpec((1, 1, d), lambda i: (i // bpb, 0, 0))],
        out_specs=pl.BlockSpec((tb, d), lambda i: (i, 0)),
        scratch_shapes=[pltpu.VMEM((tb, nsel * TILE_ROWS), F32), pltpu.VMEM((tb, nsel * TILE_ROWS), F32),
                        pltpu.VMEM((nsel, nsel * TILE_ROWS), BF16)],
        compiler_params=_params(("arbitrary",)),
        name="peer_v",
    )(off, coef, par, tab, x1, gt2)


def _rope_tables(n):
    axis = MLA_ROPE // 2
    t = jnp.arange(n, dtype=F32)
    row = jnp.floor(t / GRID_W)
    col = t - row * GRID_W
    inv = ROPE_BASE ** (-jnp.arange(axis // 2, dtype=F32) * (2.0 / axis))
    ar = row[:, None] * inv
    ac = col[:, None] * inv
    cos = jnp.concatenate([jnp.cos(ar), jnp.cos(ar), jnp.cos(ac), jnp.cos(ac)], axis=1)
    sin = jnp.concatenate([-jnp.sin(ar), jnp.sin(ar), -jnp.sin(ac), jnp.sin(ac)], axis=1)
    return cos, sin


PACK_ROWS = 256


def _pack_kernel(t_ref, o_ref):
    n = t_ref.shape[0]
    rows = n // 2
    xb = t_ref[...].astype(BF16)
    col = lax.broadcasted_iota(I32, (rows, n), 1)
    row2 = 2 * lax.broadcasted_iota(I32, (rows, n), 0)
    even = pltpu.bitcast(jnp.dot((col == row2).astype(BF16), xb, preferred_element_type=F32), jnp.uint32)
    odd = pltpu.bitcast(jnp.dot((col == row2 + 1).astype(BF16), xb, preferred_element_type=F32), jnp.uint32)
    word = odd | (even >> 16)
    for s in range(SUBLANES):
        o_ref[pl.ds(s, rows, stride=SUBLANES), :] = word[:, s * LANES:(s + 1) * LANES]


def _pack_table(tab):
    e, d = tab.shape
    out_rows = PACK_ROWS // 2 * (d // LANES)
    return pl.pallas_call(
        _pack_kernel,
        out_shape=jax.ShapeDtypeStruct((e // 2 * (d // LANES), LANES), jnp.uint32),
        grid=(e // PACK_ROWS,),
        in_specs=[pl.BlockSpec((PACK_ROWS, d), lambda i: (i, 0))],
        out_specs=pl.BlockSpec((out_rows, LANES), lambda i: (i, 0)),
        compiler_params=_params(("parallel",)),
        name="pack_table",
    )(tab)


def _block(n, want):
    return want if n % want == 0 else n


def kernel(x, c, ctx, c_ctx, w_ada, b_ada, g_norm1, w_in, g_cq, w_uq, g_ckv, w_ukv, g_qn, g_kn, conv_qk, b_igate, b_fgate, g_mlstm, w_out, g_norm2, w_pq, sub_keys, expert_u, expert_v):
    B, N, D = x.shape
    NC = ctx.shape[1]
    assert w_ada.shape[0] == 1, "one layer"
    assert N % (ATTN_TQ * ATTN_TILES_PER_STEP) == 0 and NC % ML_CHUNK == 0 and D == SUBLANES * LANES
    q_rank = g_cq.shape[1]
    kv_rank = g_ckv.shape[1]
    mla_cols = q_rank + kv_rank + MLA_ROPE
    assert (q_rank, kv_rank) == (C_KV0 - C_Q0, C_KR0 - C_KV0)
    swap = jnp.arange(MLA_ROPE) ^ (MLA_ROPE // 4)

    cc = jnp.concatenate([c, c_ctx[None, :], jnp.zeros((16 - B - 1, D), F32)], axis=0)
    mod = _ada(cc, w_ada[0].astype(BF16), b_ada)
    sh1, sc1, gt1, sh2, sc2, gt2 = [mod[:, i * D:(i + 1) * D] for i in range(6)]
    lat = lambda m: m[:B].reshape(B, 1, D)
    ctxm = lambda m: jnp.broadcast_to(m[B:B + 1].reshape(1, 1, D), (B, 1, D))

    wi = w_in[0]
    n_qk = 2 * ML_HEADS * ML_QK
    n_v = ML_HEADS * ML_V
    n_g = 4 * ML_HEADS
    m0 = mla_cols
    w_cols = jnp.concatenate([
        wi[:, 0:mla_cols],
        wi[:, q_rank + kv_rank + swap],
        wi[:, m0:m0 + n_qk + 2 * n_v + n_g],
        jnp.zeros((D, C_END - C_G0 - n_g), F32)], axis=1).astype(BF16)
    assert w_cols.shape[1] == C_END
    w_gates_t = wi[:, m0 + n_qk + 2 * n_v:m0 + n_qk + 2 * n_v + n_g].T.astype(BF16)
    cols_l, gates_l = _inproj(x, g_norm1, lat(sc1), lat(sh1), w_cols, w_gates_t, _block(N, ROW_TILE))
    cols_c, gates_c = _inproj(ctx, g_norm1, ctxm(sc1), ctxm(sh1), w_cols, w_gates_t, _block(NC, ROW_TILE))

    wq = w_uq[0].reshape(q_rank, MLA_HEADS, MLA_QK)
    wuq = jnp.concatenate([wq, wq[:, :, MLA_NOPE + swap]], axis=2).reshape(q_rank, MLA_HEADS * 256).astype(BF16)
    wukv = w_ukv[0].astype(BF16)
    gqs, gks = g_qn[:, MLA_NOPE + swap], g_kn[:, MLA_NOPE + swap]
    cos_l, sin_l = _rope_tables(N)
    cos_c, sin_c = jnp.ones((NC, MLA_ROPE), F32), jnp.zeros((NC, MLA_ROPE), F32)
    q_l, k_l, v_l = _mla_prep(cols_l, g_cq, wuq, g_ckv, wukv, g_qn, gqs, g_kn, gks, cos_l, sin_l, _block(N, ROW_TILE))
    _, k_c, v_c = _mla_prep(cols_c, g_cq, wuq, g_ckv, wukv, g_qn, gqs, g_kn, gks, cos_c, sin_c, _block(NC, ROW_TILE))
    mla = _attention(q_l, k_c, k_l, v_c, v_l, ATTN_TQ)

    bias16 = jnp.concatenate([b_igate[0].reshape(-1), b_fgate[0].reshape(-1)])
    cw = conv_qk[0]
    pl_ = _ml_prep(cols_l, gates_l, cw, bias16[None, :], bias16[:, None], _block(N, ROW_TILE))
    pc_ = _ml_prep(cols_c, gates_c, cw, bias16[None, :], bias16[:, None], _block(NC, ROW_TILE))
    hsum = _ml_scan(*pl_, *pc_)

    wo = w_out[0].astype(BF16)
    hw = MLA_HEADS * MLA_V
    sk = sub_keys[0].reshape(2 * PEER_HEADS, N_KEYS, -1).astype(BF16)
    x1, h2, scores = _outproj(x, mla, hsum, cols_l, g_mlstm, wo[:hw], wo[hw:], lat(gt1), g_norm2, lat(sc2), lat(sh2),
                              w_pq[0].astype(BF16), sk, OUT_TILE)
    T = B * N
    nsel = PEER_HEADS * PEER_TOPK
    off, par, gw = [a.reshape(T, nsel) for a in _route(scores, ROUTE_TILE)]

    tb = PEER_TOKENS
    coef = _peer_u(off, h2.reshape(T, D), _pack_table(expert_u[0]), par, gw, tb)
    out = _peer_v(off, coef, par, _pack_table(expert_v[0]), x1.reshape(T, D), lat(gt2), tb, N)
    return out.reshape(B, N, D)
```

```python
import jax
import jax.numpy as jnp
from jax import lax
from jax.experimental import pallas as pl
from jax.experimental.pallas import tpu as pltpu

F32 = jnp.float32
BF16 = jnp.bfloat16
I32 = jnp.int32
EPS = 1e-6

GRID_W = 64
MLA_HEADS = 4
MLA_NOPE = 128
MLA_ROPE = 64
MLA_V = 128
MLA_QK = MLA_NOPE + MLA_ROPE
ROPE_BASE = 10000.0
ML_HEADS = 4
ML_QK = 64
ML_V = 128
ML_CHUNK = 64
PEER_HEADS = 8
N_KEYS = 128
PEER_TOPK = 16
N_PAIR_CANDIDATES = 80

LANES = 128
SUBLANES = 8
VMEM_LIMIT_BYTES = 56 * 1024 * 1024

ROW_TILE = 512
ATTN_TQ = 256
OUT_TILE = 256
ROUTE_TILE = 256
PEER_TOKENS = 128

C_Q0, C_KV0, C_KR0, C_KRS0, C_QK0, C_V0, C_O0, C_G0, C_END = 0, 256, 384, 448, 512, 1024, 1536, 2048, 2176
HIGHEST = lax.Precision.HIGHEST


def _params(sem):
    return pltpu.CompilerParams(dimension_semantics=sem, vmem_limit_bytes=VMEM_LIMIT_BYTES)


def _rms(x):
    return x * lax.rsqrt(jnp.mean(x * x, axis=-1, keepdims=True) + EPS)


def _ada_kernel(c_ref, w_ref, b_ref, o_ref):
    c = c_ref[...]
    s = c * jax.nn.sigmoid(c)
    o_ref[...] = jnp.dot(s.astype(BF16), w_ref[...], preferred_element_type=F32) + b_ref[...]


def _ada(cc, w, b):
    rows, d = cc.shape
    n = w.shape[1]
    bn = n // 4
    return pl.pallas_call(
        _ada_kernel,
        out_shape=jax.ShapeDtypeStruct((rows, n), F32),
        grid=(4,),
        in_specs=[pl.BlockSpec((rows, d), lambda j: (0, 0)),
                  pl.BlockSpec((d, bn), lambda j: (0, j)),
                  pl.BlockSpec((1, bn), lambda j: (0, j))],
        out_specs=pl.BlockSpec((rows, bn), lambda j: (0, j)),
        compiler_params=_params(("arbitrary",)),
        name="ada",
    )(cc, w, b)


def _inproj_kernel(x_ref, g_ref, sc_ref, sh_ref, w_ref, wg_ref, o_ref, gt_ref):
    x = x_ref[0]
    h = _rms(x) * g_ref[...] * (1.0 + sc_ref[0]) + sh_ref[0]
    hb = h.astype(BF16)
    o_ref[0] = jnp.dot(hb, w_ref[...], preferred_element_type=F32)
    gt_ref[0] = lax.dot_general(wg_ref[...], hb, (((1,), (1,)), ((), ())), preferred_element_type=F32)


def _inproj(x, g, sc, sh, w, wg, tm):
    b, n, d = x.shape
    nc = w.shape[1]
    return pl.pallas_call(
        _inproj_kernel,
        out_shape=(jax.ShapeDtypeStruct((b, n, nc), F32), jax.ShapeDtypeStruct((b, 16, n), F32)),
        grid=(b, n // tm),
        in_specs=[pl.BlockSpec((1, tm, d), lambda i, j: (i, j, 0)),
                  pl.BlockSpec((1, d), lambda i, j: (0, 0)),
                  pl.BlockSpec((1, 1, d), lambda i, j: (i, 0, 0)),
                  pl.BlockSpec((1, 1, d), lambda i, j: (i, 0, 0)),
                  pl.BlockSpec((d, nc), lambda i, j: (0, 0)),
                  pl.BlockSpec((16, d), lambda i, j: (0, 0))],
        out_specs=(pl.BlockSpec((1, tm, nc), lambda i, j: (i, j, 0)),
                   pl.BlockSpec((1, 16, tm), lambda i, j: (i, 0, j))),
        compiler_params=_params(("parallel", "parallel")),
        name="inproj",
    )(x, g, sc, sh, w, wg)


def _mla_prep_kernel(c_ref, gcq_ref, wuq_ref, gckv_ref, wukv_ref, gq_ref, gqs_ref, gk_ref, gks_ref,
                     cos_ref, sin_ref, q_ref, k_ref, v_ref):
    c = c_ref[0]
    cq = _rms(c[:, C_Q0:C_KV0]) * gcq_ref[...]
    ckv = _rms(c[:, C_KV0:C_KR0]) * gckv_ref[...]
    kr = c[:, C_KR0:C_KRS0]
    krs = c[:, C_KRS0:C_QK0]
    q_raw = jnp.dot(cq.astype(BF16), wuq_ref[...], preferred_element_type=F32)
    kv_raw = jnp.dot(ckv.astype(BF16), wukv_ref[...], preferred_element_type=F32)
    cos = cos_ref[...]
    sin = sin_ref[...]
    gq = gq_ref[...]
    gk = gk_ref[...]
    kr_ss = jnp.sum(kr * kr, axis=-1, keepdims=True)
    for h in range(MLA_HEADS):
        o = h * 256
        qn = q_raw[:, o:o + 128]
        qr = q_raw[:, o + 128:o + 192]
        qs = q_raw[:, o + 192:o + 256]
        ss = jnp.sum(qn * qn, axis=-1, keepdims=True) + jnp.sum(qr * qr, axis=-1, keepdims=True)
        r = lax.rsqrt(ss * (1.0 / MLA_QK) + EPS) * (MLA_QK ** -0.5)
        q_ref[0, h, :, 0:128] = (qn * r * gq[:, 0:128]).astype(BF16)
        q_ref[0, h, :, 128:192] = ((qr * r * gq[:, 128:192]) * cos + (qs * r * gqs_ref[...]) * sin).astype(BF16)
        kn = kv_raw[:, o:o + 128]
        ss = jnp.sum(kn * kn, axis=-1, keepdims=True) + kr_ss
        r = lax.rsqrt(ss * (1.0 / MLA_QK) + EPS)
        k_ref[0, h, :, 0:128] = (kn * r * gk[:, 0:128]).astype(BF16)
        k_ref[0, h, :, 128:192] = ((kr * r * gk[:, 128:192]) * cos + (krs * r * gks_ref[...]) * sin).astype(BF16)
        v_ref[0, h] = kv_raw[:, o + 128:o + 256].astype(BF16)


def _mla_prep(cols, gcq, wuq, gckv, wukv, gq, gqs, gk, gks, cos, sin, tm):
    b, n, _ = cols.shape
    full = lambda a: pl.BlockSpec(a.shape, lambda i, j: (0,) * a.ndim)
    return pl.pallas_call(
        _mla_prep_kernel,
        out_shape=(jax.ShapeDtypeStruct((b, MLA_HEADS, n, MLA_QK), BF16),
                   jax.ShapeDtypeStruct((b, MLA_HEADS, n, MLA_QK), BF16),
                   jax.ShapeDtypeStruct((b, MLA_HEADS, n, MLA_V), BF16)),
        grid=(b, n // tm),
        in_specs=[pl.BlockSpec((1, tm, 512), lambda i, j: (i, j, 0)),
                  full(gcq), full(wuq), full(gckv), full(wukv), full(gq), full(gqs), full(gk), full(gks),
                  pl.BlockSpec((tm, MLA_ROPE), lambda i, j: (j, 0)),
                  pl.BlockSpec((tm, MLA_ROPE), lambda i, j: (j, 0))],
        out_specs=(pl.BlockSpec((1, MLA_HEADS, tm, MLA_QK), lambda i, j: (i, 0, j, 0)),
                   pl.BlockSpec((1, MLA_HEADS, tm, MLA_QK), lambda i, j: (i, 0, j, 0)),
                   pl.BlockSpec((1, MLA_HEADS, tm, MLA_V), lambda i, j: (i, 0, j, 0))),
        compiler_params=_params(("parallel", "parallel")),
        name="mla_prep",
    )(cols, gcq, wuq, gckv, wukv, gq, gqs, gk, gks, cos, sin)


ATTN_TILES_PER_STEP = 4


def _attn_kernel(q_ref, kc_ref, kl_ref, vc_ref, vl_ref, o_ref):
    tq = q_ref.shape[2] // ATTN_TILES_PER_STEP
    nc = kc_ref.shape[2]
    kc, kl, vc, vl = kc_ref[0, 0], kl_ref[0, 0], vc_ref[0, 0], vl_ref[0, 0]
    nt = (((1,), (1,)), ((), ()))
    for j in range(ATTN_TILES_PER_STEP):
        q = q_ref[0, 0, j * tq:(j + 1) * tq, :]
        s = jnp.concatenate([lax.dot_general(q, kc, nt, preferred_element_type=F32),
                             lax.dot_general(q, kl, nt, preferred_element_type=F32)], axis=1)
        m = jnp.max(s, axis=-1, keepdims=True)
        p = jnp.exp(s - m)
        l = jnp.sum(p, axis=-1, keepdims=True)
        pb = p.astype(BF16)
        o = (jnp.dot(pb[:, :nc], vc, preferred_element_type=F32)
             + jnp.dot(pb[:, nc:], vl, preferred_element_type=F32))
        o_ref[0, j * tq:(j + 1) * tq, :] = (o / l).astype(BF16)


def _attention(q, k_c, k_l, v_c, v_l, tq):
    b, h, n, dk = q.shape
    nc = k_c.shape[2]
    dv = v_l.shape[3]
    tb = tq * ATTN_TILES_PER_STEP
    whole = lambda nn, dd: pl.BlockSpec((1, 1, nn, dd), lambda i, j, t: (i, j, 0, 0))
    return pl.pallas_call(
        _attn_kernel,
        out_shape=jax.ShapeDtypeStruct((b, n, h * dv), BF16),
        grid=(b, h, n // tb),
        in_specs=[pl.BlockSpec((1, 1, tb, dk), lambda i, j, t: (i, j, t, 0)),
                  whole(nc, dk), whole(n, dk), whole(nc, dv), whole(n, dv)],
        out_specs=pl.BlockSpec((1, tb, dv), lambda i, j, t: (i, t, j)),
        compiler_params=_params(("parallel", "parallel", "parallel")),
        name="attention",
    )(q, k_c, k_l, v_c, v_l)


def _ml_prep_kernel(qk_ref, prev_ref, next_ref, vin_ref, g_ref, gt_ref, cw_ref, bc_ref, br_ref,
                    q_ref, kt_ref, v_ref, a_ref, bm_ref, brow_ref):
    tn = qk_ref.shape[1]
    j = pl.program_id(1)
    nj = pl.num_programs(1)
    u = qk_ref[0]
    row = lax.broadcasted_iota(I32, (tn, 1), 0)
    before = jnp.where(j == 0, 0.0, prev_ref[0, SUBLANES - 1:SUBLANES, :])
    after = jnp.where(j == nj - 1, 0.0, next_ref[0, 0:1, :])
    up = jnp.where(row == 0, before, pltpu.roll(u, 1, 0))
    un = jnp.where(row == tn - 1, after, pltpu.roll(u, tn - 1, 0))
    cw = cw_ref[...]
    y = up * cw[0:1, :] + u * cw[1:2, :] + un * cw[2:3, :]
    y = y * jax.nn.sigmoid(y)
    hq = ML_HEADS * ML_QK
    for h in range(ML_HEADS):
        q_ref[0, h] = (y[:, h * ML_QK:(h + 1) * ML_QK] * (ML_QK ** -0.5)).astype(BF16)
    kt = y[:, hq:2 * hq].T
    L = ML_CHUNK
    for c in range(tn // L):
        kt_ref[0, c] = kt[:, c * L:(c + 1) * L].astype(BF16)
    v_ref[0] = vin_ref[0].astype(BF16)
    g = g_ref[0][:, 0:16] + bc_ref[...]
    ig_c = g[:, 0:8]
    lf_c = jax.nn.log_sigmoid(g[:, 8:16])
    gt = gt_ref[0] + br_ref[...]
    ig_r = gt[0:8, :]
    lf_r = jax.nn.log_sigmoid(gt[8:16, :])
    ti = lax.broadcasted_iota(I32, (L, L), 0)
    si = lax.broadcasted_iota(I32, (L, L), 1)
    lower = (si <= ti).astype(F32)
    upper = (si >= ti).astype(F32)
    lane_fwd = lax.broadcasted_iota(I32, (L, 8), 1) < ML_HEADS
    row_l = lax.broadcasted_iota(I32, (L, 8), 0)
    row_fwd = lax.broadcasted_iota(I32, (8, L), 0) < ML_HEADS
    for c in range(tn // L):
        lo = c * L
        lfc = lf_c[lo:lo + L, :]
        cf_c = jnp.where(
            lane_fwd,
            jnp.dot(lower, lfc, precision=HIGHEST, preferred_element_type=F32),
            jnp.dot(upper, lfc, precision=HIGHEST, preferred_element_type=F32))
        a_ref[0, lo:lo + L, :] = cf_c
        pre = suf = ig_c[lo:lo + L, :] - cf_c
        d = 1
        while d < L:
            pre = jnp.maximum(pre, jnp.where(row_l >= d, pltpu.roll(pre, d, 0), -jnp.inf))
            suf = jnp.maximum(suf, jnp.where(row_l < L - d, pltpu.roll(suf, L - d, 0), -jnp.inf))
            d *= 2
        bm_ref[0, lo:lo + L, :] = jnp.where(lane_fwd, pre, suf)
        lfr = lf_r[:, lo:lo + L]
        cf_r = jnp.where(
            row_fwd,
            jnp.dot(lfr, upper, precision=HIGHEST, preferred_element_type=F32),
            jnp.dot(lfr, lower, precision=HIGHEST, preferred_element_type=F32))
        brow_ref[0, c] = ig_r[:, lo:lo + L] - cf_r


def _ml_prep(cols, gates_t, cw, bias_col, bias_row, tn):
    b, n, _ = cols.shape
    r8 = tn // SUBLANES
    last8 = n // SUBLANES - 1
    L = ML_CHUNK
    hq = ML_HEADS * ML_QK
    return pl.pallas_call(
        _ml_prep_kernel,
        out_shape=(jax.ShapeDtypeStruct((b, ML_HEADS, n, ML_QK), BF16),
                   jax.ShapeDtypeStruct((b, n // L, hq, L), BF16),
                   jax.ShapeDtypeStruct((b, n, ML_HEADS * ML_V), BF16),
                   jax.ShapeDtypeStruct((b, n, 8), F32),
                   jax.ShapeDtypeStruct((b, n, 8), F32),
                   jax.ShapeDtypeStruct((b, n // L, 8, L), F32)),
        grid=(b, n // tn),
        in_specs=[pl.BlockSpec((1, tn, 512), lambda i, j: (i, j, C_QK0 // 512)),
                  pl.BlockSpec((1, SUBLANES, 512), lambda i, j: (i, jnp.maximum(j * r8 - 1, 0), C_QK0 // 512)),
                  pl.BlockSpec((1, SUBLANES, 512), lambda i, j: (i, jnp.minimum((j + 1) * r8, last8), C_QK0 // 512)),
                  pl.BlockSpec((1, tn, 512), lambda i, j: (i, j, C_V0 // 512)),
                  pl.BlockSpec((1, tn, 128), lambda i, j: (i, j, C_G0 // 128)),
                  pl.BlockSpec((1, 16, tn), lambda i, j: (i, 0, j)),
                  pl.BlockSpec((3, 512), lambda i, j: (0, 0)),
                  pl.BlockSpec((1, 16), lambda i, j: (0, 0)),
                  pl.BlockSpec((16, 1), lambda i, j: (0, 0))],
        out_specs=(pl.BlockSpec((1, ML_HEADS, tn, ML_QK), lambda i, j: (i, 0, j, 0)),
                   pl.BlockSpec((1, tn // L, hq, L), lambda i, j: (i, j, 0, 0)),
                   pl.BlockSpec((1, tn, ML_HEADS * ML_V), lambda i, j: (i, j, 0)),
                   pl.BlockSpec((1, tn, 8), lambda i, j: (i, j, 0)),
                   pl.BlockSpec((1, tn, 8), lambda i, j: (i, j, 0)),
                   pl.BlockSpec((1, tn // L, 8, L), lambda i, j: (i, j, 0, 0))),
        compiler_params=_params(("parallel", "parallel")),
        name="ml_prep",
    )(cols, cols, cols, cols, cols, gates_t, cw, bias_col, bias_row)


def _ml_chunk(qb, kt, vaug, a_c, bm_c, b_r, ct, m_prev, fwd, need_out):
    L = qb.shape[0]
    ti = lax.broadcasted_iota(I32, (L, L), 0)
    si = lax.broadcasted_iota(I32, (L, L), 1)
    mask = (si <= ti) if fwd else (si >= ti)
    a_b = jnp.broadcast_to(a_c, (L, LANES))
    inter = a_b + m_prev
    m_t = jnp.maximum(inter, a_b + jnp.broadcast_to(bm_c, (L, LANES)))
    last = L - 1 if fwd else 0
    m_new = m_t[last:last + 1, :]
    a_last = a_b[last:last + 1, :]
    w_r = jnp.exp(a_last[:, :L] + b_r - m_new[:, :L])
    decay = jnp.exp(a_last + m_prev - m_new)
    decay = jnp.concatenate([decay, decay], axis=1)
    kw = (kt.astype(F32) * w_r).astype(BF16)
    if not need_out:
        return decay * ct + jnp.dot(kw, vaug, preferred_element_type=F32), m_new, None
    s = jnp.dot(qb, kt, preferred_element_type=F32)
    dmat = jnp.where(mask, a_b[:, :L] + b_r, -jnp.inf)
    wmat = jnp.exp(dmat - m_t[:, :L]) * s
    sc = jnp.exp(inter - m_t)
    top = jnp.concatenate([wmat.astype(BF16), (sc[:, :qb.shape[1]] * qb.astype(F32)).astype(BF16)], axis=1)
    bot = jnp.concatenate([kw, jnp.zeros(kw.shape, BF16)], axis=1)
    res = jnp.dot(jnp.concatenate([top, bot], axis=0), jnp.concatenate([vaug, ct.astype(BF16)], axis=0),
                  preferred_element_type=F32)
    num = res[:L, 0:ML_V]
    den = res[:L, ML_V:2 * ML_V]
    return decay * ct + res[L:], m_new, num / jnp.maximum(jnp.abs(den), jnp.exp(-m_t))


def _ml_scan_kernel(ql_ref, ktl_ref, vl_ref, al_ref, bml_ref, brl_ref,
                    qc_ref, ktc_ref, vc_ref, ac_ref, bmc_ref, brc_ref, out_ref, st_ref, m_ref):
    L = ML_CHUNK
    ncl = ql_ref.shape[2] // L
    ncc = qc_ref.shape[2] // L
    ones_blk = jnp.ones((L, ML_V), BF16)
    st_ref[...] = jnp.zeros(st_ref.shape, F32)
    m_ref[...] = jnp.zeros(m_ref.shape, F32)

    def step(refs, c, d, fwd, need_out):
        q_ref, kt_ref, v_ref, a_ref, bm_ref, br_ref = refs
        sl = pl.ds(pl.multiple_of(c * L, L), L)
        a_all = a_ref[0, sl, :]
        bm_all = bm_ref[0, sl, :]
        br_all = br_ref[0, c]
        v_all = v_ref[0, sl, :]
        kt_all = kt_ref[0, c]
        hs = []
        for h in range(ML_HEADS):
            j = d * ML_HEADS + h
            vaug = jnp.concatenate([v_all[:, h * ML_V:(h + 1) * ML_V], ones_blk], axis=1)
            ct_new, m_new, hh = _ml_chunk(
                q_ref[0, h, sl, :], kt_all[h * ML_QK:(h + 1) * ML_QK, :], vaug,
                a_all[:, j:j + 1], bm_all[:, j:j + 1], br_all[j:j + 1, :], st_ref[j], m_ref[j][0:1, :],
                fwd, need_out)
            st_ref[j] = ct_new
            m_ref[j] = jnp.broadcast_to(m_new, (SUBLANES, LANES))
            hs.append(hh)
        return hs, sl

    ctx_refs = (qc_ref, ktc_ref, vc_ref, ac_ref, bmc_ref, brc_ref)
    lat_refs = (ql_ref, ktl_ref, vl_ref, al_ref, bml_ref, brl_ref)

    out_ref[...] = jnp.zeros(out_ref.shape, F32)

    def ctx_body(i, carry):
        step(ctx_refs, i, 0, True, False)
        step(ctx_refs, ncc - 1 - i, 1, False, False)
        return carry

    def lat_body(i, carry):
        for d, fwd in ((0, True), (1, False)):
            hs, sl = step(lat_refs, i if fwd else ncl - 1 - i, d, fwd, True)
            out_ref[0, sl, :] = out_ref[0, sl, :] + jnp.concatenate(hs, axis=1)
        return carry

    lax.fori_loop(0, ncc, ctx_body, 0)
    lax.fori_loop(0, ncl, lat_body, 0)


def _ml_scan(ql, ktl, vl, al, bml, brl, qc, ktc, vc, ac, bmc, brc):
    b, _, n, _ = ql.shape
    nctx = qc.shape[2]
    L = ML_CHUNK
    hv = ML_HEADS * ML_V
    hq = ML_HEADS * ML_QK
    qspec = lambda nn: pl.BlockSpec((1, ML_HEADS, nn, ML_QK), lambda i: (i, 0, 0, 0))
    ktspec = lambda nn: pl.BlockSpec((1, nn // L, hq, L), lambda i: (i, 0, 0, 0))
    vspec = lambda nn: pl.BlockSpec((1, nn, hv), lambda i: (i, 0, 0))
    aspec = lambda nn: pl.BlockSpec((1, nn, 8), lambda i: (i, 0, 0))
    rspec = lambda nn: pl.BlockSpec((1, nn // L, 8, L), lambda i: (i, 0, 0, 0))
    return pl.pallas_call(
        _ml_scan_kernel,
        out_shape=jax.ShapeDtypeStruct((b, n, hv), F32),
        grid=(b,),
        in_specs=[qspec(n), ktspec(n), vspec(n), aspec(n), aspec(n), rspec(n),
                  qspec(nctx), ktspec(nctx), vspec(nctx), aspec(nctx), aspec(nctx), rspec(nctx)],
        out_specs=pl.BlockSpec((1, n, hv), lambda i: (i, 0, 0)),
        scratch_shapes=[pltpu.VMEM((2 * ML_HEADS, ML_QK, 2 * ML_V), F32),
                        pltpu.VMEM((2 * ML_HEADS, SUBLANES, LANES), F32)],
        compiler_params=_params(("parallel",)),
        name="ml_scan",
    )(ql, ktl, vl, al, bml, brl, qc, ktc, vc, ac, bmc, brc)


def _outproj_kernel(x_ref, mla_ref, hs_ref, o_ref, gm_ref, wa_ref, wb_ref, gt_ref, g2_ref, sc_ref, sh_ref,
                    wpq_ref, sk_ref, x1_ref, h2_ref, s_ref):
    hs = hs_ref[0]
    gm = gm_ref[...]
    hn = jnp.concatenate([_rms(hs[:, h * ML_V:(h + 1) * ML_V]) * gm[:, h * ML_V:(h + 1) * ML_V]
                          for h in range(ML_HEADS)], axis=1)
    ml = (jax.nn.sigmoid(o_ref[0]) * hn).astype(BF16)
    mix = (jnp.dot(mla_ref[0], wa_ref[...], preferred_element_type=F32)
           + jnp.dot(ml, wb_ref[...], preferred_element_type=F32))
    x1 = x_ref[0] + gt_ref[0] * mix
    x1_ref[0] = x1
    h2 = _rms(x1) * g2_ref[...] * (1.0 + sc_ref[0]) + sh_ref[0]
    h2_ref[0] = h2
    qp = jnp.dot(h2.astype(BF16), wpq_ref[...], preferred_element_type=F32).astype(BF16)
    for hp in range(2 * PEER_HEADS):
        s_ref[0, hp] = lax.dot_general(sk_ref[hp], qp[:, hp * N_KEYS:(hp + 1) * N_KEYS],
                                       (((1,), (1,)), ((), ())), preferred_element_type=F32)


def _outproj(x, mla, hsum, cols, gm, wa, wb, gt1, g2, sc2, sh2, wpq, sk, tm):
    b, n, d = x.shape
    hw = mla.shape[2]
    nq = wpq.shape[1]
    mod = pl.BlockSpec((1, 1, d), lambda i, j: (i, 0, 0))
    return pl.pallas_call(
        _outproj_kernel,
        out_shape=(jax.ShapeDtypeStruct((b, n, d), F32), jax.ShapeDtypeStruct((b, n, d), F32),
                   jax.ShapeDtypeStruct((b, 2 * PEER_HEADS, N_KEYS, n), F32)),
        grid=(b, n // tm),
        in_specs=[pl.BlockSpec((1, tm, d), lambda i, j: (i, j, 0)),
                  pl.BlockSpec((1, tm, hw), lambda i, j: (i, j, 0)),
                  pl.BlockSpec((1, tm, hw), lambda i, j: (i, j, 0)),
                  pl.BlockSpec((1, tm, 512), lambda i, j: (i, j, C_O0 // 512)),
                  pl.BlockSpec((1, hw), lambda i, j: (0, 0)),
                  pl.BlockSpec((hw, d), lambda i, j: (0, 0)),
                  pl.BlockSpec((hw, d), lambda i, j: (0, 0)),
                  mod, pl.BlockSpec((1, d), lambda i, j: (0, 0)), mod, mod,
                  pl.BlockSpec((d, nq), lambda i, j: (0, 0)),
                  pl.BlockSpec((2 * PEER_HEADS, N_KEYS, N_KEYS), lambda i, j: (0, 0, 0))],
        out_specs=(pl.BlockSpec((1, tm, d), lambda i, j: (i, j, 0)),
                   pl.BlockSpec((1, tm, d), lambda i, j: (i, j, 0)),
                   pl.BlockSpec((1, 2 * PEER_HEADS, N_KEYS, tm), lambda i, j: (i, 0, 0, j))),
        compiler_params=_params(("parallel", "parallel")),
        name="outproj",
    )(x, mla, hsum, cols, gm, wa, wb, gt1, g2, sc2, sh2, wpq, sk)


def _topk_rows(work_refs, rid_ref, rows, k, vals_ref, ids_ref):
    def body(r, carry):
        for p, work_ref in enumerate(work_refs):
            s = work_ref[0:rows, :]
            rid = rid_ref[0:rows, :]
            m = jnp.max(s, axis=0, keepdims=True)
            i = jnp.min(jnp.where(s == m, rid, jnp.float32(3.0e38)), axis=0, keepdims=True)
            work_ref[0:rows, :] = jnp.where(rid == i, -jnp.inf, s)
            vals_ref[p, pl.ds(r, 1), :] = m
            ids_ref[p, pl.ds(r, 1), :] = i
        return carry

    lax.fori_loop(0, k, body, 0)
    return [(vals_ref[p], ids_ref[p]) for p in range(len(work_refs))]


def _pair_candidates(sv0, sv1):
    K, t = sv0.shape
    h = K // 2
    iid = lax.broadcasted_iota(I32, (h, t), 0)
    vals = [sv0[0:h] + sv1[0:1], sv0[h:K] + sv1[0:1]]
    ids = [iid * K, (iid + h) * K]
    for j in range(1, h):
        vals.append(sv0[0:h] + sv1[j:j + 1])
        ids.append(iid * K + j)
    vals.append(sv0[0:1] + sv1[h:K])
    ids.append(iid + h)
    return jnp.concatenate(vals, axis=0), jnp.concatenate(ids, axis=0).astype(F32)


HEADS_PER_ROUND = 4


def _route_kernel(s_ref, off_ref, par_ref, gw_ref, et_ref, gt_ref, work_ref, rid_ref, cid_ref,
                  v1_ref, i1_ref, v2_ref, i2_ref):
    K = PEER_TOPK
    nk, tt = rid_ref.shape
    rid_ref[...] = lax.broadcasted_iota(I32, (nk, tt), 0).astype(F32)
    hr = HEADS_PER_ROUND

    def round_(g, carry):
        heads = [g * hr + a for a in range(hr)]
        for a, h in enumerate(heads):
            work_ref[2 * a] = s_ref[0, 2 * h]
            work_ref[2 * a + 1] = s_ref[0, 2 * h + 1]
        first = _topk_rows([work_ref.at[b] for b in range(2 * hr)], rid_ref, nk, K, v1_ref, i1_ref)
        ncand = N_PAIR_CANDIDATES
        for a in range(hr):
            cand, cid = _pair_candidates(first[2 * a][0], first[2 * a + 1][0])
            work_ref[a, 0:ncand, :] = cand
            cid_ref[...] = cid
        second = _topk_rows([work_ref.at[a] for a in range(hr)], cid_ref, ncand, K, v2_ref, i2_ref)
        for a, h in enumerate(heads):
            best, pos = second[a]
            pos = pos.astype(I32)
            si0 = first[2 * a][1].astype(I32)
            si1 = first[2 * a + 1][1].astype(I32)
            isel = pos >> 4
            jsel = pos & (K - 1)
            e0 = jnp.zeros_like(pos)
            e1 = jnp.zeros_like(pos)
            for i in range(K):
                e0 = jnp.where(isel == i, si0[i:i + 1, :], e0)
                e1 = jnp.where(jsel == i, si1[i:i + 1, :], e1)
            ex = jnp.exp(best - best[0:1, :])
            sl = pl.ds(pl.multiple_of(h * K, K), K)
            et_ref[sl, :] = e0 * N_KEYS + e1
            gt_ref[sl, :] = ex / jnp.sum(ex, axis=0, keepdims=True)
        return carry

    lax.fori_loop(0, PEER_HEADS // hr, round_, 0)
    e = et_ref[...].T
    off_ref[0] = (e >> 1) * SUBLANES
    par_ref[0] = (e & 1).astype(F32)
    gw_ref[0] = gt_ref[...].T


def _route(scores, tt):
    b, hp, nk, n = scores.shape
    nsel = PEER_HEADS * PEER_TOPK
    ospec = pl.BlockSpec((1, tt, nsel), lambda i, j: (i, j, 0))
    return pl.pallas_call(
        _route_kernel,
        out_shape=(jax.ShapeDtypeStruct((b, n, nsel), I32), jax.ShapeDtypeStruct((b, n, nsel), F32),
                   jax.ShapeDtypeStruct((b, n, nsel), F32)),
        grid=(b, n // tt),
        in_specs=[pl.BlockSpec((1, hp, nk, tt), lambda i, j: (i, 0, 0, j))],
        out_specs=(ospec, ospec, ospec),
        scratch_shapes=[pltpu.VMEM((nsel, tt), I32), pltpu.VMEM((nsel, tt), F32),
                        pltpu.VMEM((2 * HEADS_PER_ROUND, nk, tt), F32), pltpu.VMEM((nk, tt), F32),
                        pltpu.VMEM((N_PAIR_CANDIDATES, tt), F32),
                        pltpu.VMEM((2 * HEADS_PER_ROUND, PEER_TOPK, tt), F32),
                        pltpu.VMEM((2 * HEADS_PER_ROUND, PEER_TOPK, tt), F32),
                        pltpu.VMEM((HEADS_PER_ROUND, PEER_TOPK, tt), F32),
                        pltpu.VMEM((HEADS_PER_ROUND, PEER_TOPK, tt), F32)],
        compiler_params=_params(("parallel", "parallel")),
        name="route",
    )(scores)


TILE_ROWS = 2 * SUBLANES
TOKENS_PER_ITER = 32


CHUNK_SEL = 16


def _chunk_tiles(tab_ref, row_ref, c):
    return jnp.concatenate(
        [pltpu.bitcast(tab_ref[pl.ds(pl.multiple_of(row_ref[c * CHUNK_SEL + kk], SUBLANES), SUBLANES), :], BF16)
         for kk in range(CHUNK_SEL)], axis=0)


def _init_spread(spread_ref, gather_ref=None):
    @pl.when(pl.program_id(0) == 0)
    def _():
        nsel, width = spread_ref.shape
        spread_ref[...] = (lax.broadcasted_iota(I32, (nsel, width), 1) // TILE_ROWS
                           == lax.broadcasted_iota(I32, (nsel, width), 0)).astype(BF16)
        if gather_ref is not None:
            gather_ref[...] = (lax.broadcasted_iota(I32, (width, nsel), 0) // TILE_ROWS
                               == lax.broadcasted_iota(I32, (width, nsel), 1)).astype(BF16)


def _row_masks(width):
    half = (lax.broadcasted_iota(I32, (1, width), 1) % 2).astype(F32)
    lane8 = lax.broadcasted_iota(I32, (SUBLANES, width), 1)
    diag = ((lane8 % TILE_ROWS) // 2 == lax.broadcasted_iota(I32, (SUBLANES, width), 0)).astype(F32)
    return half, diag


def _row_to_tile(row):
    return jnp.concatenate([row[:, s * LANES:(s + 1) * LANES] for s in range(SUBLANES)], axis=0)


def _split_bf16(v):
    hi = v.astype(BF16)
    return hi, (v - hi.astype(F32)).astype(BF16)


def _peer_u_kernel(off_ref, h_ref, tab_ref, par_ref, g_ref, c_ref, r_ref, spread_ref, gather_ref):
    tb, nsel = g_ref.shape
    _init_spread(spread_ref, gather_ref)
    half, diag = _row_masks(spread_ref.shape[1])

    cw = CHUNK_SEL * TILE_ROWS
    diag_c = diag[:, :cw]
    nt = (((1,), (1,)), ((), ()))

    def tok(i, carry):
        ts = [i * TOKENS_PER_ITER + u for u in range(TOKENS_PER_ITER)]
        rows = [off_ref.at[t] for t in ts]
        xs = [jnp.concatenate(_split_bf16(_row_to_tile(h_ref[pl.ds(t, 1), :])), axis=0) for t in ts]
        for c in range(nsel // CHUNK_SEL):
            for u, t in enumerate(ts):
                res = lax.dot_general(xs[u], _chunk_tiles(tab_ref, rows[u], c), nt, preferred_element_type=F32)
                r_ref[pl.ds(t, 1), c * cw:(c + 1) * cw] = jnp.sum(
                    (res[:SUBLANES] + res[SUBLANES:]) * diag_c, axis=0, keepdims=True)
        return carry

    lax.fori_loop(0, tb // TOKENS_PER_ITER, tok, 0)
    mine = jnp.dot(par_ref[...].astype(BF16), spread_ref[...], preferred_element_type=F32) == half
    r_hi, r_lo = _split_bf16(jnp.where(mine, r_ref[...], 0.0))
    both = jnp.dot(jnp.concatenate([r_hi, r_lo], axis=0), gather_ref[...], preferred_element_type=F32)
    pre = both[:tb] + both[tb:]
    c_ref[...] = g_ref[...] * (0.5 * pre * (1.0 + lax.erf(pre * (2.0 ** -0.5))))


def _smem_block(tb, nsel):
    return pl.BlockSpec((tb, nsel), lambda i: (i, 0), memory_space=pltpu.SMEM)


def _peer_u(off, h2, tab, par, gw, tb):
    t, nsel = gw.shape
    vspec = pl.BlockSpec((tb, nsel), lambda i: (i, 0))
    return pl.pallas_call(
        _peer_u_kernel,
        out_shape=jax.ShapeDtypeStruct((t, nsel), F32),
        grid=(t // tb,),
        in_specs=[_smem_block(tb, nsel),
                  pl.BlockSpec((tb, h2.shape[1]), lambda i: (i, 0)),
                  pl.BlockSpec(tab.shape, lambda i: (0, 0), pipeline_mode=pl.Buffered(1)),
                  vspec, vspec],
        out_specs=vspec,
        scratch_shapes=[pltpu.VMEM((tb, nsel * TILE_ROWS), F32),
                        pltpu.VMEM((nsel, nsel * TILE_ROWS), BF16), pltpu.VMEM((nsel * TILE_ROWS, nsel), BF16)],
        compiler_params=_params(("arbitrary",)),
        name="peer_u",
    )(off, h2, tab, par, gw)


def _peer_v_kernel(off_ref, c_ref, par_ref, tab_ref, x1_ref, gt_ref, o_ref, lhi_ref, llo_ref, spread_ref):
    tb, nsel = c_ref.shape
    _init_spread(spread_ref)
    half, diag = _row_masks(spread_ref.shape[1])
    c_hi, c_lo = _split_bf16(c_ref[...])
    ex = jnp.dot(jnp.concatenate([par_ref[...].astype(BF16), c_hi, c_lo], axis=0), spread_ref[...],
                 preferred_element_type=F32)
    mine = ex[:tb] == half
    lhi_ref[...] = jnp.where(mine, ex[tb:2 * tb], 0.0)
    llo_ref[...] = jnp.where(mine, ex[2 * tb:], 0.0)
    gt = gt_ref[0]

    cw = CHUNK_SEL * TILE_ROWS
    diag_c = diag[:, :cw]

    def tok(i, carry):
        ts = [i * TOKENS_PER_ITER + u for u in range(TOKENS_PER_ITER)]
        rows = [off_ref.at[t] for t in ts]
        accs = [jnp.zeros((TILE_ROWS, LANES), F32) for _ in ts]
        for c in range(nsel // CHUNK_SEL):
            cols = slice(c * cw, (c + 1) * cw)
            for u, t in enumerate(ts):
                left = jnp.concatenate([lhi_ref[pl.ds(t, 1), cols] * diag_c, llo_ref[pl.ds(t, 1), cols] * diag_c],
                                       axis=0).astype(BF16)
                accs[u] = accs[u] + jnp.dot(left, _chunk_tiles(tab_ref, rows[u], c), preferred_element_type=F32)
        for u, t in enumerate(ts):
            y = accs[u][:SUBLANES] + accs[u][SUBLANES:]
            y_row = jnp.concatenate([y[s:s + 1, :] for s in range(SUBLANES)], axis=1)
            o_ref[pl.ds(t, 1), :] = x1_ref[pl.ds(t, 1), :] + gt * y_row
        return carry

    lax.fori_loop(0, tb // TOKENS_PER_ITER, tok, 0)


def _peer_v(off, coef, par, tab, x1, gt2, tb, tok_per_batch):
    t, nsel = coef.shape
    d = x1.shape[1]
    bpb = tok_per_batch // tb
    vspec = pl.BlockSpec((tb, nsel), lambda i: (i, 0))
    return pl.pallas_call(
        _peer_v_kernel,
        out_shape=jax.ShapeDtypeStruct(x1.shape, F32),
        grid=(t // tb,),
        in_specs=[_smem_block(tb, nsel), vspec, vspec,
                  pl.BlockSpec(tab.shape, lambda i: (0, 0), pipeline_mode=pl.Buffered(1)),
                  pl.BlockSpec((tb, d), lambda i: (i, 0)),
                  pl.BlockSpec((1, 1, d), lambda i: (i // bpb, 0, 0))],
        out_specs=pl.BlockSpec((tb, d), lambda i: (i, 0)),
        scratch_shapes=[pltpu.VMEM((tb, nsel * TILE_ROWS), F32), pltpu.VMEM((tb, nsel * TILE_ROWS), F32),
                        pltpu.VMEM((nsel, nsel * TILE_ROWS), BF16)],
        compiler_params=_params(("arbitrary",)),
        name="peer_v",
    )(off, coef, par, tab, x1, gt2)


def _rope_tables(n):
    axis = MLA_ROPE // 2
    t = jnp.arange(n, dtype=F32)
    row = jnp.floor(t / GRID_W)
    col = t - row * GRID_W
    inv = ROPE_BASE ** (-jnp.arange(axis // 2, dtype=F32) * (2.0 / axis))
    ar = row[:, None] * inv
    ac = col[:, None] * inv
    cos = jnp.concatenate([jnp.cos(ar), jnp.cos(ar), jnp.cos(ac), jnp.cos(ac)], axis=1)
    sin = jnp.concatenate([-jnp.sin(ar), jnp.sin(ar), -jnp.sin(ac), jnp.sin(ac)], axis=1)
    return cos, sin


PACK_ROWS = 256


def _pack_kernel(t_ref, o_ref):
    n = t_ref.shape[0]
    rows = n // 2
    xb = t_ref[...].astype(BF16)
    col = lax.broadcasted_iota(I32, (rows, n), 1)
    row2 = 2 * lax.broadcasted_iota(I32, (rows, n), 0)
    even = pltpu.bitcast(jnp.dot((col == row2).astype(BF16), xb, preferred_element_type=F32), jnp.uint32)
    odd = pltpu.bitcast(jnp.dot((col == row2 + 1).astype(BF16), xb, preferred_element_type=F32), jnp.uint32)
    word = odd | (even >> 16)
    for s in range(SUBLANES):
        o_ref[pl.ds(s, rows, stride=SUBLANES), :] = word[:, s * LANES:(s + 1) * LANES]


def _pack_table(tab):
    e, d = tab.shape
    out_rows = PACK_ROWS // 2 * (d // LANES)
    return pl.pallas_call(
        _pack_kernel,
        out_shape=jax.ShapeDtypeStruct((e // 2 * (d // LANES), LANES), jnp.uint32),
        grid=(e // PACK_ROWS,),
        in_specs=[pl.BlockSpec((PACK_ROWS, d), lambda i: (i, 0))],
        out_specs=pl.BlockSpec((out_rows, LANES), lambda i: (i, 0)),
        compiler_params=_params(("parallel",)),
        name="pack_table",
    )(tab)


def _block(n, want):
    return want if n % want == 0 else n


def kernel(x, c, ctx, c_ctx, w_ada, b_ada, g_norm1, w_in, g_cq, w_uq, g_ckv, w_ukv, g_qn, g_kn, conv_qk, b_igate, b_fgate, g_mlstm, w_out, g_norm2, w_pq, sub_keys, expert_u, expert_v):
    B, N, D = x.shape
    NC = ctx.shape[1]
    assert w_ada.shape[0] == 1, "one layer"
    assert N % (ATTN_TQ * ATTN_TILES_PER_STEP) == 0 and NC % ML_CHUNK == 0 and D == SUBLANES * LANES
    q_rank = g_cq.shape[1]
    kv_rank = g_ckv.shape[1]
    mla_cols = q_rank + kv_rank + MLA_ROPE
    assert (q_rank, kv_rank) == (C_KV0 - C_Q0, C_KR0 - C_KV0)
    swap = jnp.arange(MLA_ROPE) ^ (MLA_ROPE // 4)

    cc = jnp.concatenate([c, c_ctx[None, :], jnp.zeros((16 - B - 1, D), F32)], axis=0)
    mod = _ada(cc, w_ada[0].astype(BF16), b_ada)
    sh1, sc1, gt1, sh2, sc2, gt2 = [mod[:, i * D:(i + 1) * D] for i in range(6)]
    lat = lambda m: m[:B].reshape(B, 1, D)
    ctxm = lambda m: jnp.broadcast_to(m[B:B + 1].reshape(1, 1, D), (B, 1, D))

    wi = w_in[0]
    n_qk = 2 * ML_HEADS * ML_QK
    n_v = ML_HEADS * ML_V
    n_g = 4 * ML_HEADS
    m0 = mla_cols
    w_cols = jnp.concatenate([
        wi[:, 0:mla_cols],
        wi[:, q_rank + kv_rank + swap],
        wi[:, m0:m0 + n_qk + 2 * n_v + n_g],
        jnp.zeros((D, C_END - C_G0 - n_g), F32)], axis=1).astype(BF16)
    assert w_cols.shape[1] == C_END
    w_gates_t = wi[:, m0 + n_qk + 2 * n_v:m0 + n_qk + 2 * n_v + n_g].T.astype(BF16)
    cols_l, gates_l = _inproj(x, g_norm1, lat(sc1), lat(sh1), w_cols, w_gates_t, _block(N, ROW_TILE))
    cols_c, gates_c = _inproj(ctx, g_norm1, ctxm(sc1), ctxm(sh1), w_cols, w_gates_t, _block(NC, ROW_TILE))

    wq = w_uq[0].reshape(q_rank, MLA_HEADS, MLA_QK)
    wuq = jnp.concatenate([wq, wq[:, :, MLA_NOPE + swap]], axis=2).reshape(q_rank, MLA_HEADS * 256).astype(BF16)
    wukv = w_ukv[0].astype(BF16)
    gqs, gks = g_qn[:, MLA_NOPE + swap], g_kn[:, MLA_NOPE + swap]
    cos_l, sin_l = _rope_tables(N)
    cos_c, sin_c = jnp.ones((NC, MLA_ROPE), F32), jnp.zeros((NC, MLA_ROPE), F32)
    q_l, k_l, v_l = _mla_prep(cols_l, g_cq, wuq, g_ckv, wukv, g_qn, gqs, g_kn, gks, cos_l, sin_l, _block(N, ROW_TILE))
    _, k_c, v_c = _mla_prep(cols_c, g_cq, wuq, g_ckv, wukv, g_qn, gqs, g_kn, gks, cos_c, sin_c, _block(NC, ROW_TILE))
    mla = _attention(q_l, k_c, k_l, v_c, v_l, ATTN_TQ)

    bias16 = jnp.concatenate([b_igate[0].reshape(-1), b_fgate[0].reshape(-1)])
    cw = conv_qk[0]
    pl_ = _ml_prep(cols_l, gates_l, cw, bias16[None, :], bias16[:, None], _block(N, ROW_TILE))
    pc_ = _ml_prep(cols_c, gates_c, cw, bias16[None, :], bias16[:, None], _block(NC, ROW_TILE))
    hsum = _ml_scan(*pl_, *pc_)

    wo = w_out[0].astype(BF16)
    hw = MLA_HEADS * MLA_V
    sk = sub_keys[0].reshape(2 * PEER_HEADS, N_KEYS, -1).astype(BF16)
    x1, h2, scores = _outproj(x, mla, hsum, cols_l, g_mlstm, wo[:hw], wo[hw:], lat(gt1), g_norm2, lat(sc2), lat(sh2),
                              w_pq[0].astype(BF16), sk, OUT_TILE)
    T = B * N
    nsel = PEER_HEADS * PEER_TOPK
    off, par, gw = [a.reshape(T, nsel) for a in _route(scores, ROUTE_TILE)]

    tb = PEER_TOKENS
    coef = _peer_u(off, h2.reshape(T, D), _pack_table(expert_u[0]), par, gw, tb)
    out = _peer_v(off, coef, par, _pack_table(expert_v[0]), x1.reshape(T, D), lat(gt2), tb, N)
    return out.reshape(B, N, D)
```

```python
import jax
import jax.numpy as jnp
from jax import lax
from jax.experimental import pallas as pl
from jax.experimental.pallas import tpu as pltpu

F32 = jnp.float32
BF16 = jnp.bfloat16
I32 = jnp.int32
EPS = 1e-6

GRID_W = 64
MLA_HEADS = 4
MLA_NOPE = 128
MLA_ROPE = 64
MLA_V = 128
MLA_QK = MLA_NOPE + MLA_ROPE
ROPE_BASE = 10000.0
ML_HEADS = 4
ML_QK = 64
ML_V = 128
ML_CHUNK = 64
PEER_HEADS = 8
N_KEYS = 128
PEER_TOPK = 16
N_PAIR_CANDIDATES = 80

LANES = 128
SUBLANES = 8
VMEM_LIMIT_BYTES = 56 * 1024 * 1024

ROW_TILE = 512
ATTN_TQ = 256
OUT_TILE = 256
ROUTE_TILE = 256
PEER_TOKENS = 128

C_Q0, C_KV0, C_KR0, C_KRS0, C_QK0, C_V0, C_O0, C_G0, C_END = 0, 256, 384, 448, 512, 1024, 1536, 2048, 2176
HIGHEST = lax.Precision.HIGHEST


def _params(sem):
    return pltpu.CompilerParams(dimension_semantics=sem, vmem_limit_bytes=VMEM_LIMIT_BYTES)


def _rms(x):
    return x * lax.rsqrt(jnp.mean(x * x, axis=-1, keepdims=True) + EPS)


def _ada_kernel(c_ref, w_ref, b_ref, o_ref):
    c = c_ref[...]
    s = c * jax.nn.sigmoid(c)
    o_ref[...] = jnp.dot(s.astype(BF16), w_ref[...], preferred_element_type=F32) + b_ref[...]


def _ada(cc, w, b):
    rows, d = cc.shape
    n = w.shape[1]
    bn = n // 4
    return pl.pallas_call(
        _ada_kernel,
        out_shape=jax.ShapeDtypeStruct((rows, n), F32),
        grid=(4,),
        in_specs=[pl.BlockSpec((rows, d), lambda j: (0, 0)),
                  pl.BlockSpec((d, bn), lambda j: (0, j)),
                  pl.BlockSpec((1, bn), lambda j: (0, j))],
        out_specs=pl.BlockSpec((rows, bn), lambda j: (0, j)),
        compiler_params=_params(("arbitrary",)),
        name="ada",
    )(cc, w, b)


def _inproj_kernel(x_ref, g_ref, sc_ref, sh_ref, w_ref, wg_ref, o_ref, gt_ref):
    x = x_ref[0]
    h = _rms(x) * g_ref[...] * (1.0 + sc_ref[0]) + sh_ref[0]
    hb = h.astype(BF16)
    o_ref[0] = jnp.dot(hb, w_ref[...], preferred_element_type=F32)
    gt_ref[0] = lax.dot_general(wg_ref[...], hb, (((1,), (1,)), ((), ())), preferred_element_type=F32)


def _inproj(x, g, sc, sh, w, wg, tm):
    b, n, d = x.shape
    nc = w.shape[1]
    return pl.pallas_call(
        _inproj_kernel,
        out_shape=(jax.ShapeDtypeStruct((b, n, nc), F32), jax.ShapeDtypeStruct((b, 16, n), F32)),
        grid=(b, n // tm),
        in_specs=[pl.BlockSpec((1, tm, d), lambda i, j: (i, j, 0)),
                  pl.BlockSpec((1, d), lambda i, j: (0, 0)),
                  pl.BlockSpec((1, 1, d), lambda i, j: (i, 0, 0)),
                  pl.BlockSpec((1, 1, d), lambda i, j: (i, 0, 0)),
                  pl.BlockSpec((d, nc), lambda i, j: (0, 0)),
                  pl.BlockSpec((16, d), lambda i, j: (0, 0))],
        out_specs=(pl.BlockSpec((1, tm, nc), lambda i, j: (i, j, 0)),
                   pl.BlockSpec((1, 16, tm), lambda i, j: (i, 0, j))),
        compiler_params=_params(("parallel", "parallel")),
        name="inproj",
    )(x, g, sc, sh, w, wg)


def _mla_prep_kernel(c_ref, gcq_ref, wuq_ref, gckv_ref, wukv_ref, gq_ref, gqs_ref, gk_ref, gks_ref,
                     cos_ref, sin_ref, q_ref, k_ref, v_ref):
    c = c_ref[0]
    cq = _rms(c[:, C_Q0:C_KV0]) * gcq_ref[...]
    ckv = _rms(c[:, C_KV0:C_KR0]) * gckv_ref[...]
    kr = c[:, C_KR0:C_KRS0]
    krs = c[:, C_KRS0:C_QK0]
    q_raw = jnp.dot(cq.astype(BF16), wuq_ref[...], preferred_element_type=F32)
    kv_raw = jnp.dot(ckv.astype(BF16), wukv_ref[...], preferred_element_type=F32)
    cos = cos_ref[...]
    sin = sin_ref[...]
    gq = gq_ref[...]
    gk = gk_ref[...]
    kr_ss = jnp.sum(kr * kr, axis=-1, keepdims=True)
    for h in range(MLA_HEADS):
        o = h * 256
        qn = q_raw[:, o:o + 128]
        qr = q_raw[:, o + 128:o + 192]
        qs = q_raw[:, o + 192:o + 256]
        ss = jnp.sum(qn * qn, axis=-1, keepdims=True) + jnp.sum(qr * qr, axis=-1, keepdims=True)
        r = lax.rsqrt(ss * (1.0 / MLA_QK) + EPS) * (MLA_QK ** -0.5)
        q_ref[0, h, :, 0:128] = (qn * r * gq[:, 0:128]).astype(BF16)
        q_ref[0, h, :, 128:192] = ((qr * r * gq[:, 128:192]) * cos + (qs * r * gqs_ref[...]) * sin).astype(BF16)
        kn = kv_raw[:, o:o + 128]
        ss = jnp.sum(kn * kn, axis=-1, keepdims=True) + kr_ss
        r = lax.rsqrt(ss * (1.0 / MLA_QK) + EPS)
        k_ref[0, h, :, 0:128] = (kn * r * gk[:, 0:128]).astype(BF16)
        k_ref[0, h, :, 128:192] = ((kr * r * gk[:, 128:192]) * cos + (krs * r * gks_ref[...]) * sin).astype(BF16)
        v_ref[0, h] = kv_raw[:, o + 128:o + 256].astype(BF16)


def _mla_prep(cols, gcq, wuq, gckv, wukv, gq, gqs, gk, gks, cos, sin, tm):
    b, n, _ = cols.shape
    full = lambda a: pl.BlockSpec(a.shape, lambda i, j: (0,) * a.ndim)
    return pl.pallas_call(
        _mla_prep_kernel,
        out_shape=(jax.ShapeDtypeStruct((b, MLA_HEADS, n, MLA_QK), BF16),
                   jax.ShapeDtypeStruct((b, MLA_HEADS, n, MLA_QK), BF16),
                   jax.ShapeDtypeStruct((b, MLA_HEADS, n, MLA_V), BF16)),
        grid=(b, n // tm),
        in_specs=[pl.BlockSpec((1, tm, 512), lambda i, j: (i, j, 0)),
                  full(gcq), full(wuq), full(gckv), full(wukv), full(gq), full(gqs), full(gk), full(gks),
                  pl.BlockSpec((tm, MLA_ROPE), lambda i, j: (j, 0)),
                  pl.BlockSpec((tm, MLA_ROPE), lambda i, j: (j, 0))],
        out_specs=(pl.BlockSpec((1, MLA_HEADS, tm, MLA_QK), lambda i, j: (i, 0, j, 0)),
                   pl.BlockSpec((1, MLA_HEADS, tm, MLA_QK), lambda i, j: (i, 0, j, 0)),
                   pl.BlockSpec((1, MLA_HEADS, tm, MLA_V), lambda i, j: (i, 0, j, 0))),
        compiler_params=_params(("parallel", "parallel")),
        name="mla_prep",
    )(cols, gcq, wuq, gckv, wukv, gq, gqs, gk, gks, cos, sin)


ATTN_TILES_PER_STEP = 4


def _attn_kernel(q_ref, kc_ref, kl_ref, vc_ref, vl_ref, o_ref):
    tq = q_ref.shape[2] // ATTN_TILES_PER_STEP
    nc = kc_ref.shape[2]
    kc, kl, vc, vl = kc_ref[0, 0], kl_ref[0, 0], vc_ref[0, 0], vl_ref[0, 0]
    nt = (((1,), (1,)), ((), ()))
    for j in range(ATTN_TILES_PER_STEP):
        q = q_ref[0, 0, j * tq:(j + 1) * tq, :]
        s = jnp.concatenate([lax.dot_general(q, kc, nt, preferred_element_type=F32),
                             lax.dot_general(q, kl, nt, preferred_element_type=F32)], axis=1)
        m = jnp.max(s, axis=-1, keepdims=True)
        p = jnp.exp(s - m)
        l = jnp.sum(p, axis=-1, keepdims=True)
        pb = p.astype(BF16)
        o = (jnp.dot(pb[:, :nc], vc, preferred_element_type=F32)
             + jnp.dot(pb[:, nc:], vl, preferred_element_type=F32))
        o_ref[0, j * tq:(j + 1) * tq, :] = (o / l).astype(BF16)


def _attention(q, k_c, k_l, v_c, v_l, tq):
    b, h, n, dk = q.shape
    nc = k_c.shape[2]
    dv = v_l.shape[3]
    tb = tq * ATTN_TILES_PER_STEP
    whole = lambda nn, dd: pl.BlockSpec((1, 1, nn, dd), lambda i, j, t: (i, j, 0, 0))
    return pl.pallas_call(
        _attn_kernel,
        out_shape=jax.ShapeDtypeStruct((b, n, h * dv), BF16),
        grid=(b, h, n // tb),
        in_specs=[pl.BlockSpec((1, 1, tb, dk), lambda i, j, t: (i, j, t, 0)),
                  whole(nc, dk), whole(n, dk), whole(nc, dv), whole(n, dv)],
        out_specs=pl.BlockSpec((1, tb, dv), lambda i, j, t: (i, t, j)),
        compiler_params=_params(("parallel", "parallel", "parallel")),
        name="attention",
    )(q, k_c, k_l, v_c, v_l)


def _ml_prep_kernel(qk_ref, prev_ref, next_ref, vin_ref, g_ref, gt_ref, cw_ref, bc_ref, br_ref,
                    q_ref, kt_ref, v_ref, a_ref, bm_ref, brow_ref):
    tn = qk_ref.shape[1]
    j = pl.program_id(1)
    nj = pl.num_programs(1)
    u = qk_ref[0]
    row = lax.broadcasted_iota(I32, (tn, 1), 0)
    before = jnp.where(j == 0, 0.0, prev_ref[0, SUBLANES - 1:SUBLANES, :])
    after = jnp.where(j == nj - 1, 0.0, next_ref[0, 0:1, :])
    up = jnp.where(row == 0, before, pltpu.roll(u, 1, 0))
    un = jnp.where(row == tn - 1, after, pltpu.roll(u, tn - 1, 0))
    cw = cw_ref[...]
    y = up * cw[0:1, :] + u * cw[1:2, :] + un * cw[2:3, :]
    y = y * jax.nn.sigmoid(y)
    hq = ML_HEADS * ML_QK
    for h in range(ML_HEADS):
        q_ref[0, h] = (y[:, h * ML_QK:(h + 1) * ML_QK] * (ML_QK ** -0.5)).astype(BF16)
    kt = y[:, hq:2 * hq].T
    L = ML_CHUNK
    for c in range(tn // L):
        kt_ref[0, c] = kt[:, c * L:(c + 1) * L].astype(BF16)
    v_ref[0] = vin_ref[0].astype(BF16)
    g = g_ref[0][:, 0:16] + bc_ref[...]
    ig_c = g[:, 0:8]
    lf_c = jax.nn.log_sigmoid(g[:, 8:16])
    gt = gt_ref[0] + br_ref[...]
    ig_r = gt[0:8, :]
    lf_r = jax.nn.log_sigmoid(gt[8:16, :])
    ti = lax.broadcasted_iota(I32, (L, L), 0)
    si = lax.broadcasted_iota(I32, (L, L), 1)
    lower = (si <= ti).astype(F32)
    upper = (si >= ti).astype(F32)
    lane_fwd = lax.broadcasted_iota(I32, (L, 8), 1) < ML_HEADS
    row_l = lax.broadcasted_iota(I32, (L, 8), 0)
    row_fwd = lax.broadcasted_iota(I32, (8, L), 0) < ML_HEADS
    for c in range(tn // L):
        lo = c * L
        lfc = lf_c[lo:lo + L, :]
        cf_c = jnp.where(
            lane_fwd,
            jnp.dot(lower, lfc, precision=HIGHEST, preferred_element_type=F32),
            jnp.dot(upper, lfc, precision=HIGHEST, preferred_element_type=F32))
        a_ref[0, lo:lo + L, :] = cf_c
        pre = suf = ig_c[lo:lo + L, :] - cf_c
        d = 1
        while d < L:
            pre = jnp.maximum(pre, jnp.where(row_l >= d, pltpu.roll(pre, d, 0), -jnp.inf))
            suf = jnp.maximum(suf, jnp.where(row_l < L - d, pltpu.roll(suf, L - d, 0), -jnp.inf))
            d *= 2
        bm_ref[0, lo:lo + L, :] = jnp.where(lane_fwd, pre, suf)
        lfr = lf_r[:, lo:lo + L]
        cf_r = jnp.where(
            row_fwd,
            jnp.dot(lfr, upper, precision=HIGHEST, preferred_element_type=F32),
            jnp.dot(lfr, lower, precision=HIGHEST, preferred_element_type=F32))
        brow_ref[0, c] = ig_r[:, lo:lo + L] - cf_r


def _ml_prep(cols, gates_t, cw, bias_col, bias_row, tn):
    b, n, _ = cols.shape
    r8 = tn // SUBLANES
    last8 = n // SUBLANES - 1
    L = ML_CHUNK
    hq = ML_HEADS * ML_QK
    return pl.pallas_call(
        _ml_prep_kernel,
        out_shape=(jax.ShapeDtypeStruct((b, ML_HEADS, n, ML_QK), BF16),
                   jax.ShapeDtypeStruct((b, n // L, hq, L), BF16),
                   jax.ShapeDtypeStruct((b, n, ML_HEADS * ML_V), BF16),
                   jax.ShapeDtypeStruct((b, n, 8), F32),
                   jax.ShapeDtypeStruct((b, n, 8), F32),
                   jax.ShapeDtypeStruct((b, n // L, 8, L), F32)),
        grid=(b, n // tn),
        in_specs=[pl.BlockSpec((1, tn, 512), lambda i, j: (i, j, C_QK0 // 512)),
                  pl.BlockSpec((1, SUBLANES, 512), lambda i, j: (i, jnp.maximum(j * r8 - 1, 0), C_QK0 // 512)),
                  pl.BlockSpec((1, SUBLANES, 512), lambda i, j: (i, jnp.minimum((j + 1) * r8, last8), C_QK0 // 512)),
                  pl.BlockSpec((1, tn, 512), lambda i, j: (i, j, C_V0 // 512)),
                  pl.BlockSpec((1, tn, 128), lambda i, j: (i, j, C_G0 // 128)),
                  pl.BlockSpec((1, 16, tn), lambda i, j: (i, 0, j)),
                  pl.BlockSpec((3, 512), lambda i, j: (0, 0)),
                  pl.BlockSpec((1, 16), lambda i, j: (0, 0)),
                  pl.BlockSpec((16, 1), lambda i, j: (0, 0))],
        out_specs=(pl.BlockSpec((1, ML_HEADS, tn, ML_QK), lambda i, j: (i, 0, j, 0)),
                   pl.BlockSpec((1, tn // L, hq, L), lambda i, j: (i, j, 0, 0)),
                   pl.BlockSpec((1, tn, ML_HEADS * ML_V), lambda i, j: (i, j, 0)),
                   pl.BlockSpec((1, tn, 8), lambda i, j: (i, j, 0)),
                   pl.BlockSpec((1, tn, 8), lambda i, j: (i, j, 0)),
                   pl.BlockSpec((1, tn // L, 8, L), lambda i, j: (i, j, 0, 0))),
        compiler_params=_params(("parallel", "parallel")),
        name="ml_prep",
    )(cols, cols, cols, cols, cols, gates_t, cw, bias_col, bias_row)


def _ml_chunk(qb, kt, vaug, a_c, bm_c, b_r, ct, m_prev, fwd, need_out):
    L = qb.shape[0]
    ti = lax.broadcasted_iota(I32, (L, L), 0)
    si = lax.broadcasted_iota(I32, (L, L), 1)
    mask = (si <= ti) if fwd else (si >= ti)
    a_b = jnp.broadcast_to(a_c, (L, LANES))
    inter = a_b + m_prev
    m_t = jnp.maximum(inter, a_b + jnp.broadcast_to(bm_c, (L, LANES)))
    last = L - 1 if fwd else 0
    m_new = m_t[last:last + 1, :]
    a_last = a_b[last:last + 1, :]
    w_r = jnp.exp(a_last[:, :L] + b_r - m_new[:, :L])
    decay = jnp.exp(a_last + m_prev - m_new)
    decay = jnp.concatenate([decay, decay], axis=1)
    kw = (kt.astype(F32) * w_r).astype(BF16)
    if not need_out:
        return decay * ct + jnp.dot(kw, vaug, preferred_element_type=F32), m_new, None
    s = jnp.dot(qb, kt, preferred_element_type=F32)
    dmat = jnp.where(mask, a_b[:, :L] + b_r, -jnp.inf)
    wmat = jnp.exp(dmat - m_t[:, :L]) * s
    sc = jnp.exp(inter - m_t)
    top = jnp.concatenate([wmat.astype(BF16), (sc[:, :qb.shape[1]] * qb.astype(F32)).astype(BF16)], axis=1)
    bot = jnp.concatenate([kw, jnp.zeros(kw.shape, BF16)], axis=1)
    res = jnp.dot(jnp.concatenate([top, bot], axis=0), jnp.concatenate([vaug, ct.astype(BF16)], axis=0),
                  preferred_element_type=F32)
    num = res[:L, 0:ML_V]
    den = res[:L, ML_V:2 * ML_V]
    return decay * ct + res[L:], m_new, num / jnp.maximum(jnp.abs(den), jnp.exp(-m_t))


def _ml_scan_kernel(ql_ref, ktl_ref, vl_ref, al_ref, bml_ref, brl_ref,
                    qc_ref, ktc_ref, vc_ref, ac_ref, bmc_ref, brc_ref, out_ref, st_ref, m_ref):
    L = ML_CHUNK
    ncl = ql_ref.shape[2] // L
    ncc = qc_ref.shape[2] // L
    ones_blk = jnp.ones((L, ML_V), BF16)
    st_ref[...] = jnp.zeros(st_ref.shape, F32)
    m_ref[...] = jnp.zeros(m_ref.shape, F32)

    def step(refs, c, d, fwd, need_out):
        q_ref, kt_ref, v_ref, a_ref, bm_ref, br_ref = refs
        sl = pl.ds(pl.multiple_of(c * L, L), L)
        a_all = a_ref[0, sl, :]
        bm_all = bm_ref[0, sl, :]
        br_all = br_ref[0, c]
        v_all = v_ref[0, sl, :]
        kt_all = kt_ref[0, c]
        hs = []
        for h in range(ML_HEADS):
            j = d * ML_HEADS + h
            vaug = jnp.concatenate([v_all[:, h * ML_V:(h + 1) * ML_V], ones_blk], axis=1)
            ct_new, m_new, hh = _ml_chunk(
                q_ref[0, h, sl, :], kt_all[h * ML_QK:(h + 1) * ML_QK, :], vaug,
                a_all[:, j:j + 1], bm_all[:, j:j + 1], br_all[j:j + 1, :], st_ref[j], m_ref[j][0:1, :],
                fwd, need_out)
            st_ref[j] = ct_new
            m_ref[j] = jnp.broadcast_to(m_new, (SUBLANES, LANES))
            hs.append(hh)
        return hs, sl

    ctx_refs = (qc_ref, ktc_ref, vc_ref, ac_ref, bmc_ref, brc_ref)
    lat_refs = (ql_ref, ktl_ref, vl_ref, al_ref, bml_ref, brl_ref)

    out_ref[...] = jnp.zeros(out_ref.shape, F32)

    def ctx_body(i, carry):
        step(ctx_refs, i, 0, True, False)
        step(ctx_refs, ncc - 1 - i, 1, False, False)
        return carry

    def lat_body(i, carry):
        for d, fwd in ((0, True), (1, False)):
            hs, sl = step(lat_refs, i if fwd else ncl - 1 - i, d, fwd, True)
            out_ref[0, sl, :] = out_ref[0, sl, :] + jnp.concatenate(hs, axis=1)
        return carry

    lax.fori_loop(0, ncc, ctx_body, 0)
    lax.fori_loop(0, ncl, lat_body, 0)


def _ml_scan(ql, ktl, vl, al, bml, brl, qc, ktc, vc, ac, bmc, brc):
    b, _, n, _ = ql.shape
    nctx = qc.shape[2]
    L = ML_CHUNK
    hv = ML_HEADS * ML_V
    hq = ML_HEADS * ML_QK
    qspec = lambda nn: pl.BlockSpec((1, ML_HEADS, nn, ML_QK), lambda i: (i, 0, 0, 0))
    ktspec = lambda nn: pl.BlockSpec((1, nn // L, hq, L), lambda i: (i, 0, 0, 0))
    vspec = lambda nn: pl.BlockSpec((1, nn, hv), lambda i: (i, 0, 0))
    aspec = lambda nn: pl.BlockSpec((1, nn, 8), lambda i: (i, 0, 0))
    rspec = lambda nn: pl.BlockSpec((1, nn // L, 8, L), lambda i: (i, 0, 0, 0))
    return pl.pallas_call(
        _ml_scan_kernel,
        out_shape=jax.ShapeDtypeStruct((b, n, hv), F32),
        grid=(b,),
        in_specs=[qspec(n), ktspec(n), vspec(n), aspec(n), aspec(n), rspec(n),
                  qspec(nctx), ktspec(nctx), vspec(nctx), aspec(nctx), aspec(nctx), rspec(nctx)],
        out_specs=pl.BlockSpec((1, n, hv), lambda i: (i, 0, 0)),
        scratch_shapes=[pltpu.VMEM((2 * ML_HEADS, ML_QK, 2 * ML_V), F32),
                        pltpu.VMEM((2 * ML_HEADS, SUBLANES, LANES), F32)],
        compiler_params=_params(("parallel",)),
        name="ml_scan",
    )(ql, ktl, vl, al, bml, brl, qc, ktc, vc, ac, bmc, brc)


def _outproj_kernel(x_ref, mla_ref, hs_ref, o_ref, gm_ref, wa_ref, wb_ref, gt_ref, g2_ref, sc_ref, sh_ref,
                    wpq_ref, sk_ref, x1_ref, h2_ref, s_ref):
    hs = hs_ref[0]
    gm = gm_ref[...]
    hn = jnp.concatenate([_rms(hs[:, h * ML_V:(h + 1) * ML_V]) * gm[:, h * ML_V:(h + 1) * ML_V]
                          for h in range(ML_HEADS)], axis=1)
    ml = (jax.nn.sigmoid(o_ref[0]) * hn).astype(BF16)
    mix = (jnp.dot(mla_ref[0], wa_ref[...], preferred_element_type=F32)
           + jnp.dot(ml, wb_ref[...], preferred_element_type=F32))
    x1 = x_ref[0] + gt_ref[0] * mix
    x1_ref[0] = x1
    h2 = _rms(x1) * g2_ref[...] * (1.0 + sc_ref[0]) + sh_ref[0]
    h2_ref[0] = h2
    qp = jnp.dot(h2.astype(BF16), wpq_ref[...], preferred_element_type=F32).astype(BF16)
    for hp in range(2 * PEER_HEADS):
        s_ref[0, hp] = lax.dot_general(sk_ref[hp], qp[:, hp * N_KEYS:(hp + 1) * N_KEYS],
                                       (((1,), (1,)), ((), ())), preferred_element_type=F32)


def _outproj(x, mla, hsum, cols, gm, wa, wb, gt1, g2, sc2, sh2, wpq, sk, tm):
    b, n, d = x.shape
    hw = mla.shape[2]
    nq = wpq.shape[1]
    mod = pl.BlockSpec((1, 1, d), lambda i, j: (i, 0, 0))
    return pl.pallas_call(
        _outproj_kernel,
        out_shape=(jax.ShapeDtypeStruct((b, n, d), F32), jax.ShapeDtypeStruct((b, n, d), F32),
                   jax.ShapeDtypeStruct((b, 2 * PEER_HEADS, N_KEYS, n), F32)),
        grid=(b, n // tm),
        in_specs=[pl.BlockSpec((1, tm, d), lambda i, j: (i, j, 0)),
                  pl.BlockSpec((1, tm, hw), lambda i, j: (i, j, 0)),
                  pl.BlockSpec((1, tm, hw), lambda i, j: (i, j, 0)),
                  pl.BlockSpec((1, tm, 512), lambda i, j: (i, j, C_O0 // 512)),
                  pl.BlockSpec((1, hw), lambda i, j: (0, 0)),
                  pl.BlockSpec((hw, d), lambda i, j: (0, 0)),
                  pl.BlockSpec((hw, d), lambda i, j: (0, 0)),
                  mod, pl.BlockSpec((1, d), lambda i, j: (0, 0)), mod, mod,
                  pl.BlockSpec((d, nq), lambda i, j: (0, 0)),
                  pl.BlockSpec((2 * PEER_HEADS, N_KEYS, N_KEYS), lambda i, j: (0, 0, 0))],
        out_specs=(pl.BlockSpec((1, tm, d), lambda i, j: (i, j, 0)),
                   pl.BlockSpec((1, tm, d), lambda i, j: (i, j, 0)),
                   pl.BlockSpec((1, 2 * PEER_HEADS, N_KEYS, tm), lambda i, j: (i, 0, 0, j))),
        compiler_params=_params(("parallel", "parallel")),
        name="outproj",
    )(x, mla, hsum, cols, gm, wa, wb, gt1, g2, sc2, sh2, wpq, sk)


def _topk_rows(work_refs, rid_ref, rows, k, vals_ref, ids_ref):
    def body(r, carry):
        for p, work_ref in enumerate(work_refs):
            s = work_ref[0:rows, :]
            rid = rid_ref[0:rows, :]
            m = jnp.max(s, axis=0, keepdims=True)
            i = jnp.min(jnp.where(s == m, rid, jnp.float32(3.0e38)), axis=0, keepdims=True)
            work_ref[0:rows, :] = jnp.where(rid == i, -jnp.inf, s)
            vals_ref[p, pl.ds(r, 1), :] = m
            ids_ref[p, pl.ds(r, 1), :] = i
        return carry

    lax.fori_loop(0, k, body, 0)
    return [(vals_ref[p], ids_ref[p]) for p in range(len(work_refs))]


def _pair_candidates(sv0, sv1):
    K, t = sv0.shape
    h = K // 2
    iid = lax.broadcasted_iota(I32, (h, t), 0)
    vals = [sv0[0:h] + sv1[0:1], sv0[h:K] + sv1[0:1]]
    ids = [iid * K, (iid + h) * K]
    for j in range(1, h):
        vals.append(sv0[0:h] + sv1[j:j + 1])
        ids.append(iid * K + j)
    vals.append(sv0[0:1] + sv1[h:K])
    ids.append(iid + h)
    return jnp.concatenate(vals, axis=0), jnp.concatenate(ids, axis=0).astype(F32)


HEADS_PER_ROUND = 4


def _route_kernel(s_ref, off_ref, par_ref, gw_ref, et_ref, gt_ref, work_ref, rid_ref, cid_ref,
                  v1_ref, i1_ref, v2_ref, i2_ref):
    K = PEER_TOPK
    nk, tt = rid_ref.shape
    rid_ref[...] = lax.broadcasted_iota(I32, (nk, tt), 0).astype(F32)
    hr = HEADS_PER_ROUND

    def round_(g, carry):
        heads = [g * hr + a for a in range(hr)]
        for a, h in enumerate(heads):
            work_ref[2 * a] = s_ref[0, 2 * h]
            work_ref[2 * a + 1] = s_ref[0, 2 * h + 1]
        first = _topk_rows([work_ref.at[b] for b in range(2 * hr)], rid_ref, nk, K, v1_ref, i1_ref)
        ncand = N_PAIR_CANDIDATES
        for a in range(hr):
            cand, cid = _pair_candidates(first[2 * a][0], first[2 * a + 1][0])
            work_ref[a, 0:ncand, :] = cand
            cid_ref[...] = cid
        second = _topk_rows([work_ref.at[a] for a in range(hr)], cid_ref, ncand, K, v2_ref, i2_ref)
        for a, h in enumerate(heads):
            best, pos = second[a]
            pos = pos.astype(I32)
            si0 = first[2 * a][1].astype(I32)
            si1 = first[2 * a + 1][1].astype(I32)
            isel = pos >> 4
            jsel = pos & (K - 1)
            e0 = jnp.zeros_like(pos)
            e1 = jnp.zeros_like(pos)
            for i in range(K):
                e0 = jnp.where(isel == i, si0[i:i + 1, :], e0)
                e1 = jnp.where(jsel == i, si1[i:i + 1, :], e1)
            ex = jnp.exp(best - best[0:1, :])
            sl = pl.ds(pl.multiple_of(h * K, K), K)
            et_ref[sl, :] = e0 * N_KEYS + e1
            gt_ref[sl, :] = ex / jnp.sum(ex, axis=0, keepdims=True)
        return carry

    lax.fori_loop(0, PEER_HEADS // hr, round_, 0)
    e = et_ref[...].T
    off_ref[0] = (e >> 1) * SUBLANES
    par_ref[0] = (e & 1).astype(F32)
    gw_ref[0] = gt_ref[...].T


def _route(scores, tt):
    b, hp, nk, n = scores.shape
    nsel = PEER_HEADS * PEER_TOPK
    ospec = pl.BlockSpec((1, tt, nsel), lambda i, j: (i, j, 0))
    return pl.pallas_call(
        _route_kernel,
        out_shape=(jax.ShapeDtypeStruct((b, n, nsel), I32), jax.ShapeDtypeStruct((b, n, nsel), F32),
                   jax.ShapeDtypeStruct((b, n, nsel), F32)),
        grid=(b, n // tt),
        in_specs=[pl.BlockSpec((1, hp, nk, tt), lambda i, j: (i, 0, 0, j))],
        out_specs=(ospec, ospec, ospec),
        scratch_shapes=[pltpu.VMEM((nsel, tt), I32), pltpu.VMEM((nsel, tt), F32),
                        pltpu.VMEM((2 * HEADS_PER_ROUND, nk, tt), F32), pltpu.VMEM((nk, tt), F32),
                        pltpu.VMEM((N_PAIR_CANDIDATES, tt), F32),
                        pltpu.VMEM((2 * HEADS_PER_ROUND, PEER_TOPK, tt), F32),
                        pltpu.VMEM((2 * HEADS_PER_ROUND, PEER_TOPK, tt), F32),
                        pltpu.VMEM((HEADS_PER_ROUND, PEER_TOPK, tt), F32),
                        pltpu.VMEM((HEADS_PER_ROUND, PEER_TOPK, tt), F32)],
        compiler_params=_params(("parallel", "parallel")),
        name="route",
    )(scores)


TILE_ROWS = 2 * SUBLANES
TOKENS_PER_ITER = 64


CHUNK_SEL = 16


def _chunk_tiles(tab_ref, row_ref, c):
    return jnp.concatenate(
        [pltpu.bitcast(tab_ref[pl.ds(pl.multiple_of(row_ref[c * CHUNK_SEL + kk], SUBLANES), SUBLANES), :], BF16)
         for kk in range(CHUNK_SEL)], axis=0)


def _init_spread(spread_ref, gather_ref=None):
    @pl.when(pl.program_id(0) == 0)
    def _():
        nsel, width = spread_ref.shape
        spread_ref[...] = (lax.broadcasted_iota(I32, (nsel, width), 1) // TILE_ROWS
                           == lax.broadcasted_iota(I32, (nsel, width), 0)).astype(BF16)
        if gather_ref is not None:
            gather_ref[...] = (lax.broadcasted_iota(I32, (width, nsel), 0) // TILE_ROWS
                               == lax.broadcasted_iota(I32, (width, nsel), 1)).astype(BF16)


def _row_masks(width):
    half = (lax.broadcasted_iota(I32, (1, width), 1) % 2).astype(F32)
    lane8 = lax.broadcasted_iota(I32, (SUBLANES, width), 1)
    diag = ((lane8 % TILE_ROWS) // 2 == lax.broadcasted_iota(I32, (SUBLANES, width), 0)).astype(F32)
    return half, diag


def _row_to_tile(row):
    return jnp.concatenate([row[:, s * LANES:(s + 1) * LANES] for s in range(SUBLANES)], axis=0)


def _split_bf16(v):
    hi = v.astype(BF16)
    return hi, (v - hi.astype(F32)).astype(BF16)


def _peer_u_kernel(off_ref, h_ref, tab_ref, par_ref, g_ref, c_ref, r_ref, spread_ref, gather_ref):
    tb, nsel = g_ref.shape
    _init_spread(spread_ref, gather_ref)
    half, diag = _row_masks(spread_ref.shape[1])

    cw = CHUNK_SEL * TILE_ROWS
    diag_c = diag[:, :cw]
    nt = (((1,), (1,)), ((), ()))

    def tok(i, carry):
        ts = [i * TOKENS_PER_ITER + u for u in range(TOKENS_PER_ITER)]
        rows = [off_ref.at[t] for t in ts]
        xs = [jnp.concatenate(_split_bf16(_row_to_tile(h_ref[pl.ds(t, 1), :])), axis=0) for t in ts]
        for c in range(nsel // CHUNK_SEL):
            for u, t in enumerate(ts):
                res = lax.dot_general(xs[u], _chunk_tiles(tab_ref, rows[u], c), nt, preferred_element_type=F32)
                r_ref[pl.ds(t, 1), c * cw:(c + 1) * cw] = jnp.sum(
                    (res[:SUBLANES] + res[SUBLANES:]) * diag_c, axis=0, keepdims=True)
        return carry

    lax.fori_loop(0, tb // TOKENS_PER_ITER, tok, 0)
    mine = jnp.dot(par_ref[...].astype(BF16), spread_ref[...], preferred_element_type=F32) == half
    r_hi, r_lo = _split_bf16(jnp.where(mine, r_ref[...], 0.0))
    both = jnp.dot(jnp.concatenate([r_hi, r_lo], axis=0), gather_ref[...], preferred_element_type=F32)
    pre = both[:tb] + both[tb:]
    c_ref[...] = g_ref[...] * (0.5 * pre * (1.0 + lax.erf(pre * (2.0 ** -0.5))))


def _smem_block(tb, nsel):
    return pl.BlockSpec((tb, nsel), lambda i: (i, 0), memory_space=pltpu.SMEM)


def _peer_u(off, h2, tab, par, gw, tb):
    t, nsel = gw.shape
    vspec = pl.BlockSpec((tb, nsel), lambda i: (i, 0))
    return pl.pallas_call(
        _peer_u_kernel,
        out_shape=jax.ShapeDtypeStruct((t, nsel), F32),
        grid=(t // tb,),
        in_specs=[_smem_block(tb, nsel),
                  pl.BlockSpec((tb, h2.shape[1]), lambda i: (i, 0)),
                  pl.BlockSpec(tab.shape, lambda i: (0, 0), pipeline_mode=pl.Buffered(1)),
                  vspec, vspec],
        out_specs=vspec,
        scratch_shapes=[pltpu.VMEM((tb, nsel * TILE_ROWS), F32),
                        pltpu.VMEM((nsel, nsel * TILE_ROWS), BF16), pltpu.VMEM((nsel * TILE_ROWS, nsel), BF16)],
        compiler_params=_params(("arbitrary",)),
        name="peer_u",
    )(off, h2, tab, par, gw)


def _peer_v_kernel(off_ref, c_ref, par_ref, tab_ref, x1_ref, gt_ref, o_ref, lhi_ref, llo_ref, spread_ref):
    tb, nsel = c_ref.shape
    _init_spread(spread_ref)
    half, diag = _row_masks(spread_ref.shape[1])
    c_hi, c_lo = _split_bf16(c_ref[...])
    ex = jnp.dot(jnp.concatenate([par_ref[...].astype(BF16), c_hi, c_lo], axis=0), spread_ref[...],
                 preferred_element_type=F32)
    mine = ex[:tb] == half
    lhi_ref[...] = jnp.where(mine, ex[tb:2 * tb], 0.0)
    llo_ref[...] = jnp.where(mine, ex[2 * tb:], 0.0)
    gt = gt_ref[0]

    cw = CHUNK_SEL * TILE_ROWS
    diag_c = diag[:, :cw]

    def tok(i, carry):
        ts = [i * TOKENS_PER_ITER + u for u in range(TOKENS_PER_ITER)]
        rows = [off_ref.at[t] for t in ts]
        accs = [jnp.zeros((TILE_ROWS, LANES), F32) for _ in ts]
        for c in range(nsel // CHUNK_SEL):
            cols = slice(c * cw, (c + 1) * cw)
            for u, t in enumerate(ts):
                left = jnp.concatenate([lhi_ref[pl.ds(t, 1), cols] * diag_c, llo_ref[pl.ds(t, 1), cols] * diag_c],
                                       axis=0).astype(BF16)
                accs[u] = accs[u] + jnp.dot(left, _chunk_tiles(tab_ref, rows[u], c), preferred_element_type=F32)
        for u, t in enumerate(ts):
            y = accs[u][:SUBLANES] + accs[u][SUBLANES:]
            y_row = jnp.concatenate([y[s:s + 1, :] for s in range(SUBLANES)], axis=1)
            o_ref[pl.ds(t, 1), :] = x1_ref[pl.ds(t, 1), :] + gt * y_row
        return carry

    lax.fori_loop(0, tb // TOKENS_PER_ITER, tok, 0)


def _peer_v(off, coef, par, tab, x1, gt2, tb, tok_per_batch):
    t, nsel = coef.shape
    d = x1.shape[1]
    bpb = tok_per_batch // tb
    vspec = pl.BlockSpec((tb, nsel), lambda i: (i, 0))
    return pl.pallas_call(
        _peer_v_kernel,
        out_shape=jax.ShapeDtypeStruct(x1.shape, F32),
        grid=(t // tb,),
        in_specs=[_smem_block(tb, nsel), vspec, vspec,
                  pl.BlockSpec(tab.shape, lambda i: (0, 0), pipeline_mode=pl.Buffered(1)),
                  pl.BlockSpec((tb, d), lambda i: (i, 0)),
                  pl.BlockSpec((1, 1, d), lambda i: (i // bpb, 0, 0))],
        out_specs=pl.BlockSpec((tb, d), lambda i: (i, 0)),
        scratch_shapes=[pltpu.VMEM((tb, nsel * TILE_ROWS), F32), pltpu.VMEM((tb, nsel * TILE_ROWS), F32),
                        pltpu.VMEM((nsel, nsel * TILE_ROWS), BF16)],
        compiler_params=_params(("arbitrary",)),
        name="peer_v",
    )(off, coef, par, tab, x1, gt2)


def _rope_tables(n):
    axis = MLA_ROPE // 2
    t = jnp.arange(n, dtype=F32)
    row = jnp.floor(t / GRID_W)
    col = t - row * GRID_W
    inv = ROPE_BASE ** (-jnp.arange(axis // 2, dtype=F32) * (2.0 / axis))
    ar = row[:, None] * inv
    ac = col[:, None] * inv
    cos = jnp.concatenate([jnp.cos(ar), jnp.cos(ar), jnp.cos(ac), jnp.cos(ac)], axis=1)
    sin = jnp.concatenate([-jnp.sin(ar), jnp.sin(ar), -jnp.sin(ac), jnp.sin(ac)], axis=1)
    return cos, sin


PACK_ROWS = 256


def _pack_kernel(t_ref, o_ref):
    n = t_ref.shape[0]
    rows = n // 2
    xb = t_ref[...].astype(BF16)
    col = lax.broadcasted_iota(I32, (rows, n), 1)
    row2 = 2 * lax.broadcasted_iota(I32, (rows, n), 0)
    even = pltpu.bitcast(jnp.dot((col == row2).astype(BF16), xb, preferred_element_type=F32), jnp.uint32)
    odd = pltpu.bitcast(jnp.dot((col == row2 + 1).astype(BF16), xb, preferred_element_type=F32), jnp.uint32)
    word = odd | (even >> 16)
    for s in range(SUBLANES):
        o_ref[pl.ds(s, rows, stride=SUBLANES), :] = word[:, s * LANES:(s + 1) * LANES]


def _pack_table(tab):
    e, d = tab.shape
    out_rows = PACK_ROWS // 2 * (d // LANES)
    return pl.pallas_call(
        _pack_kernel,
        out_shape=jax.ShapeDtypeStruct((e // 2 * (d // LANES), LANES), jnp.uint32),
        grid=(e // PACK_ROWS,),
        in_specs=[pl.BlockSpec((PACK_ROWS, d), lambda i: (i, 0))],
        out_specs=pl.BlockSpec((out_rows, LANES), lambda i: (i, 0)),
        compiler_params=_params(("parallel",)),
        name="pack_table",
    )(tab)


def _block(n, want):
    return want if n % want == 0 else n


def kernel(x, c, ctx, c_ctx, w_ada, b_ada, g_norm1, w_in, g_cq, w_uq, g_ckv, w_ukv, g_qn, g_kn, conv_qk, b_igate, b_fgate, g_mlstm, w_out, g_norm2, w_pq, sub_keys, expert_u, expert_v):
    B, N, D = x.shape
    NC = ctx.shape[1]
    assert w_ada.shape[0] == 1, "one layer"
    assert N % (ATTN_TQ * ATTN_TILES_PER_STEP) == 0 and NC % ML_CHUNK == 0 and D == SUBLANES * LANES
    q_rank = g_cq.shape[1]
    kv_rank = g_ckv.shape[1]
    mla_cols = q_rank + kv_rank + MLA_ROPE
    assert (q_rank, kv_rank) == (C_KV0 - C_Q0, C_KR0 - C_KV0)
    swap = jnp.arange(MLA_ROPE) ^ (MLA_ROPE // 4)

    cc = jnp.concatenate([c, c_ctx[None, :], jnp.zeros((16 - B - 1, D), F32)], axis=0)
    mod = _ada(cc, w_ada[0].astype(BF16), b_ada)
    sh1, sc1, gt1, sh2, sc2, gt2 = [mod[:, i * D:(i + 1) * D] for i in range(6)]
    lat = lambda m: m[:B].reshape(B, 1, D)
    ctxm = lambda m: jnp.broadcast_to(m[B:B + 1].reshape(1, 1, D), (B, 1, D))

    wi = w_in[0]
    n_qk = 2 * ML_HEADS * ML_QK
    n_v = ML_HEADS * ML_V
    n_g = 4 * ML_HEADS
    m0 = mla_cols
    w_cols = jnp.concatenate([
        wi[:, 0:mla_cols],
        wi[:, q_rank + kv_rank + swap],
        wi[:, m0:m0 + n_qk + 2 * n_v + n_g],
        jnp.zeros((D, C_END - C_G0 - n_g), F32)], axis=1).astype(BF16)
    assert w_cols.shape[1] == C_END
    w_gates_t = wi[:, m0 + n_qk + 2 * n_v:m0 + n_qk + 2 * n_v + n_g].T.astype(BF16)
    cols_l, gates_l = _inproj(x, g_norm1, lat(sc1), lat(sh1), w_cols, w_gates_t, _block(N, ROW_TILE))
    cols_c, gates_c = _inproj(ctx, g_norm1, ctxm(sc1), ctxm(sh1), w_cols, w_gates_t, _block(NC, ROW_TILE))

    wq = w_uq[0].reshape(q_rank, MLA_HEADS, MLA_QK)
    wuq = jnp.concatenate([wq, wq[:, :, MLA_NOPE + swap]], axis=2).reshape(q_rank, MLA_HEADS * 256).astype(BF16)
    wukv = w_ukv[0].astype(BF16)
    gqs, gks = g_qn[:, MLA_NOPE + swap], g_kn[:, MLA_NOPE + swap]
    cos_l, sin_l = _rope_tables(N)
    cos_c, sin_c = jnp.ones((NC, MLA_ROPE), F32), jnp.zeros((NC, MLA_ROPE), F32)
    q_l, k_l, v_l = _mla_prep(cols_l, g_cq, wuq, g_ckv, wukv, g_qn, gqs, g_kn, gks, cos_l, sin_l, _block(N, ROW_TILE))
    _, k_c, v_c = _mla_prep(cols_c, g_cq, wuq, g_ckv, wukv, g_qn, gqs, g_kn, gks, cos_c, sin_c, _block(NC, ROW_TILE))
    mla = _attention(q_l, k_c, k_l, v_c, v_l, ATTN_TQ)

    bias16 = jnp.concatenate([b_igate[0].reshape(-1), b_fgate[0].reshape(-1)])
    cw = conv_qk[0]
    pl_ = _ml_prep(cols_l, gates_l, cw, bias16[None, :], bias16[:, None], _block(N, ROW_TILE))
    pc_ = _ml_prep(cols_c, gates_c, cw, bias16[None, :], bias16[:, None], _block(NC, ROW_TILE))
    hsum = _ml_scan(*pl_, *pc_)

    wo = w_out[0].astype(BF16)
    hw = MLA_HEADS * MLA_V
    sk = sub_keys[0].reshape(2 * PEER_HEADS, N_KEYS, -1).astype(BF16)
    x1, h2, scores = _outproj(x, mla, hsum, cols_l, g_mlstm, wo[:hw], wo[hw:], lat(gt1), g_norm2, lat(sc2), lat(sh2),
                              w_pq[0].astype(BF16), sk, OUT_TILE)
    T = B * N
    nsel = PEER_HEADS * PEER_TOPK
    off, par, gw = [a.reshape(T, nsel) for a in _route(scores, ROUTE_TILE)]

    tb = PEER_TOKENS
    coef = _peer_u(off, h2.reshape(T, D), _pack_table(expert_u[0]), par, gw, tb)
    out = _peer_v(off, coef, par, _pack_table(expert_v[0]), x1.reshape(T, D), lat(gt2), tb, N)
    return out.reshape(B, N, D)
```

```python
import jax
import jax.numpy as jnp
from jax import lax
from jax.experimental import pallas as pl
from jax.experimental.pallas import tpu as pltpu

F32 = jnp.float32
BF16 = jnp.bfloat16
I32 = jnp.int32
EPS = 1e-6

GRID_W = 64
MLA_HEADS = 4
MLA_NOPE = 128
MLA_ROPE = 64
MLA_V = 128
MLA_QK = MLA_NOPE + MLA_ROPE
ROPE_BASE = 10000.0
ML_HEADS = 4
ML_QK = 64
ML_V = 128
ML_CHUNK = 64
PEER_HEADS = 8
N_KEYS = 128
PEER_TOPK = 16
N_PAIR_CANDIDATES = 80

LANES = 128
SUBLANES = 8
VMEM_LIMIT_BYTES = 56 * 1024 * 1024

ROW_TILE = 512
ATTN_TQ = 256
OUT_TILE = 256
ROUTE_TILE = 256
PEER_TOKENS = 256

C_Q0, C_KV0, C_KR0, C_KRS0, C_QK0, C_V0, C_O0, C_G0, C_END = 0, 256, 384, 448, 512, 1024, 1536, 2048, 2176
HIGHEST = lax.Precision.HIGHEST


def _params(sem):
    return pltpu.CompilerParams(dimension_semantics=sem, vmem_limit_bytes=VMEM_LIMIT_BYTES)


def _rms(x):
    return x * lax.rsqrt(jnp.mean(x * x, axis=-1, keepdims=True) + EPS)


def _ada_kernel(c_ref, w_ref, b_ref, o_ref):
    c = c_ref[...]
    s = c * jax.nn.sigmoid(c)
    o_ref[...] = jnp.dot(s.astype(BF16), w_ref[...], preferred_element_type=F32) + b_ref[...]


def _ada(cc, w, b):
    rows, d = cc.shape
    n = w.shape[1]
    bn = n // 4
    return pl.pallas_call(
        _ada_kernel,
        out_shape=jax.ShapeDtypeStruct((rows, n), F32),
        grid=(4,),
        in_specs=[pl.BlockSpec((rows, d), lambda j: (0, 0)),
                  pl.BlockSpec((d, bn), lambda j: (0, j)),
                  pl.BlockSpec((1, bn), lambda j: (0, j))],
        out_specs=pl.BlockSpec((rows, bn), lambda j: (0, j)),
        compiler_params=_params(("arbitrary",)),
        name="ada",
    )(cc, w, b)


def _inproj_kernel(x_ref, g_ref, sc_ref, sh_ref, w_ref, wg_ref, o_ref, gt_ref):
    x = x_ref[0]
    h = _rms(x) * g_ref[...] * (1.0 + sc_ref[0]) + sh_ref[0]
    hb = h.astype(BF16)
    o_ref[0] = jnp.dot(hb, w_ref[...], preferred_element_type=F32)
    gt_ref[0] = lax.dot_general(wg_ref[...], hb, (((1,), (1,)), ((), ())), preferred_element_type=F32)


def _inproj(x, g, sc, sh, w, wg, tm):
    b, n, d = x.shape
    nc = w.shape[1]
    return pl.pallas_call(
        _inproj_kernel,
        out_shape=(jax.ShapeDtypeStruct((b, n, nc), F32), jax.ShapeDtypeStruct((b, 16, n), F32)),
        grid=(b, n // tm),
        in_specs=[pl.BlockSpec((1, tm, d), lambda i, j: (i, j, 0)),
                  pl.BlockSpec((1, d), lambda i, j: (0, 0)),
                  pl.BlockSpec((1, 1, d), lambda i, j: (i, 0, 0)),
                  pl.BlockSpec((1, 1, d), lambda i, j: (i, 0, 0)),
                  pl.BlockSpec((d, nc), lambda i, j: (0, 0)),
                  pl.BlockSpec((16, d), lambda i, j: (0, 0))],
        out_specs=(pl.BlockSpec((1, tm, nc), lambda i, j: (i, j, 0)),
                   pl.BlockSpec((1, 16, tm), lambda i, j: (i, 0, j))),
        compiler_params=_params(("parallel", "parallel")),
        name="inproj",
    )(x, g, sc, sh, w, wg)


def _mla_prep_kernel(c_ref, gcq_ref, wuq_ref, gckv_ref, wukv_ref, gq_ref, gqs_ref, gk_ref, gks_ref,
                     cos_ref, sin_ref, q_ref, k_ref, v_ref):
    c = c_ref[0]
    cq = _rms(c[:, C_Q0:C_KV0]) * gcq_ref[...]
    ckv = _rms(c[:, C_KV0:C_KR0]) * gckv_ref[...]
    kr = c[:, C_KR0:C_KRS0]
    krs = c[:, C_KRS0:C_QK0]
    q_raw = jnp.dot(cq.astype(BF16), wuq_ref[...], preferred_element_type=F32)
    kv_raw = jnp.dot(ckv.astype(BF16), wukv_ref[...], preferred_element_type=F32)
    cos = cos_ref[...]
    sin = sin_ref[...]
    gq = gq_ref[...]
    gk = gk_ref[...]
    kr_ss = jnp.sum(kr * kr, axis=-1, keepdims=True)
    for h in range(MLA_HEADS):
        o = h * 256
        qn = q_raw[:, o:o + 128]
        qr = q_raw[:, o + 128:o + 192]
        qs = q_raw[:, o + 192:o + 256]
        ss = jnp.sum(qn * qn, axis=-1, keepdims=True) + jnp.sum(qr * qr, axis=-1, keepdims=True)
        r = lax.rsqrt(ss * (1.0 / MLA_QK) + EPS) * (MLA_QK ** -0.5)
        q_ref[0, h, :, 0:128] = (qn * r * gq[:, 0:128]).astype(BF16)
        q_ref[0, h, :, 128:192] = ((qr * r * gq[:, 128:192]) * cos + (qs * r * gqs_ref[...]) * sin).astype(BF16)
        kn = kv_raw[:, o:o + 128]
        ss = jnp.sum(kn * kn, axis=-1, keepdims=True) + kr_ss
        r = lax.rsqrt(ss * (1.0 / MLA_QK) + EPS)
        k_ref[0, h, :, 0:128] = (kn * r * gk[:, 0:128]).astype(BF16)
        k_ref[0, h, :, 128:192] = ((kr * r * gk[:, 128:192]) * cos + (krs * r * gks_ref[...]) * sin).astype(BF16)
        v_ref[0, h] = kv_raw[:, o + 128:o + 256].astype(BF16)


def _mla_prep(cols, gcq, wuq, gckv, wukv, gq, gqs, gk, gks, cos, sin, tm):
    b, n, _ = cols.shape
    full = lambda a: pl.BlockSpec(a.shape, lambda i, j: (0,) * a.ndim)
    return pl.pallas_call(
        _mla_prep_kernel,
        out_shape=(jax.ShapeDtypeStruct((b, MLA_HEADS, n, MLA_QK), BF16),
                   jax.ShapeDtypeStruct((b, MLA_HEADS, n, MLA_QK), BF16),
                   jax.ShapeDtypeStruct((b, MLA_HEADS, n, MLA_V), BF16)),
        grid=(b, n // tm),
        in_specs=[pl.BlockSpec((1, tm, 512), lambda i, j: (i, j, 0)),
                  full(gcq), full(wuq), full(gckv), full(wukv), full(gq), full(gqs), full(gk), full(gks),
                  pl.BlockSpec((tm, MLA_ROPE), lambda i, j: (j, 0)),
                  pl.BlockSpec((tm, MLA_ROPE), lambda i, j: (j, 0))],
        out_specs=(pl.BlockSpec((1, MLA_HEADS, tm, MLA_QK), lambda i, j: (i, 0, j, 0)),
                   pl.BlockSpec((1, MLA_HEADS, tm, MLA_QK), lambda i, j: (i, 0, j, 0)),
                   pl.BlockSpec((1, MLA_HEADS, tm, MLA_V), lambda i, j: (i, 0, j, 0))),
        compiler_params=_params(("parallel", "parallel")),
        name="mla_prep",
    )(cols, gcq, wuq, gckv, wukv, gq, gqs, gk, gks, cos, sin)


ATTN_TILES_PER_STEP = 4


def _attn_kernel(q_ref, kc_ref, kl_ref, vc_ref, vl_ref, o_ref):
    tq = q_ref.shape[2] // ATTN_TILES_PER_STEP
    nc = kc_ref.shape[2]
    kc, kl, vc, vl = kc_ref[0, 0], kl_ref[0, 0], vc_ref[0, 0], vl_ref[0, 0]
    nt = (((1,), (1,)), ((), ()))
    for j in range(ATTN_TILES_PER_STEP):
        q = q_ref[0, 0, j * tq:(j + 1) * tq, :]
        s = jnp.concatenate([lax.dot_general(q, kc, nt, preferred_element_type=F32),
                             lax.dot_general(q, kl, nt, preferred_element_type=F32)], axis=1)
        m = jnp.max(s, axis=-1, keepdims=True)
        p = jnp.exp(s - m)
        l = jnp.sum(p, axis=-1, keepdims=True)
        pb = p.astype(BF16)
        o = (jnp.dot(pb[:, :nc], vc, preferred_element_type=F32)
             + jnp.dot(pb[:, nc:], vl, preferred_element_type=F32))
        o_ref[0, j * tq:(j + 1) * tq, :] = (o / l).astype(BF16)


def _attention(q, k_c, k_l, v_c, v_l, tq):
    b, h, n, dk = q.shape
    nc = k_c.shape[2]
    dv = v_l.shape[3]
    tb = tq * ATTN_TILES_PER_STEP
    whole = lambda nn, dd: pl.BlockSpec((1, 1, nn, dd), lambda i, j, t: (i, j, 0, 0))
    return pl.pallas_call(
        _attn_kernel,
        out_shape=jax.ShapeDtypeStruct((b, n, h * dv), BF16),
        grid=(b, h, n // tb),
        in_specs=[pl.BlockSpec((1, 1, tb, dk), lambda i, j, t: (i, j, t, 0)),
                  whole(nc, dk), whole(n, dk), whole(nc, dv), whole(n, dv)],
        out_specs=pl.BlockSpec((1, tb, dv), lambda i, j, t: (i, t, j)),
        compiler_params=_params(("parallel", "parallel", "parallel")),
        name="attention",
    )(q, k_c, k_l, v_c, v_l)


def _ml_prep_kernel(qk_ref, prev_ref, next_ref, vin_ref, g_ref, gt_ref, cw_ref, bc_ref, br_ref,
                    q_ref, kt_ref, v_ref, a_ref, bm_ref, brow_ref):
    tn = qk_ref.shape[1]
    j = pl.program_id(1)
    nj = pl.num_programs(1)
    u = qk_ref[0]
    row = lax.broadcasted_iota(I32, (tn, 1), 0)
    before = jnp.where(j == 0, 0.0, prev_ref[0, SUBLANES - 1:SUBLANES, :])
    after = jnp.where(j == nj - 1, 0.0, next_ref[0, 0:1, :])
    up = jnp.where(row == 0, before, pltpu.roll(u, 1, 0))
    un = jnp.where(row == tn - 1, after, pltpu.roll(u, tn - 1, 0))
    cw = cw_ref[...]
    y = up * cw[0:1, :] + u * cw[1:2, :] + un * cw[2:3, :]
    y = y * jax.nn.sigmoid(y)
    hq = ML_HEADS * ML_QK
    for h in range(ML_HEADS):
        q_ref[0, h] = (y[:, h * ML_QK:(h + 1) * ML_QK] * (ML_QK ** -0.5)).astype(BF16)
    kt = y[:, hq:2 * hq].T
    L = ML_CHUNK
    for c in range(tn // L):
        kt_ref[0, c] = kt[:, c * L:(c + 1) * L].astype(BF16)
    v_ref[0] = vin_ref[0].astype(BF16)
    g = g_ref[0][:, 0:16] + bc_ref[...]
    ig_c = g[:, 0:8]
    lf_c = jax.nn.log_sigmoid(g[:, 8:16])
    gt = gt_ref[0] + br_ref[...]
    ig_r = gt[0:8, :]
    lf_r = jax.nn.log_sigmoid(gt[8:16, :])
    ti = lax.broadcasted_iota(I32, (L, L), 0)
    si = lax.broadcasted_iota(I32, (L, L), 1)
    lower = (si <= ti).astype(F32)
    upper = (si >= ti).astype(F32)
    lane_fwd = lax.broadcasted_iota(I32, (L, 8), 1) < ML_HEADS
    row_l = lax.broadcasted_iota(I32, (L, 8), 0)
    row_fwd = lax.broadcasted_iota(I32, (8, L), 0) < ML_HEADS
    for c in range(tn // L):
        lo = c * L
        lfc = lf_c[lo:lo + L, :]
        cf_c = jnp.where(
            lane_fwd,
            jnp.dot(lower, lfc, precision=HIGHEST, preferred_element_type=F32),
            jnp.dot(upper, lfc, precision=HIGHEST, preferred_element_type=F32))
        a_ref[0, lo:lo + L, :] = cf_c
        pre = suf = ig_c[lo:lo + L, :] - cf_c
        d = 1
        while d < L:
            pre = jnp.maximum(pre, jnp.where(row_l >= d, pltpu.roll(pre, d, 0), -jnp.inf))
            suf = jnp.maximum(suf, jnp.where(row_l < L - d, pltpu.roll(suf, L - d, 0), -jnp.inf))
            d *= 2
        bm_ref[0, lo:lo + L, :] = jnp.where(lane_fwd, pre, suf)
        lfr = lf_r[:, lo:lo + L]
        cf_r = jnp.where(
            row_fwd,
            jnp.dot(lfr, upper, precision=HIGHEST, preferred_element_type=F32),
            jnp.dot(lfr, lower, precision=HIGHEST, preferred_element_type=F32))
        brow_ref[0, c] = ig_r[:, lo:lo + L] - cf_r


def _ml_prep(cols, gates_t, cw, bias_col, bias_row, tn):
    b, n, _ = cols.shape
    r8 = tn // SUBLANES
    last8 = n // SUBLANES - 1
    L = ML_CHUNK
    hq = ML_HEADS * ML_QK
    return pl.pallas_call(
        _ml_prep_kernel,
        out_shape=(jax.ShapeDtypeStruct((b, ML_HEADS, n, ML_QK), BF16),
                   jax.ShapeDtypeStruct((b, n // L, hq, L), BF16),
                   jax.ShapeDtypeStruct((b, n, ML_HEADS * ML_V), BF16),
                   jax.ShapeDtypeStruct((b, n, 8), F32),
                   jax.ShapeDtypeStruct((b, n, 8), F32),
                   jax.ShapeDtypeStruct((b, n // L, 8, L), F32)),
        grid=(b, n // tn),
        in_specs=[pl.BlockSpec((1, tn, 512), lambda i, j: (i, j, C_QK0 // 512)),
                  pl.BlockSpec((1, SUBLANES, 512), lambda i, j: (i, jnp.maximum(j * r8 - 1, 0), C_QK0 // 512)),
                  pl.BlockSpec((1, SUBLANES, 512), lambda i, j: (i, jnp.minimum((j + 1) * r8, last8), C_QK0 // 512)),
                  pl.BlockSpec((1, tn, 512), lambda i, j: (i, j, C_V0 // 512)),
                  pl.BlockSpec((1, tn, 128), lambda i, j: (i, j, C_G0 // 128)),
                  pl.BlockSpec((1, 16, tn), lambda i, j: (i, 0, j)),
                  pl.BlockSpec((3, 512), lambda i, j: (0, 0)),
                  pl.BlockSpec((1, 16), lambda i, j: (0, 0)),
                  pl.BlockSpec((16, 1), lambda i, j: (0, 0))],
        out_specs=(pl.BlockSpec((1, ML_HEADS, tn, ML_QK), lambda i, j: (i, 0, j, 0)),
                   pl.BlockSpec((1, tn // L, hq, L), lambda i, j: (i, j, 0, 0)),
                   pl.BlockSpec((1, tn, ML_HEADS * ML_V), lambda i, j: (i, j, 0)),
                   pl.BlockSpec((1, tn, 8), lambda i, j: (i, j, 0)),
                   pl.BlockSpec((1, tn, 8), lambda i, j: (i, j, 0)),
                   pl.BlockSpec((1, tn // L, 8, L), lambda i, j: (i, j, 0, 0))),
        compiler_params=_params(("parallel", "parallel")),
        name="ml_prep",
    )(cols, cols, cols, cols, cols, gates_t, cw, bias_col, bias_row)


def _ml_chunk(qb, kt, vaug, a_c, bm_c, b_r, ct, m_prev, fwd, need_out):
    L = qb.shape[0]
    ti = lax.broadcasted_iota(I32, (L, L), 0)
    si = lax.broadcasted_iota(I32, (L, L), 1)
    mask = (si <= ti) if fwd else (si >= ti)
    a_b = jnp.broadcast_to(a_c, (L, LANES))
    inter = a_b + m_prev
    m_t = jnp.maximum(inter, a_b + jnp.broadcast_to(bm_c, (L, LANES)))
    last = L - 1 if fwd else 0
    m_new = m_t[last:last + 1, :]
    a_last = a_b[last:last + 1, :]
    w_r = jnp.exp(a_last[:, :L] + b_r - m_new[:, :L])
    decay = jnp.exp(a_last + m_prev - m_new)
    decay = jnp.concatenate([decay, decay], axis=1)
    kw = (kt.astype(F32) * w_r).astype(BF16)
    if not need_out:
        return decay * ct + jnp.dot(kw, vaug, preferred_element_type=F32), m_new, None
    s = jnp.dot(qb, kt, preferred_element_type=F32)
    dmat = jnp.where(mask, a_b[:, :L] + b_r, -jnp.inf)
    wmat = jnp.exp(dmat - m_t[:, :L]) * s
    sc = jnp.exp(inter - m_t)
    top = jnp.concatenate([wmat.astype(BF16), (sc[:, :qb.shape[1]] * qb.astype(F32)).astype(BF16)], axis=1)
    bot = jnp.concatenate([kw, jnp.zeros(kw.shape, BF16)], axis=1)
    res = jnp.dot(jnp.concatenate([top, bot], axis=0), jnp.concatenate([vaug, ct.astype(BF16)], axis=0),
                  preferred_element_type=F32)
    num = res[:L, 0:ML_V]
    den = res[:L, ML_V:2 * ML_V]
    return decay * ct + res[L:], m_new, num / jnp.maximum(jnp.abs(den), jnp.exp(-m_t))


def _ml_scan_kernel(ql_ref, ktl_ref, vl_ref, al_ref, bml_ref, brl_ref,
                    qc_ref, ktc_ref, vc_ref, ac_ref, bmc_ref, brc_ref, out_ref, st_ref, m_ref):
    L = ML_CHUNK
    ncl = ql_ref.shape[2] // L
    ncc = qc_ref.shape[2] // L
    ones_blk = jnp.ones((L, ML_V), BF16)
    st_ref[...] = jnp.zeros(st_ref.shape, F32)
    m_ref[...] = jnp.zeros(m_ref.shape, F32)

    def step(refs, c, d, fwd, need_out):
        q_ref, kt_ref, v_ref, a_ref, bm_ref, br_ref = refs
        sl = pl.ds(pl.multiple_of(c * L, L), L)
        a_all = a_ref[0, sl, :]
        bm_all = bm_ref[0, sl, :]
        br_all = br_ref[0, c]
        v_all = v_ref[0, sl, :]
        kt_all = kt_ref[0, c]
        hs = []
        for h in range(ML_HEADS):
            j = d * ML_HEADS + h
            vaug = jnp.concatenate([v_all[:, h * ML_V:(h + 1) * ML_V], ones_blk], axis=1)
            ct_new, m_new, hh = _ml_chunk(
                q_ref[0, h, sl, :], kt_all[h * ML_QK:(h + 1) * ML_QK, :], vaug,
                a_all[:, j:j + 1], bm_all[:, j:j + 1], br_all[j:j + 1, :], st_ref[j], m_ref[j][0:1, :],
                fwd, need_out)
            st_ref[j] = ct_new
            m_ref[j] = jnp.broadcast_to(m_new, (SUBLANES, LANES))
            hs.append(hh)
        return hs, sl

    ctx_refs = (qc_ref, ktc_ref, vc_ref, ac_ref, bmc_ref, brc_ref)
    lat_refs = (ql_ref, ktl_ref, vl_ref, al_ref, bml_ref, brl_ref)

    out_ref[...] = jnp.zeros(out_ref.shape, F32)

    def ctx_body(i, carry):
        step(ctx_refs, i, 0, True, False)
        step(ctx_refs, ncc - 1 - i, 1, False, False)
        return carry

    def lat_body(i, carry):
        for d, fwd in ((0, True), (1, False)):
            hs, sl = step(lat_refs, i if fwd else ncl - 1 - i, d, fwd, True)
            out_ref[0, sl, :] = out_ref[0, sl, :] + jnp.concatenate(hs, axis=1)
        return carry

    lax.fori_loop(0, ncc, ctx_body, 0)
    lax.fori_loop(0, ncl, lat_body, 0)


def _ml_scan(ql, ktl, vl, al, bml, brl, qc, ktc, vc, ac, bmc, brc):
    b, _, n, _ = ql.shape
    nctx = qc.shape[2]
    L = ML_CHUNK
    hv = ML_HEADS * ML_V
    hq = ML_HEADS * ML_QK
    qspec = lambda nn: pl.BlockSpec((1, ML_HEADS, nn, ML_QK), lambda i: (i, 0, 0, 0))
    ktspec = lambda nn: pl.BlockSpec((1, nn // L, hq, L), lambda i: (i, 0, 0, 0))
    vspec = lambda nn: pl.BlockSpec((1, nn, hv), lambda i: (i, 0, 0))
    aspec = lambda nn: pl.BlockSpec((1, nn, 8), lambda i: (i, 0, 0))
    rspec = lambda nn: pl.BlockSpec((1, nn // L, 8, L), lambda i: (i, 0, 0, 0))
    return pl.pallas_call(
        _ml_scan_kernel,
        out_shape=jax.ShapeDtypeStruct((b, n, hv), F32),
        grid=(b,),
        in_specs=[qspec(n), ktspec(n), vspec(n), aspec(n), aspec(n), rspec(n),
                  qspec(nctx), ktspec(nctx), vspec(nctx), aspec(nctx), aspec(nctx), rspec(nctx)],
        out_specs=pl.BlockSpec((1, n, hv), lambda i: (i, 0, 0)),
        scratch_shapes=[pltpu.VMEM((2 * ML_HEADS, ML_QK, 2 * ML_V), F32),
                        pltpu.VMEM((2 * ML_HEADS, SUBLANES, LANES), F32)],
        compiler_params=_params(("parallel",)),
        name="ml_scan",
    )(ql, ktl, vl, al, bml, brl, qc, ktc, vc, ac, bmc, brc)


def _outproj_kernel(x_ref, mla_ref, hs_ref, o_ref, gm_ref, wa_ref, wb_ref, gt_ref, g2_ref, sc_ref, sh_ref,
                    wpq_ref, sk_ref, x1_ref, h2_ref, s_ref):
    hs = hs_ref[0]
    gm = gm_ref[...]
    hn = jnp.concatenate([_rms(hs[:, h * ML_V:(h + 1) * ML_V]) * gm[:, h * ML_V:(h + 1) * ML_V]
                          for h in range(ML_HEADS)], axis=1)
    ml = (jax.nn.sigmoid(o_ref[0]) * hn).astype(BF16)
    mix = (jnp.dot(mla_ref[0], wa_ref[...], preferred_element_type=F32)
           + jnp.dot(ml, wb_ref[...], preferred_element_type=F32))
    x1 = x_ref[0] + gt_ref[0] * mix
    x1_ref[0] = x1
    h2 = _rms(x1) * g2_ref[...] * (1.0 + sc_ref[0]) + sh_ref[0]
    h2_ref[0] = h2
    qp = jnp.dot(h2.astype(BF16), wpq_ref[...], preferred_element_type=F32).astype(BF16)
    for hp in range(2 * PEER_HEADS):
        s_ref[0, hp] = lax.dot_general(sk_ref[hp], qp[:, hp * N_KEYS:(hp + 1) * N_KEYS],
                                       (((1,), (1,)), ((), ())), preferred_element_type=F32)


def _outproj(x, mla, hsum, cols, gm, wa, wb, gt1, g2, sc2, sh2, wpq, sk, tm):
    b, n, d = x.shape
    hw = mla.shape[2]
    nq = wpq.shape[1]
    mod = pl.BlockSpec((1, 1, d), lambda i, j: (i, 0, 0))
    return pl.pallas_call(
        _outproj_kernel,
        out_shape=(jax.ShapeDtypeStruct((b, n, d), F32), jax.ShapeDtypeStruct((b, n, d), F32),
                   jax.ShapeDtypeStruct((b, 2 * PEER_HEADS, N_KEYS, n), F32)),
        grid=(b, n // tm),
        in_specs=[pl.BlockSpec((1, tm, d), lambda i, j: (i, j, 0)),
                  pl.BlockSpec((1, tm, hw), lambda i, j: (i, j, 0)),
                  pl.BlockSpec((1, tm, hw), lambda i, j: (i, j, 0)),
                  pl.BlockSpec((1, tm, 512), lambda i, j: (i, j, C_O0 // 512)),
                  pl.BlockSpec((1, hw), lambda i, j: (0, 0)),
                  pl.BlockSpec((hw, d), lambda i, j: (0, 0)),
                  pl.BlockSpec((hw, d), lambda i, j: (0, 0)),
                  mod, pl.BlockSpec((1, d), lambda i, j: (0, 0)), mod, mod,
                  pl.BlockSpec((d, nq), lambda i, j: (0, 0)),
                  pl.BlockSpec((2 * PEER_HEADS, N_KEYS, N_KEYS), lambda i, j: (0, 0, 0))],
        out_specs=(pl.BlockSpec((1, tm, d), lambda i, j: (i, j, 0)),
                   pl.BlockSpec((1, tm, d), lambda i, j: (i, j, 0)),
                   pl.BlockSpec((1, 2 * PEER_HEADS, N_KEYS, tm), lambda i, j: (i, 0, 0, j))),
        compiler_params=_params(("parallel", "parallel")),
        name="outproj",
    )(x, mla, hsum, cols, gm, wa, wb, gt1, g2, sc2, sh2, wpq, sk)


def _topk_rows(work_refs, rid_ref, rows, k, vals_ref, ids_ref):
    def body(r, carry):
        for p, work_ref in enumerate(work_refs):
            s = work_ref[0:rows, :]
            rid = rid_ref[0:rows, :]
            m = jnp.max(s, axis=0, keepdims=True)
            i = jnp.min(jnp.where(s == m, rid, jnp.float32(3.0e38)), axis=0, keepdims=True)
            work_ref[0:rows, :] = jnp.where(rid == i, -jnp.inf, s)
            vals_ref[p, pl.ds(r, 1), :] = m
            ids_ref[p, pl.ds(r, 1), :] = i
        return carry

    lax.fori_loop(0, k, body, 0)
    return [(vals_ref[p], ids_ref[p]) for p in range(len(work_refs))]


def _pair_candidates(sv0, sv1):
    K, t = sv0.shape
    h = K // 2
    iid = lax.broadcasted_iota(I32, (h, t), 0)
    vals = [sv0[0:h] + sv1[0:1], sv0[h:K] + sv1[0:1]]
    ids = [iid * K, (iid + h) * K]
    for j in range(1, h):
        vals.append(sv0[0:h] + sv1[j:j + 1])
        ids.append(iid * K + j)
    vals.append(sv0[0:1] + sv1[h:K])
    ids.append(iid + h)
    return jnp.concatenate(vals, axis=0), jnp.concatenate(ids, axis=0).astype(F32)


HEADS_PER_ROUND = 4


def _route_kernel(s_ref, off_ref, par_ref, gw_ref, et_ref, gt_ref, work_ref, rid_ref, cid_ref,
                  v1_ref, i1_ref, v2_ref, i2_ref):
    K = PEER_TOPK
    nk, tt = rid_ref.shape
    rid_ref[...] = lax.broadcasted_iota(I32, (nk, tt), 0).astype(F32)
    hr = HEADS_PER_ROUND

    def round_(g, carry):
        heads = [g * hr + a for a in range(hr)]
        for a, h in enumerate(heads):
            work_ref[2 * a] = s_ref[0, 2 * h]
            work_ref[2 * a + 1] = s_ref[0, 2 * h + 1]
        first = _topk_rows([work_ref.at[b] for b in range(2 * hr)], rid_ref, nk, K, v1_ref, i1_ref)
        ncand = N_PAIR_CANDIDATES
        for a in range(hr):
            cand, cid = _pair_candidates(first[2 * a][0], first[2 * a + 1][0])
            work_ref[a, 0:ncand, :] = cand
            cid_ref[...] = cid
        second = _topk_rows([work_ref.at[a] for a in range(hr)], cid_ref, ncand, K, v2_ref, i2_ref)
        for a, h in enumerate(heads):
            best, pos = second[a]
            pos = pos.astype(I32)
            si0 = first[2 * a][1].astype(I32)
            si1 = first[2 * a + 1][1].astype(I32)
            isel = pos >> 4
            jsel = pos & (K - 1)
            e0 = jnp.zeros_like(pos)
            e1 = jnp.zeros_like(pos)
            for i in range(K):
                e0 = jnp.where(isel == i, si0[i:i + 1, :], e0)
                e1 = jnp.where(jsel == i, si1[i:i + 1, :], e1)
            ex = jnp.exp(best - best[0:1, :])
            sl = pl.ds(pl.multiple_of(h * K, K), K)
            et_ref[sl, :] = e0 * N_KEYS + e1
            gt_ref[sl, :] = ex / jnp.sum(ex, axis=0, keepdims=True)
        return carry

    lax.fori_loop(0, PEER_HEADS // hr, round_, 0)
    e = et_ref[...].T
    off_ref[0] = (e >> 1) * SUBLANES
    par_ref[0] = (e & 1).astype(F32)
    gw_ref[0] = gt_ref[...].T


def _route(scores, tt):
    b, hp, nk, n = scores.shape
    nsel = PEER_HEADS * PEER_TOPK
    ospec = pl.BlockSpec((1, tt, nsel), lambda i, j: (i, j, 0))
    return pl.pallas_call(
        _route_kernel,
        out_shape=(jax.ShapeDtypeStruct((b, n, nsel), I32), jax.ShapeDtypeStruct((b, n, nsel), F32),
                   jax.ShapeDtypeStruct((b, n, nsel), F32)),
        grid=(b, n // tt),
        in_specs=[pl.BlockSpec((1, hp, nk, tt), lambda i, j: (i, 0, 0, j))],
        out_specs=(ospec, ospec, ospec),
        scratch_shapes=[pltpu.VMEM((nsel, tt), I32), pltpu.VMEM((nsel, tt), F32),
                        pltpu.VMEM((2 * HEADS_PER_ROUND, nk, tt), F32), pltpu.VMEM((nk, tt), F32),
                        pltpu.VMEM((N_PAIR_CANDIDATES, tt), F32),
                        pltpu.VMEM((2 * HEADS_PER_ROUND, PEER_TOPK, tt), F32),
                        pltpu.VMEM((2 * HEADS_PER_ROUND, PEER_TOPK, tt), F32),
                        pltpu.VMEM((HEADS_PER_ROUND, PEER_TOPK, tt), F32),
                        pltpu.VMEM((HEADS_PER_ROUND, PEER_TOPK, tt), F32)],
        compiler_params=_params(("parallel", "parallel")),
        name="route",
    )(scores)


TILE_ROWS = 2 * SUBLANES
TOKENS_PER_ITER = 64


CHUNK_SEL = 16


def _chunk_tiles(tab_ref, row_ref, c):
    return jnp.concatenate(
        [pltpu.bitcast(tab_ref[pl.ds(pl.multiple_of(row_ref[c * CHUNK_SEL + kk], SUBLANES), SUBLANES), :], BF16)
         for kk in range(CHUNK_SEL)], axis=0)


def _init_spread(spread_ref, gather_ref=None):
    @pl.when(pl.program_id(0) == 0)
    def _():
        nsel, width = spread_ref.shape
        spread_ref[...] = (lax.broadcasted_iota(I32, (nsel, width), 1) // TILE_ROWS
                           == lax.broadcasted_iota(I32, (nsel, width), 0)).astype(BF16)
        if gather_ref is not None:
            gather_ref[...] = (lax.broadcasted_iota(I32, (width, nsel), 0) // TILE_ROWS
                               == lax.broadcasted_iota(I32, (width, nsel), 1)).astype(BF16)


def _row_masks(width):
    half = (lax.broadcasted_iota(I32, (1, width), 1) % 2).astype(F32)
    lane8 = lax.broadcasted_iota(I32, (SUBLANES, width), 1)
    diag = ((lane8 % TILE_ROWS) // 2 == lax.broadcasted_iota(I32, (SUBLANES, width), 0)).astype(F32)
    return half, diag


def _row_to_tile(row):
    return jnp.concatenate([row[:, s * LANES:(s + 1) * LANES] for s in range(SUBLANES)], axis=0)


def _split_bf16(v):
    hi = v.astype(BF16)
    return hi, (v - hi.astype(F32)).astype(BF16)


def _peer_u_kernel(off_ref, h_ref, tab_ref, par_ref, g_ref, c_ref, r_ref, spread_ref, gather_ref):
    tb, nsel = g_ref.shape
    _init_spread(spread_ref, gather_ref)
    half, diag = _row_masks(spread_ref.shape[1])

    cw = CHUNK_SEL * TILE_ROWS
    diag_c = diag[:, :cw]
    nt = (((1,), (1,)), ((), ()))

    def tok(i, carry):
        ts = [i * TOKENS_PER_ITER + u for u in range(TOKENS_PER_ITER)]
        rows = [off_ref.at[t] for t in ts]
        xs = [jnp.concatenate(_split_bf16(_row_to_tile(h_ref[pl.ds(t, 1), :])), axis=0) for t in ts]
        for c in range(nsel // CHUNK_SEL):
            for u, t in enumerate(ts):
                res = lax.dot_general(xs[u], _chunk_tiles(tab_ref, rows[u], c), nt, preferred_element_type=F32)
                r_ref[pl.ds(t, 1), c * cw:(c + 1) * cw] = jnp.sum(
                    (res[:SUBLANES] + res[SUBLANES:]) * diag_c, axis=0, keepdims=True)
        return carry

    lax.fori_loop(0, tb // TOKENS_PER_ITER, tok, 0)
    mine = jnp.dot(par_ref[...].astype(BF16), spread_ref[...], preferred_element_type=F32) == half
    r_hi, r_lo = _split_bf16(jnp.where(mine, r_ref[...], 0.0))
    both = jnp.dot(jnp.concatenate([r_hi, r_lo], axis=0), gather_ref[...], preferred_element_type=F32)
    pre = both[:tb] + both[tb:]
    c_ref[...] = g_ref[...] * (0.5 * pre * (1.0 + lax.erf(pre * (2.0 ** -0.5))))


def _smem_block(tb, nsel):
    return pl.BlockSpec((tb, nsel), lambda i: (i, 0), memory_space=pltpu.SMEM)


def _peer_u(off, h2, tab, par, gw, tb):
    t, nsel = gw.shape
    vspec = pl.BlockSpec((tb, nsel), lambda i: (i, 0))
    return pl.pallas_call(
        _peer_u_kernel,
        out_shape=jax.ShapeDtypeStruct((t, nsel), F32),
        grid=(t // tb,),
        in_specs=[_smem_block(tb, nsel),
                  pl.BlockSpec((tb, h2.shape[1]), lambda i: (i, 0)),
                  pl.BlockSpec(tab.shape, lambda i: (0, 0), pipeline_mode=pl.Buffered(1)),
                  vspec, vspec],
        out_specs=vspec,
        scratch_shapes=[pltpu.VMEM((tb, nsel * TILE_ROWS), F32),
                        pltpu.VMEM((nsel, nsel * TILE_ROWS), BF16), pltpu.VMEM((nsel * TILE_ROWS, nsel), BF16)],
        compiler_params=_params(("arbitrary",)),
        name="peer_u",
    )(off, h2, tab, par, gw)


def _peer_v_kernel(off_ref, c_ref, par_ref, tab_ref, x1_ref, gt_ref, o_ref, lhi_ref, llo_ref, spread_ref):
    tb, nsel = c_ref.shape
    _init_spread(spread_ref)
    half, diag = _row_masks(spread_ref.shape[1])
    c_hi, c_lo = _split_bf16(c_ref[...])
    ex = jnp.dot(jnp.concatenate([par_ref[...].astype(BF16), c_hi, c_lo], axis=0), spread_ref[...],
                 preferred_element_type=F32)
    mine = ex[:tb] == half
    lhi_ref[...] = jnp.where(mine, ex[tb:2 * tb], 0.0)
    llo_ref[...] = jnp.where(mine, ex[2 * tb:], 0.0)
    gt = gt_ref[0]

    cw = CHUNK_SEL * TILE_ROWS
    diag_c = diag[:, :cw]

    def tok(i, carry):
        ts = [i * TOKENS_PER_ITER + u for u in range(TOKENS_PER_ITER)]
        rows = [off_ref.at[t] for t in ts]
        accs = [jnp.zeros((TILE_ROWS, LANES), F32) for _ in ts]
        for c in range(nsel // CHUNK_SEL):
            cols = slice(c * cw, (c + 1) * cw)
            for u, t in enumerate(ts):
                left = jnp.concatenate([lhi_ref[pl.ds(t, 1), cols] * diag_c, llo_ref[pl.ds(t, 1), cols] * diag_c],
                                       axis=0).astype(BF16)
                accs[u] = accs[u] + jnp.dot(left, _chunk_tiles(tab_ref, rows[u], c), preferred_element_type=F32)
        for u, t in enumerate(ts):
            y = accs[u][:SUBLANES] + accs[u][SUBLANES:]
            y_row = jnp.concatenate([y[s:s + 1, :] for s in range(SUBLANES)], axis=1)
            o_ref[pl.ds(t, 1), :] = x1_ref[pl.ds(t, 1), :] + gt * y_row
        return carry

    lax.fori_loop(0, tb // TOKENS_PER_ITER, tok, 0)


def _peer_v(off, coef, par, tab, x1, gt2, tb, tok_per_batch):
    t, nsel = coef.shape
    d = x1.shape[1]
    bpb = tok_per_batch // tb
    vspec = pl.BlockSpec((tb, nsel), lambda i: (i, 0))
    return pl.pallas_call(
        _peer_v_kernel,
        out_shape=jax.ShapeDtypeStruct(x1.shape, F32),
        grid=(t // tb,),
        in_specs=[_smem_block(tb, nsel), vspec, vspec,
                  pl.BlockSpec(tab.shape, lambda i: (0, 0), pipeline_mode=pl.Buffered(1)),
                  pl.BlockSpec((tb, d), lambda i: (i, 0)),
                  pl.BlockSpec((1, 1, d), lambda i: (i // bpb, 0, 0))],
        out_specs=pl.BlockSpec((tb, d), lambda i: (i, 0)),
        scratch_shapes=[pltpu.VMEM((tb, nsel * TILE_ROWS), F32), pltpu.VMEM((tb, nsel * TILE_ROWS), F32),
                        pltpu.VMEM((nsel, nsel * TILE_ROWS), BF16)],
        compiler_params=_params(("arbitrary",)),
        name="peer_v",
    )(off, coef, par, tab, x1, gt2)


def _rope_tables(n):
    axis = MLA_ROPE // 2
    t = jnp.arange(n, dtype=F32)
    row = jnp.floor(t / GRID_W)
    col = t - row * GRID_W
    inv = ROPE_BASE ** (-jnp.arange(axis // 2, dtype=F32) * (2.0 / axis))
    ar = row[:, None] * inv
    ac = col[:, None] * inv
    cos = jnp.concatenate([jnp.cos(ar), jnp.cos(ar), jnp.cos(ac), jnp.cos(ac)], axis=1)
    sin = jnp.concatenate([-jnp.sin(ar), jnp.sin(ar), -jnp.sin(ac), jnp.sin(ac)], axis=1)
    return cos, sin


PACK_ROWS = 256


def _pack_kernel(t_ref, o_ref):
    n = t_ref.shape[0]
    rows = n // 2
    xb = t_ref[...].astype(BF16)
    col = lax.broadcasted_iota(I32, (rows, n), 1)
    row2 = 2 * lax.broadcasted_iota(I32, (rows, n), 0)
    even = pltpu.bitcast(jnp.dot((col == row2).astype(BF16), xb, preferred_element_type=F32), jnp.uint32)
    odd = pltpu.bitcast(jnp.dot((col == row2 + 1).astype(BF16), xb, preferred_element_type=F32), jnp.uint32)
    word = odd | (even >> 16)
    for s in range(SUBLANES):
        o_ref[pl.ds(s, rows, stride=SUBLANES), :] = word[:, s * LANES:(s + 1) * LANES]


def _pack_table(tab):
    e, d = tab.shape
    out_rows = PACK_ROWS // 2 * (d // LANES)
    return pl.pallas_call(
        _pack_kernel,
        out_shape=jax.ShapeDtypeStruct((e // 2 * (d // LANES), LANES), jnp.uint32),
        grid=(e // PACK_ROWS,),
        in_specs=[pl.BlockSpec((PACK_ROWS, d), lambda i: (i, 0))],
        out_specs=pl.BlockSpec((out_rows, LANES), lambda i: (i, 0)),
        compiler_params=_params(("parallel",)),
        name="pack_table",
    )(tab)


def _block(n, want):
    return want if n % want == 0 else n


def kernel(x, c, ctx, c_ctx, w_ada, b_ada, g_norm1, w_in, g_cq, w_uq, g_ckv, w_ukv, g_qn, g_kn, conv_qk, b_igate, b_fgate, g_mlstm, w_out, g_norm2, w_pq, sub_keys, expert_u, expert_v):
    B, N, D = x.shape
    NC = ctx.shape[1]
    assert w_ada.shape[0] == 1, "one layer"
    assert N % (ATTN_TQ * ATTN_TILES_PER_STEP) == 0 and NC % ML_CHUNK == 0 and D == SUBLANES * LANES
    q_rank = g_cq.shape[1]
    kv_rank = g_ckv.shape[1]
    mla_cols = q_rank + kv_rank + MLA_ROPE
    assert (q_rank, kv_rank) == (C_KV0 - C_Q0, C_KR0 - C_KV0)
    swap = jnp.arange(MLA_ROPE) ^ (MLA_ROPE // 4)

    cc = jnp.concatenate([c, c_ctx[None, :], jnp.zeros((16 - B - 1, D), F32)], axis=0)
    mod = _ada(cc, w_ada[0].astype(BF16), b_ada)
    sh1, sc1, gt1, sh2, sc2, gt2 = [mod[:, i * D:(i + 1) * D] for i in range(6)]
    lat = lambda m: m[:B].reshape(B, 1, D)
    ctxm = lambda m: jnp.broadcast_to(m[B:B + 1].reshape(1, 1, D), (B, 1, D))

    wi = w_in[0]
    n_qk = 2 * ML_HEADS * ML_QK
    n_v = ML_HEADS * ML_V
    n_g = 4 * ML_HEADS
    m0 = mla_cols
    w_cols = jnp.concatenate([
        wi[:, 0:mla_cols],
        wi[:, q_rank + kv_rank + swap],
        wi[:, m0:m0 + n_qk + 2 * n_v + n_g],
        jnp.zeros((D, C_END - C_G0 - n_g), F32)], axis=1).astype(BF16)
    assert w_cols.shape[1] == C_END
    w_gates_t = wi[:, m0 + n_qk + 2 * n_v:m0 + n_qk + 2 * n_v + n_g].T.astype(BF16)
    cols_l, gates_l = _inproj(x, g_norm1, lat(sc1), lat(sh1), w_cols, w_gates_t, _block(N, ROW_TILE))
    cols_c, gates_c = _inproj(ctx, g_norm1, ctxm(sc1), ctxm(sh1), w_cols, w_gates_t, _block(NC, ROW_TILE))

    wq = w_uq[0].reshape(q_rank, MLA_HEADS, MLA_QK)
    wuq = jnp.concatenate([wq, wq[:, :, MLA_NOPE + swap]], axis=2).reshape(q_rank, MLA_HEADS * 256).astype(BF16)
    wukv = w_ukv[0].astype(BF16)
    gqs, gks = g_qn[:, MLA_NOPE + swap], g_kn[:, MLA_NOPE + swap]
    cos_l, sin_l = _rope_tables(N)
    cos_c, sin_c = jnp.ones((NC, MLA_ROPE), F32), jnp.zeros((NC, MLA_ROPE), F32)
    q_l, k_l, v_l = _mla_prep(cols_l, g_cq, wuq, g_ckv, wukv, g_qn, gqs, g_kn, gks, cos_l, sin_l, _block(N, ROW_TILE))
    _, k_c, v_c = _mla_prep(cols_c, g_cq, wuq, g_ckv, wukv, g_qn, gqs, g_kn, gks, cos_c, sin_c, _block(NC, ROW_TILE))
    mla = _attention(q_l, k_c, k_l, v_c, v_l, ATTN_TQ)

    bias16 = jnp.concatenate([b_igate[0].reshape(-1), b_fgate[0].reshape(-1)])
    cw = conv_qk[0]
    pl_ = _ml_prep(cols_l, gates_l, cw, bias16[None, :], bias16[:, None], _block(N, ROW_TILE))
    pc_ = _ml_prep(cols_c, gates_c, cw, bias16[None, :], bias16[:, None], _block(NC, ROW_TILE))
    hsum = _ml_scan(*pl_, *pc_)

    wo = w_out[0].astype(BF16)
    hw = MLA_HEADS * MLA_V
    sk = sub_keys[0].reshape(2 * PEER_HEADS, N_KEYS, -1).astype(BF16)
    x1, h2, scores = _outproj(x, mla, hsum, cols_l, g_mlstm, wo[:hw], wo[hw:], lat(gt1), g_norm2, lat(sc2), lat(sh2),
                              w_pq[0].astype(BF16), sk, OUT_TILE)
    T = B * N
    nsel = PEER_HEADS * PEER_TOPK
    off, par, gw = [a.reshape(T, nsel) for a in _route(scores, ROUTE_TILE)]

    tb = PEER_TOKENS
    coef = _peer_u(off, h2.reshape(T, D), _pack_table(expert_u[0]), par, gw, tb)
    out = _peer_v(off, coef, par, _pack_table(expert_v[0]), x1.reshape(T, D), lat(gt2), tb, N)
    return out.reshape(B, N, D)
```

```python
import jax
import jax.numpy as jnp
from jax import lax
from jax.experimental import pallas as pl
from jax.experimental.pallas import tpu as pltpu

F32 = jnp.float32
BF16 = jnp.bfloat16
I32 = jnp.int32
EPS = 1e-6

GRID_W = 64
MLA_HEADS = 4
MLA_NOPE = 128
MLA_ROPE = 64
MLA_V = 128
MLA_QK = MLA_NOPE + MLA_ROPE
ROPE_BASE = 10000.0
ML_HEADS = 4
ML_QK = 64
ML_V = 128
ML_CHUNK = 64
PEER_HEADS = 8
N_KEYS = 128
PEER_TOPK = 16
N_PAIR_CANDIDATES = 80

LANES = 128
SUBLANES = 8
VMEM_LIMIT_BYTES = 56 * 1024 * 1024

ROW_TILE = 512
ATTN_TQ = 256
OUT_TILE = 256
ROUTE_TILE = 256
PEER_TOKENS = 256

C_Q0, C_KV0, C_KR0, C_KRS0, C_QK0, C_V0, C_O0, C_G0, C_END = 0, 256, 384, 448, 512, 1024, 1536, 2048, 2176
HIGHEST = lax.Precision.HIGHEST


def _params(sem):
    return pltpu.CompilerParams(dimension_semantics=sem, vmem_limit_bytes=VMEM_LIMIT_BYTES)


def _rms(x):
    return x * lax.rsqrt(jnp.mean(x * x, axis=-1, keepdims=True) + EPS)


def _ada_kernel(c_ref, w_ref, b_ref, o_ref):
    c = c_ref[...]
    s = c * jax.nn.sigmoid(c)
    o_ref[...] = jnp.dot(s.astype(BF16), w_ref[...], preferred_element_type=F32) + b_ref[...]


def _ada(cc, w, b):
    rows, d = cc.shape
    n = w.shape[1]
    bn = n // 4
    return pl.pallas_call(
        _ada_kernel,
        out_shape=jax.ShapeDtypeStruct((rows, n), F32),
        grid=(4,),
        in_specs=[pl.BlockSpec((rows, d), lambda j: (0, 0)),
                  pl.BlockSpec((d, bn), lambda j: (0, j)),
                  pl.BlockSpec((1, bn), lambda j: (0, j))],
        out_specs=pl.BlockSpec((rows, bn), lambda j: (0, j)),
        compiler_params=_params(("arbitrary",)),
        name="ada",
    )(cc, w, b)


def _inproj_kernel(x_ref, g_ref, sc_ref, sh_ref, w_ref, wg_ref, o_ref, gt_ref):
    x = x_ref[0]
    h = _rms(x) * g_ref[...] * (1.0 + sc_ref[0]) + sh_ref[0]
    hb = h.astype(BF16)
    o_ref[0] = jnp.dot(hb, w_ref[...], preferred_element_type=F32)
    gt_ref[0] = lax.dot_general(wg_ref[...], hb, (((1,), (1,)), ((), ())), preferred_element_type=F32)


def _inproj(x, g, sc, sh, w, wg, tm):
    b, n, d = x.shape
    nc = w.shape[1]
    return pl.pallas_call(
        _inproj_kernel,
        out_shape=(jax.ShapeDtypeStruct((b, n, nc), F32), jax.ShapeDtypeStruct((b, 16, n), F32)),
        grid=(b, n // tm),
        in_specs=[pl.BlockSpec((1, tm, d), lambda i, j: (i, j, 0)),
                  pl.BlockSpec((1, d), lambda i, j: (0, 0)),
                  pl.BlockSpec((1, 1, d), lambda i, j: (i, 0, 0)),
                  pl.BlockSpec((1, 1, d), lambda i, j: (i, 0, 0)),
                  pl.BlockSpec((d, nc), lambda i, j: (0, 0)),
                  pl.BlockSpec((16, d), lambda i, j: (0, 0))],
        out_specs=(pl.BlockSpec((1, tm, nc), lambda i, j: (i, j, 0)),
                   pl.BlockSpec((1, 16, tm), lambda i, j: (i, 0, j))),
        compiler_params=_params(("parallel", "parallel")),
        name="inproj",
    )(x, g, sc, sh, w, wg)


def _mla_prep_kernel(c_ref, gcq_ref, wuq_ref, gckv_ref, wukv_ref, gq_ref, gqs_ref, gk_ref, gks_ref,
                     cos_ref, sin_ref, q_ref, k_ref, v_ref):
    c = c_ref[0]
    cq = _rms(c[:, C_Q0:C_KV0]) * gcq_ref[...]
    ckv = _rms(c[:, C_KV0:C_KR0]) * gckv_ref[...]
    kr = c[:, C_KR0:C_KRS0]
    krs = c[:, C_KRS0:C_QK0]
    q_raw = jnp.dot(cq.astype(BF16), wuq_ref[...], preferred_element_type=F32)
    kv_raw = jnp.dot(ckv.astype(BF16), wukv_ref[...], preferred_element_type=F32)
    cos = cos_ref[...]
    sin = sin_ref[...]
    gq = gq_ref[...]
    gk = gk_ref[...]
    kr_ss = jnp.sum(kr * kr, axis=-1, keepdims=True)
    for h in range(MLA_HEADS):
        o = h * 256
        qn = q_raw[:, o:o + 128]
        qr = q_raw[:, o + 128:o + 192]
        qs = q_raw[:, o + 192:o + 256]
        ss = jnp.sum(qn * qn, axis=-1, keepdims=True) + jnp.sum(qr * qr, axis=-1, keepdims=True)
        r = lax.rsqrt(ss * (1.0 / MLA_QK) + EPS) * (MLA_QK ** -0.5)
        q_ref[0, h, :, 0:128] = (qn * r * gq[:, 0:128]).astype(BF16)
        q_ref[0, h, :, 128:192] = ((qr * r * gq[:, 128:192]) * cos + (qs * r * gqs_ref[...]) * sin).astype(BF16)
        kn = kv_raw[:, o:o + 128]
        ss = jnp.sum(kn * kn, axis=-1, keepdims=True) + kr_ss
        r = lax.rsqrt(ss * (1.0 / MLA_QK) + EPS)
        k_ref[0, h, :, 0:128] = (kn * r * gk[:, 0:128]).astype(BF16)
        k_ref[0, h, :, 128:192] = ((kr * r * gk[:, 128:192]) * cos + (krs * r * gks_ref[...]) * sin).astype(BF16)
        v_ref[0, h] = kv_raw[:, o + 128:o + 256].astype(BF16)


def _mla_prep(cols, gcq, wuq, gckv, wukv, gq, gqs, gk, gks, cos, sin, tm):
    b, n, _ = cols.shape
    full = lambda a: pl.BlockSpec(a.shape, lambda i, j: (0,) * a.ndim)
    return pl.pallas_call(
        _mla_prep_kernel,
        out_shape=(jax.ShapeDtypeStruct((b, MLA_HEADS, n, MLA_QK), BF16),
                   jax.ShapeDtypeStruct((b, MLA_HEADS, n, MLA_QK), BF16),
                   jax.ShapeDtypeStruct((b, MLA_HEADS, n, MLA_V), BF16)),
        grid=(b, n // tm),
        in_specs=[pl.BlockSpec((1, tm, 512), lambda i, j: (i, j, 0)),
                  full(gcq), full(wuq), full(gckv), full(wukv), full(gq), full(gqs), full(gk), full(gks),
                  pl.BlockSpec((tm, MLA_ROPE), lambda i, j: (j, 0)),
                  pl.BlockSpec((tm, MLA_ROPE), lambda i, j: (j, 0))],
        out_specs=(pl.BlockSpec((1, MLA_HEADS, tm, MLA_QK), lambda i, j: (i, 0, j, 0)),
                   pl.BlockSpec((1, MLA_HEADS, tm, MLA_QK), lambda i, j: (i, 0, j, 0)),
                   pl.BlockSpec((1, MLA_HEADS, tm, MLA_V), lambda i, j: (i, 0, j, 0))),
        compiler_params=_params(("parallel", "parallel")),
        name="mla_prep",
    )(cols, gcq, wuq, gckv, wukv, gq, gqs, gk, gks, cos, sin)


ATTN_TILES_PER_STEP = 4


def _attn_kernel(q_ref, kc_ref, kl_ref, vc_ref, vl_ref, o_ref):
    tq = q_ref.shape[2] // ATTN_TILES_PER_STEP
    nc = kc_ref.shape[2]
    kc, kl, vc, vl = kc_ref[0, 0], kl_ref[0, 0], vc_ref[0, 0], vl_ref[0, 0]
    nt = (((1,), (1,)), ((), ()))
    for j in range(ATTN_TILES_PER_STEP):
        q = q_ref[0, 0, j * tq:(j + 1) * tq, :]
        s = jnp.concatenate([lax.dot_general(q, kc, nt, preferred_element_type=F32),
                             lax.dot_general(q, kl, nt, preferred_element_type=F32)], axis=1)
        m = jnp.max(s, axis=-1, keepdims=True)
        p = jnp.exp(s - m)
        l = jnp.sum(p, axis=-1, keepdims=True)
        pb = p.astype(BF16)
        o = (jnp.dot(pb[:, :nc], vc, preferred_element_type=F32)
             + jnp.dot(pb[:, nc:], vl, preferred_element_type=F32))
        o_ref[0, j * tq:(j + 1) * tq, :] = (o / l).astype(BF16)


def _attention(q, k_c, k_l, v_c, v_l, tq):
    b, h, n, dk = q.shape
    nc = k_c.shape[2]
    dv = v_l.shape[3]
    tb = tq * ATTN_TILES_PER_STEP
    whole = lambda nn, dd: pl.BlockSpec((1, 1, nn, dd), lambda i, j, t: (i, j, 0, 0))
    return pl.pallas_call(
        _attn_kernel,
        out_shape=jax.ShapeDtypeStruct((b, n, h * dv), BF16),
        grid=(b, h, n // tb),
        in_specs=[pl.BlockSpec((1, 1, tb, dk), lambda i, j, t: (i, j, t, 0)),
                  whole(nc, dk), whole(n, dk), whole(nc, dv), whole(n, dv)],
        out_specs=pl.BlockSpec((1, tb, dv), lambda i, j, t: (i, t, j)),
        compiler_params=_params(("parallel", "parallel", "parallel")),
        name="attention",
    )(q, k_c, k_l, v_c, v_l)


def _ml_prep_kernel(qk_ref, prev_ref, next_ref, vin_ref, g_ref, gt_ref, cw_ref, bc_ref, br_ref,
                    q_ref, kt_ref, v_ref, a_ref, bm_ref, brow_ref):
    tn = qk_ref.shape[1]
    j = pl.program_id(1)
    nj = pl.num_programs(1)
    u = qk_ref[0]
    row = lax.broadcasted_iota(I32, (tn, 1), 0)
    before = jnp.where(j == 0, 0.0, prev_ref[0, SUBLANES - 1:SUBLANES, :])
    after = jnp.where(j == nj - 1, 0.0, next_ref[0, 0:1, :])
    up = jnp.where(row == 0, before, pltpu.roll(u, 1, 0))
    un = jnp.where(row == tn - 1, after, pltpu.roll(u, tn - 1, 0))
    cw = cw_ref[...]
    y = up * cw[0:1, :] + u * cw[1:2, :] + un * cw[2:3, :]
    y = y * jax.nn.sigmoid(y)
    hq = ML_HEADS * ML_QK
    for h in range(ML_HEADS):
        q_ref[0, h] = (y[:, h * ML_QK:(h + 1) * ML_QK] * (ML_QK ** -0.5)).astype(BF16)
    kt = y[:, hq:2 * hq].T
    L = ML_CHUNK
    for c in range(tn // L):
        kt_ref[0, c] = kt[:, c * L:(c + 1) * L].astype(BF16)
    v_ref[0] = vin_ref[0].astype(BF16)
    g = g_ref[0][:, 0:16] + bc_ref[...]
    ig_c = g[:, 0:8]
    lf_c = jax.nn.log_sigmoid(g[:, 8:16])
    gt = gt_ref[0] + br_ref[...]
    ig_r = gt[0:8, :]
    lf_r = jax.nn.log_sigmoid(gt[8:16, :])
    ti = lax.broadcasted_iota(I32, (L, L), 0)
    si = lax.broadcasted_iota(I32, (L, L), 1)
    lower = (si <= ti).astype(F32)
    upper = (si >= ti).astype(F32)
    lane_fwd = lax.broadcasted_iota(I32, (L, 8), 1) < ML_HEADS
    row_l = lax.broadcasted_iota(I32, (L, 8), 0)
    row_fwd = lax.broadcasted_iota(I32, (8, L), 0) < ML_HEADS
    for c in range(tn // L):
        lo = c * L
        lfc = lf_c[lo:lo + L, :]
        cf_c = jnp.where(
            lane_fwd,
            jnp.dot(lower, lfc, precision=HIGHEST, preferred_element_type=F32),
            jnp.dot(upper, lfc, precision=HIGHEST, preferred_element_type=F32))
        a_ref[0, lo:lo + L, :] = cf_c
        pre = suf = ig_c[lo:lo + L, :] - cf_c
        d = 1
        while d < L:
            pre = jnp.maximum(pre, jnp.where(row_l >= d, pltpu.roll(pre, d, 0), -jnp.inf))
            suf = jnp.maximum(suf, jnp.where(row_l < L - d, pltpu.roll(suf, L - d, 0), -jnp.inf))
            d *= 2
        bm_ref[0, lo:lo + L, :] = jnp.where(lane_fwd, pre, suf)
        lfr = lf_r[:, lo:lo + L]
        cf_r = jnp.where(
            row_fwd,
            jnp.dot(lfr, upper, precision=HIGHEST, preferred_element_type=F32),
            jnp.dot(lfr, lower, precision=HIGHEST, preferred_element_type=F32))
        brow_ref[0, c] = ig_r[:, lo:lo + L] - cf_r


def _ml_prep(cols, gates_t, cw, bias_col, bias_row, tn):
    b, n, _ = cols.shape
    r8 = tn // SUBLANES
    last8 = n // SUBLANES - 1
    L = ML_CHUNK
    hq = ML_HEADS * ML_QK
    return pl.pallas_call(
        _ml_prep_kernel,
        out_shape=(jax.ShapeDtypeStruct((b, ML_HEADS, n, ML_QK), BF16),
                   jax.ShapeDtypeStruct((b, n // L, hq, L), BF16),
                   jax.ShapeDtypeStruct((b, n, ML_HEADS * ML_V), BF16),
                   jax.ShapeDtypeStruct((b, n, 8), F32),
                   jax.ShapeDtypeStruct((b, n, 8), F32),
                   jax.ShapeDtypeStruct((b, n // L, 8, L), F32)),
        grid=(b, n // tn),
        in_specs=[pl.BlockSpec((1, tn, 512), lambda i, j: (i, j, C_QK0 // 512)),
                  pl.BlockSpec((1, SUBLANES, 512), lambda i, j: (i, jnp.maximum(j * r8 - 1, 0), C_QK0 // 512)),
                  pl.BlockSpec((1, SUBLANES, 512), lambda i, j: (i, jnp.minimum((j + 1) * r8, last8), C_QK0 // 512)),
                  pl.BlockSpec((1, tn, 512), lambda i, j: (i, j, C_V0 // 512)),
                  pl.BlockSpec((1, tn, 128), lambda i, j: (i, j, C_G0 // 128)),
                  pl.BlockSpec((1, 16, tn), lambda i, j: (i, 0, j)),
                  pl.BlockSpec((3, 512), lambda i, j: (0, 0)),
                  pl.BlockSpec((1, 16), lambda i, j: (0, 0)),
                  pl.BlockSpec((16, 1), lambda i, j: (0, 0))],
        out_specs=(pl.BlockSpec((1, ML_HEADS, tn, ML_QK), lambda i, j: (i, 0, j, 0)),
                   pl.BlockSpec((1, tn // L, hq, L), lambda i, j: (i, j, 0, 0)),
                   pl.BlockSpec((1, tn, ML_HEADS * ML_V), lambda i, j: (i, j, 0)),
                   pl.BlockSpec((1, tn, 8), lambda i, j: (i, j, 0)),
                   pl.BlockSpec((1, tn, 8), lambda i, j: (i, j, 0)),
                   pl.BlockSpec((1, tn // L, 8, L), lambda i, j: (i, j, 0, 0))),
        compiler_params=_params(("parallel", "parallel")),
        name="ml_prep",
    )(cols, cols, cols, cols, cols, gates_t, cw, bias_col, bias_row)


def _ml_chunk(qb, kt, vaug, a_c, bm_c, b_r, ct, m_prev, fwd, need_out):
    L = qb.shape[0]
    ti = lax.broadcasted_iota(I32, (L, L), 0)
    si = lax.broadcasted_iota(I32, (L, L), 1)
    mask = (si <= ti) if fwd else (si >= ti)
    a_b = jnp.broadcast_to(a_c, (L, LANES))
    inter = a_b + m_prev
    m_t = jnp.maximum(inter, a_b + jnp.broadcast_to(bm_c, (L, LANES)))
    last = L - 1 if fwd else 0
    m_new = m_t[last:last + 1, :]
    a_last = a_b[last:last + 1, :]
    w_r = jnp.exp(a_last[:, :L] + b_r - m_new[:, :L])
    decay = jnp.exp(a_last + m_prev - m_new)
    decay = jnp.concatenate([decay, decay], axis=1)
    kw = (kt.astype(F32) * w_r).astype(BF16)
    if not need_out:
        return decay * ct + jnp.dot(kw, vaug, preferred_element_type=F32), m_new, None
    s = jnp.dot(qb, kt, preferred_element_type=F32)
    dmat = jnp.where(mask, a_b[:, :L] + b_r, -jnp.inf)
    wmat = jnp.exp(dmat - m_t[:, :L]) * s
    sc = jnp.exp(inter - m_t)
    top = jnp.concatenate([wmat.astype(BF16), (sc[:, :qb.shape[1]] * qb.astype(F32)).astype(BF16)], axis=1)
    bot = jnp.concatenate([kw, jnp.zeros(kw.shape, BF16)], axis=1)
    res = jnp.dot(jnp.concatenate([top, bot], axis=0), jnp.concatenate([vaug, ct.astype(BF16)], axis=0),
                  preferred_element_type=F32)
    num = res[:L, 0:ML_V]
    den = res[:L, ML_V:2 * ML_V]
    return decay * ct + res[L:], m_new, num / jnp.maximum(jnp.abs(den), jnp.exp(-m_t))


def _ml_scan_kernel(ql_ref, ktl_ref, vl_ref, al_ref, bml_ref, brl_ref,
                    qc_ref, ktc_ref, vc_ref, ac_ref, bmc_ref, brc_ref, out_ref, st_ref, m_ref):
    L = ML_CHUNK
    ncl = ql_ref.shape[2] // L
    ncc = qc_ref.shape[2] // L
    ones_blk = jnp.ones((L, ML_V), BF16)
    st_ref[...] = jnp.zeros(st_ref.shape, F32)
    m_ref[...] = jnp.zeros(m_ref.shape, F32)

    def step(refs, c, d, fwd, need_out):
        q_ref, kt_ref, v_ref, a_ref, bm_ref, br_ref = refs
        sl = pl.ds(pl.multiple_of(c * L, L), L)
        a_all = a_ref[0, sl, :]
        bm_all = bm_ref[0, sl, :]
        br_all = br_ref[0, c]
        v_all = v_ref[0, sl, :]
        kt_all = kt_ref[0, c]
        hs = []
        for h in range(ML_HEADS):
            j = d * ML_HEADS + h
            vaug = jnp.concatenate([v_all[:, h * ML_V:(h + 1) * ML_V], ones_blk], axis=1)
            ct_new, m_new, hh = _ml_chunk(
                q_ref[0, h, sl, :], kt_all[h * ML_QK:(h + 1) * ML_QK, :], vaug,
                a_all[:, j:j + 1], bm_all[:, j:j + 1], br_all[j:j + 1, :], st_ref[j], m_ref[j][0:1, :],
                fwd, need_out)
            st_ref[j] = ct_new
            m_ref[j] = jnp.broadcast_to(m_new, (SUBLANES, LANES))
            hs.append(hh)
        return hs, sl

    ctx_refs = (qc_ref, ktc_ref, vc_ref, ac_ref, bmc_ref, brc_ref)
    lat_refs = (ql_ref, ktl_ref, vl_ref, al_ref, bml_ref, brl_ref)

    out_ref[...] = jnp.zeros(out_ref.shape, F32)

    def ctx_body(i, carry):
        step(ctx_refs, i, 0, True, False)
        step(ctx_refs, ncc - 1 - i, 1, False, False)
        return carry

    def lat_body(i, carry):
        for d, fwd in ((0, True), (1, False)):
            hs, sl = step(lat_refs, i if fwd else ncl - 1 - i, d, fwd, True)
            out_ref[0, sl, :] = out_ref[0, sl, :] + jnp.concatenate(hs, axis=1)
        return carry

    lax.fori_loop(0, ncc, ctx_body, 0)
    lax.fori_loop(0, ncl, lat_body, 0)


def _ml_scan(ql, ktl, vl, al, bml, brl, qc, ktc, vc, ac, bmc, brc):
    b, _, n, _ = ql.shape
    nctx = qc.shape[2]
    L = ML_CHUNK
    hv = ML_HEADS * ML_V
    hq = ML_HEADS * ML_QK
    qspec = lambda nn: pl.BlockSpec((1, ML_HEADS, nn, ML_QK), lambda i: (i, 0, 0, 0))
    ktspec = lambda nn: pl.BlockSpec((1, nn // L, hq, L), lambda i: (i, 0, 0, 0))
    vspec = lambda nn: pl.BlockSpec((1, nn, hv), lambda i: (i, 0, 0))
    aspec = lambda nn: pl.BlockSpec((1, nn, 8), lambda i: (i, 0, 0))
    rspec = lambda nn: pl.BlockSpec((1, nn // L, 8, L), lambda i: (i, 0, 0, 0))
    return pl.pallas_call(
        _ml_scan_kernel,
        out_shape=jax.ShapeDtypeStruct((b, n, hv), F32),
        grid=(b,),
        in_specs=[qspec(n), ktspec(n), vspec(n), aspec(n), aspec(n), rspec(n),
                  qspec(nctx), ktspec(nctx), vspec(nctx), aspec(nctx), aspec(nctx), rspec(nctx)],
        out_specs=pl.BlockSpec((1, n, hv), lambda i: (i, 0, 0)),
        scratch_shapes=[pltpu.VMEM((2 * ML_HEADS, ML_QK, 2 * ML_V), F32),
                        pltpu.VMEM((2 * ML_HEADS, SUBLANES, LANES), F32)],
        compiler_params=_params(("parallel",)),
        name="ml_scan",
    )(ql, ktl, vl, al, bml, brl, qc, ktc, vc, ac, bmc, brc)


def _outproj_kernel(x_ref, mla_ref, hs_ref, o_ref, gm_ref, wa_ref, wb_ref, gt_ref, g2_ref, sc_ref, sh_ref,
                    wpq_ref, sk_ref, x1_ref, h2_ref, s_ref):
    hs = hs_ref[0]
    gm = gm_ref[...]
    hn = jnp.concatenate([_rms(hs[:, h * ML_V:(h + 1) * ML_V]) * gm[:, h * ML_V:(h + 1) * ML_V]
                          for h in range(ML_HEADS)], axis=1)
    ml = (jax.nn.sigmoid(o_ref[0]) * hn).astype(BF16)
    mix = (jnp.dot(mla_ref[0], wa_ref[...], preferred_element_type=F32)
           + jnp.dot(ml, wb_ref[...], preferred_element_type=F32))
    x1 = x_ref[0] + gt_ref[0] * mix
    x1_ref[0] = x1
    h2 = _rms(x1) * g2_ref[...] * (1.0 + sc_ref[0]) + sh_ref[0]
    h2_ref[0] = h2
    qp = jnp.dot(h2.astype(BF16), wpq_ref[...], preferred_element_type=F32).astype(BF16)
    for hp in range(2 * PEER_HEADS):
        s_ref[0, hp] = lax.dot_general(sk_ref[hp], qp[:, hp * N_KEYS:(hp + 1) * N_KEYS],
                                       (((1,), (1,)), ((), ())), preferred_element_type=F32)


def _outproj(x, mla, hsum, cols, gm, wa, wb, gt1, g2, sc2, sh2, wpq, sk, tm):
    b, n, d = x.shape
    hw = mla.shape[2]
    nq = wpq.shape[1]
    mod = pl.BlockSpec((1, 1, d), lambda i, j: (i, 0, 0))
    return pl.pallas_call(
        _outproj_kernel,
        out_shape=(jax.ShapeDtypeStruct((b, n, d), F32), jax.ShapeDtypeStruct((b, n, d), F32),
                   jax.ShapeDtypeStruct((b, 2 * PEER_HEADS, N_KEYS, n), F32)),
        grid=(b, n // tm),
        in_specs=[pl.BlockSpec((1, tm, d), lambda i, j: (i, j, 0)),
                  pl.BlockSpec((1, tm, hw), lambda i, j: (i, j, 0)),
                  pl.BlockSpec((1, tm, hw), lambda i, j: (i, j, 0)),
                  pl.BlockSpec((1, tm, 512), lambda i, j: (i, j, C_O0 // 512)),
                  pl.BlockSpec((1, hw), lambda i, j: (0, 0)),
                  pl.BlockSpec((hw, d), lambda i, j: (0, 0)),
                  pl.BlockSpec((hw, d), lambda i, j: (0, 0)),
                  mod, pl.BlockSpec((1, d), lambda i, j: (0, 0)), mod, mod,
                  pl.BlockSpec((d, nq), lambda i, j: (0, 0)),
                  pl.BlockSpec((2 * PEER_HEADS, N_KEYS, N_KEYS), lambda i, j: (0, 0, 0))],
        out_specs=(pl.BlockSpec((1, tm, d), lambda i, j: (i, j, 0)),
                   pl.BlockSpec((1, tm, d), lambda i, j: (i, j, 0)),
                   pl.BlockSpec((1, 2 * PEER_HEADS, N_KEYS, tm), lambda i, j: (i, 0, 0, j))),
        compiler_params=_params(("parallel", "parallel")),
        name="outproj",
    )(x, mla, hsum, cols, gm, wa, wb, gt1, g2, sc2, sh2, wpq, sk)


def _topk_rows(work_refs, rid_ref, rows, k, vals_ref, ids_ref):
    def body(r, carry):
        for p, work_ref in enumerate(work_refs):
            s = work_ref[0:rows, :]
            rid = rid_ref[0:rows, :]
            m = jnp.max(s, axis=0, keepdims=True)
            i = jnp.min(jnp.where(s == m, rid, jnp.float32(3.0e38)), axis=0, keepdims=True)
            work_ref[0:rows, :] = jnp.where(rid == i, -jnp.inf, s)
            vals_ref[p, pl.ds(r, 1), :] = m
            ids_ref[p, pl.ds(r, 1), :] = i
        return carry

    lax.fori_loop(0, k, body, 0)
    return [(vals_ref[p], ids_ref[p]) for p in range(len(work_refs))]


def _pair_candidates(sv0, sv1):
    K, t = sv0.shape
    h = K // 2
    iid = lax.broadcasted_iota(I32, (h, t), 0)
    vals = [sv0[0:h] + sv1[0:1], sv0[h:K] + sv1[0:1]]
    ids = [iid * K, (iid + h) * K]
    for j in range(1, h):
        vals.append(sv0[0:h] + sv1[j:j + 1])
        ids.append(iid * K + j)
    vals.append(sv0[0:1] + sv1[h:K])
    ids.append(iid + h)
    return jnp.concatenate(vals, axis=0), jnp.concatenate(ids, axis=0).astype(F32)


HEADS_PER_ROUND = 8


def _route_kernel(s_ref, off_ref, par_ref, gw_ref, et_ref, gt_ref, work_ref, rid_ref, cid_ref,
                  v1_ref, i1_ref, v2_ref, i2_ref):
    K = PEER_TOPK
    nk, tt = rid_ref.shape
    rid_ref[...] = lax.broadcasted_iota(I32, (nk, tt), 0).astype(F32)
    hr = HEADS_PER_ROUND

    def round_(g, carry):
        heads = [g * hr + a for a in range(hr)]
        for a, h in enumerate(heads):
            work_ref[2 * a] = s_ref[0, 2 * h]
            work_ref[2 * a + 1] = s_ref[0, 2 * h + 1]
        first = _topk_rows([work_ref.at[b] for b in range(2 * hr)], rid_ref, nk, K, v1_ref, i1_ref)
        ncand = N_PAIR_CANDIDATES
        for a in range(hr):
            cand, cid = _pair_candidates(first[2 * a][0], first[2 * a + 1][0])
            work_ref[a, 0:ncand, :] = cand
            cid_ref[...] = cid
        second = _topk_rows([work_ref.at[a] for a in range(hr)], cid_ref, ncand, K, v2_ref, i2_ref)
        for a, h in enumerate(heads):
            best, pos = second[a]
            pos = pos.astype(I32)
            si0 = first[2 * a][1].astype(I32)
            si1 = first[2 * a + 1][1].astype(I32)
            isel = pos >> 4
            jsel = pos & (K - 1)
            e0 = jnp.zeros_like(pos)
            e1 = jnp.zeros_like(pos)
            for i in range(K):
                e0 = jnp.where(isel == i, si0[i:i + 1, :], e0)
                e1 = jnp.where(jsel == i, si1[i:i + 1, :], e1)
            ex = jnp.exp(best - best[0:1, :])
            sl = pl.ds(pl.multiple_of(h * K, K), K)
            et_ref[sl, :] = e0 * N_KEYS + e1
            gt_ref[sl, :] = ex / jnp.sum(ex, axis=0, keepdims=True)
        return carry

    lax.fori_loop(0, PEER_HEADS // hr, round_, 0)
    e = et_ref[...].T
    off_ref[0] = (e >> 1) * SUBLANES
    par_ref[0] = (e & 1).astype(F32)
    gw_ref[0] = gt_ref[...].T


def _route(scores, tt):
    b, hp, nk, n = scores.shape
    nsel = PEER_HEADS * PEER_TOPK
    ospec = pl.BlockSpec((1, tt, nsel), lambda i, j: (i, j, 0))
    return pl.pallas_call(
        _route_kernel,
        out_shape=(jax.ShapeDtypeStruct((b, n, nsel), I32), jax.ShapeDtypeStruct((b, n, nsel), F32),
                   jax.ShapeDtypeStruct((b, n, nsel), F32)),
        grid=(b, n // tt),
        in_specs=[pl.BlockSpec((1, hp, nk, tt), lambda i, j: (i, 0, 0, j))],
        out_specs=(ospec, ospec, ospec),
        scratch_shapes=[pltpu.VMEM((nsel, tt), I32), pltpu.VMEM((nsel, tt), F32),
                        pltpu.VMEM((2 * HEADS_PER_ROUND, nk, tt), F32), pltpu.VMEM((nk, tt), F32),
                        pltpu.VMEM((N_PAIR_CANDIDATES, tt), F32),
                        pltpu.VMEM((2 * HEADS_PER_ROUND, PEER_TOPK, tt), F32),
                        pltpu.VMEM((2 * HEADS_PER_ROUND, PEER_TOPK, tt), F32),
                        pltpu.VMEM((HEADS_PER_ROUND, PEER_TOPK, tt), F32),
                        pltpu.VMEM((HEADS_PER_ROUND, PEER_TOPK, tt), F32)],
        compiler_params=_params(("parallel", "parallel")),
        name="route",
    )(scores)


TILE_ROWS = 2 * SUBLANES
TOKENS_PER_ITER = 64


CHUNK_SEL = 16


def _chunk_tiles(tab_ref, row_ref, c):
    return jnp.concatenate(
        [pltpu.bitcast(tab_ref[pl.ds(pl.multiple_of(row_ref[c * CHUNK_SEL + kk], SUBLANES), SUBLANES), :], BF16)
         for kk in range(CHUNK_SEL)], axis=0)


def _init_spread(spread_ref, gather_ref=None):
    @pl.when(pl.program_id(0) == 0)
    def _():
        nsel, width = spread_ref.shape
        spread_ref[...] = (lax.broadcasted_iota(I32, (nsel, width), 1) // TILE_ROWS
                           == lax.broadcasted_iota(I32, (nsel, width), 0)).astype(BF16)
        if gather_ref is not None:
            gather_ref[...] = (lax.broadcasted_iota(I32, (width, nsel), 0) // TILE_ROWS
                               == lax.broadcasted_iota(I32, (width, nsel), 1)).astype(BF16)


def _row_masks(width):
    half = (lax.broadcasted_iota(I32, (1, width), 1) % 2).astype(F32)
    lane8 = lax.broadcasted_iota(I32, (SUBLANES, width), 1)
    diag = ((lane8 % TILE_ROWS) // 2 == lax.broadcasted_iota(I32, (SUBLANES, width), 0)).astype(F32)
    return half, diag


def _row_to_tile(row):
    return jnp.concatenate([row[:, s * LANES:(s + 1) * LANES] for s in range(SUBLANES)], axis=0)


def _split_bf16(v):
    hi = v.astype(BF16)
    return hi, (v - hi.astype(F32)).astype(BF16)


def _peer_u_kernel(off_ref, h_ref, tab_ref, par_ref, g_ref, c_ref, r_ref, spread_ref, gather_ref):
    tb, nsel = g_ref.shape
    _init_spread(spread_ref, gather_ref)
    half, diag = _row_masks(spread_ref.shape[1])

    cw = CHUNK_SEL * TILE_ROWS
    diag_c = diag[:, :cw]
    nt = (((1,), (1,)), ((), ()))

    def tok(i, carry):
        ts = [i * TOKENS_PER_ITER + u for u in range(TOKENS_PER_ITER)]
        rows = [off_ref.at[t] for t in ts]
        xs = [jnp.concatenate(_split_bf16(_row_to_tile(h_ref[pl.ds(t, 1), :])), axis=0) for t in ts]
        for c in range(nsel // CHUNK_SEL):
            for u, t in enumerate(ts):
                res = lax.dot_general(xs[u], _chunk_tiles(tab_ref, rows[u], c), nt, preferred_element_type=F32)
                r_ref[pl.ds(t, 1), c * cw:(c + 1) * cw] = jnp.sum(
                    (res[:SUBLANES] + res[SUBLANES:]) * diag_c, axis=0, keepdims=True)
        return carry

    lax.fori_loop(0, tb // TOKENS_PER_ITER, tok, 0)
    mine = jnp.dot(par_ref[...].astype(BF16), spread_ref[...], preferred_element_type=F32) == half
    r_hi, r_lo = _split_bf16(jnp.where(mine, r_ref[...], 0.0))
    both = jnp.dot(jnp.concatenate([r_hi, r_lo], axis=0), gather_ref[...], preferred_element_type=F32)
    pre = both[:tb] + both[tb:]
    c_ref[...] = g_ref[...] * (0.5 * pre * (1.0 + lax.erf(pre * (2.0 ** -0.5))))


def _smem_block(tb, nsel):
    return pl.BlockSpec((tb, nsel), lambda i: (i, 0), memory_space=pltpu.SMEM)


def _peer_u(off, h2, tab, par, gw, tb):
    t, nsel = gw.shape
    vspec = pl.BlockSpec((tb, nsel), lambda i: (i, 0))
    return pl.pallas_call(
        _peer_u_kernel,
        out_shape=jax.ShapeDtypeStruct((t, nsel), F32),
        grid=(t // tb,),
        in_specs=[_smem_block(tb, nsel),
                  pl.BlockSpec((tb, h2.shape[1]), lambda i: (i, 0)),
                  pl.BlockSpec(tab.shape, lambda i: (0, 0), pipeline_mode=pl.Buffered(1)),
                  vspec, vspec],
        out_specs=vspec,
        scratch_shapes=[pltpu.VMEM((tb, nsel * TILE_ROWS), F32),
                        pltpu.VMEM((nsel, nsel * TILE_ROWS), BF16), pltpu.VMEM((nsel * TILE_ROWS, nsel), BF16)],
        compiler_params=_params(("arbitrary",)),
        name="peer_u",
    )(off, h2, tab, par, gw)


def _peer_v_kernel(off_ref, c_ref, par_ref, tab_ref, x1_ref, gt_ref, o_ref, lhi_ref, llo_ref, spread_ref):
    tb, nsel = c_ref.shape
    _init_spread(spread_ref)
    half, diag = _row_masks(spread_ref.shape[1])
    c_hi, c_lo = _split_bf16(c_ref[...])
    ex = jnp.dot(jnp.concatenate([par_ref[...].astype(BF16), c_hi, c_lo], axis=0), spread_ref[...],
                 preferred_element_type=F32)
    mine = ex[:tb] == half
    lhi_ref[...] = jnp.where(mine, ex[tb:2 * tb], 0.0)
    llo_ref[...] = jnp.where(mine, ex[2 * tb:], 0.0)
    gt = gt_ref[0]

    cw = CHUNK_SEL * TILE_ROWS
    diag_c = diag[:, :cw]

    def tok(i, carry):
        ts = [i * TOKENS_PER_ITER + u for u in range(TOKENS_PER_ITER)]
        rows = [off_ref.at[t] for t in ts]
        accs = [jnp.zeros((TILE_ROWS, LANES), F32) for _ in ts]
        for c in range(nsel // CHUNK_SEL):
            cols = slice(c * cw, (c + 1) * cw)
            for u, t in enumerate(ts):
                left = jnp.concatenate([lhi_ref[pl.ds(t, 1), cols] * diag_c, llo_ref[pl.ds(t, 1), cols] * diag_c],
                                       axis=0).astype(BF16)
                accs[u] = accs[u] + jnp.dot(left, _chunk_tiles(tab_ref, rows[u], c), preferred_element_type=F32)
        for u, t in enumerate(ts):
            y = accs[u][:SUBLANES] + accs[u][SUBLANES:]
            y_row = jnp.concatenate([y[s:s + 1, :] for s in range(SUBLANES)], axis=1)
            o_ref[pl.ds(t, 1), :] = x1_ref[pl.ds(t, 1), :] + gt * y_row
        return carry

    lax.fori_loop(0, tb // TOKENS_PER_ITER, tok, 0)


def _peer_v(off, coef, par, tab, x1, gt2, tb, tok_per_batch):
    t, nsel = coef.shape
    d = x1.shape[1]
    bpb = tok_per_batch // tb
    vspec = pl.BlockSpec((tb, nsel), lambda i: (i, 0))
    return pl.pallas_call(
        _peer_v_kernel,
        out_shape=jax.ShapeDtypeStruct(x1.shape, F32),
        grid=(t // tb,),
        in_specs=[_smem_block(tb, nsel), vspec, vspec,
                  pl.BlockSpec(tab.shape, lambda i: (0, 0), pipeline_mode=pl.Buffered(1)),
                  pl.BlockSpec((tb, d), lambda i: (i, 0)),
                  pl.BlockSpec((1, 1, d), lambda i: (i // bpb, 0, 0))],
        out_specs=pl.BlockSpec((tb, d), lambda i: (i, 0)),
        scratch_shapes=[pltpu.VMEM((tb, nsel * TILE_ROWS), F32), pltpu.VMEM((tb, nsel * TILE_ROWS), F32),
                        pltpu.VMEM((nsel, nsel * TILE_ROWS), BF16)],
        compiler_params=_params(("arbitrary",)),
        name="peer_v",
    )(off, coef, par, tab, x1, gt2)


def _rope_tables(n):
    axis = MLA_ROPE // 2
    t = jnp.arange(n, dtype=F32)
    row = jnp.floor(t / GRID_W)
    col = t - row * GRID_W
    inv = ROPE_BASE ** (-jnp.arange(axis // 2, dtype=F32) * (2.0 / axis))
    ar = row[:, None] * inv
    ac = col[:, None] * inv
    cos = jnp.concatenate([jnp.cos(ar), jnp.cos(ar), jnp.cos(ac), jnp.cos(ac)], axis=1)
    sin = jnp.concatenate([-jnp.sin(ar), jnp.sin(ar), -jnp.sin(ac), jnp.sin(ac)], axis=1)
    return cos, sin


PACK_ROWS = 256


def _pack_kernel(t_ref, o_ref):
    n = t_ref.shape[0]
    rows = n // 2
    xb = t_ref[...].astype(BF16)
    col = lax.broadcasted_iota(I32, (rows, n), 1)
    row2 = 2 * lax.broadcasted_iota(I32, (rows, n), 0)
    even = pltpu.bitcast(jnp.dot((col == row2).astype(BF16), xb, preferred_element_type=F32), jnp.uint32)
    odd = pltpu.bitcast(jnp.dot((col == row2 + 1).astype(BF16), xb, preferred_element_type=F32), jnp.uint32)
    word = odd | (even >> 16)
    for s in range(SUBLANES):
        o_ref[pl.ds(s, rows, stride=SUBLANES), :] = word[:, s * LANES:(s + 1) * LANES]


def _pack_table(tab):
    e, d = tab.shape
    out_rows = PACK_ROWS // 2 * (d // LANES)
    return pl.pallas_call(
        _pack_kernel,
        out_shape=jax.ShapeDtypeStruct((e // 2 * (d // LANES), LANES), jnp.uint32),
        grid=(e // PACK_ROWS,),
        in_specs=[pl.BlockSpec((PACK_ROWS, d), lambda i: (i, 0))],
        out_specs=pl.BlockSpec((out_rows, LANES), lambda i: (i, 0)),
        compiler_params=_params(("parallel",)),
        name="pack_table",
    )(tab)


def _block(n, want):
    return want if n % want == 0 else n


def kernel(x, c, ctx, c_ctx, w_ada, b_ada, g_norm1, w_in, g_cq, w_uq, g_ckv, w_ukv, g_qn, g_kn, conv_qk, b_igate, b_fgate, g_mlstm, w_out, g_norm2, w_pq, sub_keys, expert_u, expert_v):
    B, N, D = x.shape
    NC = ctx.shape[1]
    assert w_ada.shape[0] == 1, "one layer"
    assert N % (ATTN_TQ * ATTN_TILES_PER_STEP) == 0 and NC % ML_CHUNK == 0 and D == SUBLANES * LANES
    q_rank = g_cq.shape[1]
    kv_rank = g_ckv.shape[1]
    mla_cols = q_rank + kv_rank + MLA_ROPE
    assert (q_rank, kv_rank) == (C_KV0 - C_Q0, C_KR0 - C_KV0)
    swap = jnp.arange(MLA_ROPE) ^ (MLA_ROPE // 4)

    cc = jnp.concatenate([c, c_ctx[None, :], jnp.zeros((16 - B - 1, D), F32)], axis=0)
    mod = _ada(cc, w_ada[0].astype(BF16), b_ada)
    sh1, sc1, gt1, sh2, sc2, gt2 = [mod[:, i * D:(i + 1) * D] for i in range(6)]
    lat = lambda m: m[:B].reshape(B, 1, D)
    ctxm = lambda m: jnp.broadcast_to(m[B:B + 1].reshape(1, 1, D), (B, 1, D))

    wi = w_in[0]
    n_qk = 2 * ML_HEADS * ML_QK
    n_v = ML_HEADS * ML_V
    n_g = 4 * ML_HEADS
    m0 = mla_cols
    w_cols = jnp.concatenate([
        wi[:, 0:mla_cols],
        wi[:, q_rank + kv_rank + swap],
        wi[:, m0:m0 + n_qk + 2 * n_v + n_g],
        jnp.zeros((D, C_END - C_G0 - n_g), F32)], axis=1).astype(BF16)
    assert w_cols.shape[1] == C_END
    w_gates_t = wi[:, m0 + n_qk + 2 * n_v:m0 + n_qk + 2 * n_v + n_g].T.astype(BF16)
    cols_l, gates_l = _inproj(x, g_norm1, lat(sc1), lat(sh1), w_cols, w_gates_t, _block(N, ROW_TILE))
    cols_c, gates_c = _inproj(ctx, g_norm1, ctxm(sc1), ctxm(sh1), w_cols, w_gates_t, _block(NC, ROW_TILE))

    wq = w_uq[0].reshape(q_rank, MLA_HEADS, MLA_QK)
    wuq = jnp.concatenate([wq, wq[:, :, MLA_NOPE + swap]], axis=2).reshape(q_rank, MLA_HEADS * 256).astype(BF16)
    wukv = w_ukv[0].astype(BF16)
    gqs, gks = g_qn[:, MLA_NOPE + swap], g_kn[:, MLA_NOPE + swap]
    cos_l, sin_l = _rope_tables(N)
    cos_c, sin_c = jnp.ones((NC, MLA_ROPE), F32), jnp.zeros((NC, MLA_ROPE), F32)
    q_l, k_l, v_l = _mla_prep(cols_l, g_cq, wuq, g_ckv, wukv, g_qn, gqs, g_kn, gks, cos_l, sin_l, _block(N, ROW_TILE))
    _, k_c, v_c = _mla_prep(cols_c, g_cq, wuq, g_ckv, wukv, g_qn, gqs, g_kn, gks, cos_c, sin_c, _block(NC, ROW_TILE))
    mla = _attention(q_l, k_c, k_l, v_c, v_l, ATTN_TQ)

    bias16 = jnp.concatenate([b_igate[0].reshape(-1), b_fgate[0].reshape(-1)])
    cw = conv_qk[0]
    pl_ = _ml_prep(cols_l, gates_l, cw, bias16[None, :], bias16[:, None], _block(N, ROW_TILE))
    pc_ = _ml_prep(cols_c, gates_c, cw, bias16[None, :], bias16[:, None], _block(NC, ROW_TILE))
    hsum = _ml_scan(*pl_, *pc_)

    wo = w_out[0].astype(BF16)
    hw = MLA_HEADS * MLA_V
    sk = sub_keys[0].reshape(2 * PEER_HEADS, N_KEYS, -1).astype(BF16)
    x1, h2, scores = _outproj(x, mla, hsum, cols_l, g_mlstm, wo[:hw], wo[hw:], lat(gt1), g_norm2, lat(sc2), lat(sh2),
                              w_pq[0].astype(BF16), sk, OUT_TILE)
    T = B * N
    nsel = PEER_HEADS * PEER_TOPK
    off, par, gw = [a.reshape(T, nsel) for a in _route(scores, ROUTE_TILE)]

    tb = PEER_TOKENS
    coef = _peer_u(off, h2.reshape(T, D), _pack_table(expert_u[0]), par, gw, tb)
    out = _peer_v(off, coef, par, _pack_table(expert_v[0]), x1.reshape(T, D), lat(gt2), tb, N)
    return out.reshape(B, N, D)
```
